```python
import jax
import jax.numpy as jnp
from jax import lax
import numpy as np

D_MODEL = 1024
BATCH = 16
SEQ = 2048
DEPTH = 1

N_MEM = 256
EPS = 1e-6

ML_HEADS = 4
ML_QK_DIM = 128
ML_V_DIM = D_MODEL // ML_HEADS
ML_CHUNK = 64
ML_QK_W = ML_HEADS * ML_QK_DIM
ML_V_W = ML_HEADS * ML_V_DIM

FX_HEADS = 8
FX_HEAD_DIM = D_MODEL // FX_HEADS
FX_BLOCK = 128
FX_W = FX_HEADS * FX_HEAD_DIM

XA_HEADS = 4
XA_HEAD_DIM = D_MODEL // XA_HEADS

N_GROUPS = 4
EXPERTS_PER_GROUP = 8
N_EXPERTS = N_GROUPS * EXPERTS_PER_GROUP
TOP_K = 2
D_EXPERT = D_MODEL // 2

IN_SPLITS = (ML_QK_W, ML_QK_W, ML_V_W, ML_V_W, ML_HEADS, ML_HEADS,
             FX_W, FX_W, FX_W, FX_HEADS, D_MODEL, D_MODEL)
D_IN = 2 * ML_QK_W + 2 * ML_V_W + 2 * ML_HEADS + 3 * FX_W + FX_HEADS + 2 * D_MODEL

kernel_name = 'hybrid_mlstm_fox_hmoe_block'


def _rmsnorm(x, g):
    xf = x.astype(jnp.float32)
    y = xf * lax.rsqrt(jnp.mean(xf * xf, axis=-1, keepdims=True) + EPS)
    return (y * g.astype(jnp.float32)).astype(x.dtype)


def _head_rmsnorm(y, g):
    yf = y.astype(jnp.float32)
    return yf * lax.rsqrt(jnp.mean(yf * yf, axis=-1, keepdims=True) + EPS) * g.astype(jnp.float32)


def _split_cols(z, sizes):
    parts, off = [], 0
    for s in sizes:
        parts.append(z[..., off:off + s])
        off += s
    return parts


def _to_chunks(a, n_chunks, chunk):
    b, _, h = a.shape[:3]
    a = a.reshape((b, n_chunks, chunk, h) + a.shape[3:])
    return jnp.moveaxis(a, (1, 3), (0, 2))


def _mlstm(q, k, v, i_pre, log_f):
    B, S, H, dk = q.shape
    dv = v.shape[-1]
    L = ML_CHUNK
    nc = S // L
    f32 = jnp.float32
    q = q.astype(f32)
    k = k.astype(f32) * (dk ** -0.5)
    v = v.astype(f32)
    xs = tuple(_to_chunks(a, nc, L) for a in (q, k, v, i_pre, log_f))
    causal = jnp.tril(jnp.ones((L, L), dtype=bool))

    def step(carry, inp):
        C, n, m = carry
        qc, kc, vc, ic, lfc = inp
        b = jnp.cumsum(lfc, axis=-1)
        log_d = jnp.where(causal, b[..., :, None] - b[..., None, :] + ic[..., None, :], -jnp.inf)
        log_inter = b + m[..., None]
        m_t = jnp.maximum(log_inter, jnp.max(log_d, axis=-1))
        s = jnp.einsum('bhtd,bhsd->bhts', qc, kc) * jnp.exp(log_d - m_t[..., None])
        w_inter = jnp.exp(log_inter - m_t)
        num = (jnp.einsum('bhts,bhsv->bhtv', s, vc)
               + w_inter[..., None] * jnp.einsum('bhvd,bhtd->bhtv', C, qc))
        den = jnp.sum(s, axis=-1) + w_inter * jnp.einsum('bhd,bhtd->bht', n, qc)
        h = num / jnp.maximum(jnp.abs(den), jnp.exp(-m_t))[..., None]
        b_end = b[..., -1]
        log_w = b_end[..., None] - b + ic
        m_new = jnp.maximum(b_end + m, jnp.max(log_w, axis=-1))
        w_s = jnp.exp(log_w - m_new[..., None])
        decay = jnp.exp(b_end + m - m_new)
        C_new = decay[..., None, None] * C + jnp.einsum('bhs,bhsv,bhsd->bhvd', w_s, vc, kc)
        n_new = decay[..., None] * n + jnp.einsum('bhs,bhsd->bhd', w_s, kc)
        return (C_new, n_new, m_new), h

    init = (jnp.zeros((B, H, dv, dk), f32), jnp.zeros((B, H, dk), f32), jnp.zeros((B, H), f32))
    _, hs = lax.scan(step, init, xs)
    return jnp.moveaxis(hs, (0, 2), (1, 3)).reshape(B, S, H, dv)


def _forgetting_attention(q, k, v, log_f):
    B, S, H, d = q.shape
    c = jnp.cumsum(log_f.astype(jnp.float32), axis=1).transpose(0, 2, 1)
    qh = q.transpose(0, 2, 1, 3)
    kh = k.transpose(0, 2, 1, 3)
    vh = v.transpose(0, 2, 1, 3)
    scale = d ** -0.5
    outs = []
    for blk in range(S // FX_BLOCK):
        q0 = blk * FX_BLOCK
        q1 = q0 + FX_BLOCK
        logits = jnp.einsum('bhqd,bhkd->bhqk', qh[:, :, q0:q1], kh[:, :, :q1]).astype(jnp.float32) * scale
        logits = logits + c[:, :, q0:q1, None] - c[:, :, None, :q1]
        mask = (q0 + jnp.arange(FX_BLOCK))[:, None] >= jnp.arange(q1)[None, :]
        p = jax.nn.softmax(jnp.where(mask, logits, -jnp.inf), axis=-1)
        outs.append(jnp.einsum('bhqk,bhkd->bhqd', p.astype(v.dtype), vh[:, :, :q1]))
    o = jnp.concatenate(outs, axis=2)
    return o.transpose(0, 2, 1, 3).reshape(B, S, H * d)


def _cross_attention(h, m, w_xq, w_xkv, w_xo):
    B, S, D = h.shape
    N = m.shape[1]
    q = (h @ w_xq).reshape(B, S, XA_HEADS, XA_HEAD_DIM)
    kv = m @ w_xkv
    k = kv[..., :D].reshape(B, N, XA_HEADS, XA_HEAD_DIM)
    v = kv[..., D:].reshape(B, N, XA_HEADS, XA_HEAD_DIM)
    logits = jnp.einsum('bshd,bnhd->bhsn', q, k).astype(jnp.float32) * (XA_HEAD_DIM ** -0.5)
    p = jax.nn.softmax(logits, axis=-1)
    o = jnp.einsum('bhsn,bnhd->bshd', p.astype(v.dtype), v)
    return o.reshape(B, S, D) @ w_xo


def _hier_moe(h, w_rg, b_rg, w_re, b_re, w_gate, w_up, w_down):
    B, S, D = h.shape
    f32 = jnp.float32
    t = h.reshape(B * S, D)
    g_logits = (t @ w_rg).astype(f32) + b_rg.astype(f32)
    g_prob = jax.nn.softmax(g_logits, axis=-1)
    g_idx = jnp.argmax(g_logits, axis=-1)
    g_onehot = jax.nn.one_hot(g_idx, N_GROUPS, dtype=f32)
    g_p = jnp.sum(g_prob * g_onehot, axis=-1, keepdims=True)
    e_logits = ((t @ w_re).astype(f32) + b_re.astype(f32)).reshape(-1, N_GROUPS, EXPERTS_PER_GROUP)
    e_sel = jnp.einsum('tge,tg->te', e_logits, g_onehot)
    top_v, top_i = lax.top_k(e_sel, TOP_K)
    top_w = jax.nn.softmax(top_v, axis=-1) * g_p
    expert_id = g_idx[:, None] * EXPERTS_PER_GROUP + top_i
    combine = jnp.sum(jax.nn.one_hot(expert_id, N_EXPERTS, dtype=f32) * top_w[..., None], axis=1)
    y = jnp.zeros((B * S, D), f32)
    for e in range(N_EXPERTS):
        he = jax.nn.silu(t @ w_gate[e]) * (t @ w_up[e])
        y = y + combine[:, e:e + 1] * (he @ w_down[e]).astype(f32)
    return y.astype(h.dtype).reshape(B, S, D)


def setup_inputs(seed: int = 0) -> dict:
    key = jax.random.key(seed)
    ks = jax.random.split(key, 32)
    f32 = jnp.float32
    L = DEPTH

    def nrm(k, shape, scale):
        return jax.random.normal(k, shape, f32) * scale

    def gain(k, shape):
        return 1.0 + 0.05 * jax.random.normal(k, shape, f32)

    return {
        'x': nrm(ks[0], (BATCH, SEQ, D_MODEL), 1.0),
        'mem': nrm(ks[1], (BATCH, N_MEM, D_MODEL), 1.0),
        'g_mix': gain(ks[2], (L, D_MODEL)),
        'w_in': nrm(ks[3], (L, D_MODEL, D_IN), D_MODEL ** -0.5),
        'b_ml_i': nrm(ks[4], (L, ML_HEADS), 0.1),
        'b_ml_f': jnp.linspace(3.0, 6.0, ML_HEADS, dtype=f32) + nrm(ks[5], (L, ML_HEADS), 0.1),
        'b_fx_f': jnp.linspace(1.0, 5.0, FX_HEADS, dtype=f32) + nrm(ks[6], (L, FX_HEADS), 0.1),
        'b_gate_ml': nrm(ks[7], (L, D_MODEL), 0.02),
        'b_gate_fx': nrm(ks[8], (L, D_MODEL), 0.02),
        'g_ml_head': gain(ks[9], (L, ML_V_W)),
        'w_proj_ml': nrm(ks[10], (L, ML_V_W, D_MODEL), ML_V_W ** -0.5),
        'w_proj_fx': nrm(ks[11], (L, FX_W, D_MODEL), FX_W ** -0.5),
        'w_out': nrm(ks[12], (L, D_MODEL, D_MODEL), D_MODEL ** -0.5),
        'g_xq': gain(ks[13], (L, D_MODEL)),
        'g_xmem': gain(ks[14], (L, D_MODEL)),
        'w_xq': nrm(ks[15], (L, D_MODEL, D_MODEL), D_MODEL ** -0.5),
        'w_xkv': nrm(ks[16], (L, D_MODEL, 2 * D_MODEL), D_MODEL ** -0.5),
        'w_xo': nrm(ks[17], (L, D_MODEL, D_MODEL), D_MODEL ** -0.5),
        'g_moe': gain(ks[18], (L, D_MODEL)),
        'w_rg': nrm(ks[19], (L, D_MODEL, N_GROUPS), D_MODEL ** -0.5),
        'b_rg': nrm(ks[20], (L, N_GROUPS), 0.01),
        'w_re': nrm(ks[21], (L, D_MODEL, N_EXPERTS), D_MODEL ** -0.5),
        'b_re': nrm(ks[22], (L, N_EXPERTS), 0.01),
        'w_gate': nrm(ks[23], (L, N_EXPERTS, D_MODEL, D_EXPERT), D_MODEL ** -0.5),
        'w_up': nrm(ks[24], (L, N_EXPERTS, D_MODEL, D_EXPERT), D_MODEL ** -0.5),
        'w_down': nrm(ks[25], (L, N_EXPERTS, D_EXPERT, D_MODEL), D_EXPERT ** -0.5),
        'g_final': gain(ks[26], (D_MODEL,)),
    }


def reference(x, mem, g_mix, w_in, b_ml_i, b_ml_f, b_fx_f, b_gate_ml, b_gate_fx, g_ml_head,
              w_proj_ml, w_proj_fx, w_out, g_xq, g_xmem, w_xq, w_xkv, w_xo, g_moe,
              w_rg, b_rg, w_re, b_re, w_gate, w_up, w_down, g_final):
    B, S, _ = x.shape
    f32 = jnp.float32
    for l in range(DEPTH):
        h = _rmsnorm(x, g_mix[l])
        (ml_q, ml_k, ml_v, ml_o, ml_i, ml_f,
         fx_q, fx_k, fx_v, fx_f, gt_ml, gt_fx) = _split_cols(h @ w_in[l], IN_SPLITS)
        y_ml = _mlstm(ml_q.reshape(B, S, ML_HEADS, ML_QK_DIM),
                      ml_k.reshape(B, S, ML_HEADS, ML_QK_DIM),
                      ml_v.reshape(B, S, ML_HEADS, ML_V_DIM),
                      ml_i.astype(f32) + b_ml_i[l].astype(f32),
                      jax.nn.log_sigmoid(ml_f.astype(f32) + b_ml_f[l].astype(f32)))
        y_ml = _head_rmsnorm(y_ml, g_ml_head[l].reshape(ML_HEADS, ML_V_DIM))
        y_ml = y_ml.reshape(B, S, ML_V_W).astype(x.dtype) * jax.nn.sigmoid(ml_o)
        y_fx = _forgetting_attention(fx_q.reshape(B, S, FX_HEADS, FX_HEAD_DIM),
                                     fx_k.reshape(B, S, FX_HEADS, FX_HEAD_DIM),
                                     fx_v.reshape(B, S, FX_HEADS, FX_HEAD_DIM),
                                     jax.nn.log_sigmoid(fx_f.astype(f32) + b_fx_f[l].astype(f32)))
        merged = (jax.nn.sigmoid(gt_ml + b_gate_ml[l]) * (y_ml @ w_proj_ml[l])
                  + jax.nn.sigmoid(gt_fx + b_gate_fx[l]) * (y_fx @ w_proj_fx[l]))
        x = x + merged @ w_out[l]
        x = x + _cross_attention(_rmsnorm(x, g_xq[l]), _rmsnorm(mem, g_xmem[l]),
                                 w_xq[l], w_xkv[l], w_xo[l])
        x = x + _hier_moe(_rmsnorm(x, g_moe[l]), w_rg[l], b_rg[l], w_re[l], b_re[l],
                          w_gate[l], w_up[l], w_down[l])
    return _rmsnorm(x, g_final)
```

```python
import functools

import jax
import jax.numpy as jnp
from jax import lax
from jax.experimental import pallas as pl
from jax.experimental.pallas import tpu as pltpu

F32 = jnp.float32
BF16 = jnp.bfloat16

D_MODEL = 1024
EPS = 1e-6
ML_HEADS = 4
ML_QK_DIM = 128
ML_V_DIM = 256
FX_HEADS = 8
FX_HEAD_DIM = 128
XA_HEADS = 4
XA_HEAD_DIM = 256
N_GROUPS = 4
EXPERTS_PER_GROUP = 8
N_EXPERTS = 32
D_EXPERT = 512

LANES = 128
N_GATES = 16
Z_W = 8192
Z_ML_V, Z_ML_O, Z_FX_Q, Z_FX_K, Z_FX_V, Z_GT_ML, Z_GT_FX = 1, 2, 3, 4, 5, 6, 7

VMEM_LIMIT = 56 * 1024 * 1024

TM_IN = 256
IN_CHUNK = 1024
TQ = 256
TM = 512
TM_MOE = 1024


def _params(sem):
    return pltpu.CompilerParams(dimension_semantics=sem, vmem_limit_bytes=VMEM_LIMIT)


def _rms(x, g):
    return x * lax.rsqrt(jnp.mean(x * x, axis=-1, keepdims=True) + EPS) * g


def _dot_nt(a, b, **kw):
    return lax.dot_general(a, b, (((1,), (1,)), ((), ())), preferred_element_type=F32, **kw)


def _resident(shape):
    zeros = (0,) * len(shape)
    return pl.BlockSpec(shape, lambda *_: zeros, pipeline_mode=pl.Buffered(1))


def _inproj_body(x_ref, g_ref, w_ref, wgt_ref, z_ref, gt_ref):
    h = _rms(x_ref[...], g_ref[...])
    hb = h.astype(BF16)
    for c in range(Z_W // IN_CHUNK):
        sl = slice(c * IN_CHUNK, (c + 1) * IN_CHUNK)
        z_ref[:, sl] = jnp.dot(hb, w_ref[:, sl], preferred_element_type=F32).astype(BF16)
    gt_ref[...] = _dot_nt(wgt_ref[...], h, precision=lax.Precision.HIGHEST)


def _inproj(x2d, g, w, wgt):
    t = x2d.shape[0]
    return pl.pallas_call(
        _inproj_body,
        grid=(t // TM_IN,),
        in_specs=[
            pl.BlockSpec((TM_IN, D_MODEL), lambda i: (i, 0)),
            _resident((1, D_MODEL)),
            _resident((D_MODEL, Z_W)),
            _resident((N_GATES, D_MODEL)),
        ],
        out_specs=[
            pl.BlockSpec((TM_IN, Z_W), lambda i: (i, 0)),
            pl.BlockSpec((N_GATES, TM_IN), lambda i: (0, i)),
        ],
        out_shape=[jax.ShapeDtypeStruct((t, Z_W), BF16), jax.ShapeDtypeStruct((N_GATES, t), F32)],
        compiler_params=_params(("parallel",)),
        name="inproj",
    )(x2d, g, w, wgt)


def _scan_lanes(x, op, identity):
    n = x.shape[-1]
    idx = lax.broadcasted_iota(jnp.int32, x.shape, 1)
    s = 1
    while s < n:
        shifted = pltpu.roll(x, s, axis=1)
        x = op(x, jnp.where(idx >= s, shifted, identity))
        s *= 2
    return x


def _log_sigmoid(x):
    return jnp.minimum(x, 0.0) - jnp.log1p(jnp.exp(-jnp.abs(x)))


def _gateprep_body(gt_ref, bias_ref, rows_ref, cols_ref):
    g = gt_ref[...] + bias_ref[...]
    cs = _scan_lanes(_log_sigmoid(g), jnp.add, 0.0)
    b = cs[4:8]
    c = cs[8:16]
    a = g[0:4] - b
    a8 = jnp.concatenate([a, a], axis=0)
    m = _scan_lanes(a8, jnp.maximum, -jnp.inf)[0:4]
    rows_ref[...] = jnp.concatenate([a, m, c], axis=0)
    s = g.shape[1]
    padded = jnp.concatenate([m, b, c, jnp.zeros((LANES - N_GATES, s), F32)], axis=0)
    cols_ref[...] = padded.T


def _gateprep(gt3, bias):
    nb, _, s = gt3.shape
    return pl.pallas_call(
        _gateprep_body,
        grid=(nb,),
        in_specs=[
            pl.BlockSpec((None, N_GATES, s), lambda b: (b, 0, 0)),
            _resident((N_GATES, 1)),
        ],
        out_specs=[
            pl.BlockSpec((None, N_GATES, s), lambda b: (b, 0, 0)),
            pl.BlockSpec((None, s, LANES), lambda b: (b, 0, 0)),
        ],
        out_shape=[jax.ShapeDtypeStruct((nb, N_GATES, s), F32), jax.ShapeDtypeStruct((nb, s, LANES), F32)],
        compiler_params=_params(("parallel",)),
        name="gateprep",
    )(gt3, bias)


def _causal_mask():
    r = lax.broadcasted_iota(jnp.int32, (TQ, TQ), 0)
    c = lax.broadcasted_iota(jnp.int32, (TQ, TQ), 1)
    return r >= c


def _mlstm_body(q_ref, k_ref, v_ref, o_ref, rows_ref, cols_ref, gh_ref, y_ref, num_scr, den_scr):
    i = pl.program_id(1)
    causal = _causal_mask()
    for h in range(ML_HEADS):
        qk = slice(h * ML_QK_DIM, (h + 1) * ML_QK_DIM)
        vv = slice(h * ML_V_DIM, (h + 1) * ML_V_DIM)
        qh = q_ref[:, qk]
        m_col = cols_ref[:, h:h + 1]
        b_col = cols_ref[:, ML_HEADS + h:ML_HEADS + h + 1]

        def tile(j, masked):
            ks = pl.ds(pl.multiple_of(j * TQ, TQ), TQ)
            a_row = rows_ref[h, pl.ds(j, 1), :]
            w = jnp.exp(a_row - m_col)
            if masked:
                w = jnp.where(causal, w, 0.0)
            s = _dot_nt(qh, k_ref[ks, qk]) * w
            return jnp.sum(s, axis=1, keepdims=True), jnp.dot(s.astype(BF16), v_ref[ks, vv],
                                                              preferred_element_type=F32)

        num_scr[...] = jnp.zeros_like(num_scr)
        den_scr[...] = jnp.zeros_like(den_scr)

        def body(j, carry):
            d, n = tile(j, False)
            den_scr[...] += d
            num_scr[...] += n
            return carry

        lax.fori_loop(0, i, body, 0)
        d, n = tile(i, True)
        den = den_scr[...] + d
        num = num_scr[...] + n
        hh = num * (1.0 / jnp.maximum(jnp.abs(den), jnp.exp(-(b_col + m_col))))
        y = _rms(hh, gh_ref[:, vv])
        y_ref[:, vv] = (y * jax.nn.sigmoid(o_ref[:, vv].astype(F32))).astype(BF16)


def _mlstm(z3, rows4, cols3, g_head):
    nb, s, _ = z3.shape
    nq = s // TQ
    return pl.pallas_call(
        _mlstm_body,
        grid=(nb, nq),
        in_specs=[
            pl.BlockSpec((None, TQ, ML_HEADS * ML_QK_DIM), lambda b, i: (b, i, 0)),
            pl.BlockSpec((None, s, ML_HEADS * ML_QK_DIM), lambda b, i: (b, 0, 1)),
            pl.BlockSpec((None, s, D_MODEL), lambda b, i: (b, 0, Z_ML_V)),
            pl.BlockSpec((None, TQ, D_MODEL), lambda b, i: (b, i, Z_ML_O)),
            pl.BlockSpec((None, N_GATES, nq, TQ), lambda b, i: (b, 0, 0, 0)),
            pl.BlockSpec((None, TQ, LANES), lambda b, i: (b, i, 0)),
            _resident((1, D_MODEL)),
        ],
        out_specs=pl.BlockSpec((None, TQ, D_MODEL), lambda b, i: (b, i, 0)),
        out_shape=jax.ShapeDtypeStruct((nb, s, D_MODEL), BF16),
        scratch_shapes=[pltpu.VMEM((TQ, ML_V_DIM), F32), pltpu.VMEM((TQ, 1), F32)],
        compiler_params=_params(("parallel", "arbitrary")),
        name="mlstm",
    )(z3, z3, z3, z3, rows4, cols3, g_head)


def _fox_body(q_ref, k_ref, v_ref, rows_ref, cols_ref, y_ref, acc_scr, m_scr, l_scr):
    i = pl.program_id(1)
    causal = _causal_mask()
    for h in range(FX_HEADS):
        hd = slice(h * FX_HEAD_DIM, (h + 1) * FX_HEAD_DIM)
        qh = q_ref[:, hd]
        c_col = cols_ref[:, 2 * ML_HEADS + h:2 * ML_HEADS + h + 1]

        def tile(j, masked):
            ks = pl.ds(pl.multiple_of(j * TQ, TQ), TQ)
            c_row = rows_ref[2 * ML_HEADS + h, pl.ds(j, 1), :]
            u = _dot_nt(qh, k_ref[ks, hd]) - c_row
            if masked:
                u = jnp.where(causal, u, -jnp.inf)
            m_prev = m_scr[...]
            m_new = jnp.maximum(m_prev, jnp.max(u, axis=1, keepdims=True) + c_col)
            p = jnp.exp(u - (m_new - c_col))
            alpha = jnp.exp(m_prev - m_new)
            l_scr[...] = alpha * l_scr[...] + jnp.sum(p, axis=1, keepdims=True)
            acc_scr[...] = alpha * acc_scr[...] + jnp.dot(p.astype(BF16), v_ref[ks, hd],
                                                          preferred_element_type=F32)
            m_scr[...] = m_new

        m_scr[...] = jnp.full_like(m_scr, -jnp.inf)
        l_scr[...] = jnp.zeros_like(l_scr)
        acc_scr[...] = jnp.zeros_like(acc_scr)

        def body(j, carry):
            tile(j, False)
            return carry

        lax.fori_loop(0, i, body, 0)
        tile(i, True)
        y_ref[:, hd] = (acc_scr[...] * (1.0 / l_scr[...])).astype(BF16)


def _fox(z3, rows4, cols3):
    nb, s, _ = z3.shape
    nq = s // TQ
    return pl.pallas_call(
        _fox_body,
        grid=(nb, nq),
        in_specs=[
            pl.BlockSpec((None, TQ, D_MODEL), lambda b, i: (b, i, Z_FX_Q)),
            pl.BlockSpec((None, s, D_MODEL), lambda b, i: (b, 0, Z_FX_K)),
            pl.BlockSpec((None, s, D_MODEL), lambda b, i: (b, 0, Z_FX_V)),
            pl.BlockSpec((None, N_GATES, nq, TQ), lambda b, i: (b, 0, 0, 0)),
            pl.BlockSpec((None, TQ, LANES), lambda b, i: (b, i, 0)),
        ],
        out_specs=pl.BlockSpec((None, TQ, D_MODEL), lambda b, i: (b, i, 0)),
        out_shape=jax.ShapeDtypeStruct((nb, s, D_MODEL), BF16),
        scratch_shapes=[pltpu.VMEM((TQ, FX_HEAD_DIM), F32), pltpu.VMEM((TQ, 1), F32), pltpu.VMEM((TQ, 1), F32)],
        compiler_params=_params(("parallel", "arbitrary")),
        name="fox",
    )(z3, z3, z3, rows4, cols3)


def _merge_body(x_ref, yml_ref, yfx_ref, gml_ref, gfx_ref, bml_ref, bfx_ref, wml_ref, wfx_ref, wout_ref, x1_ref):
    p_ml = jnp.dot(yml_ref[...], wml_ref[...], preferred_element_type=F32)
    p_fx = jnp.dot(yfx_ref[...], wfx_ref[...], preferred_element_type=F32)
    merged = (jax.nn.sigmoid(gml_ref[...].astype(F32) + bml_ref[...]) * p_ml
              + jax.nn.sigmoid(gfx_ref[...].astype(F32) + bfx_ref[...]) * p_fx)
    x1_ref[...] = x_ref[...] + jnp.dot(merged.astype(BF16), wout_ref[...], preferred_element_type=F32)


def _merge(x2d, yml, yfx, z2d, bml, bfx, wml, wfx, wout):
    t = x2d.shape[0]
    tile = lambda col: pl.BlockSpec((TM, D_MODEL), lambda i, col=col: (i, col))
    return pl.pallas_call(
        _merge_body,
        grid=(t // TM,),
        in_specs=[tile(0), tile(0), tile(0), tile(Z_GT_ML), tile(Z_GT_FX),
                  _resident((1, D_MODEL)), _resident((1, D_MODEL)),
                  _resident((D_MODEL, D_MODEL)), _resident((D_MODEL, D_MODEL)), _resident((D_MODEL, D_MODEL))],
        out_specs=tile(0),
        out_shape=jax.ShapeDtypeStruct((t, D_MODEL), F32),
        compiler_params=_params(("parallel",)),
        name="merge",
    )(x2d, yml, yfx, z2d, z2d, bml, bfx, wml, wfx, wout)


def _memkv_body(m_ref, g_ref, w_ref, kv_ref):
    hb = _rms(m_ref[...], g_ref[...]).astype(BF16)
    kv_ref[...] = jnp.dot(hb, w_ref[...], preferred_element_type=F32).astype(BF16)


def _memkv(mem2d, g, w):
    t = mem2d.shape[0]
    return pl.pallas_call(
        _memkv_body,
        grid=(t // TM,),
        in_specs=[pl.BlockSpec((TM, D_MODEL), lambda i: (i, 0)), _resident((1, D_MODEL)),
                  _resident((D_MODEL, 2 * D_MODEL))],
        out_specs=pl.BlockSpec((TM, 2 * D_MODEL), lambda i: (i, 0)),
        out_shape=jax.ShapeDtypeStruct((t, 2 * D_MODEL), BF16),
        compiler_params=_params(("parallel",)),
        name="memkv",
    )(mem2d, g, w)


def _route(lg):
    lane = lax.broadcasted_iota(jnp.int32, lg.shape, 1).astype(F32)
    big = jnp.float32(LANES)
    ninf = -jnp.inf
    gl = jnp.where(lane >= N_EXPERTS, jnp.where(lane < N_EXPERTS + N_GROUPS, lg, ninf), ninf)
    gmax = jnp.max(gl, axis=1, keepdims=True)
    gidx = jnp.min(jnp.where(gl == gmax, lane, big), axis=1, keepdims=True) - N_EXPERTS
    g_p = 1.0 / jnp.sum(jnp.exp(gl - gmax), axis=1, keepdims=True)
    lo = gidx * EXPERTS_PER_GROUP
    el = jnp.where(lane >= lo, jnp.where(lane < lo + EXPERTS_PER_GROUP, lg, ninf), ninf)
    v1 = jnp.max(el, axis=1, keepdims=True)
    i1 = jnp.min(jnp.where(el == v1, lane, big), axis=1, keepdims=True)
    el2 = jnp.where(lane == i1, ninf, el)
    v2 = jnp.max(el2, axis=1, keepdims=True)
    i2 = jnp.min(jnp.where(el2 == v2, lane, big), axis=1, keepdims=True)
    t = jnp.exp(v2 - v1)
    w1 = g_p / (1.0 + t)
    w2 = w1 * t
    return jnp.where(lane == i1, w1, 0.0) + jnp.where(lane == i2, w2, 0.0)


def _xattn_body(x1_ref, kv_ref, gq_ref, wq_ref, wo_ref, gm_ref, wr_ref, br_ref, x2_ref, h3_ref, cmb_ref):
    x1 = x1_ref[...]
    hb = _rms(x1, gq_ref[...]).astype(BF16)
    q = (jnp.dot(hb, wq_ref[...], preferred_element_type=F32) * (XA_HEAD_DIM ** -0.5)).astype(BF16)
    outs = []
    for h in range(XA_HEADS):
        hd = slice(h * XA_HEAD_DIM, (h + 1) * XA_HEAD_DIM)
        vd = slice(D_MODEL + h * XA_HEAD_DIM, D_MODEL + (h + 1) * XA_HEAD_DIM)
        s = _dot_nt(q[:, hd], kv_ref[:, hd])
        p = jnp.exp(s - jnp.max(s, axis=1, keepdims=True))
        p = p * (1.0 / jnp.sum(p, axis=1, keepdims=True))
        outs.append(jnp.dot(p.astype(BF16), kv_ref[:, vd], preferred_element_type=F32).astype(BF16))
    o = jnp.concatenate(outs, axis=1)
    x2 = x1 + jnp.dot(o, wo_ref[...], preferred_element_type=F32)
    x2_ref[...] = x2
    h3 = _rms(x2, gm_ref[...])
    h3_ref[...] = h3.astype(BF16)
    lg = jnp.dot(h3, wr_ref[...], precision=lax.Precision.HIGHEST, preferred_element_type=F32) + br_ref[...]
    cmb_ref[...] = _route(lg)


def _xattn(x1, kv3, gq, wq, wo, gm, wr, br, seq):
    t = x1.shape[0]
    per_b = seq // TM
    n_mem = kv3.shape[1]
    tile = pl.BlockSpec((TM, D_MODEL), lambda i: (i, 0))
    return pl.pallas_call(
        _xattn_body,
        grid=(t // TM,),
        in_specs=[tile,
                  pl.BlockSpec((None, n_mem, 2 * D_MODEL), lambda i: (i // per_b, 0, 0)),
                  _resident((1, D_MODEL)), _resident((D_MODEL, D_MODEL)), _resident((D_MODEL, D_MODEL)),
                  _resident((1, D_MODEL)), _resident((D_MODEL, LANES)), _resident((1, LANES))],
        out_specs=[tile, tile, pl.BlockSpec((TM, LANES), lambda i: (i, 0))],
        out_shape=[jax.ShapeDtypeStruct((t, D_MODEL), F32), jax.ShapeDtypeStruct((t, D_MODEL), BF16),
                   jax.ShapeDtypeStruct((t, LANES), F32)],
        compiler_params=_params(("parallel",)),
        name="xattn",
    )(x1, kv3, gq, wq, wo, gm, wr, br)


def _moe_body(h3_ref, cmb_ref, x2_ref, wg_ref, wu_ref, wd_ref, gf_ref, out_ref, acc_scr):
    e = pl.program_id(1)

    @pl.when(e == 0)
    def _():
        acc_scr[...] = jnp.zeros_like(acc_scr)

    hb = h3_ref[...]
    gate = jnp.dot(hb, wg_ref[...], preferred_element_type=F32)
    up = jnp.dot(hb, wu_ref[...], preferred_element_type=F32)
    he = (gate * jax.nn.sigmoid(gate) * up).astype(BF16)
    lane = lax.broadcasted_iota(jnp.int32, cmb_ref.shape, 1)
    w = jnp.sum(jnp.where(lane == e, cmb_ref[...], 0.0), axis=1, keepdims=True)
    acc_scr[...] += w * jnp.dot(he, wd_ref[...], preferred_element_type=F32)

    @pl.when(e == N_EXPERTS - 1)
    def _():
        out_ref[...] = _rms(x2_ref[...] + acc_scr[...], gf_ref[...])


def _moe(h3, cmb, x2, wg, wu, wd, gf):
    t = h3.shape[0]
    tile = lambda w: pl.BlockSpec((TM_MOE, w), lambda i, e: (i, 0))
    return pl.pallas_call(
        _moe_body,
        grid=(t // TM_MOE, N_EXPERTS),
        in_specs=[tile(D_MODEL), tile(LANES), tile(D_MODEL),
                  pl.BlockSpec((None, D_MODEL, D_EXPERT), lambda i, e: (e, 0, 0)),
                  pl.BlockSpec((None, D_MODEL, D_EXPERT), lambda i, e: (e, 0, 0)),
                  pl.BlockSpec((None, D_EXPERT, D_MODEL), lambda i, e: (e, 0, 0)),
                  pl.BlockSpec((1, D_MODEL), lambda i, e: (0, 0))],
        out_specs=tile(D_MODEL),
        out_shape=jax.ShapeDtypeStruct((t, D_MODEL), F32),
        scratch_shapes=[pltpu.VMEM((TM_MOE, D_MODEL), F32)],
        compiler_params=_params(("parallel", "arbitrary")),
        name="moe",
    )(h3, cmb, x2, wg, wu, wd, gf)


def _layer(x, mem, g_mix, w_in, b_ml_i, b_ml_f, b_fx_f, b_gate_ml, b_gate_fx, g_ml_head, w_proj_ml, w_proj_fx,
           w_out, g_xq, g_xmem, w_xq, w_xkv, w_xo, g_moe, w_rg, b_rg, w_re, b_re, w_gate, w_up, w_down):
    nb, seq, d = x.shape
    t = nb * seq
    row = lambda v: v.reshape(1, -1).astype(F32)

    o = 0
    parts = {}
    for name, width in (("ml_q", 512), ("ml_k", 512), ("ml_v", 1024), ("ml_o", 1024), ("ml_i", 4), ("ml_f", 4),
                        ("fx_q", 1024), ("fx_k", 1024), ("fx_v", 1024), ("fx_f", 8), ("gt_ml", 1024),
                        ("gt_fx", 1024)):
        parts[name] = w_in[:, o:o + width]
        o += width
    w_main = jnp.concatenate(
        [parts["ml_q"], parts["ml_k"] * (ML_QK_DIM ** -0.5), parts["ml_v"], parts["ml_o"],
         parts["fx_q"] * (FX_HEAD_DIM ** -0.5), parts["fx_k"], parts["fx_v"], parts["gt_ml"], parts["gt_fx"]],
        axis=1).astype(BF16)
    w_gates_t = jnp.concatenate([parts["ml_i"], parts["ml_f"], parts["fx_f"]], axis=1).T
    gate_bias = jnp.concatenate([b_ml_i, b_ml_f, b_fx_f]).reshape(N_GATES, 1).astype(F32)

    x2d = x.reshape(t, d)
    z, gates_t = _inproj(x2d, row(g_mix), w_main, w_gates_t)
    gt3 = gates_t.reshape(N_GATES, nb, seq).transpose(1, 0, 2)
    rows, cols = _gateprep(gt3, gate_bias)
    rows4 = rows.reshape(nb, N_GATES, seq // TQ, TQ)
    z3 = z.reshape(nb, seq, Z_W)
    y_ml = _mlstm(z3, rows4, cols, row(g_ml_head))
    y_fx = _fox(z3, rows4, cols)
    x1 = _merge(x2d, y_ml.reshape(t, d), y_fx.reshape(t, d), z, row(b_gate_ml), row(b_gate_fx),
                w_proj_ml.astype(BF16), w_proj_fx.astype(BF16), w_out.astype(BF16))

    n_mem = mem.shape[1]
    kv = _memkv(mem.reshape(nb * n_mem, d), row(g_xmem), w_xkv.astype(BF16))
    w_router = jnp.concatenate([w_re, w_rg, jnp.zeros((d, LANES - N_EXPERTS - N_GROUPS), F32)], axis=1)
    b_router = jnp.concatenate([b_re, b_rg, jnp.zeros((LANES - N_EXPERTS - N_GROUPS,), F32)]).reshape(1, LANES)
    x2, h3, cmb = _xattn(x1, kv.reshape(nb, n_mem, 2 * d), row(g_xq), w_xq.astype(BF16), w_xo.astype(BF16),
                         row(g_moe), w_router, b_router, seq)
    return x2, h3, cmb, (w_gate.astype(BF16), w_up.astype(BF16), w_down.astype(BF16))


def kernel(x, mem, g_mix, w_in, b_ml_i, b_ml_f, b_fx_f, b_gate_ml, b_gate_fx, g_ml_head, w_proj_ml, w_proj_fx, w_out, g_xq, g_xmem, w_xq, w_xkv, w_xo, g_moe, w_rg, b_rg, w_re, b_re, w_gate, w_up, w_down, g_final):
    nb, seq, d = x.shape
    depth = g_mix.shape[0]
    assert depth == 1, "the final rmsnorm is fused into the (single) layer's MoE kernel"
    x2, h3, cmb, (wg, wu, wd) = _layer(
        x, mem, g_mix[0], w_in[0], b_ml_i[0], b_ml_f[0], b_fx_f[0], b_gate_ml[0], b_gate_fx[0], g_ml_head[0],
        w_proj_ml[0], w_proj_fx[0], w_out[0], g_xq[0], g_xmem[0], w_xq[0], w_xkv[0], w_xo[0], g_moe[0],
        w_rg[0], b_rg[0], w_re[0], b_re[0], w_gate[0], w_up[0], w_down[0])
    out = _moe(h3, cmb, x2, wg, wu, wd, g_final.reshape(1, d).astype(F32))
    return out.reshape(nb, seq, d)
```

```python
import functools

import jax
import jax.numpy as jnp
from jax import lax
from jax.experimental import pallas as pl
from jax.experimental.pallas import tpu as pltpu

F32 = jnp.float32
BF16 = jnp.bfloat16

D_MODEL = 1024
EPS = 1e-6
ML_HEADS = 4
ML_QK_DIM = 128
ML_V_DIM = 256
FX_HEADS = 8
FX_HEAD_DIM = 128
XA_HEADS = 4
XA_HEAD_DIM = 256
N_GROUPS = 4
EXPERTS_PER_GROUP = 8
N_EXPERTS = 32
D_EXPERT = 512

LANES = 128
N_GATES = 16
LOG2E = 1.4426950408889634
Z_W = 6144
Z_ML_O, Z_FX_Q, Z_FX_K, Z_GT_ML, Z_GT_FX = 1, 2, 3, 4, 5
VT_W = 2048

VMEM_LIMIT = 56 * 1024 * 1024

IN_CHUNK = 1024
TQ = 256
MXU_LOOKAHEAD = 4
TM = 512
TM_MOE = 1024


def _params(sem, flags=None):
    return pltpu.CompilerParams(dimension_semantics=sem, vmem_limit_bytes=VMEM_LIMIT, flags=flags)


def _rms(x, g):
    return x * lax.rsqrt(jnp.mean(x * x, axis=-1, keepdims=True) + EPS) * g


def _dot_nt(a, b, **kw):
    return lax.dot_general(a, b, (((1,), (1,)), ((), ())), preferred_element_type=F32, **kw)


def _resident(shape):
    zeros = (0,) * len(shape)
    return pl.BlockSpec(shape, lambda *_: zeros, pipeline_mode=pl.Buffered(1))


def _inproj_body(x_ref, g_ref, w_ref, wvt_ref, wgt_ref, z_ref, vt_ref, gt_ref):
    h = _rms(x_ref[...], g_ref[...])
    hb = h.astype(BF16)
    for c in range(Z_W // IN_CHUNK):
        sl = slice(c * IN_CHUNK, (c + 1) * IN_CHUNK)
        z_ref[:, sl] = jnp.dot(hb, w_ref[:, sl], preferred_element_type=F32).astype(BF16)
    for c in range(VT_W // IN_CHUNK):
        sl = slice(c * IN_CHUNK, (c + 1) * IN_CHUNK)
        vt_ref[sl, :] = _dot_nt(wvt_ref[sl, :], hb).astype(BF16)
    gt_ref[...] = _dot_nt(wgt_ref[...], h, precision=lax.Precision.HIGHEST)


def _inproj(x2d, g, w, wvt, wgt):
    t = x2d.shape[0]
    return pl.pallas_call(
        _inproj_body,
        grid=(t // TQ,),
        in_specs=[
            pl.BlockSpec((TQ, D_MODEL), lambda i: (i, 0)),
            _resident((1, D_MODEL)),
            _resident((D_MODEL, Z_W)),
            _resident((VT_W, D_MODEL)),
            _resident((N_GATES, D_MODEL)),
        ],
        out_specs=[
            pl.BlockSpec((TQ, Z_W), lambda i: (i, 0)),
            pl.BlockSpec((None, VT_W, TQ), lambda i: (i, 0, 0)),
            pl.BlockSpec((N_GATES, TQ), lambda i: (0, i)),
        ],
        out_shape=[jax.ShapeDtypeStruct((t, Z_W), BF16), jax.ShapeDtypeStruct((t // TQ, VT_W, TQ), BF16),
                   jax.ShapeDtypeStruct((N_GATES, t), F32)],
        compiler_params=_params(("parallel",)),
        name="inproj",
    )(x2d, g, w, wvt, wgt)


def _scan_lanes(x, op, identity):
    n = x.shape[-1]
    idx = lax.broadcasted_iota(jnp.int32, x.shape, 1)
    s = 1
    while s < n:
        shifted = pltpu.roll(x, s, axis=1)
        x = op(x, jnp.where(idx >= s, shifted, identity))
        s *= 2
    return x


def _log_sigmoid(x):
    return jnp.minimum(x, 0.0) - jnp.log1p(jnp.exp(-jnp.abs(x)))


def _gateprep_body(gt_ref, bias_ref, rows_ref, cols_ref, caug_ref):
    g = gt_ref[...] + bias_ref[...]
    s = g.shape[1]
    cs = _scan_lanes(_log_sigmoid(g), jnp.add, 0.0)
    b = cs[4:8]
    c2 = cs[8:16] * LOG2E
    a = g[0:4] - b
    m = _scan_lanes(jnp.concatenate([a, a], axis=0), jnp.maximum, -jnp.inf)[0:4]
    rows_ref[...] = jnp.concatenate([m * LOG2E, b + m, c2], axis=0)
    cols_ref[...] = jnp.concatenate([a * LOG2E, jnp.zeros((LANES - ML_HEADS, s), F32)], axis=0).T
    hi = c2.astype(BF16).astype(F32)
    r1 = c2 - hi
    mid = r1.astype(BF16).astype(F32)
    lo = r1 - mid
    aug = jnp.concatenate([-hi, -mid, -lo, jnp.zeros((LANES - 3 * FX_HEADS, s), F32)], axis=0)
    caug_ref[...] = aug.T.astype(BF16)


def _gateprep(gt3, bias):
    nb, _, s = gt3.shape
    return pl.pallas_call(
        _gateprep_body,
        grid=(nb,),
        in_specs=[
            pl.BlockSpec((None, N_GATES, s), lambda b: (b, 0, 0)),
            _resident((N_GATES, 1)),
        ],
        out_specs=[
            pl.BlockSpec((None, N_GATES, s), lambda b: (b, 0, 0)),
            pl.BlockSpec((None, s, LANES), lambda b: (b, 0, 0)),
            pl.BlockSpec((None, s, LANES), lambda b: (b, 0, 0)),
        ],
        out_shape=[jax.ShapeDtypeStruct((nb, N_GATES, s), F32), jax.ShapeDtypeStruct((nb, s, LANES), F32),
                   jax.ShapeDtypeStruct((nb, s, LANES), BF16)],
        compiler_params=_params(("parallel",)),
        name="gateprep",
    )(gt3, bias)


def _causal_mask_t():
    s = lax.broadcasted_iota(jnp.int32, (TQ, TQ), 0)
    t = lax.broadcasted_iota(jnp.int32, (TQ, TQ), 1)
    return s <= t


def _mlstm_body(q_ref, k_ref, vt_ref, o_ref, cols_ref, rows_ref, gh_ref, y_ref, num_scr, den_scr):
    i = pl.program_id(1)
    rows = rows_ref[...]
    num_scr[...] = jnp.zeros_like(num_scr)
    den_scr[...] = jnp.zeros_like(den_scr)

    def scores(j, h):
        ks = pl.ds(pl.multiple_of(j * TQ, TQ), TQ)
        qk = slice(h * ML_QK_DIM, (h + 1) * ML_QK_DIM)
        return _dot_nt(k_ref[ks, qk], q_ref[:, qk])

    def update(j, h, s, mask):
        ks = pl.ds(pl.multiple_of(j * TQ, TQ), TQ)
        vv = slice(h * ML_V_DIM, (h + 1) * ML_V_DIM)
        w = jnp.exp2(cols_ref[ks, h:h + 1] - rows[h:h + 1])
        if mask is not None:
            w = jnp.where(mask, w, 0.0)
        s = s * w
        den_scr[h] += jnp.sum(s, axis=0, keepdims=True)
        num_scr[h] += jnp.dot(vt_ref[j, vv, :], s.astype(BF16), preferred_element_type=F32)

    def key_tile(j, mask):
        s = {h: scores(j, h) for h in range(min(MXU_LOOKAHEAD, ML_HEADS))}
        for h in range(ML_HEADS):
            if h + MXU_LOOKAHEAD < ML_HEADS:
                s[h + MXU_LOOKAHEAD] = scores(j, h + MXU_LOOKAHEAD)
            update(j, h, s.pop(h), mask)

    def body(j, carry):
        key_tile(j, None)
        return carry

    lax.fori_loop(0, i, body, 0)
    key_tile(i, _causal_mask_t())
    for h in range(ML_HEADS):
        vv = slice(h * ML_V_DIM, (h + 1) * ML_V_DIM)
        floor = jnp.exp(-rows[ML_HEADS + h:ML_HEADS + h + 1])
        hh = num_scr[h] * (1.0 / jnp.maximum(jnp.abs(den_scr[h]), floor))
        yt = hh * lax.rsqrt(jnp.mean(hh * hh, axis=0, keepdims=True) + EPS)
        y = yt.T * gh_ref[:, vv]
        y_ref[:, vv] = (y * jax.nn.sigmoid(o_ref[:, vv].astype(F32))).astype(BF16)


def _mlstm(z3, vt4, cols3, rows3, g_head):
    nb, s, _ = z3.shape
    nq = s // TQ
    return pl.pallas_call(
        _mlstm_body,
        grid=(nb, nq),
        in_specs=[
            pl.BlockSpec((None, TQ, ML_HEADS * ML_QK_DIM), lambda b, i: (b, i, 0)),
            pl.BlockSpec((None, s, ML_HEADS * ML_QK_DIM), lambda b, i: (b, 0, 1)),
            pl.BlockSpec((None, nq, D_MODEL, TQ), lambda b, i: (b, 0, 0, 0)),
            pl.BlockSpec((None, TQ, D_MODEL), lambda b, i: (b, i, Z_ML_O)),
            pl.BlockSpec((None, s, LANES), lambda b, i: (b, 0, 0)),
            pl.BlockSpec((None, N_GATES, TQ), lambda b, i: (b, 0, i)),
            _resident((1, D_MODEL)),
        ],
        out_specs=pl.BlockSpec((None, TQ, D_MODEL), lambda b, i: (b, i, 0)),
        out_shape=jax.ShapeDtypeStruct((nb, s, D_MODEL), BF16),
        scratch_shapes=[pltpu.VMEM((ML_HEADS, ML_V_DIM, TQ), F32), pltpu.VMEM((ML_HEADS, 1, TQ), F32)],
        compiler_params=_params(("parallel", "arbitrary")),
        name="mlstm",
    )(z3, z3, vt4, z3, cols3, rows3, g_head)


def _fox_body(q_ref, k_ref, vt_ref, caug_ref, rows_ref, y_ref, qa_scr, acc_scr, m_scr, l_scr):
    i = pl.program_id(1)
    rows = rows_ref[...]
    lane = lax.broadcasted_iota(jnp.int32, (TQ, LANES), 1)
    for h in range(FX_HEADS):
        hd = slice(h * FX_HEAD_DIM, (h + 1) * FX_HEAD_DIM)
        ones = jnp.where((lane < 3 * FX_HEADS) & (lane % FX_HEADS == h), 1.0, 0.0).astype(BF16)
        qa_scr[h] = jnp.concatenate([q_ref[:, hd], ones], axis=1)
    m_scr[...] = jnp.full_like(m_scr, -jnp.inf)
    l_scr[...] = jnp.zeros_like(l_scr)
    acc_scr[...] = jnp.zeros_like(acc_scr)

    def scores(j, h):
        ks = pl.ds(pl.multiple_of(j * TQ, TQ), TQ)
        hd = slice(h * FX_HEAD_DIM, (h + 1) * FX_HEAD_DIM)
        k_aug = jnp.concatenate([k_ref[ks, hd], caug_ref[ks, :]], axis=1)
        return _dot_nt(k_aug, qa_scr[h])

    def update(j, h, u, mask):
        hd = slice(h * FX_HEAD_DIM, (h + 1) * FX_HEAD_DIM)
        if mask is not None:
            u = jnp.where(mask, u, -jnp.inf)
        c_row = rows[2 * ML_HEADS + h:2 * ML_HEADS + h + 1]
        m_prev = m_scr[h]
        m_new = jnp.maximum(m_prev, jnp.max(u, axis=0, keepdims=True) + c_row)
        p = jnp.exp2(u - (m_new - c_row))
        alpha = jnp.exp2(m_prev - m_new)
        l_scr[h] = alpha * l_scr[h] + jnp.sum(p, axis=0, keepdims=True)
        acc_scr[h] = alpha * acc_scr[h] + jnp.dot(vt_ref[j, hd, :], p.astype(BF16), preferred_element_type=F32)
        m_scr[h] = m_new

    def key_tile(j, mask):
        u = {h: scores(j, h) for h in range(MXU_LOOKAHEAD)}
        for h in range(FX_HEADS):
            if h + MXU_LOOKAHEAD < FX_HEADS:
                u[h + MXU_LOOKAHEAD] = scores(j, h + MXU_LOOKAHEAD)
            update(j, h, u.pop(h), mask)

    def body(j, carry):
        key_tile(j, None)
        return carry

    lax.fori_loop(0, i, body, 0)
    key_tile(i, _causal_mask_t())
    for h in range(FX_HEADS):
        hd = slice(h * FX_HEAD_DIM, (h + 1) * FX_HEAD_DIM)
        y_ref[:, hd] = (acc_scr[h] * (1.0 / l_scr[h])).T.astype(BF16)


def _fox(z3, vt4, caug3, rows3):
    nb, s, _ = z3.shape
    nq = s // TQ
    return pl.pallas_call(
        _fox_body,
        grid=(nb, nq),
        in_specs=[
            pl.BlockSpec((None, TQ, D_MODEL), lambda b, i: (b, i, Z_FX_Q)),
            pl.BlockSpec((None, s, D_MODEL), lambda b, i: (b, 0, Z_FX_K)),
            pl.BlockSpec((None, nq, D_MODEL, TQ), lambda b, i: (b, 0, 1, 0)),
            pl.BlockSpec((None, s, LANES), lambda b, i: (b, 0, 0)),
            pl.BlockSpec((None, N_GATES, TQ), lambda b, i: (b, 0, i)),
        ],
        out_specs=pl.BlockSpec((None, TQ, D_MODEL), lambda b, i: (b, i, 0)),
        out_shape=jax.ShapeDtypeStruct((nb, s, D_MODEL), BF16),
        scratch_shapes=[pltpu.VMEM((FX_HEADS, TQ, 2 * FX_HEAD_DIM), BF16),
                        pltpu.VMEM((FX_HEADS, FX_HEAD_DIM, TQ), F32),
                        pltpu.VMEM((FX_HEADS, 1, TQ), F32), pltpu.VMEM((FX_HEADS, 1, TQ), F32)],
        compiler_params=_params(("parallel", "arbitrary")),
        name="fox",
    )(z3, z3, vt4, caug3, rows3)


def _merge_body(x_ref, yml_ref, yfx_ref, gml_ref, gfx_ref, bml_ref, bfx_ref, wml_ref, wfx_ref, wout_ref, x1_ref):
    p_ml = jnp.dot(yml_ref[...], wml_ref[...], preferred_element_type=F32)
    p_fx = jnp.dot(yfx_ref[...], wfx_ref[...], preferred_element_type=F32)
    merged = (jax.nn.sigmoid(gml_ref[...].astype(F32) + bml_ref[...]) * p_ml
              + jax.nn.sigmoid(gfx_ref[...].astype(F32) + bfx_ref[...]) * p_fx)
    x1_ref[...] = x_ref[...] + jnp.dot(merged.astype(BF16), wout_ref[...], preferred_element_type=F32)


def _merge(x2d, yml, yfx, z2d, bml, bfx, wml, wfx, wout):
    t = x2d.shape[0]
    tile = lambda col: pl.BlockSpec((TM, D_MODEL), lambda i, col=col: (i, col))
    return pl.pallas_call(
        _merge_body,
        grid=(t // TM,),
        in_specs=[tile(0), tile(0), tile(0), tile(Z_GT_ML), tile(Z_GT_FX),
                  _resident((1, D_MODEL)), _resident((1, D_MODEL)),
                  _resident((D_MODEL, D_MODEL)), _resident((D_MODEL, D_MODEL)), _resident((D_MODEL, D_MODEL))],
        out_specs=tile(0),
        out_shape=jax.ShapeDtypeStruct((t, D_MODEL), F32),
        compiler_params=_params(("parallel",)),
        name="merge",
    )(x2d, yml, yfx, z2d, z2d, bml, bfx, wml, wfx, wout)


def _memkv_body(m_ref, g_ref, w_ref, kv_ref):
    hb = _rms(m_ref[...], g_ref[...]).astype(BF16)
    kv_ref[...] = jnp.dot(hb, w_ref[...], preferred_element_type=F32).astype(BF16)


def _memkv(mem2d, g, w):
    t = mem2d.shape[0]
    return pl.pallas_call(
        _memkv_body,
        grid=(t // TM,),
        in_specs=[pl.BlockSpec((TM, D_MODEL), lambda i: (i, 0)), _resident((1, D_MODEL)),
                  _resident((D_MODEL, 2 * D_MODEL))],
        out_specs=pl.BlockSpec((TM, 2 * D_MODEL), lambda i: (i, 0)),
        out_shape=jax.ShapeDtypeStruct((t, 2 * D_MODEL), BF16),
        compiler_params=_params(("parallel",)),
        name="memkv",
    )(mem2d, g, w)


def _route(lg):
    lane = lax.broadcasted_iota(jnp.int32, lg.shape, 1).astype(F32)
    big = jnp.float32(LANES)
    ninf = -jnp.inf
    gl = jnp.where(lane >= N_EXPERTS, jnp.where(lane < N_EXPERTS + N_GROUPS, lg, ninf), ninf)
    gmax = jnp.max(gl, axis=1, keepdims=True)
    gidx = jnp.min(jnp.where(gl == gmax, lane, big), axis=1, keepdims=True) - N_EXPERTS
    g_p = 1.0 / jnp.sum(jnp.exp(gl - gmax), axis=1, keepdims=True)
    lo = gidx * EXPERTS_PER_GROUP
    el = jnp.where(lane >= lo, jnp.where(lane < lo + EXPERTS_PER_GROUP, lg, ninf), ninf)
    v1 = jnp.max(el, axis=1, keepdims=True)
    i1 = jnp.min(jnp.where(el == v1, lane, big), axis=1, keepdims=True)
    el2 = jnp.where(lane == i1, ninf, el)
    v2 = jnp.max(el2, axis=1, keepdims=True)
    i2 = jnp.min(jnp.where(el2 == v2, lane, big), axis=1, keepdims=True)
    t = jnp.exp(v2 - v1)
    w1 = g_p / (1.0 + t)
    w2 = w1 * t
    return jnp.where(lane == i1, w1, 0.0) + jnp.where(lane == i2, w2, 0.0)


def _xattn_body(x1_ref, kv_ref, gq_ref, wq_ref, wo_ref, gm_ref, wr_ref, br_ref, x2_ref, h3_ref, cmb_ref):
    x1 = x1_ref[...]
    hb = _rms(x1, gq_ref[...]).astype(BF16)
    q = (jnp.dot(hb, wq_ref[...], preferred_element_type=F32) * (XA_HEAD_DIM ** -0.5)).astype(BF16)
    outs = []
    for h in range(XA_HEADS):
        hd = slice(h * XA_HEAD_DIM, (h + 1) * XA_HEAD_DIM)
        vd = slice(D_MODEL + h * XA_HEAD_DIM, D_MODEL + (h + 1) * XA_HEAD_DIM)
        s = _dot_nt(q[:, hd], kv_ref[:, hd])
        p = jnp.exp(s - jnp.max(s, axis=1, keepdims=True))
        p = p * (1.0 / jnp.sum(p, axis=1, keepdims=True))
        outs.append(jnp.dot(p.astype(BF16), kv_ref[:, vd], preferred_element_type=F32).astype(BF16))
    o = jnp.concatenate(outs, axis=1)
    x2 = x1 + jnp.dot(o, wo_ref[...], preferred_element_type=F32)
    x2_ref[...] = x2
    h3 = _rms(x2, gm_ref[...])
    h3_ref[...] = h3.astype(BF16)
    lg = jnp.dot(h3, wr_ref[...], precision=lax.Precision.HIGHEST, preferred_element_type=F32) + br_ref[...]
    cmb_ref[...] = _route(lg)


def _xattn(x1, kv3, gq, wq, wo, gm, wr, br, seq):
    t = x1.shape[0]
    per_b = seq // TM
    n_mem = kv3.shape[1]
    tile = pl.BlockSpec((TM, D_MODEL), lambda i: (i, 0))
    return pl.pallas_call(
        _xattn_body,
        grid=(t // TM,),
        in_specs=[tile,
                  pl.BlockSpec((None, n_mem, 2 * D_MODEL), lambda i: (i // per_b, 0, 0)),
                  _resident((1, D_MODEL)), _resident((D_MODEL, D_MODEL)), _resident((D_MODEL, D_MODEL)),
                  _resident((1, D_MODEL)), _resident((D_MODEL, LANES)), _resident((1, LANES))],
        out_specs=[tile, tile, pl.BlockSpec((TM, LANES), lambda i: (i, 0))],
        out_shape=[jax.ShapeDtypeStruct((t, D_MODEL), F32), jax.ShapeDtypeStruct((t, D_MODEL), BF16),
                   jax.ShapeDtypeStruct((t, LANES), F32)],
        compiler_params=_params(("parallel",)),
        name="xattn",
    )(x1, kv3, gq, wq, wo, gm, wr, br)


def _moe_body(h3_ref, cmb_ref, x2_ref, wg_ref, wu_ref, wd_ref, gf_ref, out_ref, acc_scr):
    e = pl.program_id(1)

    @pl.when(e == 0)
    def _():
        acc_scr[...] = jnp.zeros_like(acc_scr)

    hb = h3_ref[...]
    gate = jnp.dot(hb, wg_ref[...], preferred_element_type=F32)
    up = jnp.dot(hb, wu_ref[...], preferred_element_type=F32)
    he = (gate * jax.nn.sigmoid(gate) * up).astype(BF16)
    lane = lax.broadcasted_iota(jnp.int32, cmb_ref.shape, 1)
    w = jnp.sum(jnp.where(lane == e, cmb_ref[...], 0.0), axis=1, keepdims=True)
    acc_scr[...] += w * jnp.dot(he, wd_ref[...], preferred_element_type=F32)

    @pl.when(e == N_EXPERTS - 1)
    def _():
        out_ref[...] = _rms(x2_ref[...] + acc_scr[...], gf_ref[...])


def _moe(h3, cmb, x2, wg, wu, wd, gf):
    t = h3.shape[0]
    tile = lambda w: pl.BlockSpec((TM_MOE, w), lambda i, e: (i, 0))
    return pl.pallas_call(
        _moe_body,
        grid=(t // TM_MOE, N_EXPERTS),
        in_specs=[tile(D_MODEL), tile(LANES), tile(D_MODEL),
                  pl.BlockSpec((None, D_MODEL, D_EXPERT), lambda i, e: (e, 0, 0)),
                  pl.BlockSpec((None, D_MODEL, D_EXPERT), lambda i, e: (e, 0, 0)),
                  pl.BlockSpec((None, D_EXPERT, D_MODEL), lambda i, e: (e, 0, 0)),
                  pl.BlockSpec((1, D_MODEL), lambda i, e: (0, 0))],
        out_specs=tile(D_MODEL),
        out_shape=jax.ShapeDtypeStruct((t, D_MODEL), F32),
        scratch_shapes=[pltpu.VMEM((TM_MOE, D_MODEL), F32)],
        compiler_params=_params(("parallel", "arbitrary")),
        name="moe",
    )(h3, cmb, x2, wg, wu, wd, gf)


def _layer(x, mem, g_mix, w_in, b_ml_i, b_ml_f, b_fx_f, b_gate_ml, b_gate_fx, g_ml_head, w_proj_ml, w_proj_fx,
           w_out, g_xq, g_xmem, w_xq, w_xkv, w_xo, g_moe, w_rg, b_rg, w_re, b_re, w_gate, w_up, w_down):
    nb, seq, d = x.shape
    t = nb * seq
    row = lambda v: v.reshape(1, -1).astype(F32)

    o = 0
    parts = {}
    for name, width in (("ml_q", 512), ("ml_k", 512), ("ml_v", 1024), ("ml_o", 1024), ("ml_i", 4), ("ml_f", 4),
                        ("fx_q", 1024), ("fx_k", 1024), ("fx_v", 1024), ("fx_f", 8), ("gt_ml", 1024),
                        ("gt_fx", 1024)):
        parts[name] = w_in[:, o:o + width]
        o += width
    w_main = jnp.concatenate(
        [parts["ml_q"], parts["ml_k"] * (ML_QK_DIM ** -0.5), parts["ml_o"],
         parts["fx_q"] * (FX_HEAD_DIM ** -0.5 * LOG2E), parts["fx_k"], parts["gt_ml"], parts["gt_fx"]],
        axis=1).astype(BF16)
    w_vt = jnp.concatenate([parts["ml_v"], parts["fx_v"]], axis=1).T.astype(BF16)
    w_gates_t = jnp.concatenate([parts["ml_i"], parts["ml_f"], parts["fx_f"]], axis=1).T
    gate_bias = jnp.concatenate([b_ml_i, b_ml_f, b_fx_f]).reshape(N_GATES, 1).astype(F32)

    x2d = x.reshape(t, d)
    z, vt, gates_t = _inproj(x2d, row(g_mix), w_main, w_vt, w_gates_t)
    gt3 = gates_t.reshape(N_GATES, nb, seq).transpose(1, 0, 2)
    rows, cols, caug = _gateprep(gt3, gate_bias)
    z3 = z.reshape(nb, seq, Z_W)
    vt4 = vt.reshape(nb, seq // TQ, VT_W, TQ)
    y_ml = _mlstm(z3, vt4, cols, rows, row(g_ml_head))
    y_fx = _fox(z3, vt4, caug, rows)
    x1 = _merge(x2d, y_ml.reshape(t, d), y_fx.reshape(t, d), z, row(b_gate_ml), row(b_gate_fx),
                w_proj_ml.astype(BF16), w_proj_fx.astype(BF16), w_out.astype(BF16))

    n_mem = mem.shape[1]
    kv = _memkv(mem.reshape(nb * n_mem, d), row(g_xmem), w_xkv.astype(BF16))
    w_router = jnp.concatenate([w_re, w_rg, jnp.zeros((d, LANES - N_EXPERTS - N_GROUPS), F32)], axis=1)
    b_router = jnp.concatenate([b_re, b_rg, jnp.zeros((LANES - N_EXPERTS - N_GROUPS,), F32)]).reshape(1, LANES)
    x2, h3, cmb = _xattn(x1, kv.reshape(nb, n_mem, 2 * d), row(g_xq), w_xq.astype(BF16), w_xo.astype(BF16),
                         row(g_moe), w_router, b_router, seq)
    return x2, h3, cmb, (w_gate.astype(BF16), w_up.astype(BF16), w_down.astype(BF16))


def kernel(x, mem, g_mix, w_in, b_ml_i, b_ml_f, b_fx_f, b_gate_ml, b_gate_fx, g_ml_head, w_proj_ml, w_proj_fx, w_out, g_xq, g_xmem, w_xq, w_xkv, w_xo, g_moe, w_rg, b_rg, w_re, b_re, w_gate, w_up, w_down, g_final):
    nb, seq, d = x.shape
    depth = g_mix.shape[0]
    assert depth == 1, "the final rmsnorm is fused into the (single) layer's MoE kernel"
    x2, h3, cmb, (wg, wu, wd) = _layer(
        x, mem, g_mix[0], w_in[0], b_ml_i[0], b_ml_f[0], b_fx_f[0], b_gate_ml[0], b_gate_fx[0], g_ml_head[0],
        w_proj_ml[0], w_proj_fx[0], w_out[0], g_xq[0], g_xmem[0], w_xq[0], w_xkv[0], w_xo[0], g_moe[0],
        w_rg[0], b_rg[0], w_re[0], b_re[0], w_gate[0], w_up[0], w_down[0])
    out = _moe(h3, cmb, x2, wg, wu, wd, g_final.reshape(1, d).astype(F32))
    return out.reshape(nb, seq, d)
```

```python
import functools

import jax
import jax.numpy as jnp
from jax import lax
from jax.experimental import pallas as pl
from jax.experimental.pallas import tpu as pltpu

F32 = jnp.float32
BF16 = jnp.bfloat16

D_MODEL = 1024
EPS = 1e-6
ML_HEADS = 4
ML_QK_DIM = 128
ML_V_DIM = 256
FX_HEADS = 8
FX_HEAD_DIM = 128
XA_HEADS = 4
XA_HEAD_DIM = 256
N_GROUPS = 4
EXPERTS_PER_GROUP = 8
N_EXPERTS = 32
D_EXPERT = 512

LANES = 128
N_GATES = 16
LOG2E = 1.4426950408889634
Z_W = 6144
Z_ML_O, Z_FX_Q, Z_FX_K, Z_GT_ML, Z_GT_FX = 1, 2, 3, 4, 5
VT_W = 2048

VMEM_LIMIT = 56 * 1024 * 1024

IN_CHUNK = 1024
TQ = 256
MXU_LOOKAHEAD = 4
TM = 512
TME = 256
TP = 512
TB = 1024
TF = 256


def _params(sem, flags=None):
    return pltpu.CompilerParams(dimension_semantics=sem, vmem_limit_bytes=VMEM_LIMIT, flags=flags)


def _rms(x, g):
    return x * lax.rsqrt(jnp.mean(x * x, axis=-1, keepdims=True) + EPS) * g


def _dot_nt(a, b, **kw):
    return lax.dot_general(a, b, (((1,), (1,)), ((), ())), preferred_element_type=F32, **kw)


def _resident(shape):
    zeros = (0,) * len(shape)
    return pl.BlockSpec(shape, lambda *_: zeros, pipeline_mode=pl.Buffered(1))


def _inproj_body(x_ref, g_ref, w_ref, wvt_ref, wgt_ref, z_ref, vt_ref, gt_ref):
    h = _rms(x_ref[...], g_ref[...])
    hb = h.astype(BF16)
    for c in range(Z_W // IN_CHUNK):
        sl = slice(c * IN_CHUNK, (c + 1) * IN_CHUNK)
        z_ref[:, sl] = jnp.dot(hb, w_ref[:, sl], preferred_element_type=F32).astype(BF16)
    for c in range(VT_W // IN_CHUNK):
        sl = slice(c * IN_CHUNK, (c + 1) * IN_CHUNK)
        vt_ref[sl, :] = _dot_nt(wvt_ref[sl, :], hb).astype(BF16)
    gt_ref[...] = _dot_nt(wgt_ref[...], h, precision=lax.Precision.HIGHEST)


def _inproj(x2d, g, w, wvt, wgt):
    t = x2d.shape[0]
    return pl.pallas_call(
        _inproj_body,
        grid=(t // TQ,),
        in_specs=[
            pl.BlockSpec((TQ, D_MODEL), lambda i: (i, 0)),
            _resident((1, D_MODEL)),
            _resident((D_MODEL, Z_W)),
            _resident((VT_W, D_MODEL)),
            _resident((N_GATES, D_MODEL)),
        ],
        out_specs=[
            pl.BlockSpec((TQ, Z_W), lambda i: (i, 0)),
            pl.BlockSpec((None, VT_W, TQ), lambda i: (i, 0, 0)),
            pl.BlockSpec((N_GATES, TQ), lambda i: (0, i)),
        ],
        out_shape=[jax.ShapeDtypeStruct((t, Z_W), BF16), jax.ShapeDtypeStruct((t // TQ, VT_W, TQ), BF16),
                   jax.ShapeDtypeStruct((N_GATES, t), F32)],
        compiler_params=_params(("parallel",)),
        name="inproj",
    )(x2d, g, w, wvt, wgt)


def _scan_lanes(x, op, identity):
    n = x.shape[-1]
    idx = lax.broadcasted_iota(jnp.int32, x.shape, 1)
    s = 1
    while s < n:
        shifted = pltpu.roll(x, s, axis=1)
        x = op(x, jnp.where(idx >= s, shifted, identity))
        s *= 2
    return x


def _log_sigmoid(x):
    return jnp.minimum(x, 0.0) - jnp.log1p(jnp.exp(-jnp.abs(x)))


def _gateprep_body(gt_ref, bias_ref, rows_ref, cols_ref, caug_ref):
    g = gt_ref[...] + bias_ref[...]
    s = g.shape[1]
    cs = _scan_lanes(_log_sigmoid(g), jnp.add, 0.0)
    b = cs[4:8]
    c2 = cs[8:16] * LOG2E
    a = g[0:4] - b
    m = _scan_lanes(jnp.concatenate([a, a], axis=0), jnp.maximum, -jnp.inf)[0:4]
    rows_ref[...] = jnp.concatenate([m * LOG2E, b + m, c2], axis=0)
    cols_ref[...] = jnp.concatenate([a * LOG2E, jnp.zeros((LANES - ML_HEADS, s), F32)], axis=0).T
    hi = c2.astype(BF16).astype(F32)
    r1 = c2 - hi
    mid = r1.astype(BF16).astype(F32)
    lo = r1 - mid
    aug = jnp.concatenate([-hi, -mid, -lo, jnp.zeros((LANES - 3 * FX_HEADS, s), F32)], axis=0)
    caug_ref[...] = aug.T.astype(BF16)


def _gateprep(gt3, bias):
    nb, _, s = gt3.shape
    return pl.pallas_call(
        _gateprep_body,
        grid=(nb,),
        in_specs=[
            pl.BlockSpec((None, N_GATES, s), lambda b: (b, 0, 0)),
            _resident((N_GATES, 1)),
        ],
        out_specs=[
            pl.BlockSpec((None, N_GATES, s), lambda b: (b, 0, 0)),
            pl.BlockSpec((None, s, LANES), lambda b: (b, 0, 0)),
            pl.BlockSpec((None, s, LANES), lambda b: (b, 0, 0)),
        ],
        out_shape=[jax.ShapeDtypeStruct((nb, N_GATES, s), F32), jax.ShapeDtypeStruct((nb, s, LANES), F32),
                   jax.ShapeDtypeStruct((nb, s, LANES), BF16)],
        compiler_params=_params(("parallel",)),
        name="gateprep",
    )(gt3, bias)


def _causal_mask_t():
    s = lax.broadcasted_iota(jnp.int32, (TQ, TQ), 0)
    t = lax.broadcasted_iota(jnp.int32, (TQ, TQ), 1)
    return s <= t


def _mlstm_body(q_ref, k_ref, vt_ref, o_ref, cols_ref, rows_ref, gh_ref, y_ref, num_scr, den_scr):
    i = pl.program_id(1)
    rows = rows_ref[...]
    num_scr[...] = jnp.zeros_like(num_scr)
    den_scr[...] = jnp.zeros_like(den_scr)

    def scores(j, h):
        ks = pl.ds(pl.multiple_of(j * TQ, TQ), TQ)
        qk = slice(h * ML_QK_DIM, (h + 1) * ML_QK_DIM)
        return _dot_nt(k_ref[ks, qk], q_ref[:, qk])

    def update(j, h, s, mask):
        ks = pl.ds(pl.multiple_of(j * TQ, TQ), TQ)
        vv = slice(h * ML_V_DIM, (h + 1) * ML_V_DIM)
        w = jnp.exp2(cols_ref[ks, h:h + 1] - rows[h:h + 1])
        if mask is not None:
            w = jnp.where(mask, w, 0.0)
        s = s * w
        den_scr[h] += jnp.sum(s, axis=0, keepdims=True)
        num_scr[h] += jnp.dot(vt_ref[j, vv, :], s.astype(BF16), preferred_element_type=F32)

    def key_tile(j, mask):
        s = {h: scores(j, h) for h in range(min(MXU_LOOKAHEAD, ML_HEADS))}
        for h in range(ML_HEADS):
            if h + MXU_LOOKAHEAD < ML_HEADS:
                s[h + MXU_LOOKAHEAD] = scores(j, h + MXU_LOOKAHEAD)
            update(j, h, s.pop(h), mask)

    def body(j, carry):
        key_tile(j, None)
        return carry

    lax.fori_loop(0, i, body, 0)
    key_tile(i, _causal_mask_t())
    for h in range(ML_HEADS):
        vv = slice(h * ML_V_DIM, (h + 1) * ML_V_DIM)
        floor = jnp.exp(-rows[ML_HEADS + h:ML_HEADS + h + 1])
        hh = num_scr[h] * (1.0 / jnp.maximum(jnp.abs(den_scr[h]), floor))
        yt = hh * lax.rsqrt(jnp.mean(hh * hh, axis=0, keepdims=True) + EPS)
        y = yt.T * gh_ref[:, vv]
        y_ref[:, vv] = (y * jax.nn.sigmoid(o_ref[:, vv].astype(F32))).astype(BF16)


def _mlstm(z3, vt4, cols3, rows3, g_head):
    nb, s, _ = z3.shape
    nq = s // TQ
    return pl.pallas_call(
        _mlstm_body,
        grid=(nb, nq),
        in_specs=[
            pl.BlockSpec((None, TQ, ML_HEADS * ML_QK_DIM), lambda b, i: (b, i, 0)),
            pl.BlockSpec((None, s, ML_HEADS * ML_QK_DIM), lambda b, i: (b, 0, 1)),
            pl.BlockSpec((None, nq, D_MODEL, TQ), lambda b, i: (b, 0, 0, 0)),
            pl.BlockSpec((None, TQ, D_MODEL), lambda b, i: (b, i, Z_ML_O)),
            pl.BlockSpec((None, s, LANES), lambda b, i: (b, 0, 0)),
            pl.BlockSpec((None, N_GATES, TQ), lambda b, i: (b, 0, i)),
            _resident((1, D_MODEL)),
        ],
        out_specs=pl.BlockSpec((None, TQ, D_MODEL), lambda b, i: (b, i, 0)),
        out_shape=jax.ShapeDtypeStruct((nb, s, D_MODEL), BF16),
        scratch_shapes=[pltpu.VMEM((ML_HEADS, ML_V_DIM, TQ), F32), pltpu.VMEM((ML_HEADS, 1, TQ), F32)],
        compiler_params=_params(("parallel", "arbitrary")),
        name="mlstm",
    )(z3, z3, vt4, z3, cols3, rows3, g_head)


def _fox_body(q_ref, k_ref, vt_ref, caug_ref, rows_ref, y_ref, qa_scr, acc_scr, m_scr, l_scr):
    i = pl.program_id(1)
    rows = rows_ref[...]
    lane = lax.broadcasted_iota(jnp.int32, (TQ, LANES), 1)
    for h in range(FX_HEADS):
        hd = slice(h * FX_HEAD_DIM, (h + 1) * FX_HEAD_DIM)
        ones = jnp.where((lane < 3 * FX_HEADS) & (lane % FX_HEADS == h), 1.0, 0.0).astype(BF16)
        qa_scr[h] = jnp.concatenate([q_ref[:, hd], ones], axis=1)
    m_scr[...] = jnp.full_like(m_scr, -jnp.inf)
    l_scr[...] = jnp.zeros_like(l_scr)
    acc_scr[...] = jnp.zeros_like(acc_scr)

    def scores(j, h):
        ks = pl.ds(pl.multiple_of(j * TQ, TQ), TQ)
        hd = slice(h * FX_HEAD_DIM, (h + 1) * FX_HEAD_DIM)
        k_aug = jnp.concatenate([k_ref[ks, hd], caug_ref[ks, :]], axis=1)
        return _dot_nt(k_aug, qa_scr[h])

    def update(j, h, u, mask):
        hd = slice(h * FX_HEAD_DIM, (h + 1) * FX_HEAD_DIM)
        if mask is not None:
            u = jnp.where(mask, u, -jnp.inf)
        c_row = rows[2 * ML_HEADS + h:2 * ML_HEADS + h + 1]
        m_prev = m_scr[h]
        m_new = jnp.maximum(m_prev, jnp.max(u, axis=0, keepdims=True) + c_row)
        p = jnp.exp2(u - (m_new - c_row))
        alpha = jnp.exp2(m_prev - m_new)
        l_scr[h] = alpha * l_scr[h] + jnp.sum(p, axis=0, keepdims=True)
        acc_scr[h] = alpha * acc_scr[h] + jnp.dot(vt_ref[j, hd, :], p.astype(BF16), preferred_element_type=F32)
        m_scr[h] = m_new

    def key_tile(j, mask):
        u = {h: scores(j, h) for h in range(MXU_LOOKAHEAD)}
        for h in range(FX_HEADS):
            if h + MXU_LOOKAHEAD < FX_HEADS:
                u[h + MXU_LOOKAHEAD] = scores(j, h + MXU_LOOKAHEAD)
            update(j, h, u.pop(h), mask)

    def body(j, carry):
        key_tile(j, None)
        return carry

    lax.fori_loop(0, i, body, 0)
    key_tile(i, _causal_mask_t())
    for h in range(FX_HEADS):
        hd = slice(h * FX_HEAD_DIM, (h + 1) * FX_HEAD_DIM)
        y_ref[:, hd] = (acc_scr[h] * (1.0 / l_scr[h])).T.astype(BF16)


def _fox(z3, vt4, caug3, rows3):
    nb, s, _ = z3.shape
    nq = s // TQ
    return pl.pallas_call(
        _fox_body,
        grid=(nb, nq),
        in_specs=[
            pl.BlockSpec((None, TQ, D_MODEL), lambda b, i: (b, i, Z_FX_Q)),
            pl.BlockSpec((None, s, D_MODEL), lambda b, i: (b, 0, Z_FX_K)),
            pl.BlockSpec((None, nq, D_MODEL, TQ), lambda b, i: (b, 0, 1, 0)),
            pl.BlockSpec((None, s, LANES), lambda b, i: (b, 0, 0)),
            pl.BlockSpec((None, N_GATES, TQ), lambda b, i: (b, 0, i)),
        ],
        out_specs=pl.BlockSpec((None, TQ, D_MODEL), lambda b, i: (b, i, 0)),
        out_shape=jax.ShapeDtypeStruct((nb, s, D_MODEL), BF16),
        scratch_shapes=[pltpu.VMEM((FX_HEADS, TQ, 2 * FX_HEAD_DIM), BF16),
                        pltpu.VMEM((FX_HEADS, FX_HEAD_DIM, TQ), F32),
                        pltpu.VMEM((FX_HEADS, 1, TQ), F32), pltpu.VMEM((FX_HEADS, 1, TQ), F32)],
        compiler_params=_params(("parallel", "arbitrary")),
        name="fox",
    )(z3, z3, vt4, caug3, rows3)


def _merge_body(x_ref, yml_ref, yfx_ref, gml_ref, gfx_ref, bml_ref, bfx_ref, wml_ref, wfx_ref, wout_ref, x1_ref):
    p_ml = jnp.dot(yml_ref[...], wml_ref[...], preferred_element_type=F32)
    p_fx = jnp.dot(yfx_ref[...], wfx_ref[...], preferred_element_type=F32)
    merged = (jax.nn.sigmoid(gml_ref[...].astype(F32) + bml_ref[...]) * p_ml
              + jax.nn.sigmoid(gfx_ref[...].astype(F32) + bfx_ref[...]) * p_fx)
    x1_ref[...] = x_ref[...] + jnp.dot(merged.astype(BF16), wout_ref[...], preferred_element_type=F32)


def _merge(x2d, yml, yfx, z2d, bml, bfx, wml, wfx, wout):
    t = x2d.shape[0]
    tile = lambda col: pl.BlockSpec((TM, D_MODEL), lambda i, col=col: (i, col))
    return pl.pallas_call(
        _merge_body,
        grid=(t // TM,),
        in_specs=[tile(0), tile(0), tile(0), tile(Z_GT_ML), tile(Z_GT_FX),
                  _resident((1, D_MODEL)), _resident((1, D_MODEL)),
                  _resident((D_MODEL, D_MODEL)), _resident((D_MODEL, D_MODEL)), _resident((D_MODEL, D_MODEL))],
        out_specs=tile(0),
        out_shape=jax.ShapeDtypeStruct((t, D_MODEL), F32),
        compiler_params=_params(("parallel",)),
        name="merge",
    )(x2d, yml, yfx, z2d, z2d, bml, bfx, wml, wfx, wout)


def _memkv_body(m_ref, g_ref, w_ref, kv_ref):
    hb = _rms(m_ref[...], g_ref[...]).astype(BF16)
    kv_ref[...] = jnp.dot(hb, w_ref[...], preferred_element_type=F32).astype(BF16)


def _memkv(mem2d, g, w):
    t = mem2d.shape[0]
    return pl.pallas_call(
        _memkv_body,
        grid=(t // TM,),
        in_specs=[pl.BlockSpec((TM, D_MODEL), lambda i: (i, 0)), _resident((1, D_MODEL)),
                  _resident((D_MODEL, 2 * D_MODEL))],
        out_specs=pl.BlockSpec((TM, 2 * D_MODEL), lambda i: (i, 0)),
        out_shape=jax.ShapeDtypeStruct((t, 2 * D_MODEL), BF16),
        compiler_params=_params(("parallel",)),
        name="memkv",
    )(mem2d, g, w)


def _route_t(lg_t):
    tm = lg_t.shape[1]
    ninf = -jnp.inf
    big = jnp.float32(LANES)
    gid = lax.broadcasted_iota(jnp.int32, (8, tm), 0).astype(F32)
    eid = lax.broadcasted_iota(jnp.int32, (N_EXPERTS, tm), 0).astype(F32)
    gl = jnp.where(gid < N_GROUPS, lg_t[N_EXPERTS:N_EXPERTS + 8], ninf)
    gmax = jnp.max(gl, axis=0, keepdims=True)
    gidx = jnp.min(jnp.where(gl == gmax, gid, big), axis=0, keepdims=True)
    g_p = 1.0 / jnp.sum(jnp.exp(gl - gmax), axis=0, keepdims=True)
    lo = gidx * EXPERTS_PER_GROUP
    el = jnp.where(eid >= lo, jnp.where(eid < lo + EXPERTS_PER_GROUP, lg_t[0:N_EXPERTS], ninf), ninf)
    v1 = jnp.max(el, axis=0, keepdims=True)
    i1 = jnp.min(jnp.where(el == v1, eid, big), axis=0, keepdims=True)
    el2 = jnp.where(eid == i1, ninf, el)
    v2 = jnp.max(el2, axis=0, keepdims=True)
    i2 = jnp.min(jnp.where(el2 == v2, eid, big), axis=0, keepdims=True)
    t = jnp.exp(v2 - v1)
    w1 = g_p / (1.0 + t)
    w2 = w1 * t
    return jnp.concatenate([i1, i2, w1, w2, jnp.zeros((4, tm), F32)], axis=0)


def _xattn_body(x1_ref, kv_ref, gq_ref, wq_ref, wo_ref, gm_ref, wrt_ref, brt_ref, x2_ref, h3_ref, route_ref,
                wcol_ref):
    x1 = x1_ref[...]
    hb = _rms(x1, gq_ref[...]).astype(BF16)
    q = (jnp.dot(hb, wq_ref[...], preferred_element_type=F32) * (XA_HEAD_DIM ** -0.5)).astype(BF16)
    outs = []
    for h in range(XA_HEADS):
        hd = slice(h * XA_HEAD_DIM, (h + 1) * XA_HEAD_DIM)
        vd = slice(D_MODEL + h * XA_HEAD_DIM, D_MODEL + (h + 1) * XA_HEAD_DIM)
        s = _dot_nt(q[:, hd], kv_ref[:, hd])
        p = jnp.exp(s - jnp.max(s, axis=1, keepdims=True))
        p = p * (1.0 / jnp.sum(p, axis=1, keepdims=True))
        outs.append(jnp.dot(p.astype(BF16), kv_ref[:, vd], preferred_element_type=F32).astype(BF16))
    o = jnp.concatenate(outs, axis=1)
    x2 = x1 + jnp.dot(o, wo_ref[...], preferred_element_type=F32)
    x2_ref[...] = x2
    h3 = _rms(x2, gm_ref[...])
    h3_ref[...] = h3
    lg_t = _dot_nt(wrt_ref[...], h3, precision=lax.Precision.HIGHEST) + brt_ref[...]
    route = _route_t(lg_t)
    route_ref[...] = route
    wcol_ref[...] = jnp.concatenate([route, jnp.zeros((LANES - 8, route.shape[1]), F32)], axis=0).T


def _xattn(x1, kv3, gq, wq, wo, gm, wrt, brt, seq):
    t = x1.shape[0]
    per_b = seq // TM
    n_mem = kv3.shape[1]
    tile = pl.BlockSpec((TM, D_MODEL), lambda i: (i, 0))
    return pl.pallas_call(
        _xattn_body,
        grid=(t // TM,),
        in_specs=[tile,
                  pl.BlockSpec((None, n_mem, 2 * D_MODEL), lambda i: (i // per_b, 0, 0)),
                  _resident((1, D_MODEL)), _resident((D_MODEL, D_MODEL)), _resident((D_MODEL, D_MODEL)),
                  _resident((1, D_MODEL)), _resident((LANES, D_MODEL)), _resident((LANES, 1))],
        out_specs=[tile, tile, pl.BlockSpec((8, TM), lambda i: (0, i)), pl.BlockSpec((TM, LANES), lambda i: (i, 0))],
        out_shape=[jax.ShapeDtypeStruct((t, D_MODEL), F32), jax.ShapeDtypeStruct((t, D_MODEL), F32),
                   jax.ShapeDtypeStruct((8, t), F32), jax.ShapeDtypeStruct((t, LANES), F32)],
        compiler_params=_params(("parallel",)),
        name="xattn",
    )(x1, kv3, gq, wq, wo, gm, wrt, brt)


def _n_expert_tiles(n_tokens):
    return 2 * n_tokens // TME + N_EXPERTS


def _plan_body(route_ref, pos_ref, tinfo_ref, cnt_scr, run_scr, start_scr, *, nt_pad):
    phase = pl.program_id(0)
    b = pl.program_id(1)
    r = route_ref[...]
    eid = lax.broadcasted_iota(jnp.int32, (N_EXPERTS, TP), 0).astype(F32)
    oh1 = eid == r[0:1]
    oh2 = eid == r[1:2]
    oh = jnp.where(oh1 | oh2, 1.0, 0.0)

    @pl.when((phase == 0) & (b == 0))
    def _():
        cnt_scr[...] = jnp.zeros_like(cnt_scr)

    @pl.when(phase == 0)
    def _():
        cnt_scr[...] += jnp.sum(oh, axis=1, keepdims=True)

    @pl.when((phase == 1) & (b == 0))
    def _():
        cnt = cnt_scr[...]
        n_tiles = jnp.floor((cnt + (TME - 1)) * (1.0 / TME))
        ri = lax.broadcasted_iota(jnp.int32, (N_EXPERTS, N_EXPERTS), 0)
        ci = lax.broadcasted_iota(jnp.int32, (N_EXPERTS, N_EXPERTS), 1)
        lower = jnp.where(ci < ri, 1.0, 0.0)
        start = jnp.dot(lower, jnp.broadcast_to(n_tiles, (N_EXPERTS, LANES)), precision=lax.Precision.HIGHEST,
                        preferred_element_type=F32)[:, 0:1]
        start_scr[...] = start * TME
        run_scr[...] = jnp.zeros_like(run_scr)
        n = lax.broadcasted_iota(jnp.int32, (N_EXPERTS, nt_pad), 1).astype(F32)
        e_n = lax.broadcasted_iota(jnp.int32, (N_EXPERTS, nt_pad), 0).astype(F32)
        owner = jnp.sum(jnp.where(start <= n, 1.0, 0.0), axis=0, keepdims=True) - 1.0
        own = e_n == owner
        cnt_o = jnp.sum(jnp.where(own, cnt, 0.0), axis=0, keepdims=True)
        start_o = jnp.sum(jnp.where(own, start, 0.0), axis=0, keepdims=True)
        valid = jnp.clip(cnt_o - (n[0:1] - start_o) * TME, 0.0, float(TME))
        tinfo_ref[...] = jnp.concatenate([owner, valid, jnp.zeros((6, nt_pad), F32)], axis=0).astype(jnp.int32)

    @pl.when(phase == 1)
    def _():
        ti = lax.broadcasted_iota(jnp.int32, (TP, TP), 0)
        tj = lax.broadcasted_iota(jnp.int32, (TP, TP), 1)
        upper = jnp.where(ti < tj, 1.0, 0.0).astype(BF16)
        before = jnp.dot(oh.astype(BF16), upper, preferred_element_type=F32)
        row = start_scr[...] + run_scr[...] + before
        p1 = jnp.sum(jnp.where(oh1, row, 0.0), axis=0, keepdims=True)
        p2 = jnp.sum(jnp.where(oh2, row, 0.0), axis=0, keepdims=True)
        pos_ref[...] = jnp.concatenate([p1, p2, jnp.zeros((6, TP), F32)], axis=0).astype(jnp.int32)
        run_scr[...] += jnp.sum(oh, axis=1, keepdims=True)


def _plan(route):
    t = route.shape[1]
    nt_pad = -(-_n_expert_tiles(t) // LANES) * LANES
    col = pltpu.VMEM((N_EXPERTS, 1), F32)
    return pl.pallas_call(
        functools.partial(_plan_body, nt_pad=nt_pad),
        grid=(2, t // TP),
        in_specs=[pl.BlockSpec((8, TP), lambda ph, b: (0, b))],
        out_specs=[pl.BlockSpec((8, TP), lambda ph, b: (0, b * ph)),
                   pl.BlockSpec((8, nt_pad), lambda ph, b: (0, 0))],
        out_shape=[jax.ShapeDtypeStruct((8, t), jnp.int32), jax.ShapeDtypeStruct((8, nt_pad), jnp.int32)],
        scratch_shapes=[col, col, col],
        compiler_params=_params(("arbitrary", "arbitrary")),
        name="plan",
    )(route)


def _dispatch_body(pos_ref, tvalid_ref, h3_hbm, xs_hbm, sem, *, nt):
    step = pl.program_id(0)
    base = step * TB

    def row_copy(t, k):
        return pltpu.make_async_copy(h3_hbm.at[pl.ds(base + t, 1)], xs_hbm.at[pl.ds(pos_ref[k, t], 1)], sem)

    def pad_copy(n, r):
        return pltpu.make_async_copy(h3_hbm.at[pl.ds(0, 1)], xs_hbm.at[pl.ds(n * TME + r, 1)], sem)

    def tile_copy(n):
        return pltpu.make_async_copy(h3_hbm.at[pl.ds(0, TME)], xs_hbm.at[pl.ds(pl.multiple_of(n * TME, TME), TME)],
                                     sem)

    def for_unwritten_rows(op):
        def tile(n, c):
            v = tvalid_ref[n]

            @pl.when(v == 0)
            def _():
                op(tile_copy(n))

            lax.fori_loop(jnp.where(v > 0, v, TME), TME, lambda r, c2: (op(pad_copy(n, r)), c2)[1], 0)
            return c
        lax.fori_loop(0, nt, tile, 0)

    @pl.when(step == 0)
    def _():
        for_unwritten_rows(lambda cp: cp.start())
        for_unwritten_rows(lambda cp: cp.wait())

    def issue(t, c):
        row_copy(t, 0).start()
        row_copy(t, 1).start()
        return c

    def drain(t, c):
        row_copy(t, 0).wait()
        row_copy(t, 1).wait()
        return c

    lax.fori_loop(0, TB, issue, 0, unroll=8)
    lax.fori_loop(0, TB, drain, 0, unroll=8)


def _dispatch(pos, tvalid, h3):
    t = h3.shape[0]
    nt = tvalid.shape[0]
    return pl.pallas_call(
        functools.partial(_dispatch_body, nt=nt),
        grid=(t // TB,),
        in_specs=[pl.BlockSpec((8, TB), lambda i: (0, i), memory_space=pltpu.SMEM),
                  pl.BlockSpec(memory_space=pltpu.SMEM),
                  pl.BlockSpec(memory_space=pl.ANY)],
        out_specs=pl.BlockSpec(memory_space=pl.ANY),
        out_shape=jax.ShapeDtypeStruct((nt * TME, D_MODEL), F32),
        scratch_shapes=[pltpu.SemaphoreType.DMA(())],
        compiler_params=_params(("arbitrary",)),
        name="dispatch",
    )(pos, tvalid, h3)


def _experts_body(texp_ref, tvalid_ref, xs_ref, wg_ref, wu_ref, wd_ref, ys_ref, wgb_scr, wub_scr, wdb_scr):
    n = pl.program_id(0)

    @pl.when((n == 0) | (texp_ref[n] != texp_ref[jnp.maximum(n - 1, 0)]))
    def _():
        wgb_scr[...] = wg_ref[...].astype(BF16)
        wub_scr[...] = wu_ref[...].astype(BF16)
        wdb_scr[...] = wd_ref[...].astype(BF16)

    @pl.when(tvalid_ref[n] > 0)
    def _():
        xb = xs_ref[...].astype(BF16)
        gate = jnp.dot(xb, wgb_scr[...], preferred_element_type=F32)
        up = jnp.dot(xb, wub_scr[...], preferred_element_type=F32)
        he = (gate * jax.nn.sigmoid(gate) * up).astype(BF16)
        ys_ref[...] = jnp.dot(he, wdb_scr[...], preferred_element_type=F32)

    @pl.when(tvalid_ref[n] == 0)
    def _():
        ys_ref[...] = jnp.zeros_like(ys_ref)


def _experts(texp, tvalid, xs, wg, wu, wd):
    nt = texp.shape[0]
    weight = lambda shape: pl.BlockSpec((None,) + shape, lambda n, te, tv: (te[n], 0, 0))
    return pl.pallas_call(
        _experts_body,
        grid_spec=pltpu.PrefetchScalarGridSpec(
            num_scalar_prefetch=2,
            grid=(nt,),
            in_specs=[pl.BlockSpec((TME, D_MODEL), lambda n, te, tv: (jnp.where(tv[n] > 0, n, 0), 0)),
                      weight((D_MODEL, D_EXPERT)), weight((D_MODEL, D_EXPERT)), weight((D_EXPERT, D_MODEL))],
            out_specs=pl.BlockSpec((TME, D_MODEL), lambda n, te, tv: (n, 0)),
            scratch_shapes=[pltpu.VMEM((D_MODEL, D_EXPERT), BF16), pltpu.VMEM((D_MODEL, D_EXPERT), BF16),
                            pltpu.VMEM((D_EXPERT, D_MODEL), BF16)],
        ),
        out_shape=jax.ShapeDtypeStruct((nt * TME, D_MODEL), F32),
        compiler_params=_params(("arbitrary",)),
        name="experts",
    )(texp, tvalid, xs, wg, wu, wd)


def _combine_body(pos_ref, ys_hbm, x2_ref, wcol_ref, gf_ref, out_ref, g_scr, sem):
    def row_copy(t, k):
        return pltpu.make_async_copy(ys_hbm.at[pl.ds(pos_ref[k, t], 1)], g_scr.at[k, pl.ds(t, 1)], sem)

    def issue(t, c):
        row_copy(t, 0).start()
        row_copy(t, 1).start()
        return c

    def drain(t, c):
        row_copy(t, 0).wait()
        row_copy(t, 1).wait()
        return c

    lax.fori_loop(0, TF, issue, 0, unroll=8)
    lax.fori_loop(0, TF, drain, 0, unroll=8)
    w = wcol_ref[...]
    y = w[:, 2:3] * g_scr[0] + w[:, 3:4] * g_scr[1]
    out_ref[...] = _rms(x2_ref[...] + y, gf_ref[...])


def _combine(pos, ys, x2, wcol, gf):
    t = x2.shape[0]
    return pl.pallas_call(
        _combine_body,
        grid=(t // TF,),
        in_specs=[pl.BlockSpec((8, TF), lambda i: (0, i), memory_space=pltpu.SMEM),
                  pl.BlockSpec(memory_space=pl.ANY),
                  pl.BlockSpec((TF, D_MODEL), lambda i: (i, 0)),
                  pl.BlockSpec((TF, LANES), lambda i: (i, 0)),
                  _resident((1, D_MODEL))],
        out_specs=pl.BlockSpec((TF, D_MODEL), lambda i: (i, 0)),
        out_shape=jax.ShapeDtypeStruct((t, D_MODEL), F32),
        scratch_shapes=[pltpu.VMEM((2, TF, D_MODEL), F32), pltpu.SemaphoreType.DMA(())],
        compiler_params=_params(("arbitrary",)),
        name="combine",
    )(pos, ys, x2, wcol, gf)


def _layer(x, mem, g_mix, w_in, b_ml_i, b_ml_f, b_fx_f, b_gate_ml, b_gate_fx, g_ml_head, w_proj_ml, w_proj_fx,
           w_out, g_xq, g_xmem, w_xq, w_xkv, w_xo, g_moe, w_rg, b_rg, w_re, b_re, w_gate, w_up, w_down):
    nb, seq, d = x.shape
    t = nb * seq
    row = lambda v: v.reshape(1, -1).astype(F32)

    o = 0
    parts = {}
    for name, width in (("ml_q", 512), ("ml_k", 512), ("ml_v", 1024), ("ml_o", 1024), ("ml_i", 4), ("ml_f", 4),
                        ("fx_q", 1024), ("fx_k", 1024), ("fx_v", 1024), ("fx_f", 8), ("gt_ml", 1024),
                        ("gt_fx", 1024)):
        parts[name] = w_in[:, o:o + width]
        o += width
    w_main = jnp.concatenate(
        [parts["ml_q"], parts["ml_k"] * (ML_QK_DIM ** -0.5), parts["ml_o"],
         parts["fx_q"] * (FX_HEAD_DIM ** -0.5 * LOG2E), parts["fx_k"], parts["gt_ml"], parts["gt_fx"]],
        axis=1).astype(BF16)
    w_vt = jnp.concatenate([parts["ml_v"], parts["fx_v"]], axis=1).T.astype(BF16)
    w_gates_t = jnp.concatenate([parts["ml_i"], parts["ml_f"], parts["fx_f"]], axis=1).T
    gate_bias = jnp.concatenate([b_ml_i, b_ml_f, b_fx_f]).reshape(N_GATES, 1).astype(F32)

    x2d = x.reshape(t, d)
    z, vt, gates_t = _inproj(x2d, row(g_mix), w_main, w_vt, w_gates_t)
    gt3 = gates_t.reshape(N_GATES, nb, seq).transpose(1, 0, 2)
    rows, cols, caug = _gateprep(gt3, gate_bias)
    z3 = z.reshape(nb, seq, Z_W)
    vt4 = vt.reshape(nb, seq // TQ, VT_W, TQ)
    y_ml = _mlstm(z3, vt4, cols, rows, row(g_ml_head))
    y_fx = _fox(z3, vt4, caug, rows)
    x1 = _merge(x2d, y_ml.reshape(t, d), y_fx.reshape(t, d), z, row(b_gate_ml), row(b_gate_fx),
                w_proj_ml.astype(BF16), w_proj_fx.astype(BF16), w_out.astype(BF16))

    n_mem = mem.shape[1]
    kv = _memkv(mem.reshape(nb * n_mem, d), row(g_xmem), w_xkv.astype(BF16))
    w_router_t = jnp.concatenate([w_re, w_rg, jnp.zeros((d, LANES - N_EXPERTS - N_GROUPS), F32)], axis=1).T
    b_router_t = jnp.concatenate([b_re, b_rg, jnp.zeros((LANES - N_EXPERTS - N_GROUPS,), F32)]).reshape(LANES, 1)
    x2, h3, route, wcol = _xattn(x1, kv.reshape(nb, n_mem, 2 * d), row(g_xq), w_xq.astype(BF16),
                                 w_xo.astype(BF16), row(g_moe), w_router_t, b_router_t, seq)
    pos, tinfo = _plan(route)
    nt = _n_expert_tiles(t)
    texp, tvalid = tinfo[0, :nt], tinfo[1, :nt]
    xs = _dispatch(pos, tvalid, h3)
    ys = _experts(texp, tvalid, xs, w_gate, w_up, w_down)
    return x2, ys, pos, wcol


def kernel(x, mem, g_mix, w_in, b_ml_i, b_ml_f, b_fx_f, b_gate_ml, b_gate_fx, g_ml_head, w_proj_ml, w_proj_fx, w_out, g_xq, g_xmem, w_xq, w_xkv, w_xo, g_moe, w_rg, b_rg, w_re, b_re, w_gate, w_up, w_down, g_final):
    nb, seq, d = x.shape
    depth = g_mix.shape[0]
    assert depth == 1, "the final rmsnorm is fused into the (single) layer's combine kernel"
    x2, ys, pos, wcol = _layer(
        x, mem, g_mix[0], w_in[0], b_ml_i[0], b_ml_f[0], b_fx_f[0], b_gate_ml[0], b_gate_fx[0], g_ml_head[0],
        w_proj_ml[0], w_proj_fx[0], w_out[0], g_xq[0], g_xmem[0], w_xq[0], w_xkv[0], w_xo[0], g_moe[0],
        w_rg[0], b_rg[0], w_re[0], b_re[0], w_gate[0], w_up[0], w_down[0])
    out = _combine(pos, ys, x2, wcol, g_final.reshape(1, d).astype(F32))
    return out.reshape(nb, seq, d)
```

```python
import functools

import jax
import jax.numpy as jnp
from jax import lax
from jax.experimental import pallas as pl
from jax.experimental.pallas import tpu as pltpu

F32 = jnp.float32
BF16 = jnp.bfloat16

D_MODEL = 1024
EPS = 1e-6
ML_HEADS = 4
ML_QK_DIM = 128
ML_V_DIM = 256
FX_HEADS = 8
FX_HEAD_DIM = 128
XA_HEADS = 4
XA_HEAD_DIM = 256
N_GROUPS = 4
EXPERTS_PER_GROUP = 8
N_EXPERTS = 32
D_EXPERT = 512

LANES = 128
N_GATES = 16
LOG2E = 1.4426950408889634
Z_W = 6144
Z_ML_O, Z_FX_Q, Z_FX_K, Z_GT_ML, Z_GT_FX = 1, 2, 3, 4, 5
VT_W = 2048

VMEM_LIMIT = 56 * 1024 * 1024

IN_CHUNK = 1024
TQ = 256
MXU_LOOKAHEAD = 4
TM = 512
TME = 256
TP = 512
TF = 256


def _params(sem, flags=None):
    return pltpu.CompilerParams(dimension_semantics=sem, vmem_limit_bytes=VMEM_LIMIT, flags=flags)


def _rms(x, g):
    return x * lax.rsqrt(jnp.mean(x * x, axis=-1, keepdims=True) + EPS) * g


def _dot_nt(a, b, **kw):
    return lax.dot_general(a, b, (((1,), (1,)), ((), ())), preferred_element_type=F32, **kw)


def _resident(shape):
    zeros = (0,) * len(shape)
    return pl.BlockSpec(shape, lambda *_: zeros, pipeline_mode=pl.Buffered(1))


def _inproj_body(x_ref, g_ref, w_ref, wvt_ref, wgt_ref, z_ref, vt_ref, gt_ref):
    h = _rms(x_ref[...], g_ref[...])
    hb = h.astype(BF16)
    for c in range(Z_W // IN_CHUNK):
        sl = slice(c * IN_CHUNK, (c + 1) * IN_CHUNK)
        z_ref[:, sl] = jnp.dot(hb, w_ref[:, sl], preferred_element_type=F32).astype(BF16)
    for c in range(VT_W // IN_CHUNK):
        sl = slice(c * IN_CHUNK, (c + 1) * IN_CHUNK)
        vt_ref[sl, :] = _dot_nt(wvt_ref[sl, :], hb).astype(BF16)
    gt_ref[...] = _dot_nt(wgt_ref[...], h, precision=lax.Precision.HIGHEST)


def _inproj(x2d, g, w, wvt, wgt):
    t = x2d.shape[0]
    return pl.pallas_call(
        _inproj_body,
        grid=(t // TQ,),
        in_specs=[
            pl.BlockSpec((TQ, D_MODEL), lambda i: (i, 0)),
            _resident((1, D_MODEL)),
            _resident((D_MODEL, Z_W)),
            _resident((VT_W, D_MODEL)),
            _resident((N_GATES, D_MODEL)),
        ],
        out_specs=[
            pl.BlockSpec((TQ, Z_W), lambda i: (i, 0)),
            pl.BlockSpec((None, VT_W, TQ), lambda i: (i, 0, 0)),
            pl.BlockSpec((N_GATES, TQ), lambda i: (0, i)),
        ],
        out_shape=[jax.ShapeDtypeStruct((t, Z_W), BF16), jax.ShapeDtypeStruct((t // TQ, VT_W, TQ), BF16),
                   jax.ShapeDtypeStruct((N_GATES, t), F32)],
        compiler_params=_params(("parallel",)),
        name="inproj",
    )(x2d, g, w, wvt, wgt)


def _scan_lanes(x, op, identity):
    n = x.shape[-1]
    idx = lax.broadcasted_iota(jnp.int32, x.shape, 1)
    s = 1
    while s < n:
        shifted = pltpu.roll(x, s, axis=1)
        x = op(x, jnp.where(idx >= s, shifted, identity))
        s *= 2
    return x


def _log_sigmoid(x):
    return jnp.minimum(x, 0.0) - jnp.log1p(jnp.exp(-jnp.abs(x)))


def _gateprep_body(gt_ref, bias_ref, rows_ref, cols_ref, caug_ref):
    g = gt_ref[...] + bias_ref[...]
    s = g.shape[1]
    cs = _scan_lanes(_log_sigmoid(g), jnp.add, 0.0)
    b = cs[4:8]
    c2 = cs[8:16] * LOG2E
    a = g[0:4] - b
    m = _scan_lanes(jnp.concatenate([a, a], axis=0), jnp.maximum, -jnp.inf)[0:4]
    rows_ref[...] = jnp.concatenate([m * LOG2E, b + m, c2], axis=0)
    cols_ref[...] = jnp.concatenate([a * LOG2E, jnp.zeros((LANES - ML_HEADS, s), F32)], axis=0).T
    hi = c2.astype(BF16).astype(F32)
    r1 = c2 - hi
    mid = r1.astype(BF16).astype(F32)
    lo = r1 - mid
    aug = jnp.concatenate([-hi, -mid, -lo, jnp.zeros((LANES - 3 * FX_HEADS, s), F32)], axis=0)
    caug_ref[...] = aug.T.astype(BF16)


def _gateprep(gt3, bias):
    nb, _, s = gt3.shape
    return pl.pallas_call(
        _gateprep_body,
        grid=(nb,),
        in_specs=[
            pl.BlockSpec((None, N_GATES, s), lambda b: (b, 0, 0)),
            _resident((N_GATES, 1)),
        ],
        out_specs=[
            pl.BlockSpec((None, N_GATES, s), lambda b: (b, 0, 0)),
            pl.BlockSpec((None, s, LANES), lambda b: (b, 0, 0)),
            pl.BlockSpec((None, s, LANES), lambda b: (b, 0, 0)),
        ],
        out_shape=[jax.ShapeDtypeStruct((nb, N_GATES, s), F32), jax.ShapeDtypeStruct((nb, s, LANES), F32),
                   jax.ShapeDtypeStruct((nb, s, LANES), BF16)],
        compiler_params=_params(("parallel",)),
        name="gateprep",
    )(gt3, bias)


def _causal_mask_t():
    s = lax.broadcasted_iota(jnp.int32, (TQ, TQ), 0)
    t = lax.broadcasted_iota(jnp.int32, (TQ, TQ), 1)
    return s <= t


def _mlstm_body(q_ref, k_ref, vt_ref, o_ref, cols_ref, rows_ref, gh_ref, y_ref, num_scr, den_scr):
    i = pl.program_id(1)
    rows = rows_ref[...]
    num_scr[...] = jnp.zeros_like(num_scr)
    den_scr[...] = jnp.zeros_like(den_scr)

    def scores(j, h):
        ks = pl.ds(pl.multiple_of(j * TQ, TQ), TQ)
        qk = slice(h * ML_QK_DIM, (h + 1) * ML_QK_DIM)
        return _dot_nt(k_ref[ks, qk], q_ref[:, qk])

    def update(j, h, s, mask):
        ks = pl.ds(pl.multiple_of(j * TQ, TQ), TQ)
        vv = slice(h * ML_V_DIM, (h + 1) * ML_V_DIM)
        w = jnp.exp2(cols_ref[ks, h:h + 1] - rows[h:h + 1])
        if mask is not None:
            w = jnp.where(mask, w, 0.0)
        s = s * w
        den_scr[h] += jnp.sum(s, axis=0, keepdims=True)
        num_scr[h] += jnp.dot(vt_ref[j, vv, :], s.astype(BF16), preferred_element_type=F32)

    def key_tile(j, mask):
        s = {h: scores(j, h) for h in range(min(MXU_LOOKAHEAD, ML_HEADS))}
        for h in range(ML_HEADS):
            if h + MXU_LOOKAHEAD < ML_HEADS:
                s[h + MXU_LOOKAHEAD] = scores(j, h + MXU_LOOKAHEAD)
            update(j, h, s.pop(h), mask)

    def body(j, carry):
        key_tile(j, None)
        return carry

    lax.fori_loop(0, i, body, 0)
    key_tile(i, _causal_mask_t())
    for h in range(ML_HEADS):
        vv = slice(h * ML_V_DIM, (h + 1) * ML_V_DIM)
        floor = jnp.exp(-rows[ML_HEADS + h:ML_HEADS + h + 1])
        hh = num_scr[h] * (1.0 / jnp.maximum(jnp.abs(den_scr[h]), floor))
        yt = hh * lax.rsqrt(jnp.mean(hh * hh, axis=0, keepdims=True) + EPS)
        y = yt.T * gh_ref[:, vv]
        y_ref[:, vv] = (y * jax.nn.sigmoid(o_ref[:, vv].astype(F32))).astype(BF16)


def _mlstm(z3, vt4, cols3, rows3, g_head):
    nb, s, _ = z3.shape
    nq = s // TQ
    return pl.pallas_call(
        _mlstm_body,
        grid=(nb, nq),
        in_specs=[
            pl.BlockSpec((None, TQ, ML_HEADS * ML_QK_DIM), lambda b, i: (b, i, 0)),
            pl.BlockSpec((None, s, ML_HEADS * ML_QK_DIM), lambda b, i: (b, 0, 1)),
            pl.BlockSpec((None, nq, D_MODEL, TQ), lambda b, i: (b, 0, 0, 0)),
            pl.BlockSpec((None, TQ, D_MODEL), lambda b, i: (b, i, Z_ML_O)),
            pl.BlockSpec((None, s, LANES), lambda b, i: (b, 0, 0)),
            pl.BlockSpec((None, N_GATES, TQ), lambda b, i: (b, 0, i)),
            _resident((1, D_MODEL)),
        ],
        out_specs=pl.BlockSpec((None, TQ, D_MODEL), lambda b, i: (b, i, 0)),
        out_shape=jax.ShapeDtypeStruct((nb, s, D_MODEL), BF16),
        scratch_shapes=[pltpu.VMEM((ML_HEADS, ML_V_DIM, TQ), F32), pltpu.VMEM((ML_HEADS, 1, TQ), F32)],
        compiler_params=_params(("parallel", "arbitrary")),
        name="mlstm",
    )(z3, z3, vt4, z3, cols3, rows3, g_head)


def _fox_body(q_ref, k_ref, vt_ref, caug_ref, rows_ref, y_ref, qa_scr, acc_scr, m_scr, l_scr):
    i = pl.program_id(1)
    rows = rows_ref[...]
    lane = lax.broadcasted_iota(jnp.int32, (TQ, LANES), 1)
    for h in range(FX_HEADS):
        hd = slice(h * FX_HEAD_DIM, (h + 1) * FX_HEAD_DIM)
        ones = jnp.where((lane < 3 * FX_HEADS) & (lane % FX_HEADS == h), 1.0, 0.0).astype(BF16)
        qa_scr[h] = jnp.concatenate([q_ref[:, hd], ones], axis=1)
    m_scr[...] = jnp.full_like(m_scr, -jnp.inf)
    l_scr[...] = jnp.zeros_like(l_scr)
    acc_scr[...] = jnp.zeros_like(acc_scr)

    def scores(j, h):
        ks = pl.ds(pl.multiple_of(j * TQ, TQ), TQ)
        hd = slice(h * FX_HEAD_DIM, (h + 1) * FX_HEAD_DIM)
        k_aug = jnp.concatenate([k_ref[ks, hd], caug_ref[ks, :]], axis=1)
        return _dot_nt(k_aug, qa_scr[h])

    def update(j, h, u, mask):
        hd = slice(h * FX_HEAD_DIM, (h + 1) * FX_HEAD_DIM)
        if mask is not None:
            u = jnp.where(mask, u, -jnp.inf)
        c_row = rows[2 * ML_HEADS + h:2 * ML_HEADS + h + 1]
        m_prev = m_scr[h]
        m_new = jnp.maximum(m_prev, jnp.max(u, axis=0, keepdims=True) + c_row)
        p = jnp.exp2(u - (m_new - c_row))
        alpha = jnp.exp2(m_prev - m_new)
        l_scr[h] = alpha * l_scr[h] + jnp.sum(p, axis=0, keepdims=True)
        acc_scr[h] = alpha * acc_scr[h] + jnp.dot(vt_ref[j, hd, :], p.astype(BF16), preferred_element_type=F32)
        m_scr[h] = m_new

    def key_tile(j, mask):
        u = {h: scores(j, h) for h in range(MXU_LOOKAHEAD)}
        for h in range(FX_HEADS):
            if h + MXU_LOOKAHEAD < FX_HEADS:
                u[h + MXU_LOOKAHEAD] = scores(j, h + MXU_LOOKAHEAD)
            update(j, h, u.pop(h), mask)

    def body(j, carry):
        key_tile(j, None)
        return carry

    lax.fori_loop(0, i, body, 0)
    key_tile(i, _causal_mask_t())
    for h in range(FX_HEADS):
        hd = slice(h * FX_HEAD_DIM, (h + 1) * FX_HEAD_DIM)
        y_ref[:, hd] = (acc_scr[h] * (1.0 / l_scr[h])).T.astype(BF16)


def _fox(z3, vt4, caug3, rows3):
    nb, s, _ = z3.shape
    nq = s // TQ
    return pl.pallas_call(
        _fox_body,
        grid=(nb, nq),
        in_specs=[
            pl.BlockSpec((None, TQ, D_MODEL), lambda b, i: (b, i, Z_FX_Q)),
            pl.BlockSpec((None, s, D_MODEL), lambda b, i: (b, 0, Z_FX_K)),
            pl.BlockSpec((None, nq, D_MODEL, TQ), lambda b, i: (b, 0, 1, 0)),
            pl.BlockSpec((None, s, LANES), lambda b, i: (b, 0, 0)),
            pl.BlockSpec((None, N_GATES, TQ), lambda b, i: (b, 0, i)),
        ],
        out_specs=pl.BlockSpec((None, TQ, D_MODEL), lambda b, i: (b, i, 0)),
        out_shape=jax.ShapeDtypeStruct((nb, s, D_MODEL), BF16),
        scratch_shapes=[pltpu.VMEM((FX_HEADS, TQ, 2 * FX_HEAD_DIM), BF16),
                        pltpu.VMEM((FX_HEADS, FX_HEAD_DIM, TQ), F32),
                        pltpu.VMEM((FX_HEADS, 1, TQ), F32), pltpu.VMEM((FX_HEADS, 1, TQ), F32)],
        compiler_params=_params(("parallel", "arbitrary")),
        name="fox",
    )(z3, z3, vt4, caug3, rows3)


def _merge_body(x_ref, yml_ref, yfx_ref, gml_ref, gfx_ref, bml_ref, bfx_ref, wml_ref, wfx_ref, wout_ref, x1_ref):
    p_ml = jnp.dot(yml_ref[...], wml_ref[...], preferred_element_type=F32)
    p_fx = jnp.dot(yfx_ref[...], wfx_ref[...], preferred_element_type=F32)
    merged = (jax.nn.sigmoid(gml_ref[...].astype(F32) + bml_ref[...]) * p_ml
              + jax.nn.sigmoid(gfx_ref[...].astype(F32) + bfx_ref[...]) * p_fx)
    x1_ref[...] = x_ref[...] + jnp.dot(merged.astype(BF16), wout_ref[...], preferred_element_type=F32)


def _merge(x2d, yml, yfx, z2d, bml, bfx, wml, wfx, wout):
    t = x2d.shape[0]
    tile = lambda col: pl.BlockSpec((TM, D_MODEL), lambda i, col=col: (i, col))
    return pl.pallas_call(
        _merge_body,
        grid=(t // TM,),
        in_specs=[tile(0), tile(0), tile(0), tile(Z_GT_ML), tile(Z_GT_FX),
                  _resident((1, D_MODEL)), _resident((1, D_MODEL)),
                  _resident((D_MODEL, D_MODEL)), _resident((D_MODEL, D_MODEL)), _resident((D_MODEL, D_MODEL))],
        out_specs=tile(0),
        out_shape=jax.ShapeDtypeStruct((t, D_MODEL), F32),
        compiler_params=_params(("parallel",)),
        name="merge",
    )(x2d, yml, yfx, z2d, z2d, bml, bfx, wml, wfx, wout)


def _memkv_body(m_ref, g_ref, w_ref, kv_ref):
    hb = _rms(m_ref[...], g_ref[...]).astype(BF16)
    kv_ref[...] = jnp.dot(hb, w_ref[...], preferred_element_type=F32).astype(BF16)


def _memkv(mem2d, g, w):
    t = mem2d.shape[0]
    return pl.pallas_call(
        _memkv_body,
        grid=(t // TM,),
        in_specs=[pl.BlockSpec((TM, D_MODEL), lambda i: (i, 0)), _resident((1, D_MODEL)),
                  _resident((D_MODEL, 2 * D_MODEL))],
        out_specs=pl.BlockSpec((TM, 2 * D_MODEL), lambda i: (i, 0)),
        out_shape=jax.ShapeDtypeStruct((t, 2 * D_MODEL), BF16),
        compiler_params=_params(("parallel",)),
        name="memkv",
    )(mem2d, g, w)


def _route_t(lg_t):
    tm = lg_t.shape[1]
    ninf = -jnp.inf
    big = jnp.float32(LANES)
    gid = lax.broadcasted_iota(jnp.int32, (8, tm), 0).astype(F32)
    eid = lax.broadcasted_iota(jnp.int32, (N_EXPERTS, tm), 0).astype(F32)
    gl = jnp.where(gid < N_GROUPS, lg_t[N_EXPERTS:N_EXPERTS + 8], ninf)
    gmax = jnp.max(gl, axis=0, keepdims=True)
    gidx = jnp.min(jnp.where(gl == gmax, gid, big), axis=0, keepdims=True)
    g_p = 1.0 / jnp.sum(jnp.exp(gl - gmax), axis=0, keepdims=True)
    lo = gidx * EXPERTS_PER_GROUP
    el = jnp.where(eid >= lo, jnp.where(eid < lo + EXPERTS_PER_GROUP, lg_t[0:N_EXPERTS], ninf), ninf)
    v1 = jnp.max(el, axis=0, keepdims=True)
    i1 = jnp.min(jnp.where(el == v1, eid, big), axis=0, keepdims=True)
    el2 = jnp.where(eid == i1, ninf, el)
    v2 = jnp.max(el2, axis=0, keepdims=True)
    i2 = jnp.min(jnp.where(el2 == v2, eid, big), axis=0, keepdims=True)
    t = jnp.exp(v2 - v1)
    w1 = g_p / (1.0 + t)
    w2 = w1 * t
    return jnp.concatenate([i1, i2, w1, w2, jnp.zeros((4, tm), F32)], axis=0)


def _xattn_body(x1_ref, kv_ref, gq_ref, wq_ref, wo_ref, gm_ref, wrt_ref, brt_ref, x2_ref, h3_ref, route_ref,
                wcol_ref):
    x1 = x1_ref[...]
    hb = _rms(x1, gq_ref[...]).astype(BF16)
    q = (jnp.dot(hb, wq_ref[...], preferred_element_type=F32) * (XA_HEAD_DIM ** -0.5)).astype(BF16)
    outs = []
    for h in range(XA_HEADS):
        hd = slice(h * XA_HEAD_DIM, (h + 1) * XA_HEAD_DIM)
        vd = slice(D_MODEL + h * XA_HEAD_DIM, D_MODEL + (h + 1) * XA_HEAD_DIM)
        s = _dot_nt(q[:, hd], kv_ref[:, hd])
        p = jnp.exp(s - jnp.max(s, axis=1, keepdims=True))
        p = p * (1.0 / jnp.sum(p, axis=1, keepdims=True))
        outs.append(jnp.dot(p.astype(BF16), kv_ref[:, vd], preferred_element_type=F32).astype(BF16))
    o = jnp.concatenate(outs, axis=1)
    x2 = x1 + jnp.dot(o, wo_ref[...], preferred_element_type=F32)
    x2_ref[...] = x2
    h3 = _rms(x2, gm_ref[...])
    h3_ref[...] = h3
    lg_t = _dot_nt(wrt_ref[...], h3, precision=lax.Precision.HIGHEST) + brt_ref[...]
    route = _route_t(lg_t)
    route_ref[...] = route
    wcol_ref[...] = jnp.concatenate([route, jnp.zeros((LANES - 8, route.shape[1]), F32)], axis=0).T


def _xattn(x1, kv3, gq, wq, wo, gm, wrt, brt, seq):
    t = x1.shape[0]
    per_b = seq // TM
    n_mem = kv3.shape[1]
    tile = pl.BlockSpec((TM, D_MODEL), lambda i: (i, 0))
    return pl.pallas_call(
        _xattn_body,
        grid=(t // TM,),
        in_specs=[tile,
                  pl.BlockSpec((None, n_mem, 2 * D_MODEL), lambda i: (i // per_b, 0, 0)),
                  _resident((1, D_MODEL)), _resident((D_MODEL, D_MODEL)), _resident((D_MODEL, D_MODEL)),
                  _resident((1, D_MODEL)), _resident((LANES, D_MODEL)), _resident((LANES, 1))],
        out_specs=[tile, tile, pl.BlockSpec((8, TM), lambda i: (0, i)), pl.BlockSpec((TM, LANES), lambda i: (i, 0))],
        out_shape=[jax.ShapeDtypeStruct((t, D_MODEL), F32), jax.ShapeDtypeStruct((t, D_MODEL), F32),
                   jax.ShapeDtypeStruct((8, t), F32), jax.ShapeDtypeStruct((t, LANES), F32)],
        compiler_params=_params(("parallel",)),
        name="xattn",
    )(x1, kv3, gq, wq, wo, gm, wrt, brt)


def _n_expert_tiles(n_tokens):
    return 2 * n_tokens // TME + N_EXPERTS


def _plan_body(route_ref, pos_ref, tinfo_ref, tok_ref, cnt_scr, run_scr, start_scr, tokhi_scr, toklo_scr, *, nt_pad):
    phase = pl.program_id(0)
    b = pl.program_id(1)
    r = route_ref[...]
    eid = lax.broadcasted_iota(jnp.int32, (N_EXPERTS, TP), 0).astype(F32)
    oh1 = eid == r[0:1]
    oh2 = eid == r[1:2]
    oh = jnp.where(oh1 | oh2, 1.0, 0.0)

    @pl.when((phase == 0) & (b == 0))
    def _():
        cnt_scr[...] = jnp.zeros_like(cnt_scr)

    @pl.when(phase == 0)
    def _():
        cnt_scr[...] += jnp.sum(oh, axis=1, keepdims=True)

    @pl.when((phase == 1) & (b == 0))
    def _():
        cnt = cnt_scr[...]
        n_tiles = jnp.floor((cnt + (TME - 1)) * (1.0 / TME))
        ri = lax.broadcasted_iota(jnp.int32, (N_EXPERTS, N_EXPERTS), 0)
        ci = lax.broadcasted_iota(jnp.int32, (N_EXPERTS, N_EXPERTS), 1)
        lower = jnp.where(ci < ri, 1.0, 0.0)
        start = jnp.dot(lower, jnp.broadcast_to(n_tiles, (N_EXPERTS, LANES)), precision=lax.Precision.HIGHEST,
                        preferred_element_type=F32)[:, 0:1]
        start_scr[...] = start * TME
        run_scr[...] = jnp.zeros_like(run_scr)
        tokhi_scr[...] = jnp.zeros_like(tokhi_scr)
        toklo_scr[...] = jnp.zeros_like(toklo_scr)
        n = lax.broadcasted_iota(jnp.int32, (N_EXPERTS, nt_pad), 1).astype(F32)
        e_n = lax.broadcasted_iota(jnp.int32, (N_EXPERTS, nt_pad), 0).astype(F32)
        owner = jnp.sum(jnp.where(start <= n, 1.0, 0.0), axis=0, keepdims=True) - 1.0
        own = e_n == owner
        cnt_o = jnp.sum(jnp.where(own, cnt, 0.0), axis=0, keepdims=True)
        start_o = jnp.sum(jnp.where(own, start, 0.0), axis=0, keepdims=True)
        valid = jnp.clip(cnt_o - (n[0:1] - start_o) * TME, 0.0, float(TME))
        tinfo_ref[...] = jnp.concatenate([owner, valid, jnp.zeros((6, nt_pad), F32)], axis=0).astype(jnp.int32)

    @pl.when(phase == 1)
    def _():
        ti = lax.broadcasted_iota(jnp.int32, (TP, TP), 0)
        tj = lax.broadcasted_iota(jnp.int32, (TP, TP), 1)
        upper = jnp.where(ti < tj, 1.0, 0.0).astype(BF16)
        before = jnp.dot(oh.astype(BF16), upper, preferred_element_type=F32)
        row = start_scr[...] + run_scr[...] + before
        p1 = jnp.sum(jnp.where(oh1, row, 0.0), axis=0, keepdims=True)
        p2 = jnp.sum(jnp.where(oh2, row, 0.0), axis=0, keepdims=True)
        pos_ref[...] = jnp.concatenate([p1, p2, jnp.zeros((6, TP), F32)], axis=0).astype(jnp.int32)
        run_scr[...] += jnp.sum(oh, axis=1, keepdims=True)

        tid = lax.broadcasted_iota(jnp.int32, (1, TP), 1) + b * TP
        t_hi = (tid // 256).astype(F32)
        t_lo = (tid % 256).astype(F32)
        tile_id = lax.broadcasted_iota(jnp.int32, (nt_pad, TP), 0).astype(F32)
        row_id = lax.broadcasted_iota(jnp.int32, (TME, TP), 0).astype(F32)
        for p in (p1, p2):
            hi = jnp.floor(p * (1.0 / TME))
            lo = p - hi * TME
            in_tile = jnp.where(tile_id == hi, 1.0, 0.0).astype(BF16)
            at_row = row_id == lo
            tokhi_scr[...] += _dot_nt(in_tile, jnp.where(at_row, t_hi, 0.0).astype(BF16))
            toklo_scr[...] += _dot_nt(in_tile, jnp.where(at_row, t_lo, 0.0).astype(BF16))

    @pl.when((phase == 1) & (b == pl.num_programs(1) - 1))
    def _():
        tok_ref[...] = (tokhi_scr[...] * 256.0 + toklo_scr[...]).astype(jnp.int32)


def _plan(route):
    t = route.shape[1]
    assert t <= 256 * 256, "token ids are carried as two base-256 digits"
    nt_pad = -(-_n_expert_tiles(t) // LANES) * LANES
    col = pltpu.VMEM((N_EXPERTS, 1), F32)
    table = pltpu.VMEM((nt_pad, TME), F32)
    return pl.pallas_call(
        functools.partial(_plan_body, nt_pad=nt_pad),
        grid=(2, t // TP),
        in_specs=[pl.BlockSpec((8, TP), lambda ph, b: (0, b))],
        out_specs=[pl.BlockSpec((8, TP), lambda ph, b: (0, b * ph)),
                   pl.BlockSpec((8, nt_pad), lambda ph, b: (0, 0)),
                   pl.BlockSpec((nt_pad, TME), lambda ph, b: (0, 0))],
        out_shape=[jax.ShapeDtypeStruct((8, t), jnp.int32), jax.ShapeDtypeStruct((8, nt_pad), jnp.int32),
                   jax.ShapeDtypeStruct((nt_pad, TME), jnp.int32)],
        scratch_shapes=[col, col, col, table, table],
        compiler_params=_params(("arbitrary", "arbitrary")),
        name="plan",
    )(route)


def _experts_body(texp_ref, tvalid_ref, tok_ref, tok_next_ref, h3_hbm, wg_ref, wu_ref, wd_ref, ys_ref,
                  x_scr, sem, wgb_scr, wub_scr, wdb_scr, *, nt):
    n = pl.program_id(0)
    slot = n % 2
    nxt = jnp.minimum(n + 1, nt - 1)

    def row_copy(tok, r, s):
        return pltpu.make_async_copy(h3_hbm.at[pl.ds(tok[0, r], 1)], x_scr.at[s, pl.ds(r, 1)], sem.at[s])

    def for_rows(tok, s, op):
        lax.fori_loop(0, TME, lambda r, c: (op(row_copy(tok, r, s)), c)[1], 0, unroll=8)

    @pl.when((n == 0) & (tvalid_ref[0] > 0))
    def _():
        for_rows(tok_ref, 0, lambda cp: cp.start())

    @pl.when((n + 1 < nt) & (tvalid_ref[nxt] > 0))
    def _():
        for_rows(tok_next_ref, 1 - slot, lambda cp: cp.start())

    @pl.when((n == 0) | (texp_ref[n] != texp_ref[jnp.maximum(n - 1, 0)]))
    def _():
        wgb_scr[...] = wg_ref[...].astype(BF16)
        wub_scr[...] = wu_ref[...].astype(BF16)
        wdb_scr[...] = wd_ref[...].astype(BF16)

    @pl.when(tvalid_ref[n] > 0)
    def _():
        for_rows(tok_ref, slot, lambda cp: cp.wait())
        xb = x_scr[slot].astype(BF16)
        gate = jnp.dot(xb, wgb_scr[...], preferred_element_type=F32)
        up = jnp.dot(xb, wub_scr[...], preferred_element_type=F32)
        he = (gate * jax.nn.sigmoid(gate) * up).astype(BF16)
        ys_ref[...] = jnp.dot(he, wdb_scr[...], preferred_element_type=F32)

    @pl.when(tvalid_ref[n] == 0)
    def _():
        ys_ref[...] = jnp.zeros_like(ys_ref)


def _experts(texp, tvalid, tok, h3, wg, wu, wd):
    nt = texp.shape[0]
    tok3 = tok.reshape(tok.shape[0], 1, TME)
    weight = lambda shape: pl.BlockSpec((None,) + shape, lambda n, te, tv: (te[n], 0, 0))
    tok_rows = lambda ahead: pl.BlockSpec((None, 1, TME), lambda n, te, tv: (jnp.minimum(n + ahead, nt - 1), 0, 0),
                                          memory_space=pltpu.SMEM)
    return pl.pallas_call(
        functools.partial(_experts_body, nt=nt),
        grid_spec=pltpu.PrefetchScalarGridSpec(
            num_scalar_prefetch=2,
            grid=(nt,),
            in_specs=[tok_rows(0), tok_rows(1), pl.BlockSpec(memory_space=pl.ANY),
                      weight((D_MODEL, D_EXPERT)), weight((D_MODEL, D_EXPERT)), weight((D_EXPERT, D_MODEL))],
            out_specs=pl.BlockSpec((TME, D_MODEL), lambda n, te, tv: (n, 0)),
            scratch_shapes=[pltpu.VMEM((2, TME, D_MODEL), F32), pltpu.SemaphoreType.DMA((2,)),
                            pltpu.VMEM((D_MODEL, D_EXPERT), BF16), pltpu.VMEM((D_MODEL, D_EXPERT), BF16),
                            pltpu.VMEM((D_EXPERT, D_MODEL), BF16)],
        ),
        out_shape=jax.ShapeDtypeStruct((nt * TME, D_MODEL), F32),
        compiler_params=_params(("arbitrary",)),
        name="experts",
    )(texp, tvalid, tok3, tok3, h3, wg, wu, wd)


def _combine_body(pos_ref, ys_hbm, x2_ref, wcol_ref, gf_ref, out_ref, g_scr, sem):
    def row_copy(t, k):
        return pltpu.make_async_copy(ys_hbm.at[pl.ds(pos_ref[k, t], 1)], g_scr.at[k, pl.ds(t, 1)], sem)

    def issue(t, c):
        row_copy(t, 0).start()
        row_copy(t, 1).start()
        return c

    def drain(t, c):
        row_copy(t, 0).wait()
        row_copy(t, 1).wait()
        return c

    lax.fori_loop(0, TF, issue, 0, unroll=8)
    lax.fori_loop(0, TF, drain, 0, unroll=8)
    w = wcol_ref[...]
    y = w[:, 2:3] * g_scr[0] + w[:, 3:4] * g_scr[1]
    out_ref[...] = _rms(x2_ref[...] + y, gf_ref[...])


def _combine(pos, ys, x2, wcol, gf):
    t = x2.shape[0]
    return pl.pallas_call(
        _combine_body,
        grid=(t // TF,),
        in_specs=[pl.BlockSpec((8, TF), lambda i: (0, i), memory_space=pltpu.SMEM),
                  pl.BlockSpec(memory_space=pl.ANY),
                  pl.BlockSpec((TF, D_MODEL), lambda i: (i, 0)),
                  pl.BlockSpec((TF, LANES), lambda i: (i, 0)),
                  _resident((1, D_MODEL))],
        out_specs=pl.BlockSpec((TF, D_MODEL), lambda i: (i, 0)),
        out_shape=jax.ShapeDtypeStruct((t, D_MODEL), F32),
        scratch_shapes=[pltpu.VMEM((2, TF, D_MODEL), F32), pltpu.SemaphoreType.DMA(())],
        compiler_params=_params(("arbitrary",)),
        name="combine",
    )(pos, ys, x2, wcol, gf)


def _layer(x, mem, g_mix, w_in, b_ml_i, b_ml_f, b_fx_f, b_gate_ml, b_gate_fx, g_ml_head, w_proj_ml, w_proj_fx,
           w_out, g_xq, g_xmem, w_xq, w_xkv, w_xo, g_moe, w_rg, b_rg, w_re, b_re, w_gate, w_up, w_down):
    nb, seq, d = x.shape
    t = nb * seq
    row = lambda v: v.reshape(1, -1).astype(F32)

    o = 0
    parts = {}
    for name, width in (("ml_q", 512), ("ml_k", 512), ("ml_v", 1024), ("ml_o", 1024), ("ml_i", 4), ("ml_f", 4),
                        ("fx_q", 1024), ("fx_k", 1024), ("fx_v", 1024), ("fx_f", 8), ("gt_ml", 1024),
                        ("gt_fx", 1024)):
        parts[name] = w_in[:, o:o + width]
        o += width
    w_main = jnp.concatenate(
        [parts["ml_q"], parts["ml_k"] * (ML_QK_DIM ** -0.5), parts["ml_o"],
         parts["fx_q"] * (FX_HEAD_DIM ** -0.5 * LOG2E), parts["fx_k"], parts["gt_ml"], parts["gt_fx"]],
        axis=1).astype(BF16)
    w_vt = jnp.concatenate([parts["ml_v"], parts["fx_v"]], axis=1).T.astype(BF16)
    w_gates_t = jnp.concatenate([parts["ml_i"], parts["ml_f"], parts["fx_f"]], axis=1).T
    gate_bias = jnp.concatenate([b_ml_i, b_ml_f, b_fx_f]).reshape(N_GATES, 1).astype(F32)

    x2d = x.reshape(t, d)
    z, vt, gates_t = _inproj(x2d, row(g_mix), w_main, w_vt, w_gates_t)
    gt3 = gates_t.reshape(N_GATES, nb, seq).transpose(1, 0, 2)
    rows, cols, caug = _gateprep(gt3, gate_bias)
    z3 = z.reshape(nb, seq, Z_W)
    vt4 = vt.reshape(nb, seq // TQ, VT_W, TQ)
    y_ml = _mlstm(z3, vt4, cols, rows, row(g_ml_head))
    y_fx = _fox(z3, vt4, caug, rows)
    x1 = _merge(x2d, y_ml.reshape(t, d), y_fx.reshape(t, d), z, row(b_gate_ml), row(b_gate_fx),
                w_proj_ml.astype(BF16), w_proj_fx.astype(BF16), w_out.astype(BF16))

    n_mem = mem.shape[1]
    kv = _memkv(mem.reshape(nb * n_mem, d), row(g_xmem), w_xkv.astype(BF16))
    w_router_t = jnp.concatenate([w_re, w_rg, jnp.zeros((d, LANES - N_EXPERTS - N_GROUPS), F32)], axis=1).T
    b_router_t = jnp.concatenate([b_re, b_rg, jnp.zeros((LANES - N_EXPERTS - N_GROUPS,), F32)]).reshape(LANES, 1)
    x2, h3, route, wcol = _xattn(x1, kv.reshape(nb, n_mem, 2 * d), row(g_xq), w_xq.astype(BF16),
                                 w_xo.astype(BF16), row(g_moe), w_router_t, b_router_t, seq)
    pos, tinfo, tok = _plan(route)
    nt = _n_expert_tiles(t)
    texp, tvalid = tinfo[0, :nt], tinfo[1, :nt]
    ys = _experts(texp, tvalid, tok, h3, w_gate, w_up, w_down)
    return x2, ys, pos, wcol


def kernel(x, mem, g_mix, w_in, b_ml_i, b_ml_f, b_fx_f, b_gate_ml, b_gate_fx, g_ml_head, w_proj_ml, w_proj_fx, w_out, g_xq, g_xmem, w_xq, w_xkv, w_xo, g_moe, w_rg, b_rg, w_re, b_re, w_gate, w_up, w_down, g_final):
    nb, seq, d = x.shape
    depth = g_mix.shape[0]
    assert depth == 1, "the final rmsnorm is fused into the (single) layer's combine kernel"
    x2, ys, pos, wcol = _layer(
        x, mem, g_mix[0], w_in[0], b_ml_i[0], b_ml_f[0], b_fx_f[0], b_gate_ml[0], b_gate_fx[0], g_ml_head[0],
        w_proj_ml[0], w_proj_fx[0], w_out[0], g_xq[0], g_xmem[0], w_xq[0], w_xkv[0], w_xo[0], g_moe[0],
        w_rg[0], b_rg[0], w_re[0], b_re[0], w_gate[0], w_up[0], w_down[0])
    out = _combine(pos, ys, x2, wcol, g_final.reshape(1, d).astype(F32))
    return out.reshape(nb, seq, d)
```

```python
import functools

import jax
import jax.numpy as jnp
from jax import lax
from jax.experimental import pallas as pl
from jax.experimental.pallas import tpu as pltpu
from jax.experimental.pallas import tpu_sc as plsc

F32 = jnp.float32
BF16 = jnp.bfloat16

D_MODEL = 1024
EPS = 1e-6
ML_HEADS = 4
ML_QK_DIM = 128
ML_V_DIM = 256
FX_HEADS = 8
FX_HEAD_DIM = 128
XA_HEADS = 4
XA_HEAD_DIM = 256
N_GROUPS = 4
EXPERTS_PER_GROUP = 8
N_EXPERTS = 32
D_EXPERT = 512

LANES = 128
N_GATES = 16
LOG2E = 1.4426950408889634
Z_W = 6144
Z_ML_O, Z_FX_Q, Z_FX_K, Z_GT_ML, Z_GT_FX = 1, 2, 3, 4, 5
VT_W = 2048

VMEM_LIMIT = 56 * 1024 * 1024

IN_CHUNK = 1024
TQ = 256
MXU_LOOKAHEAD = 4
TM = 512
TME = 256
TP = 512
TF = 512
SC_CORES, SC_SUBCORES = 2, 16
SC_ROWS = 64


def _params(sem, flags=None):
    return pltpu.CompilerParams(dimension_semantics=sem, vmem_limit_bytes=VMEM_LIMIT, flags=flags)


def _rms(x, g):
    return x * lax.rsqrt(jnp.mean(x * x, axis=-1, keepdims=True) + EPS) * g


def _dot_nt(a, b, **kw):
    return lax.dot_general(a, b, (((1,), (1,)), ((), ())), preferred_element_type=F32, **kw)


def _resident(shape):
    zeros = (0,) * len(shape)
    return pl.BlockSpec(shape, lambda *_: zeros, pipeline_mode=pl.Buffered(1))


def _inproj_body(x_ref, g_ref, w_ref, wvt_ref, wgt_ref, z_ref, vt_ref, gt_ref):
    h = _rms(x_ref[...], g_ref[...])
    hb = h.astype(BF16)
    for c in range(Z_W // IN_CHUNK):
        sl = slice(c * IN_CHUNK, (c + 1) * IN_CHUNK)
        z_ref[:, sl] = jnp.dot(hb, w_ref[:, sl], preferred_element_type=F32).astype(BF16)
    for c in range(VT_W // IN_CHUNK):
        sl = slice(c * IN_CHUNK, (c + 1) * IN_CHUNK)
        vt_ref[sl, :] = _dot_nt(wvt_ref[sl, :], hb).astype(BF16)
    gt_ref[...] = _dot_nt(wgt_ref[...], h, precision=lax.Precision.HIGHEST)


def _inproj(x2d, g, w, wvt, wgt):
    t = x2d.shape[0]
    return pl.pallas_call(
        _inproj_body,
        grid=(t // TQ,),
        in_specs=[
            pl.BlockSpec((TQ, D_MODEL), lambda i: (i, 0)),
            _resident((1, D_MODEL)),
            _resident((D_MODEL, Z_W)),
            _resident((VT_W, D_MODEL)),
            _resident((N_GATES, D_MODEL)),
        ],
        out_specs=[
            pl.BlockSpec((TQ, Z_W), lambda i: (i, 0)),
            pl.BlockSpec((None, VT_W, TQ), lambda i: (i, 0, 0)),
            pl.BlockSpec((N_GATES, TQ), lambda i: (0, i)),
        ],
        out_shape=[jax.ShapeDtypeStruct((t, Z_W), BF16), jax.ShapeDtypeStruct((t // TQ, VT_W, TQ), BF16),
                   jax.ShapeDtypeStruct((N_GATES, t), F32)],
        compiler_params=_params(("parallel",)),
        name="inproj",
    )(x2d, g, w, wvt, wgt)


def _scan_lanes(x, op, identity):
    n = x.shape[-1]
    idx = lax.broadcasted_iota(jnp.int32, x.shape, 1)
    s = 1
    while s < n:
        shifted = pltpu.roll(x, s, axis=1)
        x = op(x, jnp.where(idx >= s, shifted, identity))
        s *= 2
    return x


def _log_sigmoid(x):
    return jnp.minimum(x, 0.0) - jnp.log1p(jnp.exp(-jnp.abs(x)))


def _gateprep_body(gt_ref, bias_ref, rows_ref, cols_ref, caug_ref):
    g = gt_ref[...] + bias_ref[...]
    s = g.shape[1]
    cs = _scan_lanes(_log_sigmoid(g), jnp.add, 0.0)
    b = cs[4:8]
    c2 = cs[8:16] * LOG2E
    a = g[0:4] - b
    m = _scan_lanes(jnp.concatenate([a, a], axis=0), jnp.maximum, -jnp.inf)[0:4]
    rows_ref[...] = jnp.concatenate([m * LOG2E, b + m, c2], axis=0)
    cols_ref[...] = jnp.concatenate([a * LOG2E, jnp.zeros((LANES - ML_HEADS, s), F32)], axis=0).T
    hi = c2.astype(BF16).astype(F32)
    r1 = c2 - hi
    mid = r1.astype(BF16).astype(F32)
    lo = r1 - mid
    aug = jnp.concatenate([-hi, -mid, -lo, jnp.zeros((LANES - 3 * FX_HEADS, s), F32)], axis=0)
    caug_ref[...] = aug.T.astype(BF16)


def _gateprep(gt3, bias):
    nb, _, s = gt3.shape
    return pl.pallas_call(
        _gateprep_body,
        grid=(nb,),
        in_specs=[
            pl.BlockSpec((None, N_GATES, s), lambda b: (b, 0, 0)),
            _resident((N_GATES, 1)),
        ],
        out_specs=[
            pl.BlockSpec((None, N_GATES, s), lambda b: (b, 0, 0)),
            pl.BlockSpec((None, s, LANES), lambda b: (b, 0, 0)),
            pl.BlockSpec((None, s, LANES), lambda b: (b, 0, 0)),
        ],
        out_shape=[jax.ShapeDtypeStruct((nb, N_GATES, s), F32), jax.ShapeDtypeStruct((nb, s, LANES), F32),
                   jax.ShapeDtypeStruct((nb, s, LANES), BF16)],
        compiler_params=_params(("parallel",)),
        name="gateprep",
    )(gt3, bias)


def _causal_mask_t():
    s = lax.broadcasted_iota(jnp.int32, (TQ, TQ), 0)
    t = lax.broadcasted_iota(jnp.int32, (TQ, TQ), 1)
    return s <= t


def _mlstm_body(q_ref, k_ref, vt_ref, o_ref, cols_ref, rows_ref, gh_ref, y_ref, num_scr, den_scr):
    i = pl.program_id(1)
    rows = rows_ref[...]
    num_scr[...] = jnp.zeros_like(num_scr)
    den_scr[...] = jnp.zeros_like(den_scr)

    def scores(j, h):
        ks = pl.ds(pl.multiple_of(j * TQ, TQ), TQ)
        qk = slice(h * ML_QK_DIM, (h + 1) * ML_QK_DIM)
        return _dot_nt(k_ref[ks, qk], q_ref[:, qk])

    def update(j, h, s, mask):
        ks = pl.ds(pl.multiple_of(j * TQ, TQ), TQ)
        vv = slice(h * ML_V_DIM, (h + 1) * ML_V_DIM)
        w = jnp.exp2(cols_ref[ks, h:h + 1] - rows[h:h + 1])
        if mask is not None:
            w = jnp.where(mask, w, 0.0)
        s = s * w
        den_scr[h] += jnp.sum(s, axis=0, keepdims=True)
        num_scr[h] += jnp.dot(vt_ref[j, vv, :], s.astype(BF16), preferred_element_type=F32)

    def key_tile(j, mask):
        s = {h: scores(j, h) for h in range(min(MXU_LOOKAHEAD, ML_HEADS))}
        for h in range(ML_HEADS):
            if h + MXU_LOOKAHEAD < ML_HEADS:
                s[h + MXU_LOOKAHEAD] = scores(j, h + MXU_LOOKAHEAD)
            update(j, h, s.pop(h), mask)

    def body(j, carry):
        key_tile(j, None)
        return carry

    lax.fori_loop(0, i, body, 0)
    key_tile(i, _causal_mask_t())
    for h in range(ML_HEADS):
        vv = slice(h * ML_V_DIM, (h + 1) * ML_V_DIM)
        floor = jnp.exp(-rows[ML_HEADS + h:ML_HEADS + h + 1])
        hh = num_scr[h] * (1.0 / jnp.maximum(jnp.abs(den_scr[h]), floor))
        yt = hh * lax.rsqrt(jnp.mean(hh * hh, axis=0, keepdims=True) + EPS)
        y = yt.T * gh_ref[:, vv]
        y_ref[:, vv] = (y * jax.nn.sigmoid(o_ref[:, vv].astype(F32))).astype(BF16)


def _mlstm(z3, vt4, cols3, rows3, g_head):
    nb, s, _ = z3.shape
    nq = s // TQ
    return pl.pallas_call(
        _mlstm_body,
        grid=(nb, nq),
        in_specs=[
            pl.BlockSpec((None, TQ, ML_HEADS * ML_QK_DIM), lambda b, i: (b, i, 0)),
            pl.BlockSpec((None, s, ML_HEADS * ML_QK_DIM), lambda b, i: (b, 0, 1)),
            pl.BlockSpec((None, nq, D_MODEL, TQ), lambda b, i: (b, 0, 0, 0)),
            pl.BlockSpec((None, TQ, D_MODEL), lambda b, i: (b, i, Z_ML_O)),
            pl.BlockSpec((None, s, LANES), lambda b, i: (b, 0, 0)),
            pl.BlockSpec((None, N_GATES, TQ), lambda b, i: (b, 0, i)),
            _resident((1, D_MODEL)),
        ],
        out_specs=pl.BlockSpec((None, TQ, D_MODEL), lambda b, i: (b, i, 0)),
        out_shape=jax.ShapeDtypeStruct((nb, s, D_MODEL), BF16),
        scratch_shapes=[pltpu.VMEM((ML_HEADS, ML_V_DIM, TQ), F32), pltpu.VMEM((ML_HEADS, 1, TQ), F32)],
        compiler_params=_params(("parallel", "arbitrary")),
        name="mlstm",
    )(z3, z3, vt4, z3, cols3, rows3, g_head)


def _fox_body(q_ref, k_ref, vt_ref, caug_ref, rows_ref, y_ref, qa_scr, acc_scr, m_scr, l_scr):
    i = pl.program_id(1)
    rows = rows_ref[...]
    lane = lax.broadcasted_iota(jnp.int32, (TQ, LANES), 1)
    for h in range(FX_HEADS):
        hd = slice(h * FX_HEAD_DIM, (h + 1) * FX_HEAD_DIM)
        ones = jnp.where((lane < 3 * FX_HEADS) & (lane % FX_HEADS == h), 1.0, 0.0).astype(BF16)
        qa_scr[h] = jnp.concatenate([q_ref[:, hd], ones], axis=1)
    m_scr[...] = jnp.full_like(m_scr, -jnp.inf)
    l_scr[...] = jnp.zeros_like(l_scr)
    acc_scr[...] = jnp.zeros_like(acc_scr)

    def scores(j, h):
        ks = pl.ds(pl.multiple_of(j * TQ, TQ), TQ)
        hd = slice(h * FX_HEAD_DIM, (h + 1) * FX_HEAD_DIM)
        k_aug = jnp.concatenate([k_ref[ks, hd], caug_ref[ks, :]], axis=1)
        return _dot_nt(k_aug, qa_scr[h])

    def update(j, h, u, mask):
        hd = slice(h * FX_HEAD_DIM, (h + 1) * FX_HEAD_DIM)
        if mask is not None:
            u = jnp.where(mask, u, -jnp.inf)
        c_row = rows[2 * ML_HEADS + h:2 * ML_HEADS + h + 1]
        m_prev = m_scr[h]
        m_new = jnp.maximum(m_prev, jnp.max(u, axis=0, keepdims=True) + c_row)
        p = jnp.exp2(u - (m_new - c_row))
        alpha = jnp.exp2(m_prev - m_new)
        l_scr[h] = alpha * l_scr[h] + jnp.sum(p, axis=0, keepdims=True)
        acc_scr[h] = alpha * acc_scr[h] + jnp.dot(vt_ref[j, hd, :], p.astype(BF16), preferred_element_type=F32)
        m_scr[h] = m_new

    def key_tile(j, mask):
        u = {h: scores(j, h) for h in range(MXU_LOOKAHEAD)}
        for h in range(FX_HEADS):
            if h + MXU_LOOKAHEAD < FX_HEADS:
                u[h + MXU_LOOKAHEAD] = scores(j, h + MXU_LOOKAHEAD)
            update(j, h, u.pop(h), mask)

    def body(j, carry):
        key_tile(j, None)
        return carry

    lax.fori_loop(0, i, body, 0)
    key_tile(i, _causal_mask_t())
    for h in range(FX_HEADS):
        hd = slice(h * FX_HEAD_DIM, (h + 1) * FX_HEAD_DIM)
        y_ref[:, hd] = (acc_scr[h] * (1.0 / l_scr[h])).T.astype(BF16)


def _fox(z3, vt4, caug3, rows3):
    nb, s, _ = z3.shape
    nq = s // TQ
    return pl.pallas_call(
        _fox_body,
        grid=(nb, nq),
        in_specs=[
            pl.BlockSpec((None, TQ, D_MODEL), lambda b, i: (b, i, Z_FX_Q)),
            pl.BlockSpec((None, s, D_MODEL), lambda b, i: (b, 0, Z_FX_K)),
            pl.BlockSpec((None, nq, D_MODEL, TQ), lambda b, i: (b, 0, 1, 0)),
            pl.BlockSpec((None, s, LANES), lambda b, i: (b, 0, 0)),
            pl.BlockSpec((None, N_GATES, TQ), lambda b, i: (b, 0, i)),
        ],
        out_specs=pl.BlockSpec((None, TQ, D_MODEL), lambda b, i: (b, i, 0)),
        out_shape=jax.ShapeDtypeStruct((nb, s, D_MODEL), BF16),
        scratch_shapes=[pltpu.VMEM((FX_HEADS, TQ, 2 * FX_HEAD_DIM), BF16),
                        pltpu.VMEM((FX_HEADS, FX_HEAD_DIM, TQ), F32),
                        pltpu.VMEM((FX_HEADS, 1, TQ), F32), pltpu.VMEM((FX_HEADS, 1, TQ), F32)],
        compiler_params=_params(("parallel", "arbitrary")),
        name="fox",
    )(z3, z3, vt4, caug3, rows3)


def _merge_body(x_ref, yml_ref, yfx_ref, gml_ref, gfx_ref, bml_ref, bfx_ref, wml_ref, wfx_ref, wout_ref, x1_ref):
    p_ml = jnp.dot(yml_ref[...], wml_ref[...], preferred_element_type=F32)
    p_fx = jnp.dot(yfx_ref[...], wfx_ref[...], preferred_element_type=F32)
    merged = (jax.nn.sigmoid(gml_ref[...].astype(F32) + bml_ref[...]) * p_ml
              + jax.nn.sigmoid(gfx_ref[...].astype(F32) + bfx_ref[...]) * p_fx)
    x1_ref[...] = x_ref[...] + jnp.dot(merged.astype(BF16), wout_ref[...], preferred_element_type=F32)


def _merge(x2d, yml, yfx, z2d, bml, bfx, wml, wfx, wout):
    t = x2d.shape[0]
    tile = lambda col: pl.BlockSpec((TM, D_MODEL), lambda i, col=col: (i, col))
    return pl.pallas_call(
        _merge_body,
        grid=(t // TM,),
        in_specs=[tile(0), tile(0), tile(0), tile(Z_GT_ML), tile(Z_GT_FX),
                  _resident((1, D_MODEL)), _resident((1, D_MODEL)),
                  _resident((D_MODEL, D_MODEL)), _resident((D_MODEL, D_MODEL)), _resident((D_MODEL, D_MODEL))],
        out_specs=tile(0),
        out_shape=jax.ShapeDtypeStruct((t, D_MODEL), F32),
        compiler_params=_params(("parallel",)),
        name="merge",
    )(x2d, yml, yfx, z2d, z2d, bml, bfx, wml, wfx, wout)


def _memkv_body(m_ref, g_ref, w_ref, kv_ref):
    hb = _rms(m_ref[...], g_ref[...]).astype(BF16)
    kv_ref[...] = jnp.dot(hb, w_ref[...], preferred_element_type=F32).astype(BF16)


def _memkv(mem2d, g, w):
    t = mem2d.shape[0]
    return pl.pallas_call(
        _memkv_body,
        grid=(t // TM,),
        in_specs=[pl.BlockSpec((TM, D_MODEL), lambda i: (i, 0)), _resident((1, D_MODEL)),
                  _resident((D_MODEL, 2 * D_MODEL))],
        out_specs=pl.BlockSpec((TM, 2 * D_MODEL), lambda i: (i, 0)),
        out_shape=jax.ShapeDtypeStruct((t, 2 * D_MODEL), BF16),
        compiler_params=_params(("parallel",)),
        name="memkv",
    )(mem2d, g, w)


def _route_t(lg_t):
    tm = lg_t.shape[1]
    ninf = -jnp.inf
    big = jnp.float32(LANES)
    gid = lax.broadcasted_iota(jnp.int32, (8, tm), 0).astype(F32)
    eid = lax.broadcasted_iota(jnp.int32, (N_EXPERTS, tm), 0).astype(F32)
    gl = jnp.where(gid < N_GROUPS, lg_t[N_EXPERTS:N_EXPERTS + 8], ninf)
    gmax = jnp.max(gl, axis=0, keepdims=True)
    gidx = jnp.min(jnp.where(gl == gmax, gid, big), axis=0, keepdims=True)
    g_p = 1.0 / jnp.sum(jnp.exp(gl - gmax), axis=0, keepdims=True)
    lo = gidx * EXPERTS_PER_GROUP
    el = jnp.where(eid >= lo, jnp.where(eid < lo + EXPERTS_PER_GROUP, lg_t[0:N_EXPERTS], ninf), ninf)
    v1 = jnp.max(el, axis=0, keepdims=True)
    i1 = jnp.min(jnp.where(el == v1, eid, big), axis=0, keepdims=True)
    el2 = jnp.where(eid == i1, ninf, el)
    v2 = jnp.max(el2, axis=0, keepdims=True)
    i2 = jnp.min(jnp.where(el2 == v2, eid, big), axis=0, keepdims=True)
    t = jnp.exp(v2 - v1)
    w1 = g_p / (1.0 + t)
    w2 = w1 * t
    return jnp.concatenate([i1, i2, w1, w2, jnp.zeros((4, tm), F32)], axis=0)


def _xattn_body(x1_ref, kv_ref, gq_ref, wq_ref, wo_ref, gm_ref, wrt_ref, brt_ref, x2_ref, h3_ref, route_ref,
                wcol_ref):
    x1 = x1_ref[...]
    hb = _rms(x1, gq_ref[...]).astype(BF16)
    q = (jnp.dot(hb, wq_ref[...], preferred_element_type=F32) * (XA_HEAD_DIM ** -0.5)).astype(BF16)
    outs = []
    for h in range(XA_HEADS):
        hd = slice(h * XA_HEAD_DIM, (h + 1) * XA_HEAD_DIM)
        vd = slice(D_MODEL + h * XA_HEAD_DIM, D_MODEL + (h + 1) * XA_HEAD_DIM)
        s = _dot_nt(q[:, hd], kv_ref[:, hd])
        p = jnp.exp(s - jnp.max(s, axis=1, keepdims=True))
        p = p * (1.0 / jnp.sum(p, axis=1, keepdims=True))
        outs.append(jnp.dot(p.astype(BF16), kv_ref[:, vd], preferred_element_type=F32).astype(BF16))
    o = jnp.concatenate(outs, axis=1)
    x2 = x1 + jnp.dot(o, wo_ref[...], preferred_element_type=F32)
    x2_ref[...] = x2
    h3 = _rms(x2, gm_ref[...])
    h3_ref[...] = h3
    lg_t = _dot_nt(wrt_ref[...], h3, precision=lax.Precision.HIGHEST) + brt_ref[...]
    route = _route_t(lg_t)
    route_ref[...] = route
    wcol_ref[...] = jnp.concatenate([route, jnp.zeros((LANES - 8, route.shape[1]), F32)], axis=0).T


def _xattn(x1, kv3, gq, wq, wo, gm, wrt, brt, seq):
    t = x1.shape[0]
    per_b = seq // TM
    n_mem = kv3.shape[1]
    tile = pl.BlockSpec((TM, D_MODEL), lambda i: (i, 0))
    return pl.pallas_call(
        _xattn_body,
        grid=(t // TM,),
        in_specs=[tile,
                  pl.BlockSpec((None, n_mem, 2 * D_MODEL), lambda i: (i // per_b, 0, 0)),
                  _resident((1, D_MODEL)), _resident((D_MODEL, D_MODEL)), _resident((D_MODEL, D_MODEL)),
                  _resident((1, D_MODEL)), _resident((LANES, D_MODEL)), _resident((LANES, 1))],
        out_specs=[tile, tile, pl.BlockSpec((8, TM), lambda i: (0, i)), pl.BlockSpec((TM, LANES), lambda i: (i, 0))],
        out_shape=[jax.ShapeDtypeStruct((t, D_MODEL), F32), jax.ShapeDtypeStruct((t, D_MODEL), F32),
                   jax.ShapeDtypeStruct((8, t), F32), jax.ShapeDtypeStruct((t, LANES), F32)],
        compiler_params=_params(("parallel",)),
        name="xattn",
    )(x1, kv3, gq, wq, wo, gm, wrt, brt)


def _n_expert_tiles(n_tokens):
    return 2 * n_tokens // TME + N_EXPERTS


def _plan_body(route_ref, pos_ref, tinfo_ref, tok_ref, cnt_scr, run_scr, start_scr, tokhi_scr, toklo_scr, *, nt_pad):
    phase = pl.program_id(0)
    b = pl.program_id(1)
    r = route_ref[...]
    eid = lax.broadcasted_iota(jnp.int32, (N_EXPERTS, TP), 0).astype(F32)
    oh1 = eid == r[0:1]
    oh2 = eid == r[1:2]
    oh = jnp.where(oh1 | oh2, 1.0, 0.0)

    @pl.when((phase == 0) & (b == 0))
    def _():
        cnt_scr[...] = jnp.zeros_like(cnt_scr)

    @pl.when(phase == 0)
    def _():
        cnt_scr[...] += jnp.sum(oh, axis=1, keepdims=True)

    @pl.when((phase == 1) & (b == 0))
    def _():
        cnt = cnt_scr[...]
        n_tiles = jnp.floor((cnt + (TME - 1)) * (1.0 / TME))
        ri = lax.broadcasted_iota(jnp.int32, (N_EXPERTS, N_EXPERTS), 0)
        ci = lax.broadcasted_iota(jnp.int32, (N_EXPERTS, N_EXPERTS), 1)
        lower = jnp.where(ci < ri, 1.0, 0.0)
        start = jnp.dot(lower, jnp.broadcast_to(n_tiles, (N_EXPERTS, LANES)), precision=lax.Precision.HIGHEST,
                        preferred_element_type=F32)[:, 0:1]
        start_scr[...] = start * TME
        run_scr[...] = jnp.zeros_like(run_scr)
        tokhi_scr[...] = jnp.zeros_like(tokhi_scr)
        toklo_scr[...] = jnp.zeros_like(toklo_scr)
        n = lax.broadcasted_iota(jnp.int32, (N_EXPERTS, nt_pad), 1).astype(F32)
        e_n = lax.broadcasted_iota(jnp.int32, (N_EXPERTS, nt_pad), 0).astype(F32)
        owner = jnp.sum(jnp.where(start <= n, 1.0, 0.0), axis=0, keepdims=True) - 1.0
        own = e_n == owner
        cnt_o = jnp.sum(jnp.where(own, cnt, 0.0), axis=0, keepdims=True)
        start_o = jnp.sum(jnp.where(own, start, 0.0), axis=0, keepdims=True)
        valid = jnp.clip(cnt_o - (n[0:1] - start_o) * TME, 0.0, float(TME))
        tinfo_ref[...] = jnp.concatenate([owner, valid, jnp.zeros((6, nt_pad), F32)], axis=0).astype(jnp.int32)

    @pl.when(phase == 1)
    def _():
        ti = lax.broadcasted_iota(jnp.int32, (TP, TP), 0)
        tj = lax.broadcasted_iota(jnp.int32, (TP, TP), 1)
        upper = jnp.where(ti < tj, 1.0, 0.0).astype(BF16)
        before = jnp.dot(oh.astype(BF16), upper, preferred_element_type=F32)
        row = start_scr[...] + run_scr[...] + before
        p1 = jnp.sum(jnp.where(oh1, row, 0.0), axis=0, keepdims=True)
        p2 = jnp.sum(jnp.where(oh2, row, 0.0), axis=0, keepdims=True)
        pos_ref[...] = jnp.concatenate([p1, p2, jnp.zeros((6, TP), F32)], axis=0).astype(jnp.int32)
        run_scr[...] += jnp.sum(oh, axis=1, keepdims=True)

        tid = lax.broadcasted_iota(jnp.int32, (1, TP), 1) + b * TP
        t_hi = (tid // 256).astype(F32)
        t_lo = (tid % 256).astype(F32)
        tile_id = lax.broadcasted_iota(jnp.int32, (nt_pad, TP), 0).astype(F32)
        row_id = lax.broadcasted_iota(jnp.int32, (TME, TP), 0).astype(F32)
        for p in (p1, p2):
            hi = jnp.floor(p * (1.0 / TME))
            lo = p - hi * TME
            in_tile = jnp.where(tile_id == hi, 1.0, 0.0).astype(BF16)
            at_row = row_id == lo
            tokhi_scr[...] += _dot_nt(in_tile, jnp.where(at_row, t_hi, 0.0).astype(BF16))
            toklo_scr[...] += _dot_nt(in_tile, jnp.where(at_row, t_lo, 0.0).astype(BF16))

    @pl.when((phase == 1) & (b == pl.num_programs(1) - 1))
    def _():
        tok_ref[...] = (tokhi_scr[...] * 256.0 + toklo_scr[...]).astype(jnp.int32)


def _plan(route):
    t = route.shape[1]
    assert t <= 256 * 256, "token ids are carried as two base-256 digits"
    nt_pad = -(-_n_expert_tiles(t) // LANES) * LANES
    col = pltpu.VMEM((N_EXPERTS, 1), F32)
    table = pltpu.VMEM((nt_pad, TME), F32)
    return pl.pallas_call(
        functools.partial(_plan_body, nt_pad=nt_pad),
        grid=(2, t // TP),
        in_specs=[pl.BlockSpec((8, TP), lambda ph, b: (0, b))],
        out_specs=[pl.BlockSpec((8, TP), lambda ph, b: (0, b * ph)),
                   pl.BlockSpec((8, nt_pad), lambda ph, b: (0, 0)),
                   pl.BlockSpec((nt_pad, TME), lambda ph, b: (0, 0))],
        out_shape=[jax.ShapeDtypeStruct((8, t), jnp.int32), jax.ShapeDtypeStruct((8, nt_pad), jnp.int32),
                   jax.ShapeDtypeStruct((nt_pad, TME), jnp.int32)],
        scratch_shapes=[col, col, col, table, table],
        compiler_params=_params(("arbitrary", "arbitrary")),
        name="plan",
    )(route)


def _experts_body(texp_ref, tvalid_ref, tok_ref, tok_next_ref, h3_hbm, wg_ref, wu_ref, wd_ref, ys_ref,
                  x_scr, sem, wgb_scr, wub_scr, wdb_scr, *, nt):
    n = pl.program_id(0)
    slot = n % 2
    nxt = jnp.minimum(n + 1, nt - 1)

    def row_copy(tok, r, s):
        return pltpu.make_async_copy(h3_hbm.at[pl.ds(tok[0, r], 1)], x_scr.at[s, pl.ds(r, 1)], sem.at[s])

    def for_rows(tok, s, op):
        lax.fori_loop(0, TME, lambda r, c: (op(row_copy(tok, r, s)), c)[1], 0, unroll=8)

    @pl.when((n == 0) & (tvalid_ref[0] > 0))
    def _():
        for_rows(tok_ref, 0, lambda cp: cp.start())

    @pl.when((n + 1 < nt) & (tvalid_ref[nxt] > 0))
    def _():
        for_rows(tok_next_ref, 1 - slot, lambda cp: cp.start())

    @pl.when((n == 0) | (texp_ref[n] != texp_ref[jnp.maximum(n - 1, 0)]))
    def _():
        wgb_scr[...] = wg_ref[...].astype(BF16)
        wub_scr[...] = wu_ref[...].astype(BF16)
        wdb_scr[...] = wd_ref[...].astype(BF16)

    @pl.when(tvalid_ref[n] > 0)
    def _():
        for_rows(tok_ref, slot, lambda cp: cp.wait())
        xb = x_scr[slot].astype(BF16)
        gate = jnp.dot(xb, wgb_scr[...], preferred_element_type=F32)
        up = jnp.dot(xb, wub_scr[...], preferred_element_type=F32)
        he = (gate * jax.nn.sigmoid(gate) * up).astype(BF16)
        ys_ref[...] = jnp.dot(he, wdb_scr[...], preferred_element_type=F32)

    @pl.when(tvalid_ref[n] == 0)
    def _():
        ys_ref[...] = jnp.zeros_like(ys_ref)


def _experts(texp, tvalid, tok, h3, wg, wu, wd):
    nt = texp.shape[0]
    tok3 = tok.reshape(tok.shape[0], 1, TME)
    weight = lambda shape: pl.BlockSpec((None,) + shape, lambda n, te, tv: (te[n], 0, 0))
    tok_rows = lambda ahead: pl.BlockSpec((None, 1, TME), lambda n, te, tv: (jnp.minimum(n + ahead, nt - 1), 0, 0),
                                          memory_space=pltpu.SMEM)
    return pl.pallas_call(
        functools.partial(_experts_body, nt=nt),
        grid_spec=pltpu.PrefetchScalarGridSpec(
            num_scalar_prefetch=2,
            grid=(nt,),
            in_specs=[tok_rows(0), tok_rows(1), pl.BlockSpec(memory_space=pl.ANY),
                      weight((D_MODEL, D_EXPERT)), weight((D_MODEL, D_EXPERT)), weight((D_EXPERT, D_MODEL))],
            out_specs=pl.BlockSpec((TME, D_MODEL), lambda n, te, tv: (n, 0)),
            scratch_shapes=[pltpu.VMEM((2, TME, D_MODEL), F32), pltpu.SemaphoreType.DMA((2,)),
                            pltpu.VMEM((D_MODEL, D_EXPERT), BF16), pltpu.VMEM((D_MODEL, D_EXPERT), BF16),
                            pltpu.VMEM((D_EXPERT, D_MODEL), BF16)],
        ),
        out_shape=jax.ShapeDtypeStruct((nt * TME, D_MODEL), F32),
        compiler_params=_params(("arbitrary",)),
        name="experts",
    )(texp, tvalid, tok3, tok3, h3, wg, wu, wd)


def _rowgather(table, idx):
    n_rows, width = idx.shape[0], table.shape[1]
    n_workers = SC_CORES * SC_SUBCORES
    per_worker = n_rows // n_workers
    assert per_worker * n_workers == n_rows and per_worker % SC_ROWS == 0
    mesh = plsc.VectorSubcoreMesh(core_axis_name="c", subcore_axis_name="s", num_cores=SC_CORES,
                                  num_subcores=SC_SUBCORES)

    @functools.partial(
        pl.kernel, mesh=mesh, out_type=jax.ShapeDtypeStruct((n_rows, width), table.dtype),
        scratch_types=[pltpu.VMEM((SC_ROWS,), jnp.int32), pltpu.VMEM((SC_ROWS, width), table.dtype),
                       pltpu.SemaphoreType.DMA],
        name="rowgather")
    def gather(table_hbm, idx_hbm, out_hbm, idx_v, rows_v, sem):
        worker = lax.axis_index("s") * SC_CORES + lax.axis_index("c")
        base = worker * per_worker

        @pl.loop(0, per_worker // SC_ROWS)
        def _(c):
            off = base + c * SC_ROWS
            pltpu.sync_copy(idx_hbm.at[pl.ds(off, SC_ROWS)], idx_v)
            pltpu.async_copy(table_hbm.at[idx_v], rows_v, sem).wait()
            pltpu.sync_copy(rows_v, out_hbm.at[pl.ds(off, SC_ROWS)])

    return gather(table, idx)


def _combine_body(g1_ref, g2_ref, x2_ref, wcol_ref, gf_ref, out_ref):
    w = wcol_ref[...]
    y = w[:, 2:3] * g1_ref[...] + w[:, 3:4] * g2_ref[...]
    out_ref[...] = _rms(x2_ref[...] + y, gf_ref[...])


def _combine(g, x2, wcol, gf):
    t = x2.shape[0]
    steps = t // TF
    return pl.pallas_call(
        _combine_body,
        grid=(steps,),
        in_specs=[pl.BlockSpec((TF, D_MODEL), lambda i: (i, 0)),
                  pl.BlockSpec((TF, D_MODEL), lambda i: (i + steps, 0)),
                  pl.BlockSpec((TF, D_MODEL), lambda i: (i, 0)),
                  pl.BlockSpec((TF, LANES), lambda i: (i, 0)),
                  _resident((1, D_MODEL))],
        out_specs=pl.BlockSpec((TF, D_MODEL), lambda i: (i, 0)),
        out_shape=jax.ShapeDtypeStruct((t, D_MODEL), F32),
        compiler_params=_params(("parallel",)),
        name="combine",
    )(g, g, x2, wcol, gf)


def _layer(x, mem, g_mix, w_in, b_ml_i, b_ml_f, b_fx_f, b_gate_ml, b_gate_fx, g_ml_head, w_proj_ml, w_proj_fx,
           w_out, g_xq, g_xmem, w_xq, w_xkv, w_xo, g_moe, w_rg, b_rg, w_re, b_re, w_gate, w_up, w_down):
    nb, seq, d = x.shape
    t = nb * seq
    row = lambda v: v.reshape(1, -1).astype(F32)

    o = 0
    parts = {}
    for name, width in (("ml_q", 512), ("ml_k", 512), ("ml_v", 1024), ("ml_o", 1024), ("ml_i", 4), ("ml_f", 4),
                        ("fx_q", 1024), ("fx_k", 1024), ("fx_v", 1024), ("fx_f", 8), ("gt_ml", 1024),
                        ("gt_fx", 1024)):
        parts[name] = w_in[:, o:o + width]
        o += width
    w_main = jnp.concatenate(
        [parts["ml_q"], parts["ml_k"] * (ML_QK_DIM ** -0.5), parts["ml_o"],
         parts["fx_q"] * (FX_HEAD_DIM ** -0.5 * LOG2E), parts["fx_k"], parts["gt_ml"], parts["gt_fx"]],
        axis=1).astype(BF16)
    w_vt = jnp.concatenate([parts["ml_v"], parts["fx_v"]], axis=1).T.astype(BF16)
    w_gates_t = jnp.concatenate([parts["ml_i"], parts["ml_f"], parts["fx_f"]], axis=1).T
    gate_bias = jnp.concatenate([b_ml_i, b_ml_f, b_fx_f]).reshape(N_GATES, 1).astype(F32)

    x2d = x.reshape(t, d)
    z, vt, gates_t = _inproj(x2d, row(g_mix), w_main, w_vt, w_gates_t)
    gt3 = gates_t.reshape(N_GATES, nb, seq).transpose(1, 0, 2)
    rows, cols, caug = _gateprep(gt3, gate_bias)
    z3 = z.reshape(nb, seq, Z_W)
    vt4 = vt.reshape(nb, seq // TQ, VT_W, TQ)
    y_ml = _mlstm(z3, vt4, cols, rows, row(g_ml_head))
    y_fx = _fox(z3, vt4, caug, rows)
    x1 = _merge(x2d, y_ml.reshape(t, d), y_fx.reshape(t, d), z, row(b_gate_ml), row(b_gate_fx),
                w_proj_ml.astype(BF16), w_proj_fx.astype(BF16), w_out.astype(BF16))

    n_mem = mem.shape[1]
    kv = _memkv(mem.reshape(nb * n_mem, d), row(g_xmem), w_xkv.astype(BF16))
    w_router_t = jnp.concatenate([w_re, w_rg, jnp.zeros((d, LANES - N_EXPERTS - N_GROUPS), F32)], axis=1).T
    b_router_t = jnp.concatenate([b_re, b_rg, jnp.zeros((LANES - N_EXPERTS - N_GROUPS,), F32)]).reshape(LANES, 1)
    x2, h3, route, wcol = _xattn(x1, kv.reshape(nb, n_mem, 2 * d), row(g_xq), w_xq.astype(BF16),
                                 w_xo.astype(BF16), row(g_moe), w_router_t, b_router_t, seq)
    pos, tinfo, tok = _plan(route)
    nt = _n_expert_tiles(t)
    texp, tvalid = tinfo[0, :nt], tinfo[1, :nt]
    ys = _experts(texp, tvalid, tok, h3, w_gate, w_up, w_down)
    return x2, ys, pos, wcol


def kernel(x, mem, g_mix, w_in, b_ml_i, b_ml_f, b_fx_f, b_gate_ml, b_gate_fx, g_ml_head, w_proj_ml, w_proj_fx, w_out, g_xq, g_xmem, w_xq, w_xkv, w_xo, g_moe, w_rg, b_rg, w_re, b_re, w_gate, w_up, w_down, g_final):
    nb, seq, d = x.shape
    depth = g_mix.shape[0]
    assert depth == 1, "the final rmsnorm is fused into the (single) layer's combine kernel"
    x2, ys, pos, wcol = _layer(
        x, mem, g_mix[0], w_in[0], b_ml_i[0], b_ml_f[0], b_fx_f[0], b_gate_ml[0], b_gate_fx[0], g_ml_head[0],
        w_proj_ml[0], w_proj_fx[0], w_out[0], g_xq[0], g_xmem[0], w_xq[0], w_xkv[0], w_xo[0], g_moe[0],
        w_rg[0], b_rg[0], w_re[0], b_re[0], w_gate[0], w_up[0], w_down[0])
    g = _rowgather(ys, pos[0:2].reshape(-1))
    out = _combine(g, x2, wcol, g_final.reshape(1, d).astype(F32))
    return out.reshape(nb, seq, d)
```

```python
import functools

import jax
import jax.numpy as jnp
from jax import lax
from jax.experimental import pallas as pl
from jax.experimental.pallas import tpu as pltpu
from jax.experimental.pallas import tpu_sc as plsc

F32 = jnp.float32
BF16 = jnp.bfloat16

D_MODEL = 1024
EPS = 1e-6
ML_HEADS = 4
ML_QK_DIM = 128
ML_V_DIM = 256
FX_HEADS = 8
FX_HEAD_DIM = 128
XA_HEADS = 4
XA_HEAD_DIM = 256
N_GROUPS = 4
EXPERTS_PER_GROUP = 8
N_EXPERTS = 32
D_EXPERT = 512

LANES = 128
N_GATES = 16
LOG2E = 1.4426950408889634
Z_W = 6144
Z_ML_O, Z_FX_Q, Z_FX_K, Z_GT_ML, Z_GT_FX = 1, 2, 3, 4, 5
VT_W = 2048

VMEM_LIMIT = 56 * 1024 * 1024

IN_CHUNK = 1024
TQ = 256
MXU_LOOKAHEAD = 4
TM = 512
TME = 256
TP = 512
TF = 512
SC_CORES, SC_SUBCORES = 2, 16
SC_ROWS = 32


def _params(sem, flags=None):
    return pltpu.CompilerParams(dimension_semantics=sem, vmem_limit_bytes=VMEM_LIMIT, flags=flags)


def _rms(x, g):
    return x * lax.rsqrt(jnp.mean(x * x, axis=-1, keepdims=True) + EPS) * g


def _dot_nt(a, b, **kw):
    return lax.dot_general(a, b, (((1,), (1,)), ((), ())), preferred_element_type=F32, **kw)


def _resident(shape):
    zeros = (0,) * len(shape)
    return pl.BlockSpec(shape, lambda *_: zeros, pipeline_mode=pl.Buffered(1))


def _inproj_body(x_ref, g_ref, w_ref, wvt_ref, wgt_ref, z_ref, vt_ref, gt_ref):
    h = _rms(x_ref[...], g_ref[...])
    hb = h.astype(BF16)
    for c in range(Z_W // IN_CHUNK):
        sl = slice(c * IN_CHUNK, (c + 1) * IN_CHUNK)
        z_ref[:, sl] = jnp.dot(hb, w_ref[:, sl], preferred_element_type=F32).astype(BF16)
    for c in range(VT_W // IN_CHUNK):
        sl = slice(c * IN_CHUNK, (c + 1) * IN_CHUNK)
        vt_ref[sl, :] = _dot_nt(wvt_ref[sl, :], hb).astype(BF16)
    gt_ref[...] = _dot_nt(wgt_ref[...], h, precision=lax.Precision.HIGHEST)


def _inproj(x2d, g, w, wvt, wgt):
    t = x2d.shape[0]
    return pl.pallas_call(
        _inproj_body,
        grid=(t // TQ,),
        in_specs=[
            pl.BlockSpec((TQ, D_MODEL), lambda i: (i, 0)),
            _resident((1, D_MODEL)),
            _resident((D_MODEL, Z_W)),
            _resident((VT_W, D_MODEL)),
            _resident((N_GATES, D_MODEL)),
        ],
        out_specs=[
            pl.BlockSpec((TQ, Z_W), lambda i: (i, 0)),
            pl.BlockSpec((None, VT_W, TQ), lambda i: (i, 0, 0)),
            pl.BlockSpec((N_GATES, TQ), lambda i: (0, i)),
        ],
        out_shape=[jax.ShapeDtypeStruct((t, Z_W), BF16), jax.ShapeDtypeStruct((t // TQ, VT_W, TQ), BF16),
                   jax.ShapeDtypeStruct((N_GATES, t), F32)],
        compiler_params=_params(("parallel",)),
        name="inproj",
    )(x2d, g, w, wvt, wgt)


def _scan_lanes(x, op, identity):
    n = x.shape[-1]
    idx = lax.broadcasted_iota(jnp.int32, x.shape, 1)
    s = 1
    while s < n:
        shifted = pltpu.roll(x, s, axis=1)
        x = op(x, jnp.where(idx >= s, shifted, identity))
        s *= 2
    return x


def _log_sigmoid(x):
    return jnp.minimum(x, 0.0) - jnp.log1p(jnp.exp(-jnp.abs(x)))


def _gateprep_body(gt_ref, bias_ref, rows_ref, cols_ref, caug_ref):
    g = gt_ref[...] + bias_ref[...]
    s = g.shape[1]
    cs = _scan_lanes(_log_sigmoid(g), jnp.add, 0.0)
    b = cs[4:8]
    c2 = cs[8:16] * LOG2E
    a = g[0:4] - b
    m = _scan_lanes(jnp.concatenate([a, a], axis=0), jnp.maximum, -jnp.inf)[0:4]
    rows_ref[...] = jnp.concatenate([m * LOG2E, b + m, c2], axis=0)
    cols_ref[...] = jnp.concatenate([a * LOG2E, jnp.zeros((LANES - ML_HEADS, s), F32)], axis=0).T
    hi = c2.astype(BF16).astype(F32)
    r1 = c2 - hi
    mid = r1.astype(BF16).astype(F32)
    lo = r1 - mid
    aug = jnp.concatenate([-hi, -mid, -lo, jnp.zeros((LANES - 3 * FX_HEADS, s), F32)], axis=0)
    caug_ref[...] = aug.T.astype(BF16)


def _gateprep(gt3, bias):
    nb, _, s = gt3.shape
    return pl.pallas_call(
        _gateprep_body,
        grid=(nb,),
        in_specs=[
            pl.BlockSpec((None, N_GATES, s), lambda b: (b, 0, 0)),
            _resident((N_GATES, 1)),
        ],
        out_specs=[
            pl.BlockSpec((None, N_GATES, s), lambda b: (b, 0, 0)),
            pl.BlockSpec((None, s, LANES), lambda b: (b, 0, 0)),
            pl.BlockSpec((None, s, LANES), lambda b: (b, 0, 0)),
        ],
        out_shape=[jax.ShapeDtypeStruct((nb, N_GATES, s), F32), jax.ShapeDtypeStruct((nb, s, LANES), F32),
                   jax.ShapeDtypeStruct((nb, s, LANES), BF16)],
        compiler_params=_params(("parallel",)),
        name="gateprep",
    )(gt3, bias)


def _causal_mask_t():
    s = lax.broadcasted_iota(jnp.int32, (TQ, TQ), 0)
    t = lax.broadcasted_iota(jnp.int32, (TQ, TQ), 1)
    return s <= t


def _mlstm_body(q_ref, k_ref, vt_ref, o_ref, cols_ref, rows_ref, gh_ref, y_ref, num_scr, den_scr):
    i = pl.program_id(1)
    rows = rows_ref[...]
    num_scr[...] = jnp.zeros_like(num_scr)
    den_scr[...] = jnp.zeros_like(den_scr)

    def scores(j, h):
        ks = pl.ds(pl.multiple_of(j * TQ, TQ), TQ)
        qk = slice(h * ML_QK_DIM, (h + 1) * ML_QK_DIM)
        return _dot_nt(k_ref[ks, qk], q_ref[:, qk])

    def update(j, h, s, mask):
        ks = pl.ds(pl.multiple_of(j * TQ, TQ), TQ)
        vv = slice(h * ML_V_DIM, (h + 1) * ML_V_DIM)
        w = jnp.exp2(cols_ref[ks, h:h + 1] - rows[h:h + 1])
        if mask is not None:
            w = jnp.where(mask, w, 0.0)
        s = s * w
        den_scr[h] += jnp.sum(s, axis=0, keepdims=True)
        num_scr[h] += jnp.dot(vt_ref[j, vv, :], s.astype(BF16), preferred_element_type=F32)

    def key_tile(j, mask):
        s = {h: scores(j, h) for h in range(min(MXU_LOOKAHEAD, ML_HEADS))}
        for h in range(ML_HEADS):
            if h + MXU_LOOKAHEAD < ML_HEADS:
                s[h + MXU_LOOKAHEAD] = scores(j, h + MXU_LOOKAHEAD)
            update(j, h, s.pop(h), mask)

    def body(j, carry):
        key_tile(j, None)
        return carry

    lax.fori_loop(0, i, body, 0)
    key_tile(i, _causal_mask_t())
    for h in range(ML_HEADS):
        vv = slice(h * ML_V_DIM, (h + 1) * ML_V_DIM)
        floor = jnp.exp(-rows[ML_HEADS + h:ML_HEADS + h + 1])
        hh = num_scr[h] * (1.0 / jnp.maximum(jnp.abs(den_scr[h]), floor))
        yt = hh * lax.rsqrt(jnp.mean(hh * hh, axis=0, keepdims=True) + EPS)
        y = yt.T * gh_ref[:, vv]
        y_ref[:, vv] = (y * jax.nn.sigmoid(o_ref[:, vv].astype(F32))).astype(BF16)


def _mlstm(z3, vt4, cols3, rows3, g_head):
    nb, s, _ = z3.shape
    nq = s // TQ
    return pl.pallas_call(
        _mlstm_body,
        grid=(nb, nq),
        in_specs=[
            pl.BlockSpec((None, TQ, ML_HEADS * ML_QK_DIM), lambda b, i: (b, i, 0)),
            pl.BlockSpec((None, s, ML_HEADS * ML_QK_DIM), lambda b, i: (b, 0, 1)),
            pl.BlockSpec((None, nq, D_MODEL, TQ), lambda b, i: (b, 0, 0, 0)),
            pl.BlockSpec((None, TQ, D_MODEL), lambda b, i: (b, i, Z_ML_O)),
            pl.BlockSpec((None, s, LANES), lambda b, i: (b, 0, 0)),
            pl.BlockSpec((None, N_GATES, TQ), lambda b, i: (b, 0, i)),
            _resident((1, D_MODEL)),
        ],
        out_specs=pl.BlockSpec((None, TQ, D_MODEL), lambda b, i: (b, i, 0)),
        out_shape=jax.ShapeDtypeStruct((nb, s, D_MODEL), BF16),
        scratch_shapes=[pltpu.VMEM((ML_HEADS, ML_V_DIM, TQ), F32), pltpu.VMEM((ML_HEADS, 1, TQ), F32)],
        compiler_params=_params(("parallel", "arbitrary")),
        name="mlstm",
    )(z3, z3, vt4, z3, cols3, rows3, g_head)


def _fox_body(q_ref, k_ref, vt_ref, caug_ref, rows_ref, y_ref, qa_scr, acc_scr, m_scr, l_scr):
    i = pl.program_id(1)
    rows = rows_ref[...]
    lane = lax.broadcasted_iota(jnp.int32, (TQ, LANES), 1)
    for h in range(FX_HEADS):
        hd = slice(h * FX_HEAD_DIM, (h + 1) * FX_HEAD_DIM)
        ones = jnp.where((lane < 3 * FX_HEADS) & (lane % FX_HEADS == h), 1.0, 0.0).astype(BF16)
        qa_scr[h] = jnp.concatenate([q_ref[:, hd], ones], axis=1)
    m_scr[...] = jnp.full_like(m_scr, -jnp.inf)
    l_scr[...] = jnp.zeros_like(l_scr)
    acc_scr[...] = jnp.zeros_like(acc_scr)

    def scores(j, h):
        ks = pl.ds(pl.multiple_of(j * TQ, TQ), TQ)
        hd = slice(h * FX_HEAD_DIM, (h + 1) * FX_HEAD_DIM)
        k_aug = jnp.concatenate([k_ref[ks, hd], caug_ref[ks, :]], axis=1)
        return _dot_nt(k_aug, qa_scr[h])

    def update(j, h, u, mask):
        hd = slice(h * FX_HEAD_DIM, (h + 1) * FX_HEAD_DIM)
        if mask is not None:
            u = jnp.where(mask, u, -jnp.inf)
        c_row = rows[2 * ML_HEADS + h:2 * ML_HEADS + h + 1]
        m_prev = m_scr[h]
        m_new = jnp.maximum(m_prev, jnp.max(u, axis=0, keepdims=True) + c_row)
        p = jnp.exp2(u - (m_new - c_row))
        alpha = jnp.exp2(m_prev - m_new)
        l_scr[h] = alpha * l_scr[h] + jnp.sum(p, axis=0, keepdims=True)
        acc_scr[h] = alpha * acc_scr[h] + jnp.dot(vt_ref[j, hd, :], p.astype(BF16), preferred_element_type=F32)
        m_scr[h] = m_new

    def key_tile(j, mask):
        u = {h: scores(j, h) for h in range(MXU_LOOKAHEAD)}
        for h in range(FX_HEADS):
            if h + MXU_LOOKAHEAD < FX_HEADS:
                u[h + MXU_LOOKAHEAD] = scores(j, h + MXU_LOOKAHEAD)
            update(j, h, u.pop(h), mask)

    def body(j, carry):
        key_tile(j, None)
        return carry

    lax.fori_loop(0, i, body, 0)
    key_tile(i, _causal_mask_t())
    for h in range(FX_HEADS):
        hd = slice(h * FX_HEAD_DIM, (h + 1) * FX_HEAD_DIM)
        y_ref[:, hd] = (acc_scr[h] * (1.0 / l_scr[h])).T.astype(BF16)


def _fox(z3, vt4, caug3, rows3):
    nb, s, _ = z3.shape
    nq = s // TQ
    return pl.pallas_call(
        _fox_body,
        grid=(nb, nq),
        in_specs=[
            pl.BlockSpec((None, TQ, D_MODEL), lambda b, i: (b, i, Z_FX_Q)),
            pl.BlockSpec((None, s, D_MODEL), lambda b, i: (b, 0, Z_FX_K)),
            pl.BlockSpec((None, nq, D_MODEL, TQ), lambda b, i: (b, 0, 1, 0)),
            pl.BlockSpec((None, s, LANES), lambda b, i: (b, 0, 0)),
            pl.BlockSpec((None, N_GATES, TQ), lambda b, i: (b, 0, i)),
        ],
        out_specs=pl.BlockSpec((None, TQ, D_MODEL), lambda b, i: (b, i, 0)),
        out_shape=jax.ShapeDtypeStruct((nb, s, D_MODEL), BF16),
        scratch_shapes=[pltpu.VMEM((FX_HEADS, TQ, 2 * FX_HEAD_DIM), BF16),
                        pltpu.VMEM((FX_HEADS, FX_HEAD_DIM, TQ), F32),
                        pltpu.VMEM((FX_HEADS, 1, TQ), F32), pltpu.VMEM((FX_HEADS, 1, TQ), F32)],
        compiler_params=_params(("parallel", "arbitrary")),
        name="fox",
    )(z3, z3, vt4, caug3, rows3)


def _merge_body(x_ref, yml_ref, yfx_ref, gml_ref, gfx_ref, bml_ref, bfx_ref, wml_ref, wfx_ref, wout_ref, x1_ref):
    p_ml = jnp.dot(yml_ref[...], wml_ref[...], preferred_element_type=F32)
    p_fx = jnp.dot(yfx_ref[...], wfx_ref[...], preferred_element_type=F32)
    merged = (jax.nn.sigmoid(gml_ref[...].astype(F32) + bml_ref[...]) * p_ml
              + jax.nn.sigmoid(gfx_ref[...].astype(F32) + bfx_ref[...]) * p_fx)
    x1_ref[...] = x_ref[...] + jnp.dot(merged.astype(BF16), wout_ref[...], preferred_element_type=F32)


def _merge(x2d, yml, yfx, z2d, bml, bfx, wml, wfx, wout):
    t = x2d.shape[0]
    tile = lambda col: pl.BlockSpec((TM, D_MODEL), lambda i, col=col: (i, col))
    return pl.pallas_call(
        _merge_body,
        grid=(t // TM,),
        in_specs=[tile(0), tile(0), tile(0), tile(Z_GT_ML), tile(Z_GT_FX),
                  _resident((1, D_MODEL)), _resident((1, D_MODEL)),
                  _resident((D_MODEL, D_MODEL)), _resident((D_MODEL, D_MODEL)), _resident((D_MODEL, D_MODEL))],
        out_specs=tile(0),
        out_shape=jax.ShapeDtypeStruct((t, D_MODEL), F32),
        compiler_params=_params(("parallel",)),
        name="merge",
    )(x2d, yml, yfx, z2d, z2d, bml, bfx, wml, wfx, wout)


def _memkv_body(m_ref, g_ref, w_ref, kv_ref):
    hb = _rms(m_ref[...], g_ref[...]).astype(BF16)
    kv_ref[...] = jnp.dot(hb, w_ref[...], preferred_element_type=F32).astype(BF16)


def _memkv(mem2d, g, w):
    t = mem2d.shape[0]
    return pl.pallas_call(
        _memkv_body,
        grid=(t // TM,),
        in_specs=[pl.BlockSpec((TM, D_MODEL), lambda i: (i, 0)), _resident((1, D_MODEL)),
                  _resident((D_MODEL, 2 * D_MODEL))],
        out_specs=pl.BlockSpec((TM, 2 * D_MODEL), lambda i: (i, 0)),
        out_shape=jax.ShapeDtypeStruct((t, 2 * D_MODEL), BF16),
        compiler_params=_params(("parallel",)),
        name="memkv",
    )(mem2d, g, w)


def _route_t(lg_t):
    tm = lg_t.shape[1]
    ninf = -jnp.inf
    big = jnp.float32(LANES)
    gid = lax.broadcasted_iota(jnp.int32, (8, tm), 0).astype(F32)
    eid = lax.broadcasted_iota(jnp.int32, (N_EXPERTS, tm), 0).astype(F32)
    gl = jnp.where(gid < N_GROUPS, lg_t[N_EXPERTS:N_EXPERTS + 8], ninf)
    gmax = jnp.max(gl, axis=0, keepdims=True)
    gidx = jnp.min(jnp.where(gl == gmax, gid, big), axis=0, keepdims=True)
    g_p = 1.0 / jnp.sum(jnp.exp(gl - gmax), axis=0, keepdims=True)
    lo = gidx * EXPERTS_PER_GROUP
    el = jnp.where(eid >= lo, jnp.where(eid < lo + EXPERTS_PER_GROUP, lg_t[0:N_EXPERTS], ninf), ninf)
    v1 = jnp.max(el, axis=0, keepdims=True)
    i1 = jnp.min(jnp.where(el == v1, eid, big), axis=0, keepdims=True)
    el2 = jnp.where(eid == i1, ninf, el)
    v2 = jnp.max(el2, axis=0, keepdims=True)
    i2 = jnp.min(jnp.where(el2 == v2, eid, big), axis=0, keepdims=True)
    t = jnp.exp(v2 - v1)
    w1 = g_p / (1.0 + t)
    w2 = w1 * t
    return jnp.concatenate([i1, i2, w1, w2, jnp.zeros((4, tm), F32)], axis=0)


def _xattn_body(x1_ref, kv_ref, gq_ref, wq_ref, wo_ref, gm_ref, wrt_ref, brt_ref, x2_ref, h3_ref, route_ref,
                wcol_ref):
    x1 = x1_ref[...]
    hb = _rms(x1, gq_ref[...]).astype(BF16)
    q = (jnp.dot(hb, wq_ref[...], preferred_element_type=F32) * (XA_HEAD_DIM ** -0.5)).astype(BF16)
    outs = []
    for h in range(XA_HEADS):
        hd = slice(h * XA_HEAD_DIM, (h + 1) * XA_HEAD_DIM)
        vd = slice(D_MODEL + h * XA_HEAD_DIM, D_MODEL + (h + 1) * XA_HEAD_DIM)
        s = _dot_nt(q[:, hd], kv_ref[:, hd])
        p = jnp.exp(s - jnp.max(s, axis=1, keepdims=True))
        p = p * (1.0 / jnp.sum(p, axis=1, keepdims=True))
        outs.append(jnp.dot(p.astype(BF16), kv_ref[:, vd], preferred_element_type=F32).astype(BF16))
    o = jnp.concatenate(outs, axis=1)
    x2 = x1 + jnp.dot(o, wo_ref[...], preferred_element_type=F32)
    x2_ref[...] = x2
    h3 = _rms(x2, gm_ref[...])
    h3_ref[...] = h3
    lg_t = _dot_nt(wrt_ref[...], h3, precision=lax.Precision.HIGHEST) + brt_ref[...]
    route = _route_t(lg_t)
    route_ref[...] = route
    wcol_ref[...] = jnp.concatenate([route, jnp.zeros((LANES - 8, route.shape[1]), F32)], axis=0).T


def _xattn(x1, kv3, gq, wq, wo, gm, wrt, brt, seq):
    t = x1.shape[0]
    per_b = seq // TM
    n_mem = kv3.shape[1]
    tile = pl.BlockSpec((TM, D_MODEL), lambda i: (i, 0))
    return pl.pallas_call(
        _xattn_body,
        grid=(t // TM,),
        in_specs=[tile,
                  pl.BlockSpec((None, n_mem, 2 * D_MODEL), lambda i: (i // per_b, 0, 0)),
                  _resident((1, D_MODEL)), _resident((D_MODEL, D_MODEL)), _resident((D_MODEL, D_MODEL)),
                  _resident((1, D_MODEL)), _resident((LANES, D_MODEL)), _resident((LANES, 1))],
        out_specs=[tile, tile, pl.BlockSpec((8, TM), lambda i: (0, i)), pl.BlockSpec((TM, LANES), lambda i: (i, 0))],
        out_shape=[jax.ShapeDtypeStruct((t, D_MODEL), F32), jax.ShapeDtypeStruct((t, D_MODEL), F32),
                   jax.ShapeDtypeStruct((8, t), F32), jax.ShapeDtypeStruct((t, LANES), F32)],
        compiler_params=_params(("parallel",)),
        name="xattn",
    )(x1, kv3, gq, wq, wo, gm, wrt, brt)


def _n_expert_tiles(n_tokens):
    return 2 * n_tokens // TME + N_EXPERTS


def _plan_body(route_ref, pos_ref, tinfo_ref, tok_ref, cnt_scr, run_scr, start_scr, tokhi_scr, toklo_scr, *, nt_pad):
    phase = pl.program_id(0)
    b = pl.program_id(1)
    r = route_ref[...]
    eid = lax.broadcasted_iota(jnp.int32, (N_EXPERTS, TP), 0).astype(F32)
    oh1 = eid == r[0:1]
    oh2 = eid == r[1:2]
    oh = jnp.where(oh1 | oh2, 1.0, 0.0)

    @pl.when((phase == 0) & (b == 0))
    def _():
        cnt_scr[...] = jnp.zeros_like(cnt_scr)

    @pl.when(phase == 0)
    def _():
        cnt_scr[...] += jnp.sum(oh, axis=1, keepdims=True)

    @pl.when((phase == 1) & (b == 0))
    def _():
        cnt = cnt_scr[...]
        n_tiles = jnp.floor((cnt + (TME - 1)) * (1.0 / TME))
        ri = lax.broadcasted_iota(jnp.int32, (N_EXPERTS, N_EXPERTS), 0)
        ci = lax.broadcasted_iota(jnp.int32, (N_EXPERTS, N_EXPERTS), 1)
        lower = jnp.where(ci < ri, 1.0, 0.0)
        start = jnp.dot(lower, jnp.broadcast_to(n_tiles, (N_EXPERTS, LANES)), precision=lax.Precision.HIGHEST,
                        preferred_element_type=F32)[:, 0:1]
        start_scr[...] = start * TME
        run_scr[...] = jnp.zeros_like(run_scr)
        tokhi_scr[...] = jnp.zeros_like(tokhi_scr)
        toklo_scr[...] = jnp.zeros_like(toklo_scr)
        n = lax.broadcasted_iota(jnp.int32, (N_EXPERTS, nt_pad), 1).astype(F32)
        e_n = lax.broadcasted_iota(jnp.int32, (N_EXPERTS, nt_pad), 0).astype(F32)
        owner = jnp.sum(jnp.where(start <= n, 1.0, 0.0), axis=0, keepdims=True) - 1.0
        own = e_n == owner
        cnt_o = jnp.sum(jnp.where(own, cnt, 0.0), axis=0, keepdims=True)
        start_o = jnp.sum(jnp.where(own, start, 0.0), axis=0, keepdims=True)
        valid = jnp.clip(cnt_o - (n[0:1] - start_o) * TME, 0.0, float(TME))
        tinfo_ref[...] = jnp.concatenate([owner, valid, jnp.zeros((6, nt_pad), F32)], axis=0).astype(jnp.int32)

    @pl.when(phase == 1)
    def _():
        ti = lax.broadcasted_iota(jnp.int32, (TP, TP), 0)
        tj = lax.broadcasted_iota(jnp.int32, (TP, TP), 1)
        upper = jnp.where(ti < tj, 1.0, 0.0).astype(BF16)
        before = jnp.dot(oh.astype(BF16), upper, preferred_element_type=F32)
        row = start_scr[...] + run_scr[...] + before
        p1 = jnp.sum(jnp.where(oh1, row, 0.0), axis=0, keepdims=True)
        p2 = jnp.sum(jnp.where(oh2, row, 0.0), axis=0, keepdims=True)
        pos_ref[...] = jnp.concatenate([p1, p2, jnp.zeros((6, TP), F32)], axis=0).astype(jnp.int32)
        run_scr[...] += jnp.sum(oh, axis=1, keepdims=True)

        tid = lax.broadcasted_iota(jnp.int32, (1, TP), 1) + b * TP
        t_hi = (tid // 256).astype(F32)
        t_lo = (tid % 256).astype(F32)
        tile_id = lax.broadcasted_iota(jnp.int32, (nt_pad, TP), 0).astype(F32)
        row_id = lax.broadcasted_iota(jnp.int32, (TME, TP), 0).astype(F32)
        for p in (p1, p2):
            hi = jnp.floor(p * (1.0 / TME))
            lo = p - hi * TME
            in_tile = jnp.where(tile_id == hi, 1.0, 0.0).astype(BF16)
            at_row = row_id == lo
            tokhi_scr[...] += _dot_nt(in_tile, jnp.where(at_row, t_hi, 0.0).astype(BF16))
            toklo_scr[...] += _dot_nt(in_tile, jnp.where(at_row, t_lo, 0.0).astype(BF16))

    @pl.when((phase == 1) & (b == pl.num_programs(1) - 1))
    def _():
        tok_ref[...] = (tokhi_scr[...] * 256.0 + toklo_scr[...]).astype(jnp.int32)


def _plan(route):
    t = route.shape[1]
    assert t <= 256 * 256, "token ids are carried as two base-256 digits"
    nt_pad = -(-_n_expert_tiles(t) // LANES) * LANES
    col = pltpu.VMEM((N_EXPERTS, 1), F32)
    table = pltpu.VMEM((nt_pad, TME), F32)
    return pl.pallas_call(
        functools.partial(_plan_body, nt_pad=nt_pad),
        grid=(2, t // TP),
        in_specs=[pl.BlockSpec((8, TP), lambda ph, b: (0, b))],
        out_specs=[pl.BlockSpec((8, TP), lambda ph, b: (0, b * ph)),
                   pl.BlockSpec((8, nt_pad), lambda ph, b: (0, 0)),
                   pl.BlockSpec((nt_pad, TME), lambda ph, b: (0, 0))],
        out_shape=[jax.ShapeDtypeStruct((8, t), jnp.int32), jax.ShapeDtypeStruct((8, nt_pad), jnp.int32),
                   jax.ShapeDtypeStruct((nt_pad, TME), jnp.int32)],
        scratch_shapes=[col, col, col, table, table],
        compiler_params=_params(("arbitrary", "arbitrary")),
        name="plan",
    )(route)


def _experts_body(texp_ref, tvalid_ref, xs_ref, wg_ref, wu_ref, wd_ref, ys_ref, wgb_scr, wub_scr, wdb_scr):
    n = pl.program_id(0)

    @pl.when((n == 0) | (texp_ref[n] != texp_ref[jnp.maximum(n - 1, 0)]))
    def _():
        wgb_scr[...] = wg_ref[...].astype(BF16)
        wub_scr[...] = wu_ref[...].astype(BF16)
        wdb_scr[...] = wd_ref[...].astype(BF16)

    @pl.when(tvalid_ref[n] > 0)
    def _():
        xb = xs_ref[...].astype(BF16)
        gate = jnp.dot(xb, wgb_scr[...], preferred_element_type=F32)
        up = jnp.dot(xb, wub_scr[...], preferred_element_type=F32)
        he = (gate * jax.nn.sigmoid(gate) * up).astype(BF16)
        ys_ref[...] = jnp.dot(he, wdb_scr[...], preferred_element_type=F32)

    @pl.when(tvalid_ref[n] == 0)
    def _():
        ys_ref[...] = jnp.zeros_like(ys_ref)


def _experts(texp, tvalid, xs, wg, wu, wd):
    nt = texp.shape[0]
    weight = lambda shape: pl.BlockSpec((None,) + shape, lambda n, te, tv: (te[n], 0, 0))
    return pl.pallas_call(
        _experts_body,
        grid_spec=pltpu.PrefetchScalarGridSpec(
            num_scalar_prefetch=2,
            grid=(nt,),
            in_specs=[pl.BlockSpec((TME, D_MODEL), lambda n, te, tv: (jnp.where(tv[n] > 0, n, 0), 0)),
                      weight((D_MODEL, D_EXPERT)), weight((D_MODEL, D_EXPERT)), weight((D_EXPERT, D_MODEL))],
            out_specs=pl.BlockSpec((TME, D_MODEL), lambda n, te, tv: (n, 0)),
            scratch_shapes=[pltpu.VMEM((D_MODEL, D_EXPERT), BF16), pltpu.VMEM((D_MODEL, D_EXPERT), BF16),
                            pltpu.VMEM((D_EXPERT, D_MODEL), BF16)],
        ),
        out_shape=jax.ShapeDtypeStruct((nt * TME, D_MODEL), F32),
        compiler_params=_params(("arbitrary",)),
        name="experts",
    )(texp, tvalid, xs, wg, wu, wd)


def _rowgather(table, idx):
    n_rows, width = idx.shape[0], table.shape[1]
    n_workers = SC_CORES * SC_SUBCORES
    per_worker = n_rows // n_workers
    assert per_worker * n_workers == n_rows and per_worker % SC_ROWS == 0
    mesh = plsc.VectorSubcoreMesh(core_axis_name="c", subcore_axis_name="s", num_cores=SC_CORES,
                                  num_subcores=SC_SUBCORES)

    chunks = per_worker // SC_ROWS
    assert chunks % 2 == 0
    buf = lambda: [pltpu.VMEM((SC_ROWS,), jnp.int32), pltpu.VMEM((SC_ROWS, width), table.dtype),
                   pltpu.SemaphoreType.DMA]

    @functools.partial(pl.kernel, mesh=mesh, out_type=jax.ShapeDtypeStruct((n_rows, width), table.dtype),
                       scratch_types=buf() + buf(), name="rowgather")
    def gather(table_hbm, idx_hbm, out_hbm, idx_a, rows_a, sem_a, idx_b, rows_b, sem_b):
        worker = lax.axis_index("s") * SC_CORES + lax.axis_index("c")
        base = worker * per_worker

        def fetch(c, idx_v, rows_v, sem):
            pltpu.sync_copy(idx_hbm.at[pl.ds(base + c * SC_ROWS, SC_ROWS)], idx_v)
            pltpu.async_copy(table_hbm.at[idx_v], rows_v, sem)

        def flush(c, idx_v, rows_v, sem):
            pltpu.make_async_copy(table_hbm.at[idx_v], rows_v, sem).wait()
            pltpu.sync_copy(rows_v, out_hbm.at[pl.ds(base + c * SC_ROWS, SC_ROWS)])

        fetch(0, idx_a, rows_a, sem_a)

        @pl.loop(0, chunks, step=2)
        def _(c):
            fetch(c + 1, idx_b, rows_b, sem_b)
            flush(c, idx_a, rows_a, sem_a)

            @pl.when(c + 2 < chunks)
            def _():
                fetch(c + 2, idx_a, rows_a, sem_a)

            flush(c + 1, idx_b, rows_b, sem_b)

    return gather(table, idx)


def _combine_body(g1_ref, g2_ref, x2_ref, wcol_ref, gf_ref, out_ref):
    w = wcol_ref[...]
    y = w[:, 2:3] * g1_ref[...] + w[:, 3:4] * g2_ref[...]
    out_ref[...] = _rms(x2_ref[...] + y, gf_ref[...])


def _combine(g, x2, wcol, gf):
    t = x2.shape[0]
    steps = t // TF
    return pl.pallas_call(
        _combine_body,
        grid=(steps,),
        in_specs=[pl.BlockSpec((TF, D_MODEL), lambda i: (i, 0)),
                  pl.BlockSpec((TF, D_MODEL), lambda i: (i + steps, 0)),
                  pl.BlockSpec((TF, D_MODEL), lambda i: (i, 0)),
                  pl.BlockSpec((TF, LANES), lambda i: (i, 0)),
                  _resident((1, D_MODEL))],
        out_specs=pl.BlockSpec((TF, D_MODEL), lambda i: (i, 0)),
        out_shape=jax.ShapeDtypeStruct((t, D_MODEL), F32),
        compiler_params=_params(("parallel",)),
        name="combine",
    )(g, g, x2, wcol, gf)


def _layer(x, mem, g_mix, w_in, b_ml_i, b_ml_f, b_fx_f, b_gate_ml, b_gate_fx, g_ml_head, w_proj_ml, w_proj_fx,
           w_out, g_xq, g_xmem, w_xq, w_xkv, w_xo, g_moe, w_rg, b_rg, w_re, b_re, w_gate, w_up, w_down):
    nb, seq, d = x.shape
    t = nb * seq
    row = lambda v: v.reshape(1, -1).astype(F32)

    o = 0
    parts = {}
    for name, width in (("ml_q", 512), ("ml_k", 512), ("ml_v", 1024), ("ml_o", 1024), ("ml_i", 4), ("ml_f", 4),
                        ("fx_q", 1024), ("fx_k", 1024), ("fx_v", 1024), ("fx_f", 8), ("gt_ml", 1024),
                        ("gt_fx", 1024)):
        parts[name] = w_in[:, o:o + width]
        o += width
    w_main = jnp.concatenate(
        [parts["ml_q"], parts["ml_k"] * (ML_QK_DIM ** -0.5), parts["ml_o"],
         parts["fx_q"] * (FX_HEAD_DIM ** -0.5 * LOG2E), parts["fx_k"], parts["gt_ml"], parts["gt_fx"]],
        axis=1).astype(BF16)
    w_vt = jnp.concatenate([parts["ml_v"], parts["fx_v"]], axis=1).T.astype(BF16)
    w_gates_t = jnp.concatenate([parts["ml_i"], parts["ml_f"], parts["fx_f"]], axis=1).T
    gate_bias = jnp.concatenate([b_ml_i, b_ml_f, b_fx_f]).reshape(N_GATES, 1).astype(F32)

    x2d = x.reshape(t, d)
    z, vt, gates_t = _inproj(x2d, row(g_mix), w_main, w_vt, w_gates_t)
    gt3 = gates_t.reshape(N_GATES, nb, seq).transpose(1, 0, 2)
    rows, cols, caug = _gateprep(gt3, gate_bias)
    z3 = z.reshape(nb, seq, Z_W)
    vt4 = vt.reshape(nb, seq // TQ, VT_W, TQ)
    y_ml = _mlstm(z3, vt4, cols, rows, row(g_ml_head))
    y_fx = _fox(z3, vt4, caug, rows)
    x1 = _merge(x2d, y_ml.reshape(t, d), y_fx.reshape(t, d), z, row(b_gate_ml), row(b_gate_fx),
                w_proj_ml.astype(BF16), w_proj_fx.astype(BF16), w_out.astype(BF16))

    n_mem = mem.shape[1]
    kv = _memkv(mem.reshape(nb * n_mem, d), row(g_xmem), w_xkv.astype(BF16))
    w_router_t = jnp.concatenate([w_re, w_rg, jnp.zeros((d, LANES - N_EXPERTS - N_GROUPS), F32)], axis=1).T
    b_router_t = jnp.concatenate([b_re, b_rg, jnp.zeros((LANES - N_EXPERTS - N_GROUPS,), F32)]).reshape(LANES, 1)
    x2, h3, route, wcol = _xattn(x1, kv.reshape(nb, n_mem, 2 * d), row(g_xq), w_xq.astype(BF16),
                                 w_xo.astype(BF16), row(g_moe), w_router_t, b_router_t, seq)
    pos, tinfo, tok = _plan(route)
    nt = _n_expert_tiles(t)
    texp, tvalid = tinfo[0, :nt], tinfo[1, :nt]
    xs = _rowgather(h3, tok[:nt].reshape(-1))
    ys = _experts(texp, tvalid, xs, w_gate, w_up, w_down)
    return x2, ys, pos, wcol


def kernel(x, mem, g_mix, w_in, b_ml_i, b_ml_f, b_fx_f, b_gate_ml, b_gate_fx, g_ml_head, w_proj_ml, w_proj_fx, w_out, g_xq, g_xmem, w_xq, w_xkv, w_xo, g_moe, w_rg, b_rg, w_re, b_re, w_gate, w_up, w_down, g_final):
    nb, seq, d = x.shape
    depth = g_mix.shape[0]
    assert depth == 1, "the final rmsnorm is fused into the (single) layer's combine kernel"
    x2, ys, pos, wcol = _layer(
        x, mem, g_mix[0], w_in[0], b_ml_i[0], b_ml_f[0], b_fx_f[0], b_gate_ml[0], b_gate_fx[0], g_ml_head[0],
        w_proj_ml[0], w_proj_fx[0], w_out[0], g_xq[0], g_xmem[0], w_xq[0], w_xkv[0], w_xo[0], g_moe[0],
        w_rg[0], b_rg[0], w_re[0], b_re[0], w_gate[0], w_up[0], w_down[0])
    g = _rowgather(ys, pos[0:2].reshape(-1))
    out = _combine(g, x2, wcol, g_final.reshape(1, d).astype(F32))
    return out.reshape(nb, seq, d)
```

```python
import functools

import jax
import jax.numpy as jnp
from jax import lax
from jax.experimental import pallas as pl
from jax.experimental.pallas import tpu as pltpu
from jax.experimental.pallas import tpu_sc as plsc

F32 = jnp.float32
BF16 = jnp.bfloat16

D_MODEL = 1024
EPS = 1e-6
ML_HEADS = 4
ML_QK_DIM = 128
ML_V_DIM = 256
FX_HEADS = 8
FX_HEAD_DIM = 128
XA_HEADS = 4
XA_HEAD_DIM = 256
N_GROUPS = 4
EXPERTS_PER_GROUP = 8
N_EXPERTS = 32
D_EXPERT = 512

LANES = 128
N_GATES = 16
LOG2E = 1.4426950408889634
Z_W = 6144
Z_ML_O, Z_FX_Q, Z_FX_K, Z_GT_ML, Z_GT_FX = 1, 2, 3, 4, 5
VT_W = 2048

VMEM_LIMIT = 56 * 1024 * 1024

IN_CHUNK = 1024
TQ = 256
MXU_LOOKAHEAD = 4
TM = 512
TME = 256
TP = 512
TF = 512
SC_CORES, SC_SUBCORES = 2, 16
SC_ROWS = 32


def _params(sem, flags=None):
    return pltpu.CompilerParams(dimension_semantics=sem, vmem_limit_bytes=VMEM_LIMIT, flags=flags)


def _rms(x, g):
    return x * lax.rsqrt(jnp.mean(x * x, axis=-1, keepdims=True) + EPS) * g


def _dot_nt(a, b, **kw):
    return lax.dot_general(a, b, (((1,), (1,)), ((), ())), preferred_element_type=F32, **kw)


def _resident(shape):
    zeros = (0,) * len(shape)
    return pl.BlockSpec(shape, lambda *_: zeros, pipeline_mode=pl.Buffered(1))


def _inproj_body(x_ref, g_ref, w_ref, wvt_ref, wgt_ref, z_ref, vt_ref, gt_ref):
    h = _rms(x_ref[...], g_ref[...])
    hb = h.astype(BF16)
    for c in range(Z_W // IN_CHUNK):
        sl = slice(c * IN_CHUNK, (c + 1) * IN_CHUNK)
        z_ref[:, sl] = jnp.dot(hb, w_ref[:, sl], preferred_element_type=F32).astype(BF16)
    for c in range(VT_W // IN_CHUNK):
        sl = slice(c * IN_CHUNK, (c + 1) * IN_CHUNK)
        vt_ref[sl, :] = _dot_nt(wvt_ref[sl, :], hb).astype(BF16)
    gt_ref[...] = _dot_nt(wgt_ref[...], h, precision=lax.Precision.HIGHEST)


def _inproj(x2d, g, w, wvt, wgt):
    t = x2d.shape[0]
    return pl.pallas_call(
        _inproj_body,
        grid=(t // TQ,),
        in_specs=[
            pl.BlockSpec((TQ, D_MODEL), lambda i: (i, 0)),
            _resident((1, D_MODEL)),
            _resident((D_MODEL, Z_W)),
            _resident((VT_W, D_MODEL)),
            _resident((N_GATES, D_MODEL)),
        ],
        out_specs=[
            pl.BlockSpec((TQ, Z_W), lambda i: (i, 0)),
            pl.BlockSpec((None, VT_W, TQ), lambda i: (i, 0, 0)),
            pl.BlockSpec((N_GATES, TQ), lambda i: (0, i)),
        ],
        out_shape=[jax.ShapeDtypeStruct((t, Z_W), BF16), jax.ShapeDtypeStruct((t // TQ, VT_W, TQ), BF16),
                   jax.ShapeDtypeStruct((N_GATES, t), F32)],
        compiler_params=_params(("parallel",)),
        name="inproj",
    )(x2d, g, w, wvt, wgt)


def _scan_lanes(x, op, identity):
    n = x.shape[-1]
    idx = lax.broadcasted_iota(jnp.int32, x.shape, 1)
    s = 1
    while s < n:
        shifted = pltpu.roll(x, s, axis=1)
        x = op(x, jnp.where(idx >= s, shifted, identity))
        s *= 2
    return x


def _log_sigmoid(x):
    return jnp.minimum(x, 0.0) - jnp.log1p(jnp.exp(-jnp.abs(x)))


def _gateprep_body(gt_ref, bias_ref, rows_ref, cols_ref, caug_ref):
    g = gt_ref[...] + bias_ref[...]
    s = g.shape[1]
    cs = _scan_lanes(_log_sigmoid(g), jnp.add, 0.0)
    b = cs[4:8]
    c2 = cs[8:16] * LOG2E
    a = g[0:4] - b
    m = _scan_lanes(jnp.concatenate([a, a], axis=0), jnp.maximum, -jnp.inf)[0:4]
    rows_ref[...] = jnp.concatenate([m * LOG2E, b + m, c2], axis=0)
    cols_ref[...] = jnp.concatenate([a * LOG2E, jnp.zeros((LANES - ML_HEADS, s), F32)], axis=0).T
    hi = c2.astype(BF16).astype(F32)
    r1 = c2 - hi
    mid = r1.astype(BF16).astype(F32)
    lo = r1 - mid
    aug = jnp.concatenate([-hi, -mid, -lo, jnp.zeros((LANES - 3 * FX_HEADS, s), F32)], axis=0)
    caug_ref[...] = aug.T.astype(BF16)


def _gateprep(gt3, bias):
    nb, _, s = gt3.shape
    return pl.pallas_call(
        _gateprep_body,
        grid=(nb,),
        in_specs=[
            pl.BlockSpec((None, N_GATES, s), lambda b: (b, 0, 0)),
            _resident((N_GATES, 1)),
        ],
        out_specs=[
            pl.BlockSpec((None, N_GATES, s), lambda b: (b, 0, 0)),
            pl.BlockSpec((None, s, LANES), lambda b: (b, 0, 0)),
            pl.BlockSpec((None, s, LANES), lambda b: (b, 0, 0)),
        ],
        out_shape=[jax.ShapeDtypeStruct((nb, N_GATES, s), F32), jax.ShapeDtypeStruct((nb, s, LANES), F32),
                   jax.ShapeDtypeStruct((nb, s, LANES), BF16)],
        compiler_params=_params(("parallel",)),
        name="gateprep",
    )(gt3, bias)


def _causal_mask_t():
    s = lax.broadcasted_iota(jnp.int32, (TQ, TQ), 0)
    t = lax.broadcasted_iota(jnp.int32, (TQ, TQ), 1)
    return s <= t


def _mlstm_body(q_ref, k_ref, vt_ref, o_ref, cols_ref, rows_ref, gh_ref, y_ref, num_scr, den_scr):
    i = pl.program_id(1)
    rows = rows_ref[...]
    num_scr[...] = jnp.zeros_like(num_scr)
    den_scr[...] = jnp.zeros_like(den_scr)

    def scores(j, h):
        ks = pl.ds(pl.multiple_of(j * TQ, TQ), TQ)
        qk = slice(h * ML_QK_DIM, (h + 1) * ML_QK_DIM)
        return _dot_nt(k_ref[ks, qk], q_ref[:, qk])

    def update(j, h, s, mask):
        ks = pl.ds(pl.multiple_of(j * TQ, TQ), TQ)
        vv = slice(h * ML_V_DIM, (h + 1) * ML_V_DIM)
        w = jnp.exp2(cols_ref[ks, h:h + 1] - rows[h:h + 1])
        if mask is not None:
            w = jnp.where(mask, w, 0.0)
        s = s * w
        den_scr[h] += jnp.sum(s, axis=0, keepdims=True)
        num_scr[h] += jnp.dot(vt_ref[j, vv, :], s.astype(BF16), preferred_element_type=F32)

    def key_tile(j, mask):
        s = {h: scores(j, h) for h in range(min(MXU_LOOKAHEAD, ML_HEADS))}
        for h in range(ML_HEADS):
            if h + MXU_LOOKAHEAD < ML_HEADS:
                s[h + MXU_LOOKAHEAD] = scores(j, h + MXU_LOOKAHEAD)
            update(j, h, s.pop(h), mask)

    def body(j, carry):
        key_tile(j, None)
        return carry

    lax.fori_loop(0, i, body, 0)
    key_tile(i, _causal_mask_t())
    for h in range(ML_HEADS):
        vv = slice(h * ML_V_DIM, (h + 1) * ML_V_DIM)
        floor = jnp.exp(-rows[ML_HEADS + h:ML_HEADS + h + 1])
        hh = num_scr[h] * (1.0 / jnp.maximum(jnp.abs(den_scr[h]), floor))
        yt = hh * lax.rsqrt(jnp.mean(hh * hh, axis=0, keepdims=True) + EPS)
        y = yt.T * gh_ref[:, vv]
        y_ref[:, vv] = (y * jax.nn.sigmoid(o_ref[:, vv].astype(F32))).astype(BF16)


def _mlstm(z3, vt4, cols3, rows3, g_head):
    nb, s, _ = z3.shape
    nq = s // TQ
    return pl.pallas_call(
        _mlstm_body,
        grid=(nb, nq),
        in_specs=[
            pl.BlockSpec((None, TQ, ML_HEADS * ML_QK_DIM), lambda b, i: (b, i, 0)),
            pl.BlockSpec((None, s, ML_HEADS * ML_QK_DIM), lambda b, i: (b, 0, 1)),
            pl.BlockSpec((None, nq, D_MODEL, TQ), lambda b, i: (b, 0, 0, 0)),
            pl.BlockSpec((None, TQ, D_MODEL), lambda b, i: (b, i, Z_ML_O)),
            pl.BlockSpec((None, s, LANES), lambda b, i: (b, 0, 0)),
            pl.BlockSpec((None, N_GATES, TQ), lambda b, i: (b, 0, i)),
            _resident((1, D_MODEL)),
        ],
        out_specs=pl.BlockSpec((None, TQ, D_MODEL), lambda b, i: (b, i, 0)),
        out_shape=jax.ShapeDtypeStruct((nb, s, D_MODEL), BF16),
        scratch_shapes=[pltpu.VMEM((ML_HEADS, ML_V_DIM, TQ), F32), pltpu.VMEM((ML_HEADS, 1, TQ), F32)],
        compiler_params=_params(("parallel", "arbitrary")),
        name="mlstm",
    )(z3, z3, vt4, z3, cols3, rows3, g_head)


def _fox_body(q_ref, k_ref, vt_ref, caug_ref, rows_ref, y_ref, qa_scr, acc_scr, m_scr, l_scr):
    i = pl.program_id(1)
    rows = rows_ref[...]
    lane = lax.broadcasted_iota(jnp.int32, (TQ, LANES), 1)
    for h in range(FX_HEADS):
        hd = slice(h * FX_HEAD_DIM, (h + 1) * FX_HEAD_DIM)
        ones = jnp.where((lane < 3 * FX_HEADS) & (lane % FX_HEADS == h), 1.0, 0.0).astype(BF16)
        qa_scr[h] = jnp.concatenate([q_ref[:, hd], ones], axis=1)
    m_scr[...] = jnp.full_like(m_scr, -jnp.inf)
    l_scr[...] = jnp.zeros_like(l_scr)
    acc_scr[...] = jnp.zeros_like(acc_scr)

    def scores(j, h):
        ks = pl.ds(pl.multiple_of(j * TQ, TQ), TQ)
        hd = slice(h * FX_HEAD_DIM, (h + 1) * FX_HEAD_DIM)
        k_aug = jnp.concatenate([k_ref[ks, hd], caug_ref[ks, :]], axis=1)
        return _dot_nt(k_aug, qa_scr[h])

    def update(j, h, u, mask):
        hd = slice(h * FX_HEAD_DIM, (h + 1) * FX_HEAD_DIM)
        if mask is not None:
            u = jnp.where(mask, u, -jnp.inf)
        c_row = rows[2 * ML_HEADS + h:2 * ML_HEADS + h + 1]
        m_prev = m_scr[h]
        m_new = jnp.maximum(m_prev, jnp.max(u, axis=0, keepdims=True) + c_row)
        p = jnp.exp2(u - (m_new - c_row))
        alpha = jnp.exp2(m_prev - m_new)
        l_scr[h] = alpha * l_scr[h] + jnp.sum(p, axis=0, keepdims=True)
        acc_scr[h] = alpha * acc_scr[h] + jnp.dot(vt_ref[j, hd, :], p.astype(BF16), preferred_element_type=F32)
        m_scr[h] = m_new

    def key_tile(j, mask):
        u = {h: scores(j, h) for h in range(MXU_LOOKAHEAD)}
        for h in range(FX_HEADS):
            if h + MXU_LOOKAHEAD < FX_HEADS:
                u[h + MXU_LOOKAHEAD] = scores(j, h + MXU_LOOKAHEAD)
            update(j, h, u.pop(h), mask)

    def body(j, carry):
        key_tile(j, None)
        return carry

    lax.fori_loop(0, i, body, 0)
    key_tile(i, _causal_mask_t())
    for h in range(FX_HEADS):
        hd = slice(h * FX_HEAD_DIM, (h + 1) * FX_HEAD_DIM)
        y_ref[:, hd] = (acc_scr[h] * (1.0 / l_scr[h])).T.astype(BF16)


def _fox(z3, vt4, caug3, rows3):
    nb, s, _ = z3.shape
    nq = s // TQ
    return pl.pallas_call(
        _fox_body,
        grid=(nb, nq),
        in_specs=[
            pl.BlockSpec((None, TQ, D_MODEL), lambda b, i: (b, i, Z_FX_Q)),
            pl.BlockSpec((None, s, D_MODEL), lambda b, i: (b, 0, Z_FX_K)),
            pl.BlockSpec((None, nq, D_MODEL, TQ), lambda b, i: (b, 0, 1, 0)),
            pl.BlockSpec((None, s, LANES), lambda b, i: (b, 0, 0)),
            pl.BlockSpec((None, N_GATES, TQ), lambda b, i: (b, 0, i)),
        ],
        out_specs=pl.BlockSpec((None, TQ, D_MODEL), lambda b, i: (b, i, 0)),
        out_shape=jax.ShapeDtypeStruct((nb, s, D_MODEL), BF16),
        scratch_shapes=[pltpu.VMEM((FX_HEADS, TQ, 2 * FX_HEAD_DIM), BF16),
                        pltpu.VMEM((FX_HEADS, FX_HEAD_DIM, TQ), F32),
                        pltpu.VMEM((FX_HEADS, 1, TQ), F32), pltpu.VMEM((FX_HEADS, 1, TQ), F32)],
        compiler_params=_params(("parallel", "arbitrary")),
        name="fox",
    )(z3, z3, vt4, caug3, rows3)


def _merge_body(x_ref, yml_ref, yfx_ref, gml_ref, gfx_ref, bml_ref, bfx_ref, wml_ref, wfx_ref, wout_ref, x1_ref):
    p_ml = jnp.dot(yml_ref[...], wml_ref[...], preferred_element_type=F32)
    p_fx = jnp.dot(yfx_ref[...], wfx_ref[...], preferred_element_type=F32)
    merged = (jax.nn.sigmoid(gml_ref[...].astype(F32) + bml_ref[...]) * p_ml
              + jax.nn.sigmoid(gfx_ref[...].astype(F32) + bfx_ref[...]) * p_fx)
    x1_ref[...] = x_ref[...] + jnp.dot(merged.astype(BF16), wout_ref[...], preferred_element_type=F32)


def _merge(x2d, yml, yfx, z2d, bml, bfx, wml, wfx, wout):
    t = x2d.shape[0]
    tile = lambda col: pl.BlockSpec((TM, D_MODEL), lambda i, col=col: (i, col))
    return pl.pallas_call(
        _merge_body,
        grid=(t // TM,),
        in_specs=[tile(0), tile(0), tile(0), tile(Z_GT_ML), tile(Z_GT_FX),
                  _resident((1, D_MODEL)), _resident((1, D_MODEL)),
                  _resident((D_MODEL, D_MODEL)), _resident((D_MODEL, D_MODEL)), _resident((D_MODEL, D_MODEL))],
        out_specs=tile(0),
        out_shape=jax.ShapeDtypeStruct((t, D_MODEL), F32),
        compiler_params=_params(("parallel",)),
        name="merge",
    )(x2d, yml, yfx, z2d, z2d, bml, bfx, wml, wfx, wout)


def _memkv_body(m_ref, g_ref, w_ref, kv_ref):
    hb = _rms(m_ref[...], g_ref[...]).astype(BF16)
    kv_ref[...] = jnp.dot(hb, w_ref[...], preferred_element_type=F32).astype(BF16)


def _memkv(mem2d, g, w):
    t = mem2d.shape[0]
    return pl.pallas_call(
        _memkv_body,
        grid=(t // TM,),
        in_specs=[pl.BlockSpec((TM, D_MODEL), lambda i: (i, 0)), _resident((1, D_MODEL)),
                  _resident((D_MODEL, 2 * D_MODEL))],
        out_specs=pl.BlockSpec((TM, 2 * D_MODEL), lambda i: (i, 0)),
        out_shape=jax.ShapeDtypeStruct((t, 2 * D_MODEL), BF16),
        compiler_params=_params(("parallel",)),
        name="memkv",
    )(mem2d, g, w)


def _route_t(lg_t):
    tm = lg_t.shape[1]
    ninf = -jnp.inf
    big = jnp.float32(LANES)
    gid = lax.broadcasted_iota(jnp.int32, (8, tm), 0).astype(F32)
    eid = lax.broadcasted_iota(jnp.int32, (N_EXPERTS, tm), 0).astype(F32)
    gl = jnp.where(gid < N_GROUPS, lg_t[N_EXPERTS:N_EXPERTS + 8], ninf)
    gmax = jnp.max(gl, axis=0, keepdims=True)
    gidx = jnp.min(jnp.where(gl == gmax, gid, big), axis=0, keepdims=True)
    g_p = 1.0 / jnp.sum(jnp.exp(gl - gmax), axis=0, keepdims=True)
    lo = gidx * EXPERTS_PER_GROUP
    el = jnp.where(eid >= lo, jnp.where(eid < lo + EXPERTS_PER_GROUP, lg_t[0:N_EXPERTS], ninf), ninf)
    v1 = jnp.max(el, axis=0, keepdims=True)
    i1 = jnp.min(jnp.where(el == v1, eid, big), axis=0, keepdims=True)
    el2 = jnp.where(eid == i1, ninf, el)
    v2 = jnp.max(el2, axis=0, keepdims=True)
    i2 = jnp.min(jnp.where(el2 == v2, eid, big), axis=0, keepdims=True)
    t = jnp.exp(v2 - v1)
    w1 = g_p / (1.0 + t)
    w2 = w1 * t
    return jnp.concatenate([i1, i2, w1, w2, jnp.zeros((4, tm), F32)], axis=0)


def _xattn_body(x1_ref, kv_ref, gq_ref, wq_ref, wo_ref, gm_ref, wrt_ref, brt_ref, x2_ref, h3_ref, route_ref,
                wcol_ref):
    x1 = x1_ref[...]
    hb = _rms(x1, gq_ref[...]).astype(BF16)
    q = (jnp.dot(hb, wq_ref[...], preferred_element_type=F32) * (XA_HEAD_DIM ** -0.5)).astype(BF16)
    outs = []
    for h in range(XA_HEADS):
        hd = slice(h * XA_HEAD_DIM, (h + 1) * XA_HEAD_DIM)
        vd = slice(D_MODEL + h * XA_HEAD_DIM, D_MODEL + (h + 1) * XA_HEAD_DIM)
        s = _dot_nt(q[:, hd], kv_ref[:, hd])
        p = jnp.exp(s - jnp.max(s, axis=1, keepdims=True))
        p = p * (1.0 / jnp.sum(p, axis=1, keepdims=True))
        outs.append(jnp.dot(p.astype(BF16), kv_ref[:, vd], preferred_element_type=F32).astype(BF16))
    o = jnp.concatenate(outs, axis=1)
    x2 = x1 + jnp.dot(o, wo_ref[...], preferred_element_type=F32)
    x2_ref[...] = x2
    h3 = _rms(x2, gm_ref[...])
    h3_ref[...] = h3
    lg_t = _dot_nt(wrt_ref[...], h3, precision=lax.Precision.HIGHEST) + brt_ref[...]
    route = _route_t(lg_t)
    route_ref[...] = route
    wcol_ref[...] = jnp.concatenate([route, jnp.zeros((LANES - 8, route.shape[1]), F32)], axis=0).T


def _xattn(x1, kv3, gq, wq, wo, gm, wrt, brt, seq):
    t = x1.shape[0]
    per_b = seq // TM
    n_mem = kv3.shape[1]
    tile = pl.BlockSpec((TM, D_MODEL), lambda i: (i, 0))
    return pl.pallas_call(
        _xattn_body,
        grid=(t // TM,),
        in_specs=[tile,
                  pl.BlockSpec((None, n_mem, 2 * D_MODEL), lambda i: (i // per_b, 0, 0)),
                  _resident((1, D_MODEL)), _resident((D_MODEL, D_MODEL)), _resident((D_MODEL, D_MODEL)),
                  _resident((1, D_MODEL)), _resident((LANES, D_MODEL)), _resident((LANES, 1))],
        out_specs=[tile, tile, pl.BlockSpec((8, TM), lambda i: (0, i)), pl.BlockSpec((TM, LANES), lambda i: (i, 0))],
        out_shape=[jax.ShapeDtypeStruct((t, D_MODEL), F32), jax.ShapeDtypeStruct((t, D_MODEL), F32),
                   jax.ShapeDtypeStruct((8, t), F32), jax.ShapeDtypeStruct((t, LANES), F32)],
        compiler_params=_params(("parallel",)),
        name="xattn",
    )(x1, kv3, gq, wq, wo, gm, wrt, brt)


def _n_expert_tiles(n_tokens):
    return 2 * n_tokens // TME + N_EXPERTS


def _plan_body(route_ref, pos_ref, tinfo_ref, tok_ref, cnt_scr, run_scr, start_scr, tokhi_scr, toklo_scr, *, nt_pad,
               n_tokens):
    phase = pl.program_id(0)
    b = pl.program_id(1)
    r = route_ref[...]
    eid = lax.broadcasted_iota(jnp.int32, (N_EXPERTS, TP), 0).astype(F32)
    oh1 = eid == r[0:1]
    oh2 = eid == r[1:2]
    oh = jnp.where(oh1 | oh2, 1.0, 0.0)

    @pl.when((phase == 0) & (b == 0))
    def _():
        cnt_scr[...] = jnp.zeros_like(cnt_scr)

    @pl.when(phase == 0)
    def _():
        cnt_scr[...] += jnp.sum(oh, axis=1, keepdims=True)

    @pl.when((phase == 1) & (b == 0))
    def _():
        cnt = cnt_scr[...]
        n_tiles = jnp.floor((cnt + (TME - 1)) * (1.0 / TME))
        ri = lax.broadcasted_iota(jnp.int32, (N_EXPERTS, N_EXPERTS), 0)
        ci = lax.broadcasted_iota(jnp.int32, (N_EXPERTS, N_EXPERTS), 1)
        lower = jnp.where(ci < ri, 1.0, 0.0)
        start = jnp.dot(lower, jnp.broadcast_to(n_tiles, (N_EXPERTS, LANES)), precision=lax.Precision.HIGHEST,
                        preferred_element_type=F32)[:, 0:1]
        start_scr[...] = start * TME
        run_scr[...] = jnp.zeros_like(run_scr)
        tokhi_scr[...] = jnp.zeros_like(tokhi_scr)
        toklo_scr[...] = jnp.zeros_like(toklo_scr)
        n = lax.broadcasted_iota(jnp.int32, (N_EXPERTS, nt_pad), 1).astype(F32)
        e_n = lax.broadcasted_iota(jnp.int32, (N_EXPERTS, nt_pad), 0).astype(F32)
        owner = jnp.sum(jnp.where(start <= n, 1.0, 0.0), axis=0, keepdims=True) - 1.0
        own = e_n == owner
        cnt_o = jnp.sum(jnp.where(own, cnt, 0.0), axis=0, keepdims=True)
        start_o = jnp.sum(jnp.where(own, start, 0.0), axis=0, keepdims=True)
        valid = jnp.clip(cnt_o - (n[0:1] - start_o) * TME, 0.0, float(TME))
        tinfo_ref[...] = jnp.concatenate([owner, valid, jnp.zeros((6, nt_pad), F32)], axis=0).astype(jnp.int32)

    @pl.when(phase == 1)
    def _():
        ti = lax.broadcasted_iota(jnp.int32, (TP, TP), 0)
        tj = lax.broadcasted_iota(jnp.int32, (TP, TP), 1)
        upper = jnp.where(ti < tj, 1.0, 0.0).astype(BF16)
        before = jnp.dot(oh.astype(BF16), upper, preferred_element_type=F32)
        row = start_scr[...] + run_scr[...] + before
        p1 = jnp.sum(jnp.where(oh1, row, 0.0), axis=0, keepdims=True)
        p2 = jnp.sum(jnp.where(oh2, row, 0.0), axis=0, keepdims=True)
        pos_ref[...] = jnp.concatenate([p1, p2, jnp.zeros((6, TP), F32)], axis=0).astype(jnp.int32)
        run_scr[...] += jnp.sum(oh, axis=1, keepdims=True)

        tid = lax.broadcasted_iota(jnp.int32, (1, TP), 1) + (b * TP + 1)
        t_hi = (tid // 256).astype(F32)
        t_lo = (tid % 256).astype(F32)
        tile_id = lax.broadcasted_iota(jnp.int32, (nt_pad, TP), 0).astype(F32)
        row_id = lax.broadcasted_iota(jnp.int32, (TME, TP), 0).astype(F32)
        for p in (p1, p2):
            hi = jnp.floor(p * (1.0 / TME))
            lo = p - hi * TME
            in_tile = jnp.where(tile_id == hi, 1.0, 0.0).astype(BF16)
            at_row = row_id == lo
            tokhi_scr[...] += _dot_nt(in_tile, jnp.where(at_row, t_hi, 0.0).astype(BF16))
            toklo_scr[...] += _dot_nt(in_tile, jnp.where(at_row, t_lo, 0.0).astype(BF16))

    @pl.when((phase == 1) & (b == pl.num_programs(1) - 1))
    def _():
        hit = (tokhi_scr[...] * 256.0 + toklo_scr[...]).astype(jnp.int32)
        row = (lax.broadcasted_iota(jnp.int32, hit.shape, 0) * TME + lax.broadcasted_iota(jnp.int32, hit.shape, 1))
        tok_ref[...] = jnp.where(hit > 0, hit - 1, row % n_tokens)


def _plan(route):
    t = route.shape[1]
    assert t < 256 * 256, "token id + 1 is carried as two base-256 digits"
    nt_pad = -(-_n_expert_tiles(t) // LANES) * LANES
    col = pltpu.VMEM((N_EXPERTS, 1), F32)
    table = pltpu.VMEM((nt_pad, TME), F32)
    return pl.pallas_call(
        functools.partial(_plan_body, nt_pad=nt_pad, n_tokens=t),
        grid=(2, t // TP),
        in_specs=[pl.BlockSpec((8, TP), lambda ph, b: (0, b))],
        out_specs=[pl.BlockSpec((8, TP), lambda ph, b: (0, b * ph)),
                   pl.BlockSpec((8, nt_pad), lambda ph, b: (0, 0)),
                   pl.BlockSpec((nt_pad, TME), lambda ph, b: (0, 0))],
        out_shape=[jax.ShapeDtypeStruct((8, t), jnp.int32), jax.ShapeDtypeStruct((8, nt_pad), jnp.int32),
                   jax.ShapeDtypeStruct((nt_pad, TME), jnp.int32)],
        scratch_shapes=[col, col, col, table, table],
        compiler_params=_params(("arbitrary", "arbitrary")),
        name="plan",
    )(route)


def _experts_body(texp_ref, tvalid_ref, xs_ref, wg_ref, wu_ref, wd_ref, ys_ref, wgb_scr, wub_scr, wdb_scr):
    n = pl.program_id(0)

    @pl.when((n == 0) | (texp_ref[n] != texp_ref[jnp.maximum(n - 1, 0)]))
    def _():
        wgb_scr[...] = wg_ref[...].astype(BF16)
        wub_scr[...] = wu_ref[...].astype(BF16)
        wdb_scr[...] = wd_ref[...].astype(BF16)

    @pl.when(tvalid_ref[n] > 0)
    def _():
        xb = xs_ref[...].astype(BF16)
        gate = jnp.dot(xb, wgb_scr[...], preferred_element_type=F32)
        up = jnp.dot(xb, wub_scr[...], preferred_element_type=F32)
        he = (gate * jax.nn.sigmoid(gate) * up).astype(BF16)
        ys_ref[...] = jnp.dot(he, wdb_scr[...], preferred_element_type=F32)

    @pl.when(tvalid_ref[n] == 0)
    def _():
        ys_ref[...] = jnp.zeros_like(ys_ref)


def _experts(texp, tvalid, xs, wg, wu, wd):
    nt = texp.shape[0]
    weight = lambda shape: pl.BlockSpec((None,) + shape, lambda n, te, tv: (te[n], 0, 0))
    return pl.pallas_call(
        _experts_body,
        grid_spec=pltpu.PrefetchScalarGridSpec(
            num_scalar_prefetch=2,
            grid=(nt,),
            in_specs=[pl.BlockSpec((TME, D_MODEL), lambda n, te, tv: (jnp.where(tv[n] > 0, n, 0), 0)),
                      weight((D_MODEL, D_EXPERT)), weight((D_MODEL, D_EXPERT)), weight((D_EXPERT, D_MODEL))],
            out_specs=pl.BlockSpec((TME, D_MODEL), lambda n, te, tv: (n, 0)),
            scratch_shapes=[pltpu.VMEM((D_MODEL, D_EXPERT), BF16), pltpu.VMEM((D_MODEL, D_EXPERT), BF16),
                            pltpu.VMEM((D_EXPERT, D_MODEL), BF16)],
        ),
        out_shape=jax.ShapeDtypeStruct((nt * TME, D_MODEL), F32),
        compiler_params=_params(("arbitrary",)),
        name="experts",
    )(texp, tvalid, xs, wg, wu, wd)


def _rowgather(table, idx):
    n_rows, width = idx.shape[0], table.shape[1]
    n_workers = SC_CORES * SC_SUBCORES
    per_worker = n_rows // n_workers
    assert per_worker * n_workers == n_rows and per_worker % SC_ROWS == 0
    mesh = plsc.VectorSubcoreMesh(core_axis_name="c", subcore_axis_name="s", num_cores=SC_CORES,
                                  num_subcores=SC_SUBCORES)

    chunks = per_worker // SC_ROWS
    assert chunks % 2 == 0
    buf = lambda: [pltpu.VMEM((SC_ROWS,), jnp.int32), pltpu.VMEM((SC_ROWS, width), table.dtype),
                   pltpu.SemaphoreType.DMA]

    @functools.partial(pl.kernel, mesh=mesh, out_type=jax.ShapeDtypeStruct((n_rows, width), table.dtype),
                       scratch_types=buf() + buf(), name="rowgather")
    def gather(table_hbm, idx_hbm, out_hbm, idx_a, rows_a, sem_a, idx_b, rows_b, sem_b):
        worker = lax.axis_index("s") * SC_CORES + lax.axis_index("c")
        base = worker * per_worker

        def fetch(c, idx_v, rows_v, sem):
            pltpu.sync_copy(idx_hbm.at[pl.ds(base + c * SC_ROWS, SC_ROWS)], idx_v)
            pltpu.async_copy(table_hbm.at[idx_v], rows_v, sem)

        def flush(c, idx_v, rows_v, sem):
            pltpu.make_async_copy(table_hbm.at[idx_v], rows_v, sem).wait()
            pltpu.sync_copy(rows_v, out_hbm.at[pl.ds(base + c * SC_ROWS, SC_ROWS)])

        fetch(0, idx_a, rows_a, sem_a)

        @pl.loop(0, chunks, step=2)
        def _(c):
            fetch(c + 1, idx_b, rows_b, sem_b)
            flush(c, idx_a, rows_a, sem_a)

            @pl.when(c + 2 < chunks)
            def _():
                fetch(c + 2, idx_a, rows_a, sem_a)

            flush(c + 1, idx_b, rows_b, sem_b)

    return gather(table, idx)


def _combine_body(g1_ref, g2_ref, x2_ref, wcol_ref, gf_ref, out_ref):
    w = wcol_ref[...]
    y = w[:, 2:3] * g1_ref[...] + w[:, 3:4] * g2_ref[...]
    out_ref[...] = _rms(x2_ref[...] + y, gf_ref[...])


def _combine(g, x2, wcol, gf):
    t = x2.shape[0]
    steps = t // TF
    return pl.pallas_call(
        _combine_body,
        grid=(steps,),
        in_specs=[pl.BlockSpec((TF, D_MODEL), lambda i: (i, 0)),
                  pl.BlockSpec((TF, D_MODEL), lambda i: (i + steps, 0)),
                  pl.BlockSpec((TF, D_MODEL), lambda i: (i, 0)),
                  pl.BlockSpec((TF, LANES), lambda i: (i, 0)),
                  _resident((1, D_MODEL))],
        out_specs=pl.BlockSpec((TF, D_MODEL), lambda i: (i, 0)),
        out_shape=jax.ShapeDtypeStruct((t, D_MODEL), F32),
        compiler_params=_params(("parallel",)),
        name="combine",
    )(g, g, x2, wcol, gf)


def _layer(x, mem, g_mix, w_in, b_ml_i, b_ml_f, b_fx_f, b_gate_ml, b_gate_fx, g_ml_head, w_proj_ml, w_proj_fx,
           w_out, g_xq, g_xmem, w_xq, w_xkv, w_xo, g_moe, w_rg, b_rg, w_re, b_re, w_gate, w_up, w_down):
    nb, seq, d = x.shape
    t = nb * seq
    row = lambda v: v.reshape(1, -1).astype(F32)

    o = 0
    parts = {}
    for name, width in (("ml_q", 512), ("ml_k", 512), ("ml_v", 1024), ("ml_o", 1024), ("ml_i", 4), ("ml_f", 4),
                        ("fx_q", 1024), ("fx_k", 1024), ("fx_v", 1024), ("fx_f", 8), ("gt_ml", 1024),
                        ("gt_fx", 1024)):
        parts[name] = w_in[:, o:o + width]
        o += width
    w_main = jnp.concatenate(
        [parts["ml_q"], parts["ml_k"] * (ML_QK_DIM ** -0.5), parts["ml_o"],
         parts["fx_q"] * (FX_HEAD_DIM ** -0.5 * LOG2E), parts["fx_k"], parts["gt_ml"], parts["gt_fx"]],
        axis=1).astype(BF16)
    w_vt = jnp.concatenate([parts["ml_v"], parts["fx_v"]], axis=1).T.astype(BF16)
    w_gates_t = jnp.concatenate([parts["ml_i"], parts["ml_f"], parts["fx_f"]], axis=1).T
    gate_bias = jnp.concatenate([b_ml_i, b_ml_f, b_fx_f]).reshape(N_GATES, 1).astype(F32)

    x2d = x.reshape(t, d)
    z, vt, gates_t = _inproj(x2d, row(g_mix), w_main, w_vt, w_gates_t)
    gt3 = gates_t.reshape(N_GATES, nb, seq).transpose(1, 0, 2)
    rows, cols, caug = _gateprep(gt3, gate_bias)
    z3 = z.reshape(nb, seq, Z_W)
    vt4 = vt.reshape(nb, seq // TQ, VT_W, TQ)
    y_ml = _mlstm(z3, vt4, cols, rows, row(g_ml_head))
    y_fx = _fox(z3, vt4, caug, rows)
    x1 = _merge(x2d, y_ml.reshape(t, d), y_fx.reshape(t, d), z, row(b_gate_ml), row(b_gate_fx),
                w_proj_ml.astype(BF16), w_proj_fx.astype(BF16), w_out.astype(BF16))

    n_mem = mem.shape[1]
    kv = _memkv(mem.reshape(nb * n_mem, d), row(g_xmem), w_xkv.astype(BF16))
    w_router_t = jnp.concatenate([w_re, w_rg, jnp.zeros((d, LANES - N_EXPERTS - N_GROUPS), F32)], axis=1).T
    b_router_t = jnp.concatenate([b_re, b_rg, jnp.zeros((LANES - N_EXPERTS - N_GROUPS,), F32)]).reshape(LANES, 1)
    x2, h3, route, wcol = _xattn(x1, kv.reshape(nb, n_mem, 2 * d), row(g_xq), w_xq.astype(BF16),
                                 w_xo.astype(BF16), row(g_moe), w_router_t, b_router_t, seq)
    pos, tinfo, tok = _plan(route)
    nt = _n_expert_tiles(t)
    texp, tvalid = tinfo[0, :nt], tinfo[1, :nt]
    xs = _rowgather(h3, tok[:nt].reshape(-1))
    ys = _experts(texp, tvalid, xs, w_gate, w_up, w_down)
    return x2, ys, pos, wcol


def kernel(x, mem, g_mix, w_in, b_ml_i, b_ml_f, b_fx_f, b_gate_ml, b_gate_fx, g_ml_head, w_proj_ml, w_proj_fx, w_out, g_xq, g_xmem, w_xq, w_xkv, w_xo, g_moe, w_rg, b_rg, w_re, b_re, w_gate, w_up, w_down, g_final):
    nb, seq, d = x.shape
    depth = g_mix.shape[0]
    assert depth == 1, "the final rmsnorm is fused into the (single) layer's combine kernel"
    x2, ys, pos, wcol = _layer(
        x, mem, g_mix[0], w_in[0], b_ml_i[0], b_ml_f[0], b_fx_f[0], b_gate_ml[0], b_gate_fx[0], g_ml_head[0],
        w_proj_ml[0], w_proj_fx[0], w_out[0], g_xq[0], g_xmem[0], w_xq[0], w_xkv[0], w_xo[0], g_moe[0],
        w_rg[0], b_rg[0], w_re[0], b_re[0], w_gate[0], w_up[0], w_down[0])
    g = _rowgather(ys, pos[0:2].reshape(-1))
    out = _combine(g, x2, wcol, g_final.reshape(1, d).astype(F32))
    return out.reshape(nb, seq, d)
```

```python
import functools

import jax
import jax.numpy as jnp
from jax import lax
from jax.experimental import pallas as pl
from jax.experimental.pallas import tpu as pltpu
from jax.experimental.pallas import tpu_sc as plsc

F32 = jnp.float32
BF16 = jnp.bfloat16

D_MODEL = 1024
EPS = 1e-6
ML_HEADS = 4
ML_QK_DIM = 128
ML_V_DIM = 256
FX_HEADS = 8
FX_HEAD_DIM = 128
XA_HEADS = 4
XA_HEAD_DIM = 256
N_GROUPS = 4
EXPERTS_PER_GROUP = 8
N_EXPERTS = 32
D_EXPERT = 512

LANES = 128
N_GATES = 16
LOG2E = 1.4426950408889634
Z_W = 6144
Z_ML_O, Z_FX_Q, Z_FX_K, Z_GT_ML, Z_GT_FX = 1, 2, 3, 4, 5
VT_W = 2048

VMEM_LIMIT = 56 * 1024 * 1024

IN_CHUNK = 1024
TK = 256
TQ = 512
TIN = 512
KT_PER_Q = TQ // TK
MXU_LOOKAHEAD = 4
TM = 512
TME = 256
TP = 512
TF = 512
SC_CORES, SC_SUBCORES = 2, 16
SC_ROWS = 32


def _params(sem, flags=None):
    return pltpu.CompilerParams(dimension_semantics=sem, vmem_limit_bytes=VMEM_LIMIT, flags=flags)


def _rms(x, g):
    return x * lax.rsqrt(jnp.mean(x * x, axis=-1, keepdims=True) + EPS) * g


def _dot_nt(a, b, **kw):
    return lax.dot_general(a, b, (((1,), (1,)), ((), ())), preferred_element_type=F32, **kw)


def _resident(shape):
    zeros = (0,) * len(shape)
    return pl.BlockSpec(shape, lambda *_: zeros, pipeline_mode=pl.Buffered(1))


def _inproj_body(x_ref, g_ref, w_ref, wvt_ref, wgt_ref, z_ref, vt_ref, gt_ref):
    h = _rms(x_ref[...], g_ref[...])
    hb = h.astype(BF16)
    for c in range(Z_W // IN_CHUNK):
        sl = slice(c * IN_CHUNK, (c + 1) * IN_CHUNK)
        z_ref[:, sl] = jnp.dot(hb, w_ref[:, sl], preferred_element_type=F32).astype(BF16)
    for kt in range(TIN // TK):
        hk = hb[kt * TK:(kt + 1) * TK]
        for c in range(VT_W // IN_CHUNK):
            sl = slice(c * IN_CHUNK, (c + 1) * IN_CHUNK)
            vt_ref[kt, sl, :] = _dot_nt(wvt_ref[sl, :], hk).astype(BF16)
    gt_ref[...] = _dot_nt(wgt_ref[...], h, precision=lax.Precision.HIGHEST)


def _inproj(x2d, g, w, wvt, wgt):
    t = x2d.shape[0]
    return pl.pallas_call(
        _inproj_body,
        grid=(t // TIN,),
        in_specs=[
            pl.BlockSpec((TIN, D_MODEL), lambda i: (i, 0)),
            _resident((1, D_MODEL)),
            _resident((D_MODEL, Z_W)),
            _resident((VT_W, D_MODEL)),
            _resident((N_GATES, D_MODEL)),
        ],
        out_specs=[
            pl.BlockSpec((TIN, Z_W), lambda i: (i, 0)),
            pl.BlockSpec((TIN // TK, VT_W, TK), lambda i: (i, 0, 0)),
            pl.BlockSpec((N_GATES, TIN), lambda i: (0, i)),
        ],
        out_shape=[jax.ShapeDtypeStruct((t, Z_W), BF16), jax.ShapeDtypeStruct((t // TK, VT_W, TK), BF16),
                   jax.ShapeDtypeStruct((N_GATES, t), F32)],
        compiler_params=_params(("parallel",)),
        name="inproj",
    )(x2d, g, w, wvt, wgt)


def _scan_lanes(x, op, identity):
    n = x.shape[-1]
    idx = lax.broadcasted_iota(jnp.int32, x.shape, 1)
    s = 1
    while s < n:
        shifted = pltpu.roll(x, s, axis=1)
        x = op(x, jnp.where(idx >= s, shifted, identity))
        s *= 2
    return x


def _log_sigmoid(x):
    return jnp.minimum(x, 0.0) - jnp.log1p(jnp.exp(-jnp.abs(x)))


def _gateprep_body(gt_ref, bias_ref, rows_ref, cols_ref, caug_ref):
    g = gt_ref[...] + bias_ref[...]
    s = g.shape[1]
    cs = _scan_lanes(_log_sigmoid(g), jnp.add, 0.0)
    b = cs[4:8]
    c2 = cs[8:16] * LOG2E
    a = g[0:4] - b
    m = _scan_lanes(jnp.concatenate([a, a], axis=0), jnp.maximum, -jnp.inf)[0:4]
    rows_ref[...] = jnp.concatenate([m * LOG2E, b + m, c2], axis=0)
    cols_ref[...] = jnp.concatenate([a * LOG2E, jnp.zeros((LANES - ML_HEADS, s), F32)], axis=0).T
    hi = c2.astype(BF16).astype(F32)
    r1 = c2 - hi
    mid = r1.astype(BF16).astype(F32)
    lo = r1 - mid
    aug = jnp.concatenate([-hi, -mid, -lo, jnp.zeros((LANES - 3 * FX_HEADS, s), F32)], axis=0)
    caug_ref[...] = aug.T.astype(BF16)


def _gateprep(gt3, bias):
    nb, _, s = gt3.shape
    return pl.pallas_call(
        _gateprep_body,
        grid=(nb,),
        in_specs=[
            pl.BlockSpec((None, N_GATES, s), lambda b: (b, 0, 0)),
            _resident((N_GATES, 1)),
        ],
        out_specs=[
            pl.BlockSpec((None, N_GATES, s), lambda b: (b, 0, 0)),
            pl.BlockSpec((None, s, LANES), lambda b: (b, 0, 0)),
            pl.BlockSpec((None, s, LANES), lambda b: (b, 0, 0)),
        ],
        out_shape=[jax.ShapeDtypeStruct((nb, N_GATES, s), F32), jax.ShapeDtypeStruct((nb, s, LANES), F32),
                   jax.ShapeDtypeStruct((nb, s, LANES), BF16)],
        compiler_params=_params(("parallel",)),
        name="gateprep",
    )(gt3, bias)


def _causal_mask_t(d):
    s = lax.broadcasted_iota(jnp.int32, (TK, TQ), 0) + d * TK
    t = lax.broadcasted_iota(jnp.int32, (TK, TQ), 1)
    return s <= t


def _mlstm_body(q_ref, k_ref, vt_ref, o_ref, cols_ref, rows_ref, gh_ref, y_ref, num_scr, den_scr):
    i = pl.program_id(1)
    rows = rows_ref[...]
    num_scr[...] = jnp.zeros_like(num_scr)
    den_scr[...] = jnp.zeros_like(den_scr)

    def scores(j, h):
        ks = pl.ds(pl.multiple_of(j * TK, TK), TK)
        qk = slice(h * ML_QK_DIM, (h + 1) * ML_QK_DIM)
        return _dot_nt(k_ref[ks, qk], q_ref[:, qk])

    def update(j, h, s, mask):
        ks = pl.ds(pl.multiple_of(j * TK, TK), TK)
        vv = slice(h * ML_V_DIM, (h + 1) * ML_V_DIM)
        w = jnp.exp2(cols_ref[ks, h:h + 1] - rows[h:h + 1])
        if mask is not None:
            w = jnp.where(mask, w, 0.0)
        s = s * w
        den_scr[h] += jnp.sum(s, axis=0, keepdims=True)
        num_scr[h] += jnp.dot(vt_ref[j, vv, :], s.astype(BF16), preferred_element_type=F32)

    def key_tile(j, mask):
        s = {h: scores(j, h) for h in range(min(MXU_LOOKAHEAD, ML_HEADS))}
        for h in range(ML_HEADS):
            if h + MXU_LOOKAHEAD < ML_HEADS:
                s[h + MXU_LOOKAHEAD] = scores(j, h + MXU_LOOKAHEAD)
            update(j, h, s.pop(h), mask)

    def body(j, carry):
        key_tile(j, None)
        return carry

    lax.fori_loop(0, i * KT_PER_Q, body, 0)
    for d in range(KT_PER_Q):
        key_tile(i * KT_PER_Q + d, _causal_mask_t(d))
    for h in range(ML_HEADS):
        vv = slice(h * ML_V_DIM, (h + 1) * ML_V_DIM)
        floor = jnp.exp(-rows[ML_HEADS + h:ML_HEADS + h + 1])
        hh = num_scr[h] * (1.0 / jnp.maximum(jnp.abs(den_scr[h]), floor))
        yt = hh * lax.rsqrt(jnp.mean(hh * hh, axis=0, keepdims=True) + EPS)
        y = yt.T * gh_ref[:, vv]
        y_ref[:, vv] = (y * jax.nn.sigmoid(o_ref[:, vv].astype(F32))).astype(BF16)


def _mlstm(z3, vt4, cols3, rows3, g_head):
    nb, s, _ = z3.shape
    nq = s // TQ
    return pl.pallas_call(
        _mlstm_body,
        grid=(nb, nq),
        in_specs=[
            pl.BlockSpec((None, TQ, ML_HEADS * ML_QK_DIM), lambda b, i: (b, i, 0)),
            pl.BlockSpec((None, s, ML_HEADS * ML_QK_DIM), lambda b, i: (b, 0, 1)),
            pl.BlockSpec((None, s // TK, D_MODEL, TK), lambda b, i: (b, 0, 0, 0)),
            pl.BlockSpec((None, TQ, D_MODEL), lambda b, i: (b, i, Z_ML_O)),
            pl.BlockSpec((None, s, LANES), lambda b, i: (b, 0, 0)),
            pl.BlockSpec((None, N_GATES, TQ), lambda b, i: (b, 0, i)),
            _resident((1, D_MODEL)),
        ],
        out_specs=pl.BlockSpec((None, TQ, D_MODEL), lambda b, i: (b, i, 0)),
        out_shape=jax.ShapeDtypeStruct((nb, s, D_MODEL), BF16),
        scratch_shapes=[pltpu.VMEM((ML_HEADS, ML_V_DIM, TQ), F32), pltpu.VMEM((ML_HEADS, 1, TQ), F32)],
        compiler_params=_params(("parallel", "arbitrary")),
        name="mlstm",
    )(z3, z3, vt4, z3, cols3, rows3, g_head)


def _fox_body(q_ref, k_ref, vt_ref, caug_ref, rows_ref, y_ref, qa_scr, acc_scr, m_scr, l_scr):
    i = pl.program_id(1)
    rows = rows_ref[...]
    lane = lax.broadcasted_iota(jnp.int32, (TQ, LANES), 1)
    for h in range(FX_HEADS):
        hd = slice(h * FX_HEAD_DIM, (h + 1) * FX_HEAD_DIM)
        ones = jnp.where((lane < 3 * FX_HEADS) & (lane % FX_HEADS == h), 1.0, 0.0).astype(BF16)
        qa_scr[h] = jnp.concatenate([q_ref[:, hd], ones], axis=1)
    m_scr[...] = jnp.full_like(m_scr, -jnp.inf)
    l_scr[...] = jnp.zeros_like(l_scr)
    acc_scr[...] = jnp.zeros_like(acc_scr)

    def scores(j, h):
        ks = pl.ds(pl.multiple_of(j * TK, TK), TK)
        hd = slice(h * FX_HEAD_DIM, (h + 1) * FX_HEAD_DIM)
        k_aug =jnp.concatenate([k_ref[ks, hd], caug_ref[ks, :]], axis=1)
        return _dot_nt(k_aug, qa_scr[h])

    def update(j, h, u, mask):
        hd = slice(h * FX_HEAD_DIM, (h + 1) * FX_HEAD_DIM)
        if mask is not None:
            u = jnp.where(mask, u, -jnp.inf)
        c_row = rows[2 * ML_HEADS + h:2 * ML_HEADS + h + 1]
        m_prev = m_scr[h]
        m_new = jnp.maximum(m_prev, jnp.max(u, axis=0, keepdims=True) + c_row)
        p = jnp.exp2(u - (m_new - c_row))
        alpha = jnp.exp2(m_prev - m_new)
        l_scr[h] = alpha * l_scr[h] + jnp.sum(p, axis=0, keepdims=True)
        acc_scr[h] = alpha * acc_scr[h] + jnp.dot(vt_ref[j, hd, :], p.astype(BF16), preferred_element_type=F32)
        m_scr[h] = m_new

    def key_tile(j, mask):
        u = {h: scores(j, h) for h in range(MXU_LOOKAHEAD)}
        for h in range(FX_HEADS):
            if h + MXU_LOOKAHEAD < FX_HEADS:
                u[h + MXU_LOOKAHEAD] = scores(j, h + MXU_LOOKAHEAD)
            update(j, h, u.pop(h), mask)

    def body(j, carry):
        key_tile(j, None)
        return carry

    lax.fori_loop(0, i * KT_PER_Q, body, 0)
    for d in range(KT_PER_Q):
        key_tile(i * KT_PER_Q + d, _causal_mask_t(d))
    for h in range(FX_HEADS):
        hd = slice(h * FX_HEAD_DIM, (h + 1) * FX_HEAD_DIM)
        y_ref[:, hd] = (acc_scr[h] * (1.0 / l_scr[h])).T.astype(BF16)


def _fox(z3, vt4, caug3, rows3):
    nb, s, _ = z3.shape
    nq = s // TQ
    return pl.pallas_call(
        _fox_body,
        grid=(nb, nq),
        in_specs=[
            pl.BlockSpec((None, TQ, D_MODEL), lambda b, i: (b, i, Z_FX_Q)),
            pl.BlockSpec((None, s, D_MODEL), lambda b, i: (b, 0, Z_FX_K)),
            pl.BlockSpec((None, s // TK, D_MODEL, TK), lambda b, i: (b, 0, 1, 0)),
            pl.BlockSpec((None, s, LANES), lambda b, i: (b, 0, 0)),
            pl.BlockSpec((None, N_GATES, TQ), lambda b, i: (b, 0, i)),
        ],
        out_specs=pl.BlockSpec((None, TQ, D_MODEL), lambda b, i: (b, i, 0)),
        out_shape=jax.ShapeDtypeStruct((nb, s, D_MODEL), BF16),
        scratch_shapes=[pltpu.VMEM((FX_HEADS, TQ, 2 * FX_HEAD_DIM), BF16),
                        pltpu.VMEM((FX_HEADS, FX_HEAD_DIM, TQ), F32),
                        pltpu.VMEM((FX_HEADS, 1, TQ), F32), pltpu.VMEM((FX_HEADS, 1, TQ), F32)],
        compiler_params=_params(("parallel", "arbitrary")),
        name="fox",
    )(z3, z3, vt4, caug3, rows3)


def _merge_body(x_ref, yml_ref, yfx_ref, gml_ref, gfx_ref, bml_ref, bfx_ref, wml_ref, wfx_ref, wout_ref, x1_ref):
    p_ml = jnp.dot(yml_ref[...], wml_ref[...], preferred_element_type=F32)
    p_fx = jnp.dot(yfx_ref[...], wfx_ref[...], preferred_element_type=F32)
    merged = (jax.nn.sigmoid(gml_ref[...].astype(F32) + bml_ref[...]) * p_ml
              + jax.nn.sigmoid(gfx_ref[...].astype(F32) + bfx_ref[...]) * p_fx)
    x1_ref[...] = x_ref[...] + jnp.dot(merged.astype(BF16), wout_ref[...], preferred_element_type=F32)


def _merge(x2d, yml, yfx, z2d, bml, bfx, wml, wfx, wout):
    t = x2d.shape[0]
    tile = lambda col: pl.BlockSpec((TM, D_MODEL), lambda i, col=col: (i, col))
    return pl.pallas_call(
        _merge_body,
        grid=(t // TM,),
        in_specs=[tile(0), tile(0), tile(0), tile(Z_GT_ML), tile(Z_GT_FX),
                  _resident((1, D_MODEL)), _resident((1, D_MODEL)),
                  _resident((D_MODEL, D_MODEL)), _resident((D_MODEL, D_MODEL)), _resident((D_MODEL, D_MODEL))],
        out_specs=tile(0),
        out_shape=jax.ShapeDtypeStruct((t, D_MODEL), F32),
        compiler_params=_params(("parallel",)),
        name="merge",
    )(x2d, yml, yfx, z2d, z2d, bml, bfx, wml, wfx, wout)


def _memkv_body(m_ref, g_ref, w_ref, kv_ref):
    hb = _rms(m_ref[...], g_ref[...]).astype(BF16)
    kv_ref[...] = jnp.dot(hb, w_ref[...], preferred_element_type=F32).astype(BF16)


def _memkv(mem2d, g, w):
    t = mem2d.shape[0]
    return pl.pallas_call(
        _memkv_body,
        grid=(t // TM,),
        in_specs=[pl.BlockSpec((TM, D_MODEL), lambda i: (i, 0)), _resident((1, D_MODEL)),
                  _resident((D_MODEL, 2 * D_MODEL))],
        out_specs=pl.BlockSpec((TM, 2 * D_MODEL), lambda i: (i, 0)),
        out_shape=jax.ShapeDtypeStruct((t, 2 * D_MODEL), BF16),
        compiler_params=_params(("parallel",)),
        name="memkv",
    )(mem2d, g, w)


def _route_t(lg_t):
    tm = lg_t.shape[1]
    ninf = -jnp.inf
    big = jnp.float32(LANES)
    gid = lax.broadcasted_iota(jnp.int32, (8, tm), 0).astype(F32)
    eid = lax.broadcasted_iota(jnp.int32, (N_EXPERTS, tm), 0).astype(F32)
    gl = jnp.where(gid < N_GROUPS, lg_t[N_EXPERTS:N_EXPERTS + 8], ninf)
    gmax = jnp.max(gl, axis=0, keepdims=True)
    gidx = jnp.min(jnp.where(gl == gmax, gid, big), axis=0, keepdims=True)
    g_p = 1.0 / jnp.sum(jnp.exp(gl - gmax), axis=0, keepdims=True)
    lo = gidx * EXPERTS_PER_GROUP
    el = jnp.where(eid >= lo, jnp.where(eid < lo + EXPERTS_PER_GROUP, lg_t[0:N_EXPERTS], ninf), ninf)
    v1 = jnp.max(el, axis=0, keepdims=True)
    i1 = jnp.min(jnp.where(el == v1, eid, big), axis=0, keepdims=True)
    el2 = jnp.where(eid == i1, ninf, el)
    v2 = jnp.max(el2, axis=0, keepdims=True)
    i2 = jnp.min(jnp.where(el2 == v2, eid, big), axis=0, keepdims=True)
    t = jnp.exp(v2 - v1)
    w1 = g_p / (1.0 + t)
    w2 = w1 * t
    return jnp.concatenate([i1, i2, w1, w2, jnp.zeros((4, tm), F32)], axis=0)


def _xattn_body(x1_ref, kv_ref, gq_ref, wq_ref, wo_ref, gm_ref, wrt_ref, brt_ref, x2_ref, h3_ref, route_ref,
                wcol_ref):
    x1 = x1_ref[...]
    hb = _rms(x1, gq_ref[...]).astype(BF16)
    q = (jnp.dot(hb, wq_ref[...], preferred_element_type=F32) * (XA_HEAD_DIM ** -0.5)).astype(BF16)
    outs = []
    for h in range(XA_HEADS):
        hd = slice(h * XA_HEAD_DIM, (h + 1) * XA_HEAD_DIM)
        vd = slice(D_MODEL + h * XA_HEAD_DIM, D_MODEL + (h + 1) * XA_HEAD_DIM)
        s = _dot_nt(q[:, hd], kv_ref[:, hd])
        p = jnp.exp(s - jnp.max(s, axis=1, keepdims=True))
        p = p * (1.0 / jnp.sum(p, axis=1, keepdims=True))
        outs.append(jnp.dot(p.astype(BF16), kv_ref[:, vd], preferred_element_type=F32).astype(BF16))
    o = jnp.concatenate(outs, axis=1)
    x2 = x1 + jnp.dot(o, wo_ref[...], preferred_element_type=F32)
    x2_ref[...] = x2
    h3 = _rms(x2, gm_ref[...])
    h3_ref[...] = h3
    lg_t = _dot_nt(wrt_ref[...], h3, precision=lax.Precision.HIGHEST) + brt_ref[...]
    route = _route_t(lg_t)
    route_ref[...] = route
    wcol_ref[...] = jnp.concatenate([route, jnp.zeros((LANES - 8, route.shape[1]), F32)], axis=0).T


def _xattn(x1, kv3, gq, wq, wo, gm, wrt, brt, seq):
    t = x1.shape[0]
    per_b = seq // TM
    n_mem = kv3.shape[1]
    tile = pl.BlockSpec((TM, D_MODEL), lambda i: (i, 0))
    return pl.pallas_call(
        _xattn_body,
        grid=(t // TM,),
        in_specs=[tile,
                  pl.BlockSpec((None, n_mem, 2 * D_MODEL), lambda i: (i // per_b, 0, 0)),
                  _resident((1, D_MODEL)), _resident((D_MODEL, D_MODEL)), _resident((D_MODEL, D_MODEL)),
                  _resident((1, D_MODEL)), _resident((LANES, D_MODEL)), _resident((LANES, 1))],
        out_specs=[tile, tile, pl.BlockSpec((8, TM), lambda i: (0, i)), pl.BlockSpec((TM, LANES), lambda i: (i, 0))],
        out_shape=[jax.ShapeDtypeStruct((t, D_MODEL), F32), jax.ShapeDtypeStruct((t, D_MODEL), F32),
                   jax.ShapeDtypeStruct((8, t), F32), jax.ShapeDtypeStruct((t, LANES), F32)],
        compiler_params=_params(("parallel",)),
        name="xattn",
    )(x1, kv3, gq, wq, wo, gm, wrt, brt)


def _n_expert_tiles(n_tokens):
    return 2 * n_tokens // TME + N_EXPERTS


def _plan_body(route_ref, pos_ref, tinfo_ref, tok_ref, cnt_scr, run_scr, start_scr, tokhi_scr, toklo_scr, *, nt_pad,
               n_tokens):
    phase = pl.program_id(0)
    b = pl.program_id(1)
    r = route_ref[...]
    eid = lax.broadcasted_iota(jnp.int32, (N_EXPERTS, TP), 0).astype(F32)
    oh1 = eid == r[0:1]
    oh2 = eid == r[1:2]
    oh = jnp.where(oh1 | oh2, 1.0, 0.0)

    @pl.when((phase == 0) & (b == 0))
    def _():
        cnt_scr[...] = jnp.zeros_like(cnt_scr)

    @pl.when(phase == 0)
    def _():
        cnt_scr[...] += jnp.sum(oh, axis=1, keepdims=True)

    @pl.when((phase == 1) & (b == 0))
    def _():
        cnt = cnt_scr[...]
        n_tiles = jnp.floor((cnt + (TME - 1)) * (1.0 / TME))
        ri = lax.broadcasted_iota(jnp.int32, (N_EXPERTS, N_EXPERTS), 0)
        ci = lax.broadcasted_iota(jnp.int32, (N_EXPERTS, N_EXPERTS), 1)
        lower = jnp.where(ci < ri, 1.0, 0.0)
        start = jnp.dot(lower, jnp.broadcast_to(n_tiles, (N_EXPERTS, LANES)), precision=lax.Precision.HIGHEST,
                        preferred_element_type=F32)[:, 0:1]
        start_scr[...] = start * TME
        run_scr[...] = jnp.zeros_like(run_scr)
        tokhi_scr[...] = jnp.zeros_like(tokhi_scr)
        toklo_scr[...] = jnp.zeros_like(toklo_scr)
        n = lax.broadcasted_iota(jnp.int32, (N_EXPERTS, nt_pad), 1).astype(F32)
        e_n = lax.broadcasted_iota(jnp.int32, (N_EXPERTS, nt_pad), 0).astype(F32)
        owner = jnp.sum(jnp.where(start <= n, 1.0, 0.0), axis=0, keepdims=True) - 1.0
        own = e_n == owner
        cnt_o = jnp.sum(jnp.where(own, cnt, 0.0), axis=0, keepdims=True)
        start_o = jnp.sum(jnp.where(own, start, 0.0), axis=0, keepdims=True)
        valid = jnp.clip(cnt_o - (n[0:1] - start_o) * TME, 0.0, float(TME))
        tinfo_ref[...] = jnp.concatenate([owner, valid, jnp.zeros((6, nt_pad), F32)], axis=0).astype(jnp.int32)

    @pl.when(phase == 1)
    def _():
        ti = lax.broadcasted_iota(jnp.int32, (TP, TP), 0)
        tj = lax.broadcasted_iota(jnp.int32, (TP, TP), 1)
        upper = jnp.where(ti < tj, 1.0, 0.0).astype(BF16)
        before = jnp.dot(oh.astype(BF16), upper, preferred_element_type=F32)
        row = start_scr[...] + run_scr[...] + before
        p1 = jnp.sum(jnp.where(oh1, row, 0.0), axis=0, keepdims=True)
        p2 = jnp.sum(jnp.where(oh2, row, 0.0), axis=0, keepdims=True)
        pos_ref[...] = jnp.concatenate([p1, p2, jnp.zeros((6, TP), F32)], axis=0).astype(jnp.int32)
        run_scr[...] += jnp.sum(oh, axis=1, keepdims=True)

        tid = lax.broadcasted_iota(jnp.int32, (1, TP), 1) + (b * TP + 1)
        t_hi = (tid // 256).astype(F32)
        t_lo = (tid % 256).astype(F32)
        tile_id = lax.broadcasted_iota(jnp.int32, (nt_pad, TP), 0).astype(F32)
        row_id = lax.broadcasted_iota(jnp.int32, (TME, TP), 0).astype(F32)
        for p in (p1, p2):
            hi = jnp.floor(p * (1.0 / TME))
            lo = p - hi * TME
            in_tile = jnp.where(tile_id == hi, 1.0, 0.0).astype(BF16)
            at_row = row_id == lo
            tokhi_scr[...] += _dot_nt(in_tile, jnp.where(at_row, t_hi, 0.0).astype(BF16))
            toklo_scr[...] += _dot_nt(in_tile, jnp.where(at_row, t_lo, 0.0).astype(BF16))

    @pl.when((phase == 1) & (b == pl.num_programs(1) - 1))
    def _():
        hit = (tokhi_scr[...] * 256.0 + toklo_scr[...]).astype(jnp.int32)
        row = (lax.broadcasted_iota(jnp.int32, hit.shape, 0) * TME + lax.broadcasted_iota(jnp.int32, hit.shape, 1))
        tok_ref[...] = jnp.where(hit > 0, hit - 1, row % n_tokens)


def _plan(route):
    t = route.shape[1]
    assert t < 256 * 256, "token id + 1 is carried as two base-256 digits"
    nt_pad = -(-_n_expert_tiles(t) // LANES) * LANES
    col = pltpu.VMEM((N_EXPERTS, 1), F32)
    table = pltpu.VMEM((nt_pad, TME), F32)
    return pl.pallas_call(
        functools.partial(_plan_body, nt_pad=nt_pad, n_tokens=t),
        grid=(2, t // TP),
        in_specs=[pl.BlockSpec((8, TP), lambda ph, b: (0, b))],
        out_specs=[pl.BlockSpec((8, TP), lambda ph, b: (0, b * ph)),
                   pl.BlockSpec((8, nt_pad), lambda ph, b: (0, 0)),
                   pl.BlockSpec((nt_pad, TME), lambda ph, b: (0, 0))],
        out_shape=[jax.ShapeDtypeStruct((8, t), jnp.int32), jax.ShapeDtypeStruct((8, nt_pad), jnp.int32),
                   jax.ShapeDtypeStruct((nt_pad, TME), jnp.int32)],
        scratch_shapes=[col, col, col, table, table],
        compiler_params=_params(("arbitrary", "arbitrary")),
        name="plan",
    )(route)


def _experts_body(texp_ref, tvalid_ref, xs_ref, wg_ref, wu_ref, wd_ref, ys_ref, wgb_scr, wub_scr, wdb_scr):
    n = pl.program_id(0)

    @pl.when((n == 0) | (texp_ref[n] != texp_ref[jnp.maximum(n - 1, 0)]))
    def _():
        wgb_scr[...] = wg_ref[...].astype(BF16)
        wub_scr[...] = wu_ref[...].astype(BF16)
        wdb_scr[...] = wd_ref[...].astype(BF16)

    @pl.when(tvalid_ref[n] > 0)
    def _():
        xb = xs_ref[...].astype(BF16)
        gate = jnp.dot(xb, wgb_scr[...], preferred_element_type=F32)
        up = jnp.dot(xb, wub_scr[...], preferred_element_type=F32)
        he = (gate * jax.nn.sigmoid(gate) * up).astype(BF16)
        ys_ref[...] = jnp.dot(he, wdb_scr[...], preferred_element_type=F32)

    @pl.when(tvalid_ref[n] == 0)
    def _():
        ys_ref[...] = jnp.zeros_like(ys_ref)


def _experts(texp, tvalid, xs, wg, wu, wd):
    nt = texp.shape[0]
    weight = lambda shape: pl.BlockSpec((None,) + shape, lambda n, te, tv: (te[n], 0, 0))
    return pl.pallas_call(
        _experts_body,
        grid_spec=pltpu.PrefetchScalarGridSpec(
            num_scalar_prefetch=2,
            grid=(nt,),
            in_specs=[pl.BlockSpec((TME, D_MODEL), lambda n, te, tv: (jnp.where(tv[n] > 0, n, 0), 0)),
                      weight((D_MODEL, D_EXPERT)), weight((D_MODEL, D_EXPERT)), weight((D_EXPERT, D_MODEL))],
            out_specs=pl.BlockSpec((TME, D_MODEL), lambda n, te, tv: (n, 0)),
            scratch_shapes=[pltpu.VMEM((D_MODEL, D_EXPERT), BF16), pltpu.VMEM((D_MODEL, D_EXPERT), BF16),
                            pltpu.VMEM((D_EXPERT, D_MODEL), BF16)],
        ),
        out_shape=jax.ShapeDtypeStruct((nt * TME, D_MODEL), F32),
        compiler_params=_params(("arbitrary",)),
        name="experts",
    )(texp, tvalid, xs, wg, wu, wd)


def _rowgather(table, idx):
    n_rows, width = idx.shape[0], table.shape[1]
    n_workers = SC_CORES * SC_SUBCORES
    per_worker = n_rows // n_workers
    assert per_worker * n_workers == n_rows and per_worker % SC_ROWS == 0
    mesh = plsc.VectorSubcoreMesh(core_axis_name="c", subcore_axis_name="s", num_cores=SC_CORES,
                                  num_subcores=SC_SUBCORES)

    chunks = per_worker // SC_ROWS
    assert chunks % 2 == 0
    buf = lambda: [pltpu.VMEM((SC_ROWS,), jnp.int32), pltpu.VMEM((SC_ROWS, width), table.dtype),
                   pltpu.SemaphoreType.DMA]

    @functools.partial(pl.kernel, mesh=mesh, out_type=jax.ShapeDtypeStruct((n_rows, width), table.dtype),
                       scratch_types=buf() + buf(), name="rowgather")
    def gather(table_hbm, idx_hbm, out_hbm, idx_a, rows_a, sem_a, idx_b, rows_b, sem_b):
        worker = lax.axis_index("s") * SC_CORES + lax.axis_index("c")
        base = worker * per_worker

        def fetch(c, idx_v, rows_v, sem):
            pltpu.sync_copy(idx_hbm.at[pl.ds(base + c * SC_ROWS, SC_ROWS)], idx_v)
            pltpu.async_copy(table_hbm.at[idx_v], rows_v, sem)

        def flush(c, idx_v, rows_v, sem):
            pltpu.make_async_copy(table_hbm.at[idx_v], rows_v, sem).wait()
            pltpu.sync_copy(rows_v, out_hbm.at[pl.ds(base + c * SC_ROWS, SC_ROWS)])

        fetch(0, idx_a, rows_a, sem_a)

        @pl.loop(0, chunks, step=2)
        def _(c):
            fetch(c + 1, idx_b, rows_b, sem_b)
            flush(c, idx_a, rows_a, sem_a)

            @pl.when(c + 2 < chunks)
            def _():
                fetch(c + 2, idx_a, rows_a, sem_a)

            flush(c + 1, idx_b, rows_b, sem_b)

    return gather(table, idx)


def _combine_body(g1_ref, g2_ref, x2_ref, wcol_ref, gf_ref, out_ref):
    w = wcol_ref[...]
    y = w[:, 2:3] * g1_ref[...] + w[:, 3:4] * g2_ref[...]
    out_ref[...] = _rms(x2_ref[...] + y, gf_ref[...])


def _combine(g, x2, wcol, gf):
    t = x2.shape[0]
    steps = t // TF
    return pl.pallas_call(
        _combine_body,
        grid=(steps,),
        in_specs=[pl.BlockSpec((TF, D_MODEL), lambda i: (i, 0)),
                  pl.BlockSpec((TF, D_MODEL), lambda i: (i + steps, 0)),
                  pl.BlockSpec((TF, D_MODEL), lambda i: (i, 0)),
                  pl.BlockSpec((TF, LANES), lambda i: (i, 0)),
                  _resident((1, D_MODEL))],
        out_specs=pl.BlockSpec((TF, D_MODEL), lambda i: (i, 0)),
        out_shape=jax.ShapeDtypeStruct((t, D_MODEL), F32),
        compiler_params=_params(("parallel",)),
        name="combine",
    )(g, g, x2, wcol, gf)


def _layer(x, mem, g_mix, w_in, b_ml_i, b_ml_f, b_fx_f, b_gate_ml, b_gate_fx, g_ml_head, w_proj_ml, w_proj_fx,
           w_out, g_xq, g_xmem, w_xq, w_xkv, w_xo, g_moe, w_rg, b_rg, w_re, b_re, w_gate, w_up, w_down):
    nb, seq, d = x.shape
    t = nb * seq
    row = lambda v: v.reshape(1, -1).astype(F32)

    o = 0
    parts = {}
    for name, width in (("ml_q", 512), ("ml_k", 512), ("ml_v", 1024), ("ml_o", 1024), ("ml_i", 4), ("ml_f", 4),
                        ("fx_q", 1024), ("fx_k", 1024), ("fx_v", 1024), ("fx_f", 8), ("gt_ml", 1024),
                        ("gt_fx", 1024)):
        parts[name] = w_in[:, o:o + width]
        o += width
    w_main = jnp.concatenate(
        [parts["ml_q"], parts["ml_k"] * (ML_QK_DIM ** -0.5), parts["ml_o"],
         parts["fx_q"] * (FX_HEAD_DIM ** -0.5 * LOG2E), parts["fx_k"], parts["gt_ml"], parts["gt_fx"]],
        axis=1).astype(BF16)
    w_vt = jnp.concatenate([parts["ml_v"], parts["fx_v"]], axis=1).T.astype(BF16)
    w_gates_t = jnp.concatenate([parts["ml_i"], parts["ml_f"], parts["fx_f"]], axis=1).T
    gate_bias = jnp.concatenate([b_ml_i, b_ml_f, b_fx_f]).reshape(N_GATES, 1).astype(F32)

    x2d = x.reshape(t, d)
    z, vt, gates_t = _inproj(x2d, row(g_mix), w_main, w_vt, w_gates_t)
    gt3 = gates_t.reshape(N_GATES, nb, seq).transpose(1, 0, 2)
    rows, cols, caug = _gateprep(gt3, gate_bias)
    z3 = z.reshape(nb, seq, Z_W)
    vt4 = vt.reshape(nb, seq // TK, VT_W, TK)
    y_ml = _mlstm(z3, vt4, cols, rows, row(g_ml_head))
    y_fx = _fox(z3, vt4, caug, rows)
    x1 = _merge(x2d, y_ml.reshape(t, d), y_fx.reshape(t, d), z, row(b_gate_ml), row(b_gate_fx),
                w_proj_ml.astype(BF16), w_proj_fx.astype(BF16), w_out.astype(BF16))

    n_mem = mem.shape[1]
    kv = _memkv(mem.reshape(nb * n_mem, d), row(g_xmem), w_xkv.astype(BF16))
    w_router_t = jnp.concatenate([w_re, w_rg, jnp.zeros((d, LANES - N_EXPERTS - N_GROUPS), F32)], axis=1).T
    b_router_t = jnp.concatenate([b_re, b_rg, jnp.zeros((LANES - N_EXPERTS - N_GROUPS,), F32)]).reshape(LANES, 1)
    x2, h3, route, wcol = _xattn(x1, kv.reshape(nb, n_mem, 2 * d), row(g_xq), w_xq.astype(BF16),
                                 w_xo.astype(BF16), row(g_moe), w_router_t, b_router_t, seq)
    pos, tinfo, tok = _plan(route)
    nt = _n_expert_tiles(t)
    texp, tvalid = tinfo[0, :nt], tinfo[1, :nt]
    xs = _rowgather(h3, tok[:nt].reshape(-1))
    ys = _experts(texp, tvalid, xs, w_gate, w_up, w_down)
    return x2, ys, pos, wcol


def kernel(x, mem, g_mix, w_in, b_ml_i, b_ml_f, b_fx_f, b_gate_ml, b_gate_fx, g_ml_head, w_proj_ml, w_proj_fx, w_out, g_xq, g_xmem, w_xq, w_xkv, w_xo, g_moe, w_rg, b_rg, w_re, b_re, w_gate, w_up, w_down, g_final):
    nb, seq, d = x.shape
    depth = g_mix.shape[0]
    assert depth == 1, "the final rmsnorm is fused into the (single) layer's combine kernel"
    x2, ys, pos, wcol = _layer(
        x, mem, g_mix[0], w_in[0], b_ml_i[0], b_ml_f[0], b_fx_f[0], b_gate_ml[0], b_gate_fx[0], g_ml_head[0],
        w_proj_ml[0], w_proj_fx[0], w_out[0], g_xq[0], g_xmem[0], w_xq[0], w_xkv[0], w_xo[0], g_moe[0],
        w_rg[0], b_rg[0], w_re[0], b_re[0], w_gate[0], w_up[0], w_down[0])
    g = _rowgather(ys, pos[0:2].reshape(-1))
    out = _combine(g, x2, wcol, g_final.reshape(1, d).astype(F32))
    return out.reshape(nb, seq, d)
```

```python
import functools

import jax
import jax.numpy as jnp
from jax import lax
from jax.experimental import pallas as pl
from jax.experimental.pallas import tpu as pltpu
from jax.experimental.pallas import tpu_sc as plsc

F32 = jnp.float32
BF16 = jnp.bfloat16

D_MODEL = 1024
EPS = 1e-6
ML_HEADS = 4
ML_QK_DIM = 128
ML_V_DIM = 256
FX_HEADS = 8
FX_HEAD_DIM = 128
XA_HEADS = 4
XA_HEAD_DIM = 256
N_GROUPS = 4
EXPERTS_PER_GROUP = 8
N_EXPERTS = 32
D_EXPERT = 512

LANES = 128
N_GATES = 16
ROUTER_ROWS = 40
LOG2E = 1.4426950408889634
Z_W = 6144
Z_ML_O, Z_FX_Q, Z_FX_K, Z_GT_ML, Z_GT_FX = 1, 2, 3, 4, 5
VT_W = 2048

VMEM_LIMIT = 56 * 1024 * 1024

IN_CHUNK = 1024
TK = 256
TQ = 512
TIN = 512
KT_PER_Q = TQ // TK
MXU_LOOKAHEAD = 4
TM = 512
TME = 256
TP = 512
TF = 512
SC_CORES, SC_SUBCORES = 2, 16
SC_ROWS = 32


def _params(sem, flags=None):
    return pltpu.CompilerParams(dimension_semantics=sem, vmem_limit_bytes=VMEM_LIMIT, flags=flags)


def _rms(x, g):
    return x * lax.rsqrt(jnp.mean(x * x, axis=-1, keepdims=True) + EPS) * g


def _dot_nt(a, b, **kw):
    return lax.dot_general(a, b, (((1,), (1,)), ((), ())), preferred_element_type=F32, **kw)


def _split_bf16(x):
    hi = x.astype(BF16)
    return hi, (x - hi.astype(F32)).astype(BF16)


def _dot_nt_split(a, b):
    a_hi, a_lo = _split_bf16(a)
    b_hi, b_lo = _split_bf16(b)
    return _dot_nt(a_hi, b_hi) + (_dot_nt(a_hi, b_lo) + _dot_nt(a_lo, b_hi))


def _resident(shape):
    zeros = (0,) * len(shape)
    return pl.BlockSpec(shape, lambda *_: zeros, pipeline_mode=pl.Buffered(1))


def _inproj_body(x_ref, g_ref, w_ref, wvt_ref, wgt_ref, z_ref, vt_ref, gt_ref):
    h = _rms(x_ref[...], g_ref[...])
    hb = h.astype(BF16)
    for c in range(Z_W // IN_CHUNK):
        sl = slice(c * IN_CHUNK, (c + 1) * IN_CHUNK)
        z_ref[:, sl] = jnp.dot(hb, w_ref[:, sl], preferred_element_type=F32).astype(BF16)
    for kt in range(TIN // TK):
        hk = hb[kt * TK:(kt + 1) * TK]
        for c in range(VT_W // IN_CHUNK):
            sl = slice(c * IN_CHUNK, (c + 1) * IN_CHUNK)
            vt_ref[kt, sl, :] = _dot_nt(wvt_ref[sl, :], hk).astype(BF16)
    gt_ref[...] = _dot_nt_split(wgt_ref[...], h)


def _inproj(x2d, g, w, wvt, wgt):
    t = x2d.shape[0]
    return pl.pallas_call(
        _inproj_body,
        grid=(t // TIN,),
        in_specs=[
            pl.BlockSpec((TIN, D_MODEL), lambda i: (i, 0)),
            _resident((1, D_MODEL)),
            _resident((D_MODEL, Z_W)),
            _resident((VT_W, D_MODEL)),
            _resident((N_GATES, D_MODEL)),
        ],
        out_specs=[
            pl.BlockSpec((TIN, Z_W), lambda i: (i, 0)),
            pl.BlockSpec((TIN // TK, VT_W, TK), lambda i: (i, 0, 0)),
            pl.BlockSpec((N_GATES, TIN), lambda i: (0, i)),
        ],
        out_shape=[jax.ShapeDtypeStruct((t, Z_W), BF16), jax.ShapeDtypeStruct((t // TK, VT_W, TK), BF16),
                   jax.ShapeDtypeStruct((N_GATES, t), F32)],
        compiler_params=_params(("parallel",)),
        name="inproj",
    )(x2d, g, w, wvt, wgt)


def _scan_lanes(x, op, identity):
    n = x.shape[-1]
    idx = lax.broadcasted_iota(jnp.int32, x.shape, 1)
    s = 1
    while s < n:
        shifted = pltpu.roll(x, s, axis=1)
        x = op(x, jnp.where(idx >= s, shifted, identity))
        s *= 2
    return x


def _log_sigmoid(x):
    return jnp.minimum(x, 0.0) - jnp.log1p(jnp.exp(-jnp.abs(x)))


def _gateprep_body(gt_ref, bias_ref, rows_ref, cols_ref, caug_ref):
    g = gt_ref[...] + bias_ref[...]
    s = g.shape[1]
    cs = _scan_lanes(_log_sigmoid(g), jnp.add, 0.0)
    b = cs[4:8]
    c2 = cs[8:16] * LOG2E
    a = g[0:4] - b
    m = _scan_lanes(jnp.concatenate([a, a], axis=0), jnp.maximum, -jnp.inf)[0:4]
    rows_ref[...] = jnp.concatenate([m * LOG2E, b + m, c2], axis=0)
    cols_ref[...] = jnp.concatenate([a * LOG2E, jnp.zeros((LANES - ML_HEADS, s), F32)], axis=0).T
    hi = c2.astype(BF16).astype(F32)
    r1 = c2 - hi
    mid = r1.astype(BF16).astype(F32)
    lo = r1 - mid
    aug = jnp.concatenate([-hi, -mid, -lo, jnp.zeros((LANES - 3 * FX_HEADS, s), F32)], axis=0)
    caug_ref[...] = aug.T.astype(BF16)


def _gateprep(gt3, bias):
    nb, _, s = gt3.shape
    return pl.pallas_call(
        _gateprep_body,
        grid=(nb,),
        in_specs=[
            pl.BlockSpec((None, N_GATES, s), lambda b: (b, 0, 0)),
            _resident((N_GATES, 1)),
        ],
        out_specs=[
            pl.BlockSpec((None, N_GATES, s), lambda b: (b, 0, 0)),
            pl.BlockSpec((None, s, LANES), lambda b: (b, 0, 0)),
            pl.BlockSpec((None, s, LANES), lambda b: (b, 0, 0)),
        ],
        out_shape=[jax.ShapeDtypeStruct((nb, N_GATES, s), F32), jax.ShapeDtypeStruct((nb, s, LANES), F32),
                   jax.ShapeDtypeStruct((nb, s, LANES), BF16)],
        compiler_params=_params(("parallel",)),
        name="gateprep",
    )(gt3, bias)


def _causal_mask_t(d):
    s = lax.broadcasted_iota(jnp.int32, (TK, TQ), 0) + d * TK
    t = lax.broadcasted_iota(jnp.int32, (TK, TQ), 1)
    return s <= t


def _mlstm_body(q_ref, k_ref, vt_ref, o_ref, cols_ref, rows_ref, gh_ref, y_ref, num_scr, den_scr):
    i = pl.program_id(1)
    rows = rows_ref[...]
    num_scr[...] = jnp.zeros_like(num_scr)
    den_scr[...] = jnp.zeros_like(den_scr)

    def scores(j, h):
        ks = pl.ds(pl.multiple_of(j * TK, TK), TK)
        qk = slice(h * ML_QK_DIM, (h + 1) * ML_QK_DIM)
        return _dot_nt(k_ref[ks, qk], q_ref[:, qk])

    def update(j, h, s, mask):
        ks = pl.ds(pl.multiple_of(j * TK, TK), TK)
        vv = slice(h * ML_V_DIM, (h + 1) * ML_V_DIM)
        w = jnp.exp2(cols_ref[ks, h:h + 1] - rows[h:h + 1])
        if mask is not None:
            w = jnp.where(mask, w, 0.0)
        s = s * w
        den_scr[h] += jnp.sum(s, axis=0, keepdims=True)
        num_scr[h] += jnp.dot(vt_ref[j, vv, :], s.astype(BF16), preferred_element_type=F32)

    def key_tile(j, mask):
        s = {h: scores(j, h) for h in range(min(MXU_LOOKAHEAD, ML_HEADS))}
        for h in range(ML_HEADS):
            if h + MXU_LOOKAHEAD < ML_HEADS:
                s[h + MXU_LOOKAHEAD] = scores(j, h + MXU_LOOKAHEAD)
            update(j, h, s.pop(h), mask)

    def body(j, carry):
        key_tile(j, None)
        return carry

    lax.fori_loop(0, i * KT_PER_Q, body, 0)
    for d in range(KT_PER_Q):
        key_tile(i * KT_PER_Q + d, _causal_mask_t(d))
    for h in range(ML_HEADS):
        vv = slice(h * ML_V_DIM, (h + 1) * ML_V_DIM)
        floor = jnp.exp(-rows[ML_HEADS + h:ML_HEADS + h + 1])
        hh = num_scr[h] * (1.0 / jnp.maximum(jnp.abs(den_scr[h]), floor))
        yt = hh * lax.rsqrt(jnp.mean(hh * hh, axis=0, keepdims=True) + EPS)
        y = yt.T * gh_ref[:, vv]
        y_ref[:, vv] = (y * jax.nn.sigmoid(o_ref[:, vv].astype(F32))).astype(BF16)


def _mlstm(z3, vt4, cols3, rows3, g_head):
    nb, s, _ = z3.shape
    nq = s // TQ
    return pl.pallas_call(
        _mlstm_body,
        grid=(nb, nq),
        in_specs=[
            pl.BlockSpec((None, TQ, ML_HEADS * ML_QK_DIM), lambda b, i: (b, i, 0)),
            pl.BlockSpec((None, s, ML_HEADS * ML_QK_DIM), lambda b, i: (b, 0, 1)),
            pl.BlockSpec((None, s // TK, D_MODEL, TK), lambda b, i: (b, 0, 0, 0)),
            pl.BlockSpec((None, TQ, D_MODEL), lambda b, i: (b, i, Z_ML_O)),
            pl.BlockSpec((None, s, LANES), lambda b, i: (b, 0, 0)),
            pl.BlockSpec((None, N_GATES, TQ), lambda b, i: (b, 0, i)),
            _resident((1, D_MODEL)),
        ],
        out_specs=pl.BlockSpec((None, TQ, D_MODEL), lambda b, i: (b, i, 0)),
        out_shape=jax.ShapeDtypeStruct((nb, s, D_MODEL), BF16),
        scratch_shapes=[pltpu.VMEM((ML_HEADS, ML_V_DIM, TQ), F32), pltpu.VMEM((ML_HEADS, 1, TQ), F32)],
        compiler_params=_params(("parallel", "arbitrary")),
        name="mlstm",
    )(z3, z3, vt4, z3, cols3, rows3, g_head)


def _fox_body(q_ref, k_ref, vt_ref, caug_ref, rows_ref, y_ref, qa_scr, acc_scr, m_scr, l_scr):
    i = pl.program_id(1)
    rows = rows_ref[...]
    lane = lax.broadcasted_iota(jnp.int32, (TQ, LANES), 1)
    for h in range(FX_HEADS):
        hd = slice(h * FX_HEAD_DIM, (h + 1) * FX_HEAD_DIM)
        ones = jnp.where((lane < 3 * FX_HEADS) & (lane % FX_HEADS == h), 1.0, 0.0).astype(BF16)
        qa_scr[h] = jnp.concatenate([q_ref[:, hd], ones], axis=1)
    m_scr[...] = jnp.full_like(m_scr, -jnp.inf)
    l_scr[...] = jnp.zeros_like(l_scr)
    acc_scr[...] = jnp.zeros_like(acc_scr)

    def scores(j, h):
        ks = pl.ds(pl.multiple_of(j * TK, TK), TK)
        hd = slice(h * FX_HEAD_DIM, (h + 1) * FX_HEAD_DIM)
        k_aug =jnp.concatenate([k_ref[ks, hd], caug_ref[ks, :]], axis=1)
        return _dot_nt(k_aug, qa_scr[h])

    def update(j, h, u, mask):
        hd = slice(h * FX_HEAD_DIM, (h + 1) * FX_HEAD_DIM)
        if mask is not None:
            u = jnp.where(mask, u, -jnp.inf)
        c_row = rows[2 * ML_HEADS + h:2 * ML_HEADS + h + 1]
        m_prev = m_scr[h]
        m_new = jnp.maximum(m_prev, jnp.max(u, axis=0, keepdims=True) + c_row)
        p = jnp.exp2(u - (m_new - c_row))
        alpha = jnp.exp2(m_prev - m_new)
        l_scr[h] = alpha * l_scr[h] + jnp.sum(p, axis=0, keepdims=True)
        acc_scr[h] = alpha * acc_scr[h] + jnp.dot(vt_ref[j, hd, :], p.astype(BF16), preferred_element_type=F32)
        m_scr[h] = m_new

    def key_tile(j, mask):
        u = {h: scores(j, h) for h in range(MXU_LOOKAHEAD)}
        for h in range(FX_HEADS):
            if h + MXU_LOOKAHEAD < FX_HEADS:
                u[h + MXU_LOOKAHEAD] = scores(j, h + MXU_LOOKAHEAD)
            update(j, h, u.pop(h), mask)

    def body(j, carry):
        key_tile(j, None)
        return carry

    lax.fori_loop(0, i * KT_PER_Q, body, 0)
    for d in range(KT_PER_Q):
        key_tile(i * KT_PER_Q + d, _causal_mask_t(d))
    for h in range(FX_HEADS):
        hd = slice(h * FX_HEAD_DIM, (h + 1) * FX_HEAD_DIM)
        y_ref[:, hd] = (acc_scr[h] * (1.0 / l_scr[h])).T.astype(BF16)


def _fox(z3, vt4, caug3, rows3):
    nb, s, _ = z3.shape
    nq = s // TQ
    return pl.pallas_call(
        _fox_body,
        grid=(nb, nq),
        in_specs=[
            pl.BlockSpec((None, TQ, D_MODEL), lambda b, i: (b, i, Z_FX_Q)),
            pl.BlockSpec((None, s, D_MODEL), lambda b, i: (b, 0, Z_FX_K)),
            pl.BlockSpec((None, s // TK, D_MODEL, TK), lambda b, i: (b, 0, 1, 0)),
            pl.BlockSpec((None, s, LANES), lambda b, i: (b, 0, 0)),
            pl.BlockSpec((None, N_GATES, TQ), lambda b, i: (b, 0, i)),
        ],
        out_specs=pl.BlockSpec((None, TQ, D_MODEL), lambda b, i: (b, i, 0)),
        out_shape=jax.ShapeDtypeStruct((nb, s, D_MODEL), BF16),
        scratch_shapes=[pltpu.VMEM((FX_HEADS, TQ, 2 * FX_HEAD_DIM), BF16),
                        pltpu.VMEM((FX_HEADS, FX_HEAD_DIM, TQ), F32),
                        pltpu.VMEM((FX_HEADS, 1, TQ), F32), pltpu.VMEM((FX_HEADS, 1, TQ), F32)],
        compiler_params=_params(("parallel", "arbitrary")),
        name="fox",
    )(z3, z3, vt4, caug3, rows3)


def _merge_body(x_ref, yml_ref, yfx_ref, gml_ref, gfx_ref, bml_ref, bfx_ref, wml_ref, wfx_ref, wout_ref, x1_ref):
    p_ml = jnp.dot(yml_ref[...], wml_ref[...], preferred_element_type=F32)
    p_fx = jnp.dot(yfx_ref[...], wfx_ref[...], preferred_element_type=F32)
    merged = (jax.nn.sigmoid(gml_ref[...].astype(F32) + bml_ref[...]) * p_ml
              + jax.nn.sigmoid(gfx_ref[...].astype(F32) + bfx_ref[...]) * p_fx)
    x1_ref[...] = x_ref[...] + jnp.dot(merged.astype(BF16), wout_ref[...], preferred_element_type=F32)


def _merge(x2d, yml, yfx, z2d, bml, bfx, wml, wfx, wout):
    t = x2d.shape[0]
    tile = lambda col: pl.BlockSpec((TM, D_MODEL), lambda i, col=col: (i, col))
    return pl.pallas_call(
        _merge_body,
        grid=(t // TM,),
        in_specs=[tile(0), tile(0), tile(0), tile(Z_GT_ML), tile(Z_GT_FX),
                  _resident((1, D_MODEL)), _resident((1, D_MODEL)),
                  _resident((D_MODEL, D_MODEL)), _resident((D_MODEL, D_MODEL)), _resident((D_MODEL, D_MODEL))],
        out_specs=tile(0),
        out_shape=jax.ShapeDtypeStruct((t, D_MODEL), F32),
        compiler_params=_params(("parallel",)),
        name="merge",
    )(x2d, yml, yfx, z2d, z2d, bml, bfx, wml, wfx, wout)


def _memkv_body(m_ref, g_ref, w_ref, kv_ref):
    hb = _rms(m_ref[...], g_ref[...]).astype(BF16)
    kv_ref[...] = jnp.dot(hb, w_ref[...], preferred_element_type=F32).astype(BF16)


def _memkv(mem2d, g, w):
    t = mem2d.shape[0]
    return pl.pallas_call(
        _memkv_body,
        grid=(t // TM,),
        in_specs=[pl.BlockSpec((TM, D_MODEL), lambda i: (i, 0)), _resident((1, D_MODEL)),
                  _resident((D_MODEL, 2 * D_MODEL))],
        out_specs=pl.BlockSpec((TM, 2 * D_MODEL), lambda i: (i, 0)),
        out_shape=jax.ShapeDtypeStruct((t, 2 * D_MODEL), BF16),
        compiler_params=_params(("parallel",)),
        name="memkv",
    )(mem2d, g, w)


def _route_t(lg_t):
    tm = lg_t.shape[1]
    ninf = -jnp.inf
    big = jnp.float32(LANES)
    gid = lax.broadcasted_iota(jnp.int32, (8, tm), 0).astype(F32)
    eid = lax.broadcasted_iota(jnp.int32, (N_EXPERTS, tm), 0).astype(F32)
    gl = jnp.where(gid < N_GROUPS, lg_t[N_EXPERTS:N_EXPERTS + 8], ninf)
    gmax = jnp.max(gl, axis=0, keepdims=True)
    gidx = jnp.min(jnp.where(gl == gmax, gid, big), axis=0, keepdims=True)
    g_p = 1.0 / jnp.sum(jnp.exp(gl - gmax), axis=0, keepdims=True)
    lo = gidx * EXPERTS_PER_GROUP
    el = jnp.where(eid >= lo, jnp.where(eid < lo + EXPERTS_PER_GROUP, lg_t[0:N_EXPERTS], ninf), ninf)
    v1 = jnp.max(el, axis=0, keepdims=True)
    i1 = jnp.min(jnp.where(el == v1, eid, big), axis=0, keepdims=True)
    el2 = jnp.where(eid == i1, ninf, el)
    v2 = jnp.max(el2, axis=0, keepdims=True)
    i2 = jnp.min(jnp.where(el2 == v2, eid, big), axis=0, keepdims=True)
    t = jnp.exp(v2 - v1)
    w1 = g_p / (1.0 + t)
    w2 = w1 * t
    return jnp.concatenate([i1, i2, w1, w2, jnp.zeros((4, tm), F32)], axis=0)


def _xattn_body(x1_ref, kv_ref, gq_ref, wq_ref, wo_ref, gm_ref, wrt_ref, brt_ref, x2_ref, h3_ref, route_ref,
                wcol_ref):
    x1 = x1_ref[...]
    hb = _rms(x1, gq_ref[...]).astype(BF16)
    q = (jnp.dot(hb, wq_ref[...], preferred_element_type=F32) * (XA_HEAD_DIM ** -0.5)).astype(BF16)
    outs = []
    for h in range(XA_HEADS):
        hd = slice(h * XA_HEAD_DIM, (h + 1) * XA_HEAD_DIM)
        vd = slice(D_MODEL + h * XA_HEAD_DIM, D_MODEL + (h + 1) * XA_HEAD_DIM)
        s = _dot_nt(q[:, hd], kv_ref[:, hd])
        p = jnp.exp(s - jnp.max(s, axis=1, keepdims=True))
        p = p * (1.0 / jnp.sum(p, axis=1, keepdims=True))
        outs.append(jnp.dot(p.astype(BF16), kv_ref[:, vd], preferred_element_type=F32).astype(BF16))
    o = jnp.concatenate(outs, axis=1)
    x2 = x1 + jnp.dot(o, wo_ref[...], preferred_element_type=F32)
    x2_ref[...] = x2
    h3 = _rms(x2, gm_ref[...])
    h3_ref[...] = h3
    lg_t = _dot_nt_split(wrt_ref[...], h3) + brt_ref[...]
    route = _route_t(lg_t)
    route_ref[...] = route
    wcol_ref[...] = jnp.concatenate([route, jnp.zeros((LANES - 8, route.shape[1]), F32)], axis=0).T


def _xattn(x1, kv3, gq, wq, wo, gm, wrt, brt, seq):
    t = x1.shape[0]
    per_b = seq // TM
    n_mem = kv3.shape[1]
    tile = pl.BlockSpec((TM, D_MODEL), lambda i: (i, 0))
    return pl.pallas_call(
        _xattn_body,
        grid=(t // TM,),
        in_specs=[tile,
                  pl.BlockSpec((None, n_mem, 2 * D_MODEL), lambda i: (i // per_b, 0, 0)),
                  _resident((1, D_MODEL)), _resident((D_MODEL, D_MODEL)), _resident((D_MODEL, D_MODEL)),
                  _resident((1, D_MODEL)), _resident((ROUTER_ROWS, D_MODEL)), _resident((ROUTER_ROWS, 1))],
        out_specs=[tile, tile, pl.BlockSpec((8, TM), lambda i: (0, i)), pl.BlockSpec((TM, LANES), lambda i: (i, 0))],
        out_shape=[jax.ShapeDtypeStruct((t, D_MODEL), F32), jax.ShapeDtypeStruct((t, D_MODEL), F32),
                   jax.ShapeDtypeStruct((8, t), F32), jax.ShapeDtypeStruct((t, LANES), F32)],
        compiler_params=_params(("parallel",)),
        name="xattn",
    )(x1, kv3, gq, wq, wo, gm, wrt, brt)


def _n_expert_tiles(n_tokens):
    return 2 * n_tokens // TME + N_EXPERTS


def _plan_body(route_ref, pos_ref, tinfo_ref, tok_ref, cnt_scr, run_scr, start_scr, tokhi_scr, toklo_scr, *, nt_pad,
               n_tokens):
    phase = pl.program_id(0)
    b = pl.program_id(1)
    r = route_ref[...]
    eid = lax.broadcasted_iota(jnp.int32, (N_EXPERTS, TP), 0).astype(F32)
    oh1 = eid == r[0:1]
    oh2 = eid == r[1:2]
    oh = jnp.where(oh1 | oh2, 1.0, 0.0)

    @pl.when((phase == 0) & (b == 0))
    def _():
        cnt_scr[...] = jnp.zeros_like(cnt_scr)

    @pl.when(phase == 0)
    def _():
        cnt_scr[...] += jnp.sum(oh, axis=1, keepdims=True)

    @pl.when((phase == 1) & (b == 0))
    def _():
        cnt = cnt_scr[...]
        n_tiles = jnp.floor((cnt + (TME - 1)) * (1.0 / TME))
        ri = lax.broadcasted_iota(jnp.int32, (N_EXPERTS, N_EXPERTS), 0)
        ci = lax.broadcasted_iota(jnp.int32, (N_EXPERTS, N_EXPERTS), 1)
        lower = jnp.where(ci < ri, 1.0, 0.0).astype(BF16)
        nt_hi, nt_lo = _split_bf16(jnp.broadcast_to(n_tiles, (N_EXPERTS, LANES)))
        start = (jnp.dot(lower, nt_hi, preferred_element_type=F32)
                 + jnp.dot(lower, nt_lo, preferred_element_type=F32))[:, 0:1]
        start_scr[...] = start * TME
        run_scr[...] = jnp.zeros_like(run_scr)
        tokhi_scr[...] = jnp.zeros_like(tokhi_scr)
        toklo_scr[...] = jnp.zeros_like(toklo_scr)
        n = lax.broadcasted_iota(jnp.int32, (N_EXPERTS, nt_pad), 1).astype(F32)
        e_n = lax.broadcasted_iota(jnp.int32, (N_EXPERTS, nt_pad), 0).astype(F32)
        owner = jnp.sum(jnp.where(start <= n, 1.0, 0.0), axis=0, keepdims=True) - 1.0
        own = e_n == owner
        cnt_o = jnp.sum(jnp.where(own, cnt, 0.0), axis=0, keepdims=True)
        start_o = jnp.sum(jnp.where(own, start, 0.0), axis=0, keepdims=True)
        valid = jnp.clip(cnt_o - (n[0:1] - start_o) * TME, 0.0, float(TME))
        tinfo_ref[...] = jnp.concatenate([owner, valid, jnp.zeros((6, nt_pad), F32)], axis=0).astype(jnp.int32)

    @pl.when(phase == 1)
    def _():
        ti = lax.broadcasted_iota(jnp.int32, (TP, TP), 0)
        tj = lax.broadcasted_iota(jnp.int32, (TP, TP), 1)
        upper = jnp.where(ti < tj, 1.0, 0.0).astype(BF16)
        before = jnp.dot(oh.astype(BF16), upper, preferred_element_type=F32)
        row = start_scr[...] + run_scr[...] + before
        p1 = jnp.sum(jnp.where(oh1, row, 0.0), axis=0, keepdims=True)
        p2 = jnp.sum(jnp.where(oh2, row, 0.0), axis=0, keepdims=True)
        pos_ref[...] = jnp.concatenate([p1, p2, jnp.zeros((6, TP), F32)], axis=0).astype(jnp.int32)
        run_scr[...] += jnp.sum(oh, axis=1, keepdims=True)

        tid = lax.broadcasted_iota(jnp.int32, (1, TP), 1) + (b * TP + 1)
        t_hi = (tid // 256).astype(F32)
        t_lo = (tid % 256).astype(F32)
        tile_id = lax.broadcasted_iota(jnp.int32, (nt_pad, TP), 0).astype(F32)
        row_id = lax.broadcasted_iota(jnp.int32, (TME, TP), 0).astype(F32)
        for p in (p1, p2):
            hi = jnp.floor(p * (1.0 / TME))
            lo = p - hi * TME
            in_tile = jnp.where(tile_id == hi, 1.0, 0.0).astype(BF16)
            at_row = row_id == lo
            tokhi_scr[...] += _dot_nt(in_tile, jnp.where(at_row, t_hi, 0.0).astype(BF16))
            toklo_scr[...] += _dot_nt(in_tile, jnp.where(at_row, t_lo, 0.0).astype(BF16))

    @pl.when((phase == 1) & (b == pl.num_programs(1) - 1))
    def _():
        hit = (tokhi_scr[...] * 256.0 + toklo_scr[...]).astype(jnp.int32)
        row = (lax.broadcasted_iota(jnp.int32, hit.shape, 0) * TME + lax.broadcasted_iota(jnp.int32, hit.shape, 1))
        tok_ref[...] = jnp.where(hit > 0, hit - 1, row % n_tokens)


def _plan(route):
    t = route.shape[1]
    assert t < 256 * 256, "token id + 1 is carried as two base-256 digits"
    nt_pad = -(-_n_expert_tiles(t) // LANES) * LANES
    col = pltpu.VMEM((N_EXPERTS, 1), F32)
    table = pltpu.VMEM((nt_pad, TME), F32)
    return pl.pallas_call(
        functools.partial(_plan_body, nt_pad=nt_pad, n_tokens=t),
        grid=(2, t // TP),
        in_specs=[pl.BlockSpec((8, TP), lambda ph, b: (0, b))],
        out_specs=[pl.BlockSpec((8, TP), lambda ph, b: (0, b * ph)),
                   pl.BlockSpec((8, nt_pad), lambda ph, b: (0, 0)),
                   pl.BlockSpec((nt_pad, TME), lambda ph, b: (0, 0))],
        out_shape=[jax.ShapeDtypeStruct((8, t), jnp.int32), jax.ShapeDtypeStruct((8, nt_pad), jnp.int32),
                   jax.ShapeDtypeStruct((nt_pad, TME), jnp.int32)],
        scratch_shapes=[col, col, col, table, table],
        compiler_params=_params(("arbitrary", "arbitrary")),
        name="plan",
    )(route)


def _experts_body(texp_ref, tvalid_ref, xs_ref, wg_ref, wu_ref, wd_ref, ys_ref, wgb_scr, wub_scr, wdb_scr):
    n = pl.program_id(0)

    @pl.when((n == 0) | (texp_ref[n] != texp_ref[jnp.maximum(n - 1, 0)]))
    def _():
        wgb_scr[...] = wg_ref[...].astype(BF16)
        wub_scr[...] = wu_ref[...].astype(BF16)
        wdb_scr[...] = wd_ref[...].astype(BF16)

    @pl.when(tvalid_ref[n] > 0)
    def _():
        xb = xs_ref[...].astype(BF16)
        gate = jnp.dot(xb, wgb_scr[...], preferred_element_type=F32)
        up = jnp.dot(xb, wub_scr[...], preferred_element_type=F32)
        he = (gate * jax.nn.sigmoid(gate) * up).astype(BF16)
        ys_ref[...] = jnp.dot(he, wdb_scr[...], preferred_element_type=F32)

    @pl.when(tvalid_ref[n] == 0)
    def _():
        ys_ref[...] = jnp.zeros_like(ys_ref)


def _experts(texp, tvalid, xs, wg, wu, wd):
    nt = texp.shape[0]
    weight = lambda shape: pl.BlockSpec((None,) + shape, lambda n, te, tv: (te[n], 0, 0))
    return pl.pallas_call(
        _experts_body,
        grid_spec=pltpu.PrefetchScalarGridSpec(
            num_scalar_prefetch=2,
            grid=(nt,),
            in_specs=[pl.BlockSpec((TME, D_MODEL), lambda n, te, tv: (jnp.where(tv[n] > 0, n, 0), 0)),
                      weight((D_MODEL, D_EXPERT)), weight((D_MODEL, D_EXPERT)), weight((D_EXPERT, D_MODEL))],
            out_specs=pl.BlockSpec((TME, D_MODEL), lambda n, te, tv: (n, 0)),
            scratch_shapes=[pltpu.VMEM((D_MODEL, D_EXPERT), BF16), pltpu.VMEM((D_MODEL, D_EXPERT), BF16),
                            pltpu.VMEM((D_EXPERT, D_MODEL), BF16)],
        ),
        out_shape=jax.ShapeDtypeStruct((nt * TME, D_MODEL), F32),
        compiler_params=_params(("arbitrary",)),
        name="experts",
    )(texp, tvalid, xs, wg, wu, wd)


def _rowgather(table, idx):
    n_rows, width = idx.shape[0], table.shape[1]
    n_workers = SC_CORES * SC_SUBCORES
    per_worker = n_rows // n_workers
    assert per_worker * n_workers == n_rows and per_worker % SC_ROWS == 0
    mesh = plsc.VectorSubcoreMesh(core_axis_name="c", subcore_axis_name="s", num_cores=SC_CORES,
                                  num_subcores=SC_SUBCORES)

    chunks = per_worker // SC_ROWS
    assert chunks % 2 == 0
    buf = lambda: [pltpu.VMEM((SC_ROWS,), jnp.int32), pltpu.VMEM((SC_ROWS, width), table.dtype),
                   pltpu.SemaphoreType.DMA]

    @functools.partial(pl.kernel, mesh=mesh, out_type=jax.ShapeDtypeStruct((n_rows, width), table.dtype),
                       scratch_types=buf() + buf(), name="rowgather")
    def gather(table_hbm, idx_hbm, out_hbm, idx_a, rows_a, sem_a, idx_b, rows_b, sem_b):
        worker = lax.axis_index("s") * SC_CORES + lax.axis_index("c")
        base = worker * per_worker

        def fetch(c, idx_v, rows_v, sem):
            pltpu.sync_copy(idx_hbm.at[pl.ds(base + c * SC_ROWS, SC_ROWS)], idx_v)
            pltpu.async_copy(table_hbm.at[idx_v], rows_v, sem)

        def flush(c, idx_v, rows_v, sem):
            pltpu.make_async_copy(table_hbm.at[idx_v], rows_v, sem).wait()
            pltpu.sync_copy(rows_v, out_hbm.at[pl.ds(base + c * SC_ROWS, SC_ROWS)])

        fetch(0, idx_a, rows_a, sem_a)

        @pl.loop(0, chunks, step=2)
        def _(c):
            fetch(c + 1, idx_b, rows_b, sem_b)
            flush(c, idx_a, rows_a, sem_a)

            @pl.when(c + 2 < chunks)
            def _():
                fetch(c + 2, idx_a, rows_a, sem_a)

            flush(c + 1, idx_b, rows_b, sem_b)

    return gather(table, idx)


def _combine_body(g1_ref, g2_ref, x2_ref, wcol_ref, gf_ref, out_ref):
    w = wcol_ref[...]
    y = w[:, 2:3] * g1_ref[...] + w[:, 3:4] * g2_ref[...]
    out_ref[...] = _rms(x2_ref[...] + y, gf_ref[...])


def _combine(g, x2, wcol, gf):
    t = x2.shape[0]
    steps = t // TF
    return pl.pallas_call(
        _combine_body,
        grid=(steps,),
        in_specs=[pl.BlockSpec((TF, D_MODEL), lambda i: (i, 0)),
                  pl.BlockSpec((TF, D_MODEL), lambda i: (i + steps, 0)),
                  pl.BlockSpec((TF, D_MODEL), lambda i: (i, 0)),
                  pl.BlockSpec((TF, LANES), lambda i: (i, 0)),
                  _resident((1, D_MODEL))],
        out_specs=pl.BlockSpec((TF, D_MODEL), lambda i: (i, 0)),
        out_shape=jax.ShapeDtypeStruct((t, D_MODEL), F32),
        compiler_params=_params(("parallel",)),
        name="combine",
    )(g, g, x2, wcol, gf)


def _layer(x, mem, g_mix, w_in, b_ml_i, b_ml_f, b_fx_f, b_gate_ml, b_gate_fx, g_ml_head, w_proj_ml, w_proj_fx,
           w_out, g_xq, g_xmem, w_xq, w_xkv, w_xo, g_moe, w_rg, b_rg, w_re, b_re, w_gate, w_up, w_down):
    nb, seq, d = x.shape
    t = nb * seq
    row = lambda v: v.reshape(1, -1).astype(F32)

    o = 0
    parts = {}
    for name, width in (("ml_q", 512), ("ml_k", 512), ("ml_v", 1024), ("ml_o", 1024), ("ml_i", 4), ("ml_f", 4),
                        ("fx_q", 1024), ("fx_k", 1024), ("fx_v", 1024), ("fx_f", 8), ("gt_ml", 1024),
                        ("gt_fx", 1024)):
        parts[name] = w_in[:, o:o + width]
        o += width
    w_main = jnp.concatenate(
        [parts["ml_q"], parts["ml_k"] * (ML_QK_DIM ** -0.5), parts["ml_o"],
         parts["fx_q"] * (FX_HEAD_DIM ** -0.5 * LOG2E), parts["fx_k"], parts["gt_ml"], parts["gt_fx"]],
        axis=1).astype(BF16)
    w_vt = jnp.concatenate([parts["ml_v"], parts["fx_v"]], axis=1).T.astype(BF16)
    w_gates_t = jnp.concatenate([parts["ml_i"], parts["ml_f"], parts["fx_f"]], axis=1).T
    gate_bias = jnp.concatenate([b_ml_i, b_ml_f, b_fx_f]).reshape(N_GATES, 1).astype(F32)

    x2d = x.reshape(t, d)
    z, vt, gates_t = _inproj(x2d, row(g_mix), w_main, w_vt, w_gates_t)
    gt3 = gates_t.reshape(N_GATES, nb, seq).transpose(1, 0, 2)
    rows, cols, caug = _gateprep(gt3, gate_bias)
    z3 = z.reshape(nb, seq, Z_W)
    vt4 = vt.reshape(nb, seq // TK, VT_W, TK)
    y_ml = _mlstm(z3, vt4, cols, rows, row(g_ml_head))
    y_fx = _fox(z3, vt4, caug, rows)
    x1 = _merge(x2d, y_ml.reshape(t, d), y_fx.reshape(t, d), z, row(b_gate_ml), row(b_gate_fx),
                w_proj_ml.astype(BF16), w_proj_fx.astype(BF16), w_out.astype(BF16))

    n_mem = mem.shape[1]
    kv = _memkv(mem.reshape(nb * n_mem, d), row(g_xmem), w_xkv.astype(BF16))
    w_router_t = jnp.concatenate([w_re, w_rg, jnp.zeros((d, ROUTER_ROWS - N_EXPERTS - N_GROUPS), F32)], axis=1).T
    b_router_t = jnp.concatenate([b_re, b_rg, jnp.zeros((ROUTER_ROWS - N_EXPERTS - N_GROUPS,), F32)]).reshape(ROUTER_ROWS, 1)
    x2, h3, route, wcol = _xattn(x1, kv.reshape(nb, n_mem, 2 * d), row(g_xq), w_xq.astype(BF16),
                                 w_xo.astype(BF16), row(g_moe), w_router_t, b_router_t, seq)
    pos, tinfo, tok = _plan(route)
    nt = _n_expert_tiles(t)
    texp, tvalid = tinfo[0, :nt], tinfo[1, :nt]
    xs = _rowgather(h3, tok[:nt].reshape(-1))
    ys = _experts(texp, tvalid, xs, w_gate, w_up, w_down)
    return x2, ys, pos, wcol


def kernel(x, mem, g_mix, w_in, b_ml_i, b_ml_f, b_fx_f, b_gate_ml, b_gate_fx, g_ml_head, w_proj_ml, w_proj_fx, w_out, g_xq, g_xmem, w_xq, w_xkv, w_xo, g_moe, w_rg, b_rg, w_re, b_re, w_gate, w_up, w_down, g_final):
    nb, seq, d = x.shape
    depth = g_mix.shape[0]
    assert depth == 1, "the final rmsnorm is fused into the (single) layer's combine kernel"
    x2, ys, pos, wcol = _layer(
        x, mem, g_mix[0], w_in[0], b_ml_i[0], b_ml_f[0], b_fx_f[0], b_gate_ml[0], b_gate_fx[0], g_ml_head[0],
        w_proj_ml[0], w_proj_fx[0], w_out[0], g_xq[0], g_xmem[0], w_xq[0], w_xkv[0], w_xo[0], g_moe[0],
        w_rg[0], b_rg[0], w_re[0], b_re[0], w_gate[0], w_up[0], w_down[0])
    g = _rowgather(ys, pos[0:2].reshape(-1))
    out = _combine(g, x2, wcol, g_final.reshape(1, d).astype(F32))
    return out.reshape(nb, seq, d)
```

```python
import functools

import jax
import jax.numpy as jnp
from jax import lax
from jax.experimental import pallas as pl
from jax.experimental.pallas import tpu as pltpu
from jax.experimental.pallas import tpu_sc as plsc

F32 = jnp.float32
BF16 = jnp.bfloat16

D_MODEL = 1024
EPS = 1e-6
ML_HEADS = 4
ML_QK_DIM = 128
ML_V_DIM = 256
FX_HEADS = 8
FX_HEAD_DIM = 128
XA_HEADS = 4
XA_HEAD_DIM = 256
N_GROUPS = 4
EXPERTS_PER_GROUP = 8
N_EXPERTS = 32
D_EXPERT = 512

LANES = 128
N_GATES = 16
ROUTER_ROWS = 40
LOG2E = 1.4426950408889634
Z_W = 6144
Z_ML_O, Z_FX_Q, Z_FX_K, Z_GT_ML, Z_GT_FX = 1, 2, 3, 4, 5
VT_W = 2048

VMEM_LIMIT = 56 * 1024 * 1024

IN_CHUNK = 1024
TK = 256
TQ = 512
TIN = 512
KT_PER_Q = TQ // TK
MXU_LOOKAHEAD = 4
TM = 512
TME = 256
TP = 512
TF = 512
MOE_PARTS = 2
SC_CORES, SC_SUBCORES = 2, 16
SC_ROWS = 32


def _params(sem, flags=None):
    return pltpu.CompilerParams(dimension_semantics=sem, vmem_limit_bytes=VMEM_LIMIT, flags=flags)


def _rms(x, g):
    return x * lax.rsqrt(jnp.mean(x * x, axis=-1, keepdims=True) + EPS) * g


def _dot_nt(a, b, **kw):
    return lax.dot_general(a, b, (((1,), (1,)), ((), ())), preferred_element_type=F32, **kw)


def _split_bf16(x):
    hi = x.astype(BF16)
    return hi, (x - hi.astype(F32)).astype(BF16)


def _dot_nt_split(a, b):
    a_hi, a_lo = _split_bf16(a)
    b_hi, b_lo = _split_bf16(b)
    return _dot_nt(a_hi, b_hi) + (_dot_nt(a_hi, b_lo) + _dot_nt(a_lo, b_hi))


def _resident(shape):
    zeros = (0,) * len(shape)
    return pl.BlockSpec(shape, lambda *_: zeros, pipeline_mode=pl.Buffered(1))


def _inproj_body(x_ref, g_ref, w_ref, wvt_ref, wgt_ref, z_ref, vt_ref, gt_ref):
    h = _rms(x_ref[...], g_ref[...])
    hb = h.astype(BF16)
    for c in range(Z_W // IN_CHUNK):
        sl = slice(c * IN_CHUNK, (c + 1) * IN_CHUNK)
        z_ref[:, sl] = jnp.dot(hb, w_ref[:, sl], preferred_element_type=F32).astype(BF16)
    for kt in range(TIN // TK):
        hk = hb[kt * TK:(kt + 1) * TK]
        for c in range(VT_W // IN_CHUNK):
            sl = slice(c * IN_CHUNK, (c + 1) * IN_CHUNK)
            vt_ref[kt, sl, :] = _dot_nt(wvt_ref[sl, :], hk).astype(BF16)
    gt_ref[...] = _dot_nt_split(wgt_ref[...], h)


def _inproj(x2d, g, w, wvt, wgt):
    t = x2d.shape[0]
    return pl.pallas_call(
        _inproj_body,
        grid=(t // TIN,),
        in_specs=[
            pl.BlockSpec((TIN, D_MODEL), lambda i: (i, 0)),
            _resident((1, D_MODEL)),
            _resident((D_MODEL, Z_W)),
            _resident((VT_W, D_MODEL)),
            _resident((N_GATES, D_MODEL)),
        ],
        out_specs=[
            pl.BlockSpec((TIN, Z_W), lambda i: (i, 0)),
            pl.BlockSpec((TIN // TK, VT_W, TK), lambda i: (i, 0, 0)),
            pl.BlockSpec((N_GATES, TIN), lambda i: (0, i)),
        ],
        out_shape=[jax.ShapeDtypeStruct((t, Z_W), BF16), jax.ShapeDtypeStruct((t // TK, VT_W, TK), BF16),
                   jax.ShapeDtypeStruct((N_GATES, t), F32)],
        compiler_params=_params(("parallel",)),
        name="inproj",
    )(x2d, g, w, wvt, wgt)


def _scan_lanes(x, op, identity):
    n = x.shape[-1]
    idx = lax.broadcasted_iota(jnp.int32, x.shape, 1)
    s = 1
    while s < n:
        shifted = pltpu.roll(x, s, axis=1)
        x = op(x, jnp.where(idx >= s, shifted, identity))
        s *= 2
    return x


def _log_sigmoid(x):
    return jnp.minimum(x, 0.0) - jnp.log1p(jnp.exp(-jnp.abs(x)))


def _gateprep_body(gt_ref, bias_ref, rows_ref, cols_ref, caug_ref):
    g = gt_ref[...] + bias_ref[...]
    s = g.shape[1]
    cs = _scan_lanes(_log_sigmoid(g), jnp.add, 0.0)
    b = cs[4:8]
    c2 = cs[8:16] * LOG2E
    a = g[0:4] - b
    m = _scan_lanes(jnp.concatenate([a, a], axis=0), jnp.maximum, -jnp.inf)[0:4]
    rows_ref[...] = jnp.concatenate([m * LOG2E, b + m, c2], axis=0)
    cols_ref[...] = jnp.concatenate([a * LOG2E, jnp.zeros((LANES - ML_HEADS, s), F32)], axis=0).T
    hi = c2.astype(BF16).astype(F32)
    r1 = c2 - hi
    mid = r1.astype(BF16).astype(F32)
    lo = r1 - mid
    aug = jnp.concatenate([-hi, -mid, -lo, jnp.zeros((LANES - 3 * FX_HEADS, s), F32)], axis=0)
    caug_ref[...] = aug.T.astype(BF16)


def _gateprep(gt3, bias):
    nb, _, s = gt3.shape
    return pl.pallas_call(
        _gateprep_body,
        grid=(nb,),
        in_specs=[
            pl.BlockSpec((None, N_GATES, s), lambda b: (b, 0, 0)),
            _resident((N_GATES, 1)),
        ],
        out_specs=[
            pl.BlockSpec((None, N_GATES, s), lambda b: (b, 0, 0)),
            pl.BlockSpec((None, s, LANES), lambda b: (b, 0, 0)),
            pl.BlockSpec((None, s, LANES), lambda b: (b, 0, 0)),
        ],
        out_shape=[jax.ShapeDtypeStruct((nb, N_GATES, s), F32), jax.ShapeDtypeStruct((nb, s, LANES), F32),
                   jax.ShapeDtypeStruct((nb, s, LANES), BF16)],
        compiler_params=_params(("parallel",)),
        name="gateprep",
    )(gt3, bias)


def _causal_mask_t(d):
    s = lax.broadcasted_iota(jnp.int32, (TK, TQ), 0) + d * TK
    t = lax.broadcasted_iota(jnp.int32, (TK, TQ), 1)
    return s <= t


def _mlstm_body(q_ref, k_ref, vt_ref, o_ref, cols_ref, rows_ref, gh_ref, y_ref, num_scr, den_scr):
    i = pl.program_id(1)
    rows = rows_ref[...]
    num_scr[...] = jnp.zeros_like(num_scr)
    den_scr[...] = jnp.zeros_like(den_scr)

    def scores(j, h):
        ks = pl.ds(pl.multiple_of(j * TK, TK), TK)
        qk = slice(h * ML_QK_DIM, (h + 1) * ML_QK_DIM)
        return _dot_nt(k_ref[ks, qk], q_ref[:, qk])

    def update(j, h, s, mask):
        ks = pl.ds(pl.multiple_of(j * TK, TK), TK)
        vv = slice(h * ML_V_DIM, (h + 1) * ML_V_DIM)
        w = jnp.exp2(cols_ref[ks, h:h + 1] - rows[h:h + 1])
        if mask is not None:
            w = jnp.where(mask, w, 0.0)
        s = s * w
        den_scr[h] += jnp.sum(s, axis=0, keepdims=True)
        num_scr[h] += jnp.dot(vt_ref[j, vv, :], s.astype(BF16), preferred_element_type=F32)

    def key_tile(j, mask):
        s = {h: scores(j, h) for h in range(min(MXU_LOOKAHEAD, ML_HEADS))}
        for h in range(ML_HEADS):
            if h + MXU_LOOKAHEAD < ML_HEADS:
                s[h + MXU_LOOKAHEAD] = scores(j, h + MXU_LOOKAHEAD)
            update(j, h, s.pop(h), mask)

    def body(j, carry):
        key_tile(j, None)
        return carry

    lax.fori_loop(0, i * KT_PER_Q, body, 0)
    for d in range(KT_PER_Q):
        key_tile(i * KT_PER_Q + d, _causal_mask_t(d))
    for h in range(ML_HEADS):
        vv = slice(h * ML_V_DIM, (h + 1) * ML_V_DIM)
        floor = jnp.exp(-rows[ML_HEADS + h:ML_HEADS + h + 1])
        hh = num_scr[h] * (1.0 / jnp.maximum(jnp.abs(den_scr[h]), floor))
        yt = hh * lax.rsqrt(jnp.mean(hh * hh, axis=0, keepdims=True) + EPS)
        y = yt.T * gh_ref[:, vv]
        y_ref[:, vv] = (y * jax.nn.sigmoid(o_ref[:, vv].astype(F32))).astype(BF16)


def _mlstm(z3, vt4, cols3, rows3, g_head):
    nb, s, _ = z3.shape
    nq = s // TQ
    return pl.pallas_call(
        _mlstm_body,
        grid=(nb, nq),
        in_specs=[
            pl.BlockSpec((None, TQ, ML_HEADS * ML_QK_DIM), lambda b, i: (b, i, 0)),
            pl.BlockSpec((None, s, ML_HEADS * ML_QK_DIM), lambda b, i: (b, 0, 1)),
            pl.BlockSpec((None, s // TK, D_MODEL, TK), lambda b, i: (b, 0, 0, 0)),
            pl.BlockSpec((None, TQ, D_MODEL), lambda b, i: (b, i, Z_ML_O)),
            pl.BlockSpec((None, s, LANES), lambda b, i: (b, 0, 0)),
            pl.BlockSpec((None, N_GATES, TQ), lambda b, i: (b, 0, i)),
            _resident((1, D_MODEL)),
        ],
        out_specs=pl.BlockSpec((None, TQ, D_MODEL), lambda b, i: (b, i, 0)),
        out_shape=jax.ShapeDtypeStruct((nb, s, D_MODEL), BF16),
        scratch_shapes=[pltpu.VMEM((ML_HEADS, ML_V_DIM, TQ), F32), pltpu.VMEM((ML_HEADS, 1, TQ), F32)],
        compiler_params=_params(("parallel", "arbitrary")),
        name="mlstm",
    )(z3, z3, vt4, z3, cols3, rows3, g_head)


def _fox_body(q_ref, k_ref, vt_ref, caug_ref, rows_ref, y_ref, qa_scr, acc_scr, m_scr, l_scr):
    i = pl.program_id(1)
    rows = rows_ref[...]
    lane = lax.broadcasted_iota(jnp.int32, (TQ, LANES), 1)
    for h in range(FX_HEADS):
        hd = slice(h * FX_HEAD_DIM, (h + 1) * FX_HEAD_DIM)
        ones = jnp.where((lane < 3 * FX_HEADS) & (lane % FX_HEADS == h), 1.0, 0.0).astype(BF16)
        qa_scr[h] = jnp.concatenate([q_ref[:, hd], ones], axis=1)
    m_scr[...] = jnp.full_like(m_scr, -jnp.inf)
    l_scr[...] = jnp.zeros_like(l_scr)
    acc_scr[...] = jnp.zeros_like(acc_scr)

    def scores(j, h):
        ks = pl.ds(pl.multiple_of(j * TK, TK), TK)
        hd = slice(h * FX_HEAD_DIM, (h + 1) * FX_HEAD_DIM)
        k_aug =jnp.concatenate([k_ref[ks, hd], caug_ref[ks, :]], axis=1)
        return _dot_nt(k_aug, qa_scr[h])

    def update(j, h, u, mask):
        hd = slice(h * FX_HEAD_DIM, (h + 1) * FX_HEAD_DIM)
        if mask is not None:
            u = jnp.where(mask, u, -jnp.inf)
        c_row = rows[2 * ML_HEADS + h:2 * ML_HEADS + h + 1]
        m_prev = m_scr[h]
        m_new = jnp.maximum(m_prev, jnp.max(u, axis=0, keepdims=True) + c_row)
        p = jnp.exp2(u - (m_new - c_row))
        alpha = jnp.exp2(m_prev - m_new)
        l_scr[h] = alpha * l_scr[h] + jnp.sum(p, axis=0, keepdims=True)
        acc_scr[h] = alpha * acc_scr[h] + jnp.dot(vt_ref[j, hd, :], p.astype(BF16), preferred_element_type=F32)
        m_scr[h] = m_new

    def key_tile(j, mask):
        u = {h: scores(j, h) for h in range(MXU_LOOKAHEAD)}
        for h in range(FX_HEADS):
            if h + MXU_LOOKAHEAD < FX_HEADS:
                u[h + MXU_LOOKAHEAD] = scores(j, h + MXU_LOOKAHEAD)
            update(j, h, u.pop(h), mask)

    def body(j, carry):
        key_tile(j, None)
        return carry

    lax.fori_loop(0, i * KT_PER_Q, body, 0)
    for d in range(KT_PER_Q):
        key_tile(i * KT_PER_Q + d, _causal_mask_t(d))
    for h in range(FX_HEADS):
        hd = slice(h * FX_HEAD_DIM, (h + 1) * FX_HEAD_DIM)
        y_ref[:, hd] = (acc_scr[h] * (1.0 / l_scr[h])).T.astype(BF16)


def _fox(z3, vt4, caug3, rows3):
    nb, s, _ = z3.shape
    nq = s // TQ
    return pl.pallas_call(
        _fox_body,
        grid=(nb, nq),
        in_specs=[
            pl.BlockSpec((None, TQ, D_MODEL), lambda b, i: (b, i, Z_FX_Q)),
            pl.BlockSpec((None, s, D_MODEL), lambda b, i: (b, 0, Z_FX_K)),
            pl.BlockSpec((None, s // TK, D_MODEL, TK), lambda b, i: (b, 0, 1, 0)),
            pl.BlockSpec((None, s, LANES), lambda b, i: (b, 0, 0)),
            pl.BlockSpec((None, N_GATES, TQ), lambda b, i: (b, 0, i)),
        ],
        out_specs=pl.BlockSpec((None, TQ, D_MODEL), lambda b, i: (b, i, 0)),
        out_shape=jax.ShapeDtypeStruct((nb, s, D_MODEL), BF16),
        scratch_shapes=[pltpu.VMEM((FX_HEADS, TQ, 2 * FX_HEAD_DIM), BF16),
                        pltpu.VMEM((FX_HEADS, FX_HEAD_DIM, TQ), F32),
                        pltpu.VMEM((FX_HEADS, 1, TQ), F32), pltpu.VMEM((FX_HEADS, 1, TQ), F32)],
        compiler_params=_params(("parallel", "arbitrary")),
        name="fox",
    )(z3, z3, vt4, caug3, rows3)


def _merge_body(x_ref, yml_ref, yfx_ref, gml_ref, gfx_ref, bml_ref, bfx_ref, wml_ref, wfx_ref, wout_ref, x1_ref):
    p_ml = jnp.dot(yml_ref[...], wml_ref[...], preferred_element_type=F32)
    p_fx = jnp.dot(yfx_ref[...], wfx_ref[...], preferred_element_type=F32)
    merged = (jax.nn.sigmoid(gml_ref[...].astype(F32) + bml_ref[...]) * p_ml
              + jax.nn.sigmoid(gfx_ref[...].astype(F32) + bfx_ref[...]) * p_fx)
    x1_ref[...] = x_ref[...] + jnp.dot(merged.astype(BF16), wout_ref[...], preferred_element_type=F32)


def _merge(x2d, yml, yfx, z2d, bml, bfx, wml, wfx, wout):
    t = x2d.shape[0]
    tile = lambda col: pl.BlockSpec((TM, D_MODEL), lambda i, col=col: (i, col))
    return pl.pallas_call(
        _merge_body,
        grid=(t // TM,),
        in_specs=[tile(0), tile(0), tile(0), tile(Z_GT_ML), tile(Z_GT_FX),
                  _resident((1, D_MODEL)), _resident((1, D_MODEL)),
                  _resident((D_MODEL, D_MODEL)), _resident((D_MODEL, D_MODEL)), _resident((D_MODEL, D_MODEL))],
        out_specs=tile(0),
        out_shape=jax.ShapeDtypeStruct((t, D_MODEL), F32),
        compiler_params=_params(("parallel",)),
        name="merge",
    )(x2d, yml, yfx, z2d, z2d, bml, bfx, wml, wfx, wout)


def _memkv_body(m_ref, g_ref, w_ref, kv_ref):
    hb = _rms(m_ref[...], g_ref[...]).astype(BF16)
    kv_ref[...] = jnp.dot(hb, w_ref[...], preferred_element_type=F32).astype(BF16)


def _memkv(mem2d, g, w):
    t = mem2d.shape[0]
    return pl.pallas_call(
        _memkv_body,
        grid=(t // TM,),
        in_specs=[pl.BlockSpec((TM, D_MODEL), lambda i: (i, 0)), _resident((1, D_MODEL)),
                  _resident((D_MODEL, 2 * D_MODEL))],
        out_specs=pl.BlockSpec((TM, 2 * D_MODEL), lambda i: (i, 0)),
        out_shape=jax.ShapeDtypeStruct((t, 2 * D_MODEL), BF16),
        compiler_params=_params(("parallel",)),
        name="memkv",
    )(mem2d, g, w)


def _route_t(lg_t):
    tm = lg_t.shape[1]
    ninf = -jnp.inf
    big = jnp.float32(LANES)
    gid = lax.broadcasted_iota(jnp.int32, (8, tm), 0).astype(F32)
    eid = lax.broadcasted_iota(jnp.int32, (N_EXPERTS, tm), 0).astype(F32)
    gl = jnp.where(gid < N_GROUPS, lg_t[N_EXPERTS:N_EXPERTS + 8], ninf)
    gmax = jnp.max(gl, axis=0, keepdims=True)
    gidx = jnp.min(jnp.where(gl == gmax, gid, big), axis=0, keepdims=True)
    g_p = 1.0 / jnp.sum(jnp.exp(gl - gmax), axis=0, keepdims=True)
    lo = gidx * EXPERTS_PER_GROUP
    el = jnp.where(eid >= lo, jnp.where(eid < lo + EXPERTS_PER_GROUP, lg_t[0:N_EXPERTS], ninf), ninf)
    v1 = jnp.max(el, axis=0, keepdims=True)
    i1 = jnp.min(jnp.where(el == v1, eid, big), axis=0, keepdims=True)
    el2 = jnp.where(eid == i1, ninf, el)
    v2 = jnp.max(el2, axis=0, keepdims=True)
    i2 = jnp.min(jnp.where(el2 == v2, eid, big), axis=0, keepdims=True)
    t = jnp.exp(v2 - v1)
    w1 = g_p / (1.0 + t)
    w2 = w1 * t
    return jnp.concatenate([i1, i2, w1, w2, jnp.zeros((4, tm), F32)], axis=0)


def _xattn_body(x1_ref, kv_ref, gq_ref, wq_ref, wo_ref, gm_ref, wrt_ref, brt_ref, x2_ref, h3_ref, route_ref,
                wcol_ref):
    x1 = x1_ref[...]
    hb = _rms(x1, gq_ref[...]).astype(BF16)
    q = (jnp.dot(hb, wq_ref[...], preferred_element_type=F32) * (XA_HEAD_DIM ** -0.5)).astype(BF16)
    outs = []
    for h in range(XA_HEADS):
        hd = slice(h * XA_HEAD_DIM, (h + 1) * XA_HEAD_DIM)
        vd = slice(D_MODEL + h * XA_HEAD_DIM, D_MODEL + (h + 1) * XA_HEAD_DIM)
        s = _dot_nt(q[:, hd], kv_ref[:, hd])
        p = jnp.exp(s - jnp.max(s, axis=1, keepdims=True))
        p = p * (1.0 / jnp.sum(p, axis=1, keepdims=True))
        outs.append(jnp.dot(p.astype(BF16), kv_ref[:, vd], preferred_element_type=F32).astype(BF16))
    o = jnp.concatenate(outs, axis=1)
    x2 = x1 + jnp.dot(o, wo_ref[...], preferred_element_type=F32)
    x2_ref[...] = x2
    h3 = _rms(x2, gm_ref[...])
    h3_ref[...] = h3
    lg_t = _dot_nt_split(wrt_ref[...], h3) + brt_ref[...]
    route = _route_t(lg_t)
    route_ref[...] = route
    wcol_ref[...] = jnp.concatenate([route, jnp.zeros((LANES - 8, route.shape[1]), F32)], axis=0).T


def _xattn(x1, kv3, gq, wq, wo, gm, wrt, brt, seq):
    t = x1.shape[0]
    per_b = seq // TM
    n_mem = kv3.shape[1]
    tile = pl.BlockSpec((TM, D_MODEL), lambda i: (i, 0))
    return pl.pallas_call(
        _xattn_body,
        grid=(t // TM,),
        in_specs=[tile,
                  pl.BlockSpec((None, n_mem, 2 * D_MODEL), lambda i: (i // per_b, 0, 0)),
                  _resident((1, D_MODEL)), _resident((D_MODEL, D_MODEL)), _resident((D_MODEL, D_MODEL)),
                  _resident((1, D_MODEL)), _resident((ROUTER_ROWS, D_MODEL)), _resident((ROUTER_ROWS, 1))],
        out_specs=[tile, tile, pl.BlockSpec((8, TM), lambda i: (0, i)), pl.BlockSpec((TM, LANES), lambda i: (i, 0))],
        out_shape=[jax.ShapeDtypeStruct((t, D_MODEL), F32), jax.ShapeDtypeStruct((t, D_MODEL), F32),
                   jax.ShapeDtypeStruct((8, t), F32), jax.ShapeDtypeStruct((t, LANES), F32)],
        compiler_params=_params(("parallel",)),
        name="xattn",
    )(x1, kv3, gq, wq, wo, gm, wrt, brt)


def _n_expert_tiles(n_tokens):
    return 2 * n_tokens // TME + N_EXPERTS


def _plan_body(route_ref, pos_ref, tinfo_ref, tok_ref, cnt_scr, run_scr, start_scr, tokhi_scr, toklo_scr, *, nt_pad,
               n_tokens, first_token):
    phase = pl.program_id(0)
    b = pl.program_id(1)
    r = route_ref[...]
    eid = lax.broadcasted_iota(jnp.int32, (N_EXPERTS, TP), 0).astype(F32)
    oh1 = eid == r[0:1]
    oh2 = eid == r[1:2]
    oh = jnp.where(oh1 | oh2, 1.0, 0.0)

    @pl.when((phase == 0) & (b == 0))
    def _():
        cnt_scr[...] = jnp.zeros_like(cnt_scr)

    @pl.when(phase == 0)
    def _():
        cnt_scr[...] += jnp.sum(oh, axis=1, keepdims=True)

    @pl.when((phase == 1) & (b == 0))
    def _():
        cnt = cnt_scr[...]
        n_tiles = jnp.floor((cnt + (TME - 1)) * (1.0 / TME))
        ri = lax.broadcasted_iota(jnp.int32, (N_EXPERTS, N_EXPERTS), 0)
        ci = lax.broadcasted_iota(jnp.int32, (N_EXPERTS, N_EXPERTS), 1)
        lower = jnp.where(ci < ri, 1.0, 0.0).astype(BF16)
        nt_hi, nt_lo = _split_bf16(jnp.broadcast_to(n_tiles, (N_EXPERTS, LANES)))
        start = (jnp.dot(lower, nt_hi, preferred_element_type=F32)
                 + jnp.dot(lower, nt_lo, preferred_element_type=F32))[:, 0:1]
        start_scr[...] = start * TME
        run_scr[...] = jnp.zeros_like(run_scr)
        tokhi_scr[...] = jnp.zeros_like(tokhi_scr)
        toklo_scr[...] = jnp.zeros_like(toklo_scr)
        n = lax.broadcasted_iota(jnp.int32, (N_EXPERTS, nt_pad), 1).astype(F32)
        e_n = lax.broadcasted_iota(jnp.int32, (N_EXPERTS, nt_pad), 0).astype(F32)
        owner = jnp.sum(jnp.where(start <= n, 1.0, 0.0), axis=0, keepdims=True) - 1.0
        own = e_n == owner
        cnt_o = jnp.sum(jnp.where(own, cnt, 0.0), axis=0, keepdims=True)
        start_o = jnp.sum(jnp.where(own, start, 0.0), axis=0, keepdims=True)
        valid = jnp.clip(cnt_o - (n[0:1] - start_o) * TME, 0.0, float(TME))
        tinfo_ref[...] = jnp.concatenate([owner, valid, jnp.zeros((6, nt_pad), F32)], axis=0).astype(jnp.int32)

    @pl.when(phase == 1)
    def _():
        ti = lax.broadcasted_iota(jnp.int32, (TP, TP), 0)
        tj = lax.broadcasted_iota(jnp.int32, (TP, TP), 1)
        upper = jnp.where(ti < tj, 1.0, 0.0).astype(BF16)
        before = jnp.dot(oh.astype(BF16), upper, preferred_element_type=F32)
        row = start_scr[...] + run_scr[...] + before
        p1 = jnp.sum(jnp.where(oh1, row, 0.0), axis=0, keepdims=True)
        p2 = jnp.sum(jnp.where(oh2, row, 0.0), axis=0, keepdims=True)
        pos_ref[...] = jnp.concatenate([p1, p2, jnp.zeros((6, TP), F32)], axis=0).astype(jnp.int32)
        run_scr[...] += jnp.sum(oh, axis=1, keepdims=True)

        tid = lax.broadcasted_iota(jnp.int32, (1, TP), 1) + (b * TP + first_token + 1)
        t_hi = (tid // 256).astype(F32)
        t_lo = (tid % 256).astype(F32)
        tile_id = lax.broadcasted_iota(jnp.int32, (nt_pad, TP), 0).astype(F32)
        row_id = lax.broadcasted_iota(jnp.int32, (TME, TP), 0).astype(F32)
        for p in (p1, p2):
            hi = jnp.floor(p * (1.0 / TME))
            lo = p - hi * TME
            in_tile = jnp.where(tile_id == hi, 1.0, 0.0).astype(BF16)
            at_row = row_id == lo
            tokhi_scr[...] += _dot_nt(in_tile, jnp.where(at_row, t_hi, 0.0).astype(BF16))
            toklo_scr[...] += _dot_nt(in_tile, jnp.where(at_row, t_lo, 0.0).astype(BF16))

    @pl.when((phase == 1) & (b == pl.num_programs(1) - 1))
    def _():
        hit = (tokhi_scr[...] * 256.0 + toklo_scr[...]).astype(jnp.int32)
        row = (lax.broadcasted_iota(jnp.int32, hit.shape, 0) * TME + lax.broadcasted_iota(jnp.int32, hit.shape, 1))
        tok_ref[...] = jnp.where(hit > 0, hit - 1, row % n_tokens)


def _plan(route, part):
    t_all = route.shape[1]
    assert t_all < 256 * 256, "token id + 1 is carried as two base-256 digits"
    t = t_all // MOE_PARTS
    steps = t // TP
    nt_pad = -(-_n_expert_tiles(t) // LANES) * LANES
    col = pltpu.VMEM((N_EXPERTS, 1), F32)
    table = pltpu.VMEM((nt_pad, TME), F32)
    return pl.pallas_call(
        functools.partial(_plan_body, nt_pad=nt_pad, n_tokens=t_all, first_token=part * t),
        grid=(2, steps),
        in_specs=[pl.BlockSpec((8, TP), lambda ph, b: (0, b + part * steps))],
        out_specs=[pl.BlockSpec((8, TP), lambda ph, b: (0, b * ph)),
                   pl.BlockSpec((8, nt_pad), lambda ph, b: (0, 0)),
                   pl.BlockSpec((nt_pad, TME), lambda ph, b: (0, 0))],
        out_shape=[jax.ShapeDtypeStruct((8, t), jnp.int32), jax.ShapeDtypeStruct((8, nt_pad), jnp.int32),
                   jax.ShapeDtypeStruct((nt_pad, TME), jnp.int32)],
        scratch_shapes=[col, col, col, table, table],
        compiler_params=_params(("arbitrary", "arbitrary")),
        name="plan",
    )(route)


def _experts_body(texp_ref, tvalid_ref, xs_ref, wg_ref, wu_ref, wd_ref, ys_ref, wgb_scr, wub_scr, wdb_scr):
    n = pl.program_id(0)

    @pl.when((n == 0) | (texp_ref[n] != texp_ref[jnp.maximum(n - 1, 0)]))
    def _():
        wgb_scr[...] = wg_ref[...].astype(BF16)
        wub_scr[...] = wu_ref[...].astype(BF16)
        wdb_scr[...] = wd_ref[...].astype(BF16)

    @pl.when(tvalid_ref[n] > 0)
    def _():
        xb = xs_ref[...].astype(BF16)
        gate = jnp.dot(xb, wgb_scr[...], preferred_element_type=F32)
        up = jnp.dot(xb, wub_scr[...], preferred_element_type=F32)
        he = (gate * jax.nn.sigmoid(gate) * up).astype(BF16)
        ys_ref[...] = jnp.dot(he, wdb_scr[...], preferred_element_type=F32)

    @pl.when(tvalid_ref[n] == 0)
    def _():
        ys_ref[...] = jnp.zeros_like(ys_ref)


def _experts(texp, tvalid, xs, wg, wu, wd):
    nt = texp.shape[0]
    weight = lambda shape: pl.BlockSpec((None,) + shape, lambda n, te, tv: (te[n], 0, 0))
    return pl.pallas_call(
        _experts_body,
        grid_spec=pltpu.PrefetchScalarGridSpec(
            num_scalar_prefetch=2,
            grid=(nt,),
            in_specs=[pl.BlockSpec((TME, D_MODEL), lambda n, te, tv: (jnp.where(tv[n] > 0, n, 0), 0)),
                      weight((D_MODEL, D_EXPERT)), weight((D_MODEL, D_EXPERT)), weight((D_EXPERT, D_MODEL))],
            out_specs=pl.BlockSpec((TME, D_MODEL), lambda n, te, tv: (n, 0)),
            scratch_shapes=[pltpu.VMEM((D_MODEL, D_EXPERT), BF16), pltpu.VMEM((D_MODEL, D_EXPERT), BF16),
                            pltpu.VMEM((D_EXPERT, D_MODEL), BF16)],
        ),
        out_shape=jax.ShapeDtypeStruct((nt * TME, D_MODEL), F32),
        compiler_params=_params(("arbitrary",)),
        name="experts",
    )(texp, tvalid, xs, wg, wu, wd)


def _rowgather(table, idx):
    n_rows, width = idx.shape[0], table.shape[1]
    n_workers = SC_CORES * SC_SUBCORES
    per_worker = n_rows // n_workers
    assert per_worker * n_workers == n_rows and per_worker % SC_ROWS == 0
    mesh = plsc.VectorSubcoreMesh(core_axis_name="c", subcore_axis_name="s", num_cores=SC_CORES,
                                  num_subcores=SC_SUBCORES)

    chunks = per_worker // SC_ROWS
    assert chunks % 2 == 0
    buf = lambda: [pltpu.VMEM((SC_ROWS,), jnp.int32), pltpu.VMEM((SC_ROWS, width), table.dtype),
                   pltpu.SemaphoreType.DMA]

    @functools.partial(pl.kernel, mesh=mesh, out_type=jax.ShapeDtypeStruct((n_rows, width), table.dtype),
                       scratch_types=buf() + buf(), name="rowgather")
    def gather(table_hbm, idx_hbm, out_hbm, idx_a, rows_a, sem_a, idx_b, rows_b, sem_b):
        worker = lax.axis_index("s") * SC_CORES + lax.axis_index("c")
        base = worker * per_worker

        def fetch(c, idx_v, rows_v, sem):
            pltpu.sync_copy(idx_hbm.at[pl.ds(base + c * SC_ROWS, SC_ROWS)], idx_v)
            pltpu.async_copy(table_hbm.at[idx_v], rows_v, sem)

        def flush(c, idx_v, rows_v, sem):
            pltpu.make_async_copy(table_hbm.at[idx_v], rows_v, sem).wait()
            pltpu.sync_copy(rows_v, out_hbm.at[pl.ds(base + c * SC_ROWS, SC_ROWS)])

        fetch(0, idx_a, rows_a, sem_a)

        @pl.loop(0, chunks, step=2)
        def _(c):
            fetch(c + 1, idx_b, rows_b, sem_b)
            flush(c, idx_a, rows_a, sem_a)

            @pl.when(c + 2 < chunks)
            def _():
                fetch(c + 2, idx_a, rows_a, sem_a)

            flush(c + 1, idx_b, rows_b, sem_b)

    return gather(table, idx)


def _combine_body(g1_ref, g2_ref, x2_ref, wcol_ref, gf_ref, *rest):
    out_ref = rest[-1]
    w = wcol_ref[...]
    y = w[:, 2:3] * g1_ref[...] + w[:, 3:4] * g2_ref[...]
    out_ref[...] = _rms(x2_ref[...] + y, gf_ref[...])


def _combine(g, x2, wcol, gf, part, out_prev):
    t = x2.shape[0]
    steps = t // MOE_PARTS // TF
    mine = lambda w: pl.BlockSpec((TF, w), lambda i: (i + part * steps, 0))
    in_specs = [pl.BlockSpec((TF, D_MODEL), lambda i: (i, 0)),
                pl.BlockSpec((TF, D_MODEL), lambda i: (i + steps, 0)),
                mine(D_MODEL), mine(LANES), _resident((1, D_MODEL))]
    operands = [g, g, x2, wcol, gf]
    aliases = {}
    if out_prev is not None:
        in_specs.append(pl.BlockSpec(memory_space=pl.ANY))
        operands.append(out_prev)
        aliases = {len(operands) - 1: 0}
    return pl.pallas_call(
        _combine_body,
        grid=(steps,),
        in_specs=in_specs,
        out_specs=mine(D_MODEL),
        out_shape=jax.ShapeDtypeStruct((t, D_MODEL), F32),
        input_output_aliases=aliases,
        compiler_params=_params(("parallel",)),
        name="combine",
    )(*operands)


def _layer(x, mem, g_mix, w_in, b_ml_i, b_ml_f, b_fx_f, b_gate_ml, b_gate_fx, g_ml_head, w_proj_ml, w_proj_fx,
           w_out, g_xq, g_xmem, w_xq, w_xkv, w_xo, g_moe, w_rg, b_rg, w_re, b_re, w_gate, w_up, w_down):
    nb, seq, d = x.shape
    t = nb * seq
    row = lambda v: v.reshape(1, -1).astype(F32)

    o = 0
    parts = {}
    for name, width in (("ml_q", 512), ("ml_k", 512), ("ml_v", 1024), ("ml_o", 1024), ("ml_i", 4), ("ml_f", 4),
                        ("fx_q", 1024), ("fx_k", 1024), ("fx_v", 1024), ("fx_f", 8), ("gt_ml", 1024),
                        ("gt_fx", 1024)):
        parts[name] = w_in[:, o:o + width]
        o += width
    w_main = jnp.concatenate(
        [parts["ml_q"], parts["ml_k"] * (ML_QK_DIM ** -0.5), parts["ml_o"],
         parts["fx_q"] * (FX_HEAD_DIM ** -0.5 * LOG2E), parts["fx_k"], parts["gt_ml"], parts["gt_fx"]],
        axis=1).astype(BF16)
    w_vt = jnp.concatenate([parts["ml_v"], parts["fx_v"]], axis=1).T.astype(BF16)
    w_gates_t = jnp.concatenate([parts["ml_i"], parts["ml_f"], parts["fx_f"]], axis=1).T
    gate_bias = jnp.concatenate([b_ml_i, b_ml_f, b_fx_f]).reshape(N_GATES, 1).astype(F32)

    x2d = x.reshape(t, d)
    z, vt, gates_t = _inproj(x2d, row(g_mix), w_main, w_vt, w_gates_t)
    gt3 = gates_t.reshape(N_GATES, nb, seq).transpose(1, 0, 2)
    rows, cols, caug = _gateprep(gt3, gate_bias)
    z3 = z.reshape(nb, seq, Z_W)
    vt4 = vt.reshape(nb, seq // TK, VT_W, TK)
    y_ml = _mlstm(z3, vt4, cols, rows, row(g_ml_head))
    y_fx = _fox(z3, vt4, caug, rows)
    x1 = _merge(x2d, y_ml.reshape(t, d), y_fx.reshape(t, d), z, row(b_gate_ml), row(b_gate_fx),
                w_proj_ml.astype(BF16), w_proj_fx.astype(BF16), w_out.astype(BF16))

    n_mem = mem.shape[1]
    kv = _memkv(mem.reshape(nb * n_mem, d), row(g_xmem), w_xkv.astype(BF16))
    w_router_t = jnp.concatenate([w_re, w_rg, jnp.zeros((d, ROUTER_ROWS - N_EXPERTS - N_GROUPS), F32)], axis=1).T
    b_router_t = jnp.concatenate([b_re, b_rg, jnp.zeros((ROUTER_ROWS - N_EXPERTS - N_GROUPS,), F32)]).reshape(ROUTER_ROWS, 1)
    x2, h3, route, wcol = _xattn(x1, kv.reshape(nb, n_mem, 2 * d), row(g_xq), w_xq.astype(BF16),
                                 w_xo.astype(BF16), row(g_moe), w_router_t, b_router_t, seq)
    return x2, h3, route, wcol


def _moe(x2, h3, route, wcol, w_gate, w_up, w_down, g_final):
    t = x2.shape[0]
    nt = _n_expert_tiles(t // MOE_PARTS)
    out = None
    for part in range(MOE_PARTS):
        pos, tinfo, tok = _plan(route, part)
        texp, tvalid = tinfo[0, :nt], tinfo[1, :nt]
        xs = _rowgather(h3, tok[:nt].reshape(-1))
        ys = _experts(texp, tvalid, xs, w_gate, w_up, w_down)
        g = _rowgather(ys, pos[0:2].reshape(-1))
        out = _combine(g, x2, wcol, g_final, part, out)
    return out


def kernel(x, mem, g_mix, w_in, b_ml_i, b_ml_f, b_fx_f, b_gate_ml, b_gate_fx, g_ml_head, w_proj_ml, w_proj_fx, w_out, g_xq, g_xmem, w_xq, w_xkv, w_xo, g_moe, w_rg, b_rg, w_re, b_re, w_gate, w_up, w_down, g_final):
    nb, seq, d = x.shape
    depth = g_mix.shape[0]
    assert depth == 1, "the final rmsnorm is fused into the (single) layer's combine kernel"
    x2, h3, route, wcol = _layer(
        x, mem, g_mix[0], w_in[0], b_ml_i[0], b_ml_f[0], b_fx_f[0], b_gate_ml[0], b_gate_fx[0], g_ml_head[0],
        w_proj_ml[0], w_proj_fx[0], w_out[0], g_xq[0], g_xmem[0], w_xq[0], w_xkv[0], w_xo[0], g_moe[0],
        w_rg[0], b_rg[0], w_re[0], b_re[0], w_gate[0], w_up[0], w_down[0])
    out = _moe(x2, h3, route, wcol, w_gate[0], w_up[0], w_down[0], g_final.reshape(1, d).astype(F32))
    return out.reshape(nb, seq, d)
```

```python
import functools

import jax
import jax.numpy as jnp
from jax import lax
from jax.experimental import pallas as pl
from jax.experimental.pallas import tpu as pltpu
from jax.experimental.pallas import tpu_sc as plsc

F32 = jnp.float32
BF16 = jnp.bfloat16

D_MODEL = 1024
EPS = 1e-6
ML_HEADS = 4
ML_QK_DIM = 128
ML_V_DIM = 256
FX_HEADS = 8
FX_HEAD_DIM = 128
XA_HEADS = 4
XA_HEAD_DIM = 256
N_GROUPS = 4
EXPERTS_PER_GROUP = 8
N_EXPERTS = 32
D_EXPERT = 512

LANES = 128
N_GATES = 16
ROUTER_ROWS = 40
LOG2E = 1.4426950408889634
Z_W = 6144
Z_ML_O, Z_FX_Q, Z_FX_K, Z_GT_ML, Z_GT_FX = 1, 2, 3, 4, 5
VT_W = 2048

VMEM_LIMIT = 56 * 1024 * 1024

IN_CHUNK = 1024
TK = 256
TQ = 512
TIN = 512
KT_PER_Q = TQ // TK
MXU_LOOKAHEAD = 4
TM = 512
TME = 256
TP = 512
TF = 512
MOE_PARTS = 1
SC_CORES, SC_SUBCORES = 2, 16
PACKED_W = D_MODEL // 2
SC_ROWS = 64


def _params(sem, flags=None):
    return pltpu.CompilerParams(dimension_semantics=sem, vmem_limit_bytes=VMEM_LIMIT, flags=flags)


def _rms(x, g):
    return x * lax.rsqrt(jnp.mean(x * x, axis=-1, keepdims=True) + EPS) * g


def _dot_nt(a, b, **kw):
    return lax.dot_general(a, b, (((1,), (1,)), ((), ())), preferred_element_type=F32, **kw)


def _split_bf16(x):
    hi = x.astype(BF16)
    return hi, (x - hi.astype(F32)).astype(BF16)


def _dot_nt_split(a, b):
    a_hi, a_lo = _split_bf16(a)
    b_hi, b_lo = _split_bf16(b)
    return _dot_nt(a_hi, b_hi) + (_dot_nt(a_hi, b_lo) + _dot_nt(a_lo, b_hi))


def _pack_halves(x):
    n = x.shape[1] // 2
    bits = lambda v: lax.bitcast_convert_type(v.astype(BF16).astype(F32), jnp.uint32)
    w = (bits(x[:, :n]) >> 16) | (bits(x[:, n:]) & jnp.uint32(0xFFFF0000))
    return lax.bitcast_convert_type(w, F32)


def _unpack_halves(p):
    w = lax.bitcast_convert_type(p, jnp.uint32)
    lo = lax.bitcast_convert_type(w << 16, F32)
    hi = lax.bitcast_convert_type(w & jnp.uint32(0xFFFF0000), F32)
    return jnp.concatenate([lo, hi], axis=1)


def _resident(shape):
    zeros = (0,) * len(shape)
    return pl.BlockSpec(shape, lambda *_: zeros, pipeline_mode=pl.Buffered(1))


def _inproj_body(x_ref, g_ref, w_ref, wvt_ref, wgt_ref, z_ref, vt_ref, gt_ref):
    h = _rms(x_ref[...], g_ref[...])
    hb = h.astype(BF16)
    for c in range(Z_W // IN_CHUNK):
        sl = slice(c * IN_CHUNK, (c + 1) * IN_CHUNK)
        z_ref[:, sl] = jnp.dot(hb, w_ref[:, sl], preferred_element_type=F32).astype(BF16)
    for kt in range(TIN // TK):
        hk = hb[kt * TK:(kt + 1) * TK]
        for c in range(VT_W // IN_CHUNK):
            sl = slice(c * IN_CHUNK, (c + 1) * IN_CHUNK)
            vt_ref[kt, sl, :] = _dot_nt(wvt_ref[sl, :], hk).astype(BF16)
    gt_ref[...] = _dot_nt_split(wgt_ref[...], h)


def _inproj(x2d, g, w, wvt, wgt):
    t = x2d.shape[0]
    return pl.pallas_call(
        _inproj_body,
        grid=(t // TIN,),
        in_specs=[
            pl.BlockSpec((TIN, D_MODEL), lambda i: (i, 0)),
            _resident((1, D_MODEL)),
            _resident((D_MODEL, Z_W)),
            _resident((VT_W, D_MODEL)),
            _resident((N_GATES, D_MODEL)),
        ],
        out_specs=[
            pl.BlockSpec((TIN, Z_W), lambda i: (i, 0)),
            pl.BlockSpec((TIN // TK, VT_W, TK), lambda i: (i, 0, 0)),
            pl.BlockSpec((N_GATES, TIN), lambda i: (0, i)),
        ],
        out_shape=[jax.ShapeDtypeStruct((t, Z_W), BF16), jax.ShapeDtypeStruct((t // TK, VT_W, TK), BF16),
                   jax.ShapeDtypeStruct((N_GATES, t), F32)],
        compiler_params=_params(("parallel",)),
        name="inproj",
    )(x2d, g, w, wvt, wgt)


def _scan_lanes(x, op, identity):
    n = x.shape[-1]
    idx = lax.broadcasted_iota(jnp.int32, x.shape, 1)
    s = 1
    while s < n:
        shifted = pltpu.roll(x, s, axis=1)
        x = op(x, jnp.where(idx >= s, shifted, identity))
        s *= 2
    return x


def _log_sigmoid(x):
    return jnp.minimum(x, 0.0) - jnp.log1p(jnp.exp(-jnp.abs(x)))


def _gateprep_body(gt_ref, bias_ref, rows_ref, cols_ref, caug_ref):
    g = gt_ref[...] + bias_ref[...]
    s = g.shape[1]
    cs = _scan_lanes(_log_sigmoid(g), jnp.add, 0.0)
    b = cs[4:8]
    c2 = cs[8:16] * LOG2E
    a = g[0:4] - b
    m = _scan_lanes(jnp.concatenate([a, a], axis=0), jnp.maximum, -jnp.inf)[0:4]
    rows_ref[...] = jnp.concatenate([m * LOG2E, b + m, c2], axis=0)
    cols_ref[...] = jnp.concatenate([a * LOG2E, jnp.zeros((LANES - ML_HEADS, s), F32)], axis=0).T
    hi = c2.astype(BF16).astype(F32)
    r1 = c2 - hi
    mid = r1.astype(BF16).astype(F32)
    lo = r1 - mid
    aug = jnp.concatenate([-hi, -mid, -lo, jnp.zeros((LANES - 3 * FX_HEADS, s), F32)], axis=0)
    caug_ref[...] = aug.T.astype(BF16)


def _gateprep(gt3, bias):
    nb, _, s = gt3.shape
    return pl.pallas_call(
        _gateprep_body,
        grid=(nb,),
        in_specs=[
            pl.BlockSpec((None, N_GATES, s), lambda b: (b, 0, 0)),
            _resident((N_GATES, 1)),
        ],
        out_specs=[
            pl.BlockSpec((None, N_GATES, s), lambda b: (b, 0, 0)),
            pl.BlockSpec((None, s, LANES), lambda b: (b, 0, 0)),
            pl.BlockSpec((None, s, LANES), lambda b: (b, 0, 0)),
        ],
        out_shape=[jax.ShapeDtypeStruct((nb, N_GATES, s), F32), jax.ShapeDtypeStruct((nb, s, LANES), F32),
                   jax.ShapeDtypeStruct((nb, s, LANES), BF16)],
        compiler_params=_params(("parallel",)),
        name="gateprep",
    )(gt3, bias)


def _causal_mask_t(d):
    s = lax.broadcasted_iota(jnp.int32, (TK, TQ), 0) + d * TK
    t = lax.broadcasted_iota(jnp.int32, (TK, TQ), 1)
    return s <= t


def _mlstm_body(q_ref, k_ref, vt_ref, o_ref, cols_ref, rows_ref, gh_ref, y_ref, num_scr, den_scr):
    i = pl.program_id(1)
    rows = rows_ref[...]
    num_scr[...] = jnp.zeros_like(num_scr)
    den_scr[...] = jnp.zeros_like(den_scr)

    def scores(j, h):
        ks = pl.ds(pl.multiple_of(j * TK, TK), TK)
        qk = slice(h * ML_QK_DIM, (h + 1) * ML_QK_DIM)
        return _dot_nt(k_ref[ks, qk], q_ref[:, qk])

    def update(j, h, s, mask):
        ks = pl.ds(pl.multiple_of(j * TK, TK), TK)
        vv = slice(h * ML_V_DIM, (h + 1) * ML_V_DIM)
        w = jnp.exp2(cols_ref[ks, h:h + 1] - rows[h:h + 1])
        if mask is not None:
            w = jnp.where(mask, w, 0.0)
        s = s * w
        den_scr[h] += jnp.sum(s, axis=0, keepdims=True)
        num_scr[h] += jnp.dot(vt_ref[j, vv, :], s.astype(BF16), preferred_element_type=F32)

    def key_tile(j, mask):
        s = {h: scores(j, h) for h in range(min(MXU_LOOKAHEAD, ML_HEADS))}
        for h in range(ML_HEADS):
            if h + MXU_LOOKAHEAD < ML_HEADS:
                s[h + MXU_LOOKAHEAD] = scores(j, h + MXU_LOOKAHEAD)
            update(j, h, s.pop(h), mask)

    def body(j, carry):
        key_tile(j, None)
        return carry

    lax.fori_loop(0, i * KT_PER_Q, body, 0)
    for d in range(KT_PER_Q):
        key_tile(i * KT_PER_Q + d, _causal_mask_t(d))
    for h in range(ML_HEADS):
        vv = slice(h * ML_V_DIM, (h + 1) * ML_V_DIM)
        floor = jnp.exp(-rows[ML_HEADS + h:ML_HEADS + h + 1])
        hh = num_scr[h] * (1.0 / jnp.maximum(jnp.abs(den_scr[h]), floor))
        yt = hh * lax.rsqrt(jnp.mean(hh * hh, axis=0, keepdims=True) + EPS)
        y = yt.T * gh_ref[:, vv]
        y_ref[:, vv] = (y * jax.nn.sigmoid(o_ref[:, vv].astype(F32))).astype(BF16)


def _mlstm(z3, vt4, cols3, rows3, g_head):
    nb, s, _ = z3.shape
    nq = s // TQ
    return pl.pallas_call(
        _mlstm_body,
        grid=(nb, nq),
        in_specs=[
            pl.BlockSpec((None, TQ, ML_HEADS * ML_QK_DIM), lambda b, i: (b, i, 0)),
            pl.BlockSpec((None, s, ML_HEADS * ML_QK_DIM), lambda b, i: (b, 0, 1)),
            pl.BlockSpec((None, s // TK, D_MODEL, TK), lambda b, i: (b, 0, 0, 0)),
            pl.BlockSpec((None, TQ, D_MODEL), lambda b, i: (b, i, Z_ML_O)),
            pl.BlockSpec((None, s, LANES), lambda b, i: (b, 0, 0)),
            pl.BlockSpec((None, N_GATES, TQ), lambda b, i: (b, 0, i)),
            _resident((1, D_MODEL)),
        ],
        out_specs=pl.BlockSpec((None, TQ, D_MODEL), lambda b, i: (b, i, 0)),
        out_shape=jax.ShapeDtypeStruct((nb, s, D_MODEL), BF16),
        scratch_shapes=[pltpu.VMEM((ML_HEADS, ML_V_DIM, TQ), F32), pltpu.VMEM((ML_HEADS, 1, TQ), F32)],
        compiler_params=_params(("parallel", "arbitrary")),
        name="mlstm",
    )(z3, z3, vt4, z3, cols3, rows3, g_head)


def _fox_body(q_ref, k_ref, vt_ref, caug_ref, rows_ref, y_ref, qa_scr, acc_scr, m_scr, l_scr):
    i = pl.program_id(1)
    rows = rows_ref[...]
    lane = lax.broadcasted_iota(jnp.int32, (TQ, LANES), 1)
    for h in range(FX_HEADS):
        hd = slice(h * FX_HEAD_DIM, (h + 1) * FX_HEAD_DIM)
        ones = jnp.where((lane < 3 * FX_HEADS) & (lane % FX_HEADS == h), 1.0, 0.0).astype(BF16)
        qa_scr[h] = jnp.concatenate([q_ref[:, hd], ones], axis=1)
    m_scr[...] = jnp.full_like(m_scr, -jnp.inf)
    l_scr[...] = jnp.zeros_like(l_scr)
    acc_scr[...] = jnp.zeros_like(acc_scr)

    def scores(j, h):
        ks = pl.ds(pl.multiple_of(j * TK, TK), TK)
        hd = slice(h * FX_HEAD_DIM, (h + 1) * FX_HEAD_DIM)
        k_aug =jnp.concatenate([k_ref[ks, hd], caug_ref[ks, :]], axis=1)
        return _dot_nt(k_aug, qa_scr[h])

    def update(j, h, u, mask):
        hd = slice(h * FX_HEAD_DIM, (h + 1) * FX_HEAD_DIM)
        if mask is not None:
            u = jnp.where(mask, u, -jnp.inf)
        c_row = rows[2 * ML_HEADS + h:2 * ML_HEADS + h + 1]
        m_prev = m_scr[h]
        m_new = jnp.maximum(m_prev, jnp.max(u, axis=0, keepdims=True) + c_row)
        p = jnp.exp2(u - (m_new - c_row))
        alpha = jnp.exp2(m_prev - m_new)
        l_scr[h] = alpha * l_scr[h] + jnp.sum(p, axis=0, keepdims=True)
        acc_scr[h] = alpha * acc_scr[h] + jnp.dot(vt_ref[j, hd, :], p.astype(BF16), preferred_element_type=F32)
        m_scr[h] = m_new

    def key_tile(j, mask):
        u = {h: scores(j, h) for h in range(MXU_LOOKAHEAD)}
        for h in range(FX_HEADS):
            if h + MXU_LOOKAHEAD < FX_HEADS:
                u[h + MXU_LOOKAHEAD] = scores(j, h + MXU_LOOKAHEAD)
            update(j, h, u.pop(h), mask)

    def body(j, carry):
        key_tile(j, None)
        return carry

    lax.fori_loop(0, i * KT_PER_Q, body, 0)
    for d in range(KT_PER_Q):
        key_tile(i * KT_PER_Q + d, _causal_mask_t(d))
    for h in range(FX_HEADS):
        hd = slice(h * FX_HEAD_DIM, (h + 1) * FX_HEAD_DIM)
        y_ref[:, hd] = (acc_scr[h] * (1.0 / l_scr[h])).T.astype(BF16)


def _fox(z3, vt4, caug3, rows3):
    nb, s, _ = z3.shape
    nq = s // TQ
    return pl.pallas_call(
        _fox_body,
        grid=(nb, nq),
        in_specs=[
            pl.BlockSpec((None, TQ, D_MODEL), lambda b, i: (b, i, Z_FX_Q)),
            pl.BlockSpec((None, s, D_MODEL), lambda b, i: (b, 0, Z_FX_K)),
            pl.BlockSpec((None, s // TK, D_MODEL, TK), lambda b, i: (b, 0, 1, 0)),
            pl.BlockSpec((None, s, LANES), lambda b, i: (b, 0, 0)),
            pl.BlockSpec((None, N_GATES, TQ), lambda b, i: (b, 0, i)),
        ],
        out_specs=pl.BlockSpec((None, TQ, D_MODEL), lambda b, i: (b, i, 0)),
        out_shape=jax.ShapeDtypeStruct((nb, s, D_MODEL), BF16),
        scratch_shapes=[pltpu.VMEM((FX_HEADS, TQ, 2 * FX_HEAD_DIM), BF16),
                        pltpu.VMEM((FX_HEADS, FX_HEAD_DIM, TQ), F32),
                        pltpu.VMEM((FX_HEADS, 1, TQ), F32), pltpu.VMEM((FX_HEADS, 1, TQ), F32)],
        compiler_params=_params(("parallel", "arbitrary")),
        name="fox",
    )(z3, z3, vt4, caug3, rows3)


def _merge_body(x_ref, yml_ref, yfx_ref, gml_ref, gfx_ref, bml_ref, bfx_ref, wml_ref, wfx_ref, wout_ref, x1_ref):
    p_ml = jnp.dot(yml_ref[...], wml_ref[...], preferred_element_type=F32)
    p_fx = jnp.dot(yfx_ref[...], wfx_ref[...], preferred_element_type=F32)
    merged = (jax.nn.sigmoid(gml_ref[...].astype(F32) + bml_ref[...]) * p_ml
              + jax.nn.sigmoid(gfx_ref[...].astype(F32) + bfx_ref[...]) * p_fx)
    x1_ref[...] = x_ref[...] + jnp.dot(merged.astype(BF16), wout_ref[...], preferred_element_type=F32)


def _merge(x2d, yml, yfx, z2d, bml, bfx, wml, wfx, wout):
    t = x2d.shape[0]
    tile = lambda col: pl.BlockSpec((TM, D_MODEL), lambda i, col=col: (i, col))
    return pl.pallas_call(
        _merge_body,
        grid=(t // TM,),
        in_specs=[tile(0), tile(0), tile(0), tile(Z_GT_ML), tile(Z_GT_FX),
                  _resident((1, D_MODEL)), _resident((1, D_MODEL)),
                  _resident((D_MODEL, D_MODEL)), _resident((D_MODEL, D_MODEL)), _resident((D_MODEL, D_MODEL))],
        out_specs=tile(0),
        out_shape=jax.ShapeDtypeStruct((t, D_MODEL), F32),
        compiler_params=_params(("parallel",)),
        name="merge",
    )(x2d, yml, yfx, z2d, z2d, bml, bfx, wml, wfx, wout)


def _memkv_body(m_ref, g_ref, w_ref, kv_ref):
    hb = _rms(m_ref[...], g_ref[...]).astype(BF16)
    kv_ref[...] = jnp.dot(hb, w_ref[...], preferred_element_type=F32).astype(BF16)


def _memkv(mem2d, g, w):
    t = mem2d.shape[0]
    return pl.pallas_call(
        _memkv_body,
        grid=(t // TM,),
        in_specs=[pl.BlockSpec((TM, D_MODEL), lambda i: (i, 0)), _resident((1, D_MODEL)),
                  _resident((D_MODEL, 2 * D_MODEL))],
        out_specs=pl.BlockSpec((TM, 2 * D_MODEL), lambda i: (i, 0)),
        out_shape=jax.ShapeDtypeStruct((t, 2 * D_MODEL), BF16),
        compiler_params=_params(("parallel",)),
        name="memkv",
    )(mem2d, g, w)


def _route_t(lg_t):
    tm = lg_t.shape[1]
    ninf = -jnp.inf
    big = jnp.float32(LANES)
    gid = lax.broadcasted_iota(jnp.int32, (8, tm), 0).astype(F32)
    eid = lax.broadcasted_iota(jnp.int32, (N_EXPERTS, tm), 0).astype(F32)
    gl = jnp.where(gid < N_GROUPS, lg_t[N_EXPERTS:N_EXPERTS + 8], ninf)
    gmax = jnp.max(gl, axis=0, keepdims=True)
    gidx = jnp.min(jnp.where(gl == gmax, gid, big), axis=0, keepdims=True)
    g_p = 1.0 / jnp.sum(jnp.exp(gl - gmax), axis=0, keepdims=True)
    lo = gidx * EXPERTS_PER_GROUP
    el = jnp.where(eid >= lo, jnp.where(eid < lo + EXPERTS_PER_GROUP, lg_t[0:N_EXPERTS], ninf), ninf)
    v1 = jnp.max(el, axis=0, keepdims=True)
    i1 = jnp.min(jnp.where(el == v1, eid, big), axis=0, keepdims=True)
    el2 = jnp.where(eid == i1, ninf, el)
    v2 = jnp.max(el2, axis=0, keepdims=True)
    i2 = jnp.min(jnp.where(el2 == v2, eid, big), axis=0, keepdims=True)
    t = jnp.exp(v2 - v1)
    w1 = g_p / (1.0 + t)
    w2 = w1 * t
    return jnp.concatenate([i1, i2, w1, w2, jnp.zeros((4, tm), F32)], axis=0)


def _xattn_body(x1_ref, kv_ref, gq_ref, wq_ref, wo_ref, gm_ref, wrt_ref, brt_ref, x2_ref, h3_ref, route_ref,
                wcol_ref):
    x1 = x1_ref[...]
    hb = _rms(x1, gq_ref[...]).astype(BF16)
    q = (jnp.dot(hb, wq_ref[...], preferred_element_type=F32) * (XA_HEAD_DIM ** -0.5)).astype(BF16)
    outs = []
    for h in range(XA_HEADS):
        hd = slice(h * XA_HEAD_DIM, (h + 1) * XA_HEAD_DIM)
        vd = slice(D_MODEL + h * XA_HEAD_DIM, D_MODEL + (h + 1) * XA_HEAD_DIM)
        s = _dot_nt(q[:, hd], kv_ref[:, hd])
        p = jnp.exp(s - jnp.max(s, axis=1, keepdims=True))
        p = p * (1.0 / jnp.sum(p, axis=1, keepdims=True))
        outs.append(jnp.dot(p.astype(BF16), kv_ref[:, vd], preferred_element_type=F32).astype(BF16))
    o = jnp.concatenate(outs, axis=1)
    x2 = x1 + jnp.dot(o, wo_ref[...], preferred_element_type=F32)
    x2_ref[...] = x2
    h3 = _rms(x2, gm_ref[...])
    h3_ref[...] = _pack_halves(h3)
    lg_t = _dot_nt_split(wrt_ref[...], h3) + brt_ref[...]
    route = _route_t(lg_t)
    route_ref[...] = route
    wcol_ref[...] = jnp.concatenate([route, jnp.zeros((LANES - 8, route.shape[1]), F32)], axis=0).T


def _xattn(x1, kv3, gq, wq, wo, gm, wrt, brt, seq):
    t = x1.shape[0]
    per_b = seq // TM
    n_mem = kv3.shape[1]
    tile = pl.BlockSpec((TM, D_MODEL), lambda i: (i, 0))
    return pl.pallas_call(
        _xattn_body,
        grid=(t // TM,),
        in_specs=[tile,
                  pl.BlockSpec((None, n_mem, 2 * D_MODEL), lambda i: (i // per_b, 0, 0)),
                  _resident((1, D_MODEL)), _resident((D_MODEL, D_MODEL)), _resident((D_MODEL, D_MODEL)),
                  _resident((1, D_MODEL)), _resident((ROUTER_ROWS, D_MODEL)), _resident((ROUTER_ROWS, 1))],
        out_specs=[tile, pl.BlockSpec((TM, PACKED_W), lambda i: (i, 0)), pl.BlockSpec((8, TM), lambda i: (0, i)),
                   pl.BlockSpec((TM, LANES), lambda i: (i, 0))],
        out_shape=[jax.ShapeDtypeStruct((t, D_MODEL), F32), jax.ShapeDtypeStruct((t, PACKED_W), F32),
                   jax.ShapeDtypeStruct((8, t), F32), jax.ShapeDtypeStruct((t, LANES), F32)],
        compiler_params=_params(("parallel",)),
        name="xattn",
    )(x1, kv3, gq, wq, wo, gm, wrt, brt)


def _n_expert_tiles(n_tokens):
    return 2 * n_tokens // TME + N_EXPERTS


def _plan_body(route_ref, pos_ref, tinfo_ref, tok_ref, cnt_scr, run_scr, start_scr, tokhi_scr, toklo_scr, *, nt_pad,
               n_tokens, first_token):
    phase = pl.program_id(0)
    b = pl.program_id(1)
    r = route_ref[...]
    eid = lax.broadcasted_iota(jnp.int32, (N_EXPERTS, TP), 0).astype(F32)
    oh1 = eid == r[0:1]
    oh2 = eid == r[1:2]
    oh = jnp.where(oh1 | oh2, 1.0, 0.0)

    @pl.when((phase == 0) & (b == 0))
    def _():
        cnt_scr[...] = jnp.zeros_like(cnt_scr)

    @pl.when(phase == 0)
    def _():
        cnt_scr[...] += jnp.sum(oh, axis=1, keepdims=True)

    @pl.when((phase == 1) & (b == 0))
    def _():
        cnt = cnt_scr[...]
        n_tiles = jnp.floor((cnt + (TME - 1)) * (1.0 / TME))
        ri = lax.broadcasted_iota(jnp.int32, (N_EXPERTS, N_EXPERTS), 0)
        ci = lax.broadcasted_iota(jnp.int32, (N_EXPERTS, N_EXPERTS), 1)
        lower = jnp.where(ci < ri, 1.0, 0.0).astype(BF16)
        nt_hi, nt_lo = _split_bf16(jnp.broadcast_to(n_tiles, (N_EXPERTS, LANES)))
        start = (jnp.dot(lower, nt_hi, preferred_element_type=F32)
                 + jnp.dot(lower, nt_lo, preferred_element_type=F32))[:, 0:1]
        start_scr[...] = start * TME
        run_scr[...] = jnp.zeros_like(run_scr)
        tokhi_scr[...] = jnp.zeros_like(tokhi_scr)
        toklo_scr[...] = jnp.zeros_like(toklo_scr)
        n = lax.broadcasted_iota(jnp.int32, (N_EXPERTS, nt_pad), 1).astype(F32)
        e_n = lax.broadcasted_iota(jnp.int32, (N_EXPERTS, nt_pad), 0).astype(F32)
        owner = jnp.sum(jnp.where(start <= n, 1.0, 0.0), axis=0, keepdims=True) - 1.0
        own = e_n == owner
        cnt_o = jnp.sum(jnp.where(own, cnt, 0.0), axis=0, keepdims=True)
        start_o = jnp.sum(jnp.where(own, start, 0.0), axis=0, keepdims=True)
        valid = jnp.clip(cnt_o - (n[0:1] - start_o) * TME, 0.0, float(TME))
        tinfo_ref[...] = jnp.concatenate([owner, valid, jnp.zeros((6, nt_pad), F32)], axis=0).astype(jnp.int32)

    @pl.when(phase == 1)
    def _():
        ti = lax.broadcasted_iota(jnp.int32, (TP, TP), 0)
        tj = lax.broadcasted_iota(jnp.int32, (TP, TP), 1)
        upper = jnp.where(ti < tj, 1.0, 0.0).astype(BF16)
        before = jnp.dot(oh.astype(BF16), upper, preferred_element_type=F32)
        row = start_scr[...] + run_scr[...] + before
        p1 = jnp.sum(jnp.where(oh1, row, 0.0), axis=0, keepdims=True)
        p2 = jnp.sum(jnp.where(oh2, row, 0.0), axis=0, keepdims=True)
        pos_ref[...] = jnp.concatenate([p1, p2, jnp.zeros((6, TP), F32)], axis=0).astype(jnp.int32)
        run_scr[...] += jnp.sum(oh, axis=1, keepdims=True)

        tid = lax.broadcasted_iota(jnp.int32, (1, TP), 1) + (b * TP + first_token + 1)
        t_hi = (tid // 256).astype(F32)
        t_lo = (tid % 256).astype(F32)
        tile_id = lax.broadcasted_iota(jnp.int32, (nt_pad, TP), 0).astype(F32)
        row_id = lax.broadcasted_iota(jnp.int32, (TME, TP), 0).astype(F32)
        for p in (p1, p2):
            hi = jnp.floor(p * (1.0 / TME))
            lo = p - hi * TME
            in_tile = jnp.where(tile_id == hi, 1.0, 0.0).astype(BF16)
            at_row = row_id == lo
            tokhi_scr[...] += _dot_nt(in_tile, jnp.where(at_row, t_hi, 0.0).astype(BF16))
            toklo_scr[...] += _dot_nt(in_tile, jnp.where(at_row, t_lo, 0.0).astype(BF16))

    @pl.when((phase == 1) & (b == pl.num_programs(1) - 1))
    def _():
        hit = (tokhi_scr[...] * 256.0 + toklo_scr[...]).astype(jnp.int32)
        row = (lax.broadcasted_iota(jnp.int32, hit.shape, 0) * TME + lax.broadcasted_iota(jnp.int32, hit.shape, 1))
        tok_ref[...] = jnp.where(hit > 0, hit - 1, row % n_tokens)


def _plan(route, part):
    t_all = route.shape[1]
    assert t_all < 256 * 256, "token id + 1 is carried as two base-256 digits"
    t = t_all // MOE_PARTS
    steps = t // TP
    nt_pad = -(-_n_expert_tiles(t) // LANES) * LANES
    col = pltpu.VMEM((N_EXPERTS, 1), F32)
    table = pltpu.VMEM((nt_pad, TME), F32)
    return pl.pallas_call(
        functools.partial(_plan_body, nt_pad=nt_pad, n_tokens=t_all, first_token=part * t),
        grid=(2, steps),
        in_specs=[pl.BlockSpec((8, TP), lambda ph, b: (0, b + part * steps))],
        out_specs=[pl.BlockSpec((8, TP), lambda ph, b: (0, b * ph)),
                   pl.BlockSpec((8, nt_pad), lambda ph, b: (0, 0)),
                   pl.BlockSpec((nt_pad, TME), lambda ph, b: (0, 0))],
        out_shape=[jax.ShapeDtypeStruct((8, t), jnp.int32), jax.ShapeDtypeStruct((8, nt_pad), jnp.int32),
                   jax.ShapeDtypeStruct((nt_pad, TME), jnp.int32)],
        scratch_shapes=[col, col, col, table, table],
        compiler_params=_params(("arbitrary", "arbitrary")),
        name="plan",
    )(route)


def _experts_body(texp_ref, tvalid_ref, xs_ref, wg_ref, wu_ref, wd_ref, ys_ref, wgb_scr, wub_scr, wdb_scr):
    n = pl.program_id(0)

    @pl.when((n == 0) | (texp_ref[n] != texp_ref[jnp.maximum(n - 1, 0)]))
    def _():
        wgb_scr[...] = wg_ref[...].astype(BF16)
        wub_scr[...] = wu_ref[...].astype(BF16)
        wdb_scr[...] = wd_ref[...].astype(BF16)

    @pl.when(tvalid_ref[n] > 0)
    def _():
        xb = _unpack_halves(xs_ref[...]).astype(BF16)
        gate = jnp.dot(xb, wgb_scr[...], preferred_element_type=F32)
        up = jnp.dot(xb, wub_scr[...], preferred_element_type=F32)
        he = (gate * jax.nn.sigmoid(gate) * up).astype(BF16)
        ys_ref[...] = _pack_halves(jnp.dot(he, wdb_scr[...], preferred_element_type=F32))

    @pl.when(tvalid_ref[n] == 0)
    def _():
        ys_ref[...] = jnp.zeros_like(ys_ref)


def _experts(texp, tvalid, xs, wg, wu, wd):
    nt = texp.shape[0]
    weight = lambda shape: pl.BlockSpec((None,) + shape, lambda n, te, tv: (te[n], 0, 0))
    return pl.pallas_call(
        _experts_body,
        grid_spec=pltpu.PrefetchScalarGridSpec(
            num_scalar_prefetch=2,
            grid=(nt,),
            in_specs=[pl.BlockSpec((TME, PACKED_W), lambda n, te, tv: (jnp.where(tv[n] > 0, n, 0), 0)),
                      weight((D_MODEL, D_EXPERT)), weight((D_MODEL, D_EXPERT)), weight((D_EXPERT, D_MODEL))],
            out_specs=pl.BlockSpec((TME, PACKED_W), lambda n, te, tv: (n, 0)),
            scratch_shapes=[pltpu.VMEM((D_MODEL, D_EXPERT), BF16), pltpu.VMEM((D_MODEL, D_EXPERT), BF16),
                            pltpu.VMEM((D_EXPERT, D_MODEL), BF16)],
        ),
        out_shape=jax.ShapeDtypeStruct((nt * TME, PACKED_W), F32),
        compiler_params=_params(("arbitrary",)),
        name="experts",
    )(texp, tvalid, xs, wg, wu, wd)


def _rowgather(table, idx):
    n_rows, width = idx.shape[0], table.shape[1]
    n_workers = SC_CORES * SC_SUBCORES
    per_worker = n_rows // n_workers
    assert per_worker * n_workers == n_rows and per_worker % SC_ROWS == 0
    mesh = plsc.VectorSubcoreMesh(core_axis_name="c", subcore_axis_name="s", num_cores=SC_CORES,
                                  num_subcores=SC_SUBCORES)

    chunks = per_worker // SC_ROWS
    assert chunks % 2 == 0
    buf = lambda: [pltpu.VMEM((SC_ROWS,), jnp.int32), pltpu.VMEM((SC_ROWS, width), table.dtype),
                   pltpu.SemaphoreType.DMA]

    @functools.partial(pl.kernel, mesh=mesh, out_type=jax.ShapeDtypeStruct((n_rows, width), table.dtype),
                       scratch_types=buf() + buf(), name="rowgather")
    def gather(table_hbm, idx_hbm, out_hbm, idx_a, rows_a, sem_a, idx_b, rows_b, sem_b):
        worker = lax.axis_index("s") * SC_CORES + lax.axis_index("c")
        base = worker * per_worker

        def fetch(c, idx_v, rows_v, sem):
            pltpu.sync_copy(idx_hbm.at[pl.ds(base + c * SC_ROWS, SC_ROWS)], idx_v)
            pltpu.async_copy(table_hbm.at[idx_v], rows_v, sem)

        def flush(c, idx_v, rows_v, sem):
            pltpu.make_async_copy(table_hbm.at[idx_v], rows_v, sem).wait()
            pltpu.sync_copy(rows_v, out_hbm.at[pl.ds(base + c * SC_ROWS, SC_ROWS)])

        fetch(0, idx_a, rows_a, sem_a)

        @pl.loop(0, chunks, step=2)
        def _(c):
            fetch(c + 1, idx_b, rows_b, sem_b)
            flush(c, idx_a, rows_a, sem_a)

            @pl.when(c + 2 < chunks)
            def _():
                fetch(c + 2, idx_a, rows_a, sem_a)

            flush(c + 1, idx_b, rows_b, sem_b)

    return gather(table, idx)


def _combine_body(g1_ref, g2_ref, x2_ref, wcol_ref, gf_ref, *rest):
    out_ref = rest[-1]
    w = wcol_ref[...]
    y = w[:, 2:3] * _unpack_halves(g1_ref[...]) + w[:, 3:4] * _unpack_halves(g2_ref[...])
    out_ref[...] = _rms(x2_ref[...] + y, gf_ref[...])


def _combine(g, x2, wcol, gf, part, out_prev):
    t = x2.shape[0]
    steps = t // MOE_PARTS // TF
    mine = lambda w: pl.BlockSpec((TF, w), lambda i: (i + part * steps, 0))
    in_specs = [pl.BlockSpec((TF, PACKED_W), lambda i: (i, 0)),
                pl.BlockSpec((TF, PACKED_W), lambda i: (i + steps, 0)),
                mine(D_MODEL), mine(LANES), _resident((1, D_MODEL))]
    operands = [g, g, x2, wcol, gf]
    aliases = {}
    if out_prev is not None:
        in_specs.append(pl.BlockSpec(memory_space=pl.ANY))
        operands.append(out_prev)
        aliases = {len(operands) - 1: 0}
    return pl.pallas_call(
        _combine_body,
        grid=(steps,),
        in_specs=in_specs,
        out_specs=mine(D_MODEL),
        out_shape=jax.ShapeDtypeStruct((t, D_MODEL), F32),
        input_output_aliases=aliases,
        compiler_params=_params(("parallel",)),
        name="combine",
    )(*operands)


def _layer(x, mem, g_mix, w_in, b_ml_i, b_ml_f, b_fx_f, b_gate_ml, b_gate_fx, g_ml_head, w_proj_ml, w_proj_fx,
           w_out, g_xq, g_xmem, w_xq, w_xkv, w_xo, g_moe, w_rg, b_rg, w_re, b_re, w_gate, w_up, w_down):
    nb, seq, d = x.shape
    t = nb * seq
    row = lambda v: v.reshape(1, -1).astype(F32)

    o = 0
    parts = {}
    for name, width in (("ml_q", 512), ("ml_k", 512), ("ml_v", 1024), ("ml_o", 1024), ("ml_i", 4), ("ml_f", 4),
                        ("fx_q", 1024), ("fx_k", 1024), ("fx_v", 1024), ("fx_f", 8), ("gt_ml", 1024),
                        ("gt_fx", 1024)):
        parts[name] = w_in[:, o:o + width]
        o += width
    w_main = jnp.concatenate(
        [parts["ml_q"], parts["ml_k"] * (ML_QK_DIM ** -0.5), parts["ml_o"],
         parts["fx_q"] * (FX_HEAD_DIM ** -0.5 * LOG2E), parts["fx_k"], parts["gt_ml"], parts["gt_fx"]],
        axis=1).astype(BF16)
    w_vt = jnp.concatenate([parts["ml_v"], parts["fx_v"]], axis=1).T.astype(BF16)
    w_gates_t = jnp.concatenate([parts["ml_i"], parts["ml_f"], parts["fx_f"]], axis=1).T
    gate_bias = jnp.concatenate([b_ml_i, b_ml_f, b_fx_f]).reshape(N_GATES, 1).astype(F32)

    x2d = x.reshape(t, d)
    z, vt, gates_t = _inproj(x2d, row(g_mix), w_main, w_vt, w_gates_t)
    gt3 = gates_t.reshape(N_GATES, nb, seq).transpose(1, 0, 2)
    rows, cols, caug = _gateprep(gt3, gate_bias)
    z3 = z.reshape(nb, seq, Z_W)
    vt4 = vt.reshape(nb, seq // TK, VT_W, TK)
    y_ml = _mlstm(z3, vt4, cols, rows, row(g_ml_head))
    y_fx = _fox(z3, vt4, caug, rows)
    x1 = _merge(x2d, y_ml.reshape(t, d), y_fx.reshape(t, d), z, row(b_gate_ml), row(b_gate_fx),
                w_proj_ml.astype(BF16), w_proj_fx.astype(BF16), w_out.astype(BF16))

    n_mem = mem.shape[1]
    kv = _memkv(mem.reshape(nb * n_mem, d), row(g_xmem), w_xkv.astype(BF16))
    w_router_t = jnp.concatenate([w_re, w_rg, jnp.zeros((d, ROUTER_ROWS - N_EXPERTS - N_GROUPS), F32)], axis=1).T
    b_router_t = jnp.concatenate([b_re, b_rg, jnp.zeros((ROUTER_ROWS - N_EXPERTS - N_GROUPS,), F32)]).reshape(ROUTER_ROWS, 1)
    x2, h3, route, wcol = _xattn(x1, kv.reshape(nb, n_mem, 2 * d), row(g_xq), w_xq.astype(BF16),
                                 w_xo.astype(BF16), row(g_moe), w_router_t, b_router_t, seq)
    return x2, h3, route, wcol


def _moe(x2, h3, route, wcol, w_gate, w_up, w_down, g_final):
    t = x2.shape[0]
    nt = _n_expert_tiles(t // MOE_PARTS)
    out = None
    for part in range(MOE_PARTS):
        pos, tinfo, tok = _plan(route, part)
        texp, tvalid = tinfo[0, :nt], tinfo[1, :nt]
        xs = _rowgather(h3, tok[:nt].reshape(-1))
        ys = _experts(texp, tvalid, xs, w_gate, w_up, w_down)
        g = _rowgather(ys, pos[0:2].reshape(-1))
        out = _combine(g, x2, wcol, g_final, part, out)
    return out


def kernel(x, mem, g_mix, w_in, b_ml_i, b_ml_f, b_fx_f, b_gate_ml, b_gate_fx, g_ml_head, w_proj_ml, w_proj_fx, w_out, g_xq, g_xmem, w_xq, w_xkv, w_xo, g_moe, w_rg, b_rg, w_re, b_re, w_gate, w_up, w_down, g_final):
    nb, seq, d = x.shape
    depth = g_mix.shape[0]
    assert depth == 1, "the final rmsnorm is fused into the (single) layer's combine kernel"
    x2, h3, route, wcol = _layer(
        x, mem, g_mix[0], w_in[0], b_ml_i[0], b_ml_f[0], b_fx_f[0], b_gate_ml[0], b_gate_fx[0], g_ml_head[0],
        w_proj_ml[0], w_proj_fx[0], w_out[0], g_xq[0], g_xmem[0], w_xq[0], w_xkv[0], w_xo[0], g_moe[0],
        w_rg[0], b_rg[0], w_re[0], b_re[0], w_gate[0], w_up[0], w_down[0])
    out = _moe(x2, h3, route, wcol, w_gate[0], w_up[0], w_down[0], g_final.reshape(1, d).astype(F32))
    return out.reshape(nb, seq, d)
```

```python
import functools

import jax
import jax.numpy as jnp
from jax import lax
from jax.experimental import pallas as pl
from jax.experimental.pallas import tpu as pltpu
from jax.experimental.pallas import tpu_sc as plsc

F32 = jnp.float32
BF16 = jnp.bfloat16

D_MODEL = 1024
EPS = 1e-6
ML_HEADS = 4
ML_QK_DIM = 128
ML_V_DIM = 256
FX_HEADS = 8
FX_HEAD_DIM = 128
XA_HEADS = 4
XA_HEAD_DIM = 256
N_GROUPS = 4
EXPERTS_PER_GROUP = 8
N_EXPERTS = 32
D_EXPERT = 512

LANES = 128
N_GATES = 16
ROUTER_ROWS = 40
LOG2E = 1.4426950408889634
Z_W = 6144
Z_ML_O, Z_FX_Q, Z_FX_K, Z_GT_ML, Z_GT_FX = 1, 2, 3, 4, 5
VT_W = 2048

VMEM_LIMIT = 56 * 1024 * 1024

IN_CHUNK = 1024
TK = 256
TQ = 512
TIN = 512
KT_PER_Q = TQ // TK
MXU_LOOKAHEAD = 4
TM = 512
TME = 512
TP = 512
TF = 512
MOE_PARTS = 1
SC_CORES, SC_SUBCORES = 2, 16
PACKED_W = D_MODEL // 2
SC_ROWS = 64


def _params(sem, flags=None):
    return pltpu.CompilerParams(dimension_semantics=sem, vmem_limit_bytes=VMEM_LIMIT, flags=flags)


def _rms(x, g):
    return x * lax.rsqrt(jnp.mean(x * x, axis=-1, keepdims=True) + EPS) * g


def _dot_nt(a, b, **kw):
    return lax.dot_general(a, b, (((1,), (1,)), ((), ())), preferred_element_type=F32, **kw)


def _split_bf16(x):
    hi = x.astype(BF16)
    return hi, (x - hi.astype(F32)).astype(BF16)


def _dot_nt_split(a, b):
    a_hi, a_lo = _split_bf16(a)
    b_hi, b_lo = _split_bf16(b)
    return _dot_nt(a_hi, b_hi) + (_dot_nt(a_hi, b_lo) + _dot_nt(a_lo, b_hi))


def _pack_halves(x):
    n = x.shape[1] // 2
    bits = lambda v: lax.bitcast_convert_type(v.astype(BF16).astype(F32), jnp.uint32)
    w = (bits(x[:, :n]) >> 16) | (bits(x[:, n:]) & jnp.uint32(0xFFFF0000))
    return lax.bitcast_convert_type(w, F32)


def _unpack_halves(p):
    w = lax.bitcast_convert_type(p, jnp.uint32)
    lo = lax.bitcast_convert_type(w << 16, F32)
    hi = lax.bitcast_convert_type(w & jnp.uint32(0xFFFF0000), F32)
    return jnp.concatenate([lo, hi], axis=1)


def _resident(shape):
    zeros = (0,) * len(shape)
    return pl.BlockSpec(shape, lambda *_: zeros, pipeline_mode=pl.Buffered(1))


def _inproj_body(x_ref, g_ref, w_ref, wvt_ref, wgt_ref, z_ref, vt_ref, gt_ref):
    h = _rms(x_ref[...], g_ref[...])
    hb = h.astype(BF16)
    for c in range(Z_W // IN_CHUNK):
        sl = slice(c * IN_CHUNK, (c + 1) * IN_CHUNK)
        z_ref[:, sl] = jnp.dot(hb, w_ref[:, sl], preferred_element_type=F32).astype(BF16)
    for kt in range(TIN // TK):
        hk = hb[kt * TK:(kt + 1) * TK]
        for c in range(VT_W // IN_CHUNK):
            sl = slice(c * IN_CHUNK, (c + 1) * IN_CHUNK)
            vt_ref[kt, sl, :] = _dot_nt(wvt_ref[sl, :], hk).astype(BF16)
    gt_ref[...] = _dot_nt_split(wgt_ref[...], h)


def _inproj(x2d, g, w, wvt, wgt):
    t = x2d.shape[0]
    return pl.pallas_call(
        _inproj_body,
        grid=(t // TIN,),
        in_specs=[
            pl.BlockSpec((TIN, D_MODEL), lambda i: (i, 0)),
            _resident((1, D_MODEL)),
            _resident((D_MODEL, Z_W)),
            _resident((VT_W, D_MODEL)),
            _resident((N_GATES, D_MODEL)),
        ],
        out_specs=[
            pl.BlockSpec((TIN, Z_W), lambda i: (i, 0)),
            pl.BlockSpec((TIN // TK, VT_W, TK), lambda i: (i, 0, 0)),
            pl.BlockSpec((N_GATES, TIN), lambda i: (0, i)),
        ],
        out_shape=[jax.ShapeDtypeStruct((t, Z_W), BF16), jax.ShapeDtypeStruct((t // TK, VT_W, TK), BF16),
                   jax.ShapeDtypeStruct((N_GATES, t), F32)],
        compiler_params=_params(("parallel",)),
        name="inproj",
    )(x2d, g, w, wvt, wgt)


def _scan_lanes(x, op, identity):
    n = x.shape[-1]
    idx = lax.broadcasted_iota(jnp.int32, x.shape, 1)
    s = 1
    while s < n:
        shifted = pltpu.roll(x, s, axis=1)
        x = op(x, jnp.where(idx >= s, shifted, identity))
        s *= 2
    return x


def _log_sigmoid(x):
    return jnp.minimum(x, 0.0) - jnp.log1p(jnp.exp(-jnp.abs(x)))


def _gateprep_body(gt_ref, bias_ref, rows_ref, cols_ref, caug_ref):
    g = gt_ref[...] + bias_ref[...]
    s = g.shape[1]
    cs = _scan_lanes(_log_sigmoid(g), jnp.add, 0.0)
    b = cs[4:8]
    c2 = cs[8:16] * LOG2E
    a = g[0:4] - b
    m = _scan_lanes(jnp.concatenate([a, a], axis=0), jnp.maximum, -jnp.inf)[0:4]
    rows_ref[...] = jnp.concatenate([m * LOG2E, b + m, c2], axis=0)
    cols_ref[...] = jnp.concatenate([a * LOG2E, jnp.zeros((LANES - ML_HEADS, s), F32)], axis=0).T
    hi = c2.astype(BF16).astype(F32)
    r1 = c2 - hi
    mid = r1.astype(BF16).astype(F32)
    lo = r1 - mid
    aug = jnp.concatenate([-hi, -mid, -lo, jnp.zeros((LANES - 3 * FX_HEADS, s), F32)], axis=0)
    caug_ref[...] = aug.T.astype(BF16)


def _gateprep(gt3, bias):
    nb, _, s = gt3.shape
    return pl.pallas_call(
        _gateprep_body,
        grid=(nb,),
        in_specs=[
            pl.BlockSpec((None, N_GATES, s), lambda b: (b, 0, 0)),
            _resident((N_GATES, 1)),
        ],
        out_specs=[
            pl.BlockSpec((None, N_GATES, s), lambda b: (b, 0, 0)),
            pl.BlockSpec((None, s, LANES), lambda b: (b, 0, 0)),
            pl.BlockSpec((None, s, LANES), lambda b: (b, 0, 0)),
        ],
        out_shape=[jax.ShapeDtypeStruct((nb, N_GATES, s), F32), jax.ShapeDtypeStruct((nb, s, LANES), F32),
                   jax.ShapeDtypeStruct((nb, s, LANES), BF16)],
        compiler_params=_params(("parallel",)),
        name="gateprep",
    )(gt3, bias)


def _causal_mask_t(d):
    s = lax.broadcasted_iota(jnp.int32, (TK, TQ - d * TK), 0)
    t = lax.broadcasted_iota(jnp.int32, (TK, TQ - d * TK), 1)
    return s <= t


def _mlstm_body(q_ref, k_ref, vt_ref, o_ref, cols_ref, rows_ref, gh_ref, y_ref, num_scr, den_scr):
    i = pl.program_id(1)
    rows = rows_ref[...]
    num_scr[...] = jnp.zeros_like(num_scr)
    den_scr[...] = jnp.zeros_like(den_scr)

    def scores(j, h, q0):
        ks = pl.ds(pl.multiple_of(j * TK, TK), TK)
        qk = slice(h * ML_QK_DIM, (h + 1) * ML_QK_DIM)
        return _dot_nt(k_ref[ks, qk], q_ref[q0:, qk])

    def update(j, h, s, mask, q0):
        ks = pl.ds(pl.multiple_of(j * TK, TK), TK)
        vv = slice(h * ML_V_DIM, (h + 1) * ML_V_DIM)
        w = jnp.exp2(cols_ref[ks, h:h + 1] - rows[h:h + 1, q0:])
        if mask is not None:
            w = jnp.where(mask, w, 0.0)
        s = s * w
        den_scr[h, :, q0:] += jnp.sum(s, axis=0, keepdims=True)
        num_scr[h, :, q0:] += jnp.dot(vt_ref[j, vv, :], s.astype(BF16), preferred_element_type=F32)

    def key_tile(j, mask=None, q0=0):
        s = {h: scores(j, h, q0) for h in range(min(MXU_LOOKAHEAD, ML_HEADS))}
        for h in range(ML_HEADS):
            if h + MXU_LOOKAHEAD < ML_HEADS:
                s[h + MXU_LOOKAHEAD] = scores(j, h + MXU_LOOKAHEAD, q0)
            update(j, h, s.pop(h), mask, q0)

    def body(j, carry):
        key_tile(j)
        return carry

    lax.fori_loop(0, i * KT_PER_Q, body, 0)
    for d in range(KT_PER_Q):
        key_tile(i * KT_PER_Q + d, _causal_mask_t(d), d * TK)
    for h in range(ML_HEADS):
        vv = slice(h * ML_V_DIM, (h + 1) * ML_V_DIM)
        floor = jnp.exp(-rows[ML_HEADS + h:ML_HEADS + h + 1])
        hh = num_scr[h] * (1.0 / jnp.maximum(jnp.abs(den_scr[h]), floor))
        yt = hh * lax.rsqrt(jnp.mean(hh * hh, axis=0, keepdims=True) + EPS)
        y = yt.T * gh_ref[:, vv]
        y_ref[:, vv] = (y * jax.nn.sigmoid(o_ref[:, vv].astype(F32))).astype(BF16)


def _mlstm(z3, vt4, cols3, rows3, g_head):
    nb, s, _ = z3.shape
    nq = s // TQ
    return pl.pallas_call(
        _mlstm_body,
        grid=(nb, nq),
        in_specs=[
            pl.BlockSpec((None, TQ, ML_HEADS * ML_QK_DIM), lambda b, i: (b, i, 0)),
            pl.BlockSpec((None, s, ML_HEADS * ML_QK_DIM), lambda b, i: (b, 0, 1)),
            pl.BlockSpec((None, s // TK, D_MODEL, TK), lambda b, i: (b, 0, 0, 0)),
            pl.BlockSpec((None, TQ, D_MODEL), lambda b, i: (b, i, Z_ML_O)),
            pl.BlockSpec((None, s, LANES), lambda b, i: (b, 0, 0)),
            pl.BlockSpec((None, N_GATES, TQ), lambda b, i: (b, 0, i)),
            _resident((1, D_MODEL)),
        ],
        out_specs=pl.BlockSpec((None, TQ, D_MODEL), lambda b, i: (b, i, 0)),
        out_shape=jax.ShapeDtypeStruct((nb, s, D_MODEL), BF16),
        scratch_shapes=[pltpu.VMEM((ML_HEADS, ML_V_DIM, TQ), F32), pltpu.VMEM((ML_HEADS, 1, TQ), F32)],
        compiler_params=_params(("parallel", "arbitrary")),
        name="mlstm",
    )(z3, z3, vt4, z3, cols3, rows3, g_head)


def _fox_body(q_ref, k_ref, vt_ref, caug_ref, rows_ref, y_ref, qa_scr, acc_scr, m_scr, l_scr):
    i = pl.program_id(1)
    rows = rows_ref[...]
    lane = lax.broadcasted_iota(jnp.int32, (TQ, LANES), 1)
    for h in range(FX_HEADS):
        hd = slice(h * FX_HEAD_DIM, (h + 1) * FX_HEAD_DIM)
        ones = jnp.where((lane < 3 * FX_HEADS) & (lane % FX_HEADS == h), 1.0, 0.0).astype(BF16)
        qa_scr[h] = jnp.concatenate([q_ref[:, hd], ones], axis=1)
    m_scr[...] = jnp.full_like(m_scr, -jnp.inf)
    l_scr[...] = jnp.zeros_like(l_scr)
    acc_scr[...] = jnp.zeros_like(acc_scr)

    def scores(j, h, q0):
        ks = pl.ds(pl.multiple_of(j * TK, TK), TK)
        hd = slice(h * FX_HEAD_DIM, (h + 1) * FX_HEAD_DIM)
        k_aug = jnp.concatenate([k_ref[ks, hd], caug_ref[ks, :]], axis=1)
        return _dot_nt(k_aug, qa_scr[h, q0:, :])

    def update(j, h, u, mask, q0):
        hd = slice(h * FX_HEAD_DIM, (h + 1) * FX_HEAD_DIM)
        if mask is not None:
            u = jnp.where(mask, u, -jnp.inf)
        c_row = rows[2 * ML_HEADS + h:2 * ML_HEADS + h + 1, q0:]
        m_prev = m_scr[h, :, q0:]
        m_new = jnp.maximum(m_prev, jnp.max(u, axis=0, keepdims=True) + c_row)
        p = jnp.exp2(u - (m_new - c_row))
        alpha = jnp.exp2(m_prev - m_new)
        l_scr[h, :, q0:] = alpha * l_scr[h, :, q0:] + jnp.sum(p, axis=0, keepdims=True)
        acc_scr[h, :, q0:] = alpha * acc_scr[h, :, q0:] + jnp.dot(vt_ref[j, hd, :], p.astype(BF16),
                                                                  preferred_element_type=F32)
        m_scr[h, :, q0:] = m_new

    def key_tile(j, mask=None, q0=0):
        u = {h: scores(j, h, q0) for h in range(MXU_LOOKAHEAD)}
        for h in range(FX_HEADS):
            if h + MXU_LOOKAHEAD < FX_HEADS:
                u[h + MXU_LOOKAHEAD] = scores(j, h + MXU_LOOKAHEAD, q0)
            update(j, h, u.pop(h), mask, q0)

    def body(j, carry):
        key_tile(j)
        return carry

    lax.fori_loop(0, i * KT_PER_Q, body, 0)
    for d in range(KT_PER_Q):
        key_tile(i * KT_PER_Q + d, _causal_mask_t(d), d * TK)
    for h in range(FX_HEADS):
        hd = slice(h * FX_HEAD_DIM, (h + 1) * FX_HEAD_DIM)
        y_ref[:, hd] = (acc_scr[h] * (1.0 / l_scr[h])).T.astype(BF16)


def _fox(z3, vt4, caug3, rows3):
    nb, s, _ = z3.shape
    nq = s // TQ
    return pl.pallas_call(
        _fox_body,
        grid=(nb, nq),
        in_specs=[
            pl.BlockSpec((None, TQ, D_MODEL), lambda b, i: (b, i, Z_FX_Q)),
            pl.BlockSpec((None, s, D_MODEL), lambda b, i: (b, 0, Z_FX_K)),
            pl.BlockSpec((None, s // TK, D_MODEL, TK), lambda b, i: (b, 0, 1, 0)),
            pl.BlockSpec((None, s, LANES), lambda b, i: (b, 0, 0)),
            pl.BlockSpec((None, N_GATES, TQ), lambda b, i: (b, 0, i)),
        ],
        out_specs=pl.BlockSpec((None, TQ, D_MODEL), lambda b, i: (b, i, 0)),
        out_shape=jax.ShapeDtypeStruct((nb, s, D_MODEL), BF16),
        scratch_shapes=[pltpu.VMEM((FX_HEADS, TQ, 2 * FX_HEAD_DIM), BF16),
                        pltpu.VMEM((FX_HEADS, FX_HEAD_DIM, TQ), F32),
                        pltpu.VMEM((FX_HEADS, 1, TQ), F32), pltpu.VMEM((FX_HEADS, 1, TQ), F32)],
        compiler_params=_params(("parallel", "arbitrary")),
        name="fox",
    )(z3, z3, vt4, caug3, rows3)


def _merge_body(x_ref, yml_ref, yfx_ref, gml_ref, gfx_ref, bml_ref, bfx_ref, wml_ref, wfx_ref, wout_ref, x1_ref):
    p_ml = jnp.dot(yml_ref[...], wml_ref[...], preferred_element_type=F32)
    p_fx = jnp.dot(yfx_ref[...], wfx_ref[...], preferred_element_type=F32)
    merged = (jax.nn.sigmoid(gml_ref[...].astype(F32) + bml_ref[...]) * p_ml
              + jax.nn.sigmoid(gfx_ref[...].astype(F32) + bfx_ref[...]) * p_fx)
    x1_ref[...] = x_ref[...] + jnp.dot(merged.astype(BF16), wout_ref[...], preferred_element_type=F32)


def _merge(x2d, yml, yfx, z2d, bml, bfx, wml, wfx, wout):
    t = x2d.shape[0]
    tile = lambda col: pl.BlockSpec((TM, D_MODEL), lambda i, col=col: (i, col))
    return pl.pallas_call(
        _merge_body,
        grid=(t // TM,),
        in_specs=[tile(0), tile(0), tile(0), tile(Z_GT_ML), tile(Z_GT_FX),
                  _resident((1, D_MODEL)), _resident((1, D_MODEL)),
                  _resident((D_MODEL, D_MODEL)), _resident((D_MODEL, D_MODEL)), _resident((D_MODEL, D_MODEL))],
        out_specs=tile(0),
        out_shape=jax.ShapeDtypeStruct((t, D_MODEL), F32),
        compiler_params=_params(("parallel",)),
        name="merge",
    )(x2d, yml, yfx, z2d, z2d, bml, bfx, wml, wfx, wout)


def _memkv_body(m_ref, g_ref, w_ref, kv_ref):
    hb = _rms(m_ref[...], g_ref[...]).astype(BF16)
    kv_ref[...] = jnp.dot(hb, w_ref[...], preferred_element_type=F32).astype(BF16)


def _memkv(mem2d, g, w):
    t = mem2d.shape[0]
    return pl.pallas_call(
        _memkv_body,
        grid=(t // TM,),
        in_specs=[pl.BlockSpec((TM, D_MODEL), lambda i: (i, 0)), _resident((1, D_MODEL)),
                  _resident((D_MODEL, 2 * D_MODEL))],
        out_specs=pl.BlockSpec((TM, 2 * D_MODEL), lambda i: (i, 0)),
        out_shape=jax.ShapeDtypeStruct((t, 2 * D_MODEL), BF16),
        compiler_params=_params(("parallel",)),
        name="memkv",
    )(mem2d, g, w)


def _route_t(lg_t):
    tm = lg_t.shape[1]
    ninf = -jnp.inf
    big = jnp.float32(LANES)
    gid = lax.broadcasted_iota(jnp.int32, (8, tm), 0).astype(F32)
    eid = lax.broadcasted_iota(jnp.int32, (N_EXPERTS, tm), 0).astype(F32)
    gl = jnp.where(gid < N_GROUPS, lg_t[N_EXPERTS:N_EXPERTS + 8], ninf)
    gmax = jnp.max(gl, axis=0, keepdims=True)
    gidx = jnp.min(jnp.where(gl == gmax, gid, big), axis=0, keepdims=True)
    g_p = 1.0 / jnp.sum(jnp.exp(gl - gmax), axis=0, keepdims=True)
    lo = gidx * EXPERTS_PER_GROUP
    el = jnp.where(eid >= lo, jnp.where(eid < lo + EXPERTS_PER_GROUP, lg_t[0:N_EXPERTS], ninf), ninf)
    v1 = jnp.max(el, axis=0, keepdims=True)
    i1 = jnp.min(jnp.where(el == v1, eid, big), axis=0, keepdims=True)
    el2 = jnp.where(eid == i1, ninf, el)
    v2 = jnp.max(el2, axis=0, keepdims=True)
    i2 = jnp.min(jnp.where(el2 == v2, eid, big), axis=0, keepdims=True)
    t = jnp.exp(v2 - v1)
    w1 = g_p / (1.0 + t)
    w2 = w1 * t
    return jnp.concatenate([i1, i2, w1, w2, jnp.zeros((4, tm), F32)], axis=0)


def _xattn_body(x1_ref, kv_ref, gq_ref, wq_ref, wo_ref, gm_ref, wrt_ref, brt_ref, x2_ref, h3_ref, route_ref,
                wcol_ref):
    x1 = x1_ref[...]
    hb = _rms(x1, gq_ref[...]).astype(BF16)
    q = (jnp.dot(hb, wq_ref[...], preferred_element_type=F32) * (XA_HEAD_DIM ** -0.5)).astype(BF16)
    outs = []
    for h in range(XA_HEADS):
        hd = slice(h * XA_HEAD_DIM, (h + 1) * XA_HEAD_DIM)
        vd = slice(D_MODEL + h * XA_HEAD_DIM, D_MODEL + (h + 1) * XA_HEAD_DIM)
        s = _dot_nt(q[:, hd], kv_ref[:, hd])
        p = jnp.exp(s - jnp.max(s, axis=1, keepdims=True))
        p = p * (1.0 / jnp.sum(p, axis=1, keepdims=True))
        outs.append(jnp.dot(p.astype(BF16), kv_ref[:, vd], preferred_element_type=F32).astype(BF16))
    o = jnp.concatenate(outs, axis=1)
    x2 = x1 + jnp.dot(o, wo_ref[...], preferred_element_type=F32)
    x2_ref[...] = x2
    h3 = _rms(x2, gm_ref[...])
    h3_ref[...] = _pack_halves(h3)
    lg_t = _dot_nt_split(wrt_ref[...], h3) + brt_ref[...]
    route = _route_t(lg_t)
    route_ref[...] = route
    wcol_ref[...] = jnp.concatenate([route, jnp.zeros((LANES - 8, route.shape[1]), F32)], axis=0).T


def _xattn(x1, kv3, gq, wq, wo, gm, wrt, brt, seq):
    t = x1.shape[0]
    per_b = seq // TM
    n_mem = kv3.shape[1]
    tile = pl.BlockSpec((TM, D_MODEL), lambda i: (i, 0))
    return pl.pallas_call(
        _xattn_body,
        grid=(t // TM,),
        in_specs=[tile,
                  pl.BlockSpec((None, n_mem, 2 * D_MODEL), lambda i: (i // per_b, 0, 0)),
                  _resident((1, D_MODEL)), _resident((D_MODEL, D_MODEL)), _resident((D_MODEL, D_MODEL)),
                  _resident((1, D_MODEL)), _resident((ROUTER_ROWS, D_MODEL)), _resident((ROUTER_ROWS, 1))],
        out_specs=[tile, pl.BlockSpec((TM, PACKED_W), lambda i: (i, 0)), pl.BlockSpec((8, TM), lambda i: (0, i)),
                   pl.BlockSpec((TM, LANES), lambda i: (i, 0))],
        out_shape=[jax.ShapeDtypeStruct((t, D_MODEL), F32), jax.ShapeDtypeStruct((t, PACKED_W), F32),
                   jax.ShapeDtypeStruct((8, t), F32), jax.ShapeDtypeStruct((t, LANES), F32)],
        compiler_params=_params(("parallel",)),
        name="xattn",
    )(x1, kv3, gq, wq, wo, gm, wrt, brt)


def _n_expert_tiles(n_tokens):
    return 2 * n_tokens // TME + N_EXPERTS


def _plan_body(route_ref, pos_ref, tinfo_ref, tok_ref, cnt_scr, run_scr, start_scr, tokhi_scr, toklo_scr, *, nt_pad,
               nt_rows, n_tokens, first_token):
    phase = pl.program_id(0)
    b = pl.program_id(1)
    r = route_ref[...]
    eid = lax.broadcasted_iota(jnp.int32, (N_EXPERTS, TP), 0).astype(F32)
    oh1 = eid == r[0:1]
    oh2 = eid == r[1:2]
    oh = jnp.where(oh1 | oh2, 1.0, 0.0)

    @pl.when((phase == 0) & (b == 0))
    def _():
        cnt_scr[...] = jnp.zeros_like(cnt_scr)

    @pl.when(phase == 0)
    def _():
        cnt_scr[...] += jnp.sum(oh, axis=1, keepdims=True)

    @pl.when((phase == 1) & (b == 0))
    def _():
        cnt = cnt_scr[...]
        n_tiles = jnp.floor((cnt + (TME - 1)) * (1.0 / TME))
        ri = lax.broadcasted_iota(jnp.int32, (N_EXPERTS, N_EXPERTS), 0)
        ci = lax.broadcasted_iota(jnp.int32, (N_EXPERTS, N_EXPERTS), 1)
        lower = jnp.where(ci < ri, 1.0, 0.0).astype(BF16)
        nt_hi, nt_lo = _split_bf16(jnp.broadcast_to(n_tiles, (N_EXPERTS, LANES)))
        start = (jnp.dot(lower, nt_hi, preferred_element_type=F32)
                 + jnp.dot(lower, nt_lo, preferred_element_type=F32))[:, 0:1]
        start_scr[...] = start * TME
        run_scr[...] = jnp.zeros_like(run_scr)
        tokhi_scr[...] = jnp.zeros_like(tokhi_scr)
        toklo_scr[...] = jnp.zeros_like(toklo_scr)
        n = lax.broadcasted_iota(jnp.int32, (N_EXPERTS, nt_pad), 1).astype(F32)
        e_n = lax.broadcasted_iota(jnp.int32, (N_EXPERTS, nt_pad), 0).astype(F32)
        owner = jnp.sum(jnp.where(start <= n, 1.0, 0.0), axis=0, keepdims=True) - 1.0
        own = e_n == owner
        cnt_o = jnp.sum(jnp.where(own, cnt, 0.0), axis=0, keepdims=True)
        start_o = jnp.sum(jnp.where(own, start, 0.0), axis=0, keepdims=True)
        valid = jnp.clip(cnt_o - (n[0:1] - start_o) * TME, 0.0, float(TME))
        tinfo_ref[...] = jnp.concatenate([owner, valid, jnp.zeros((6, nt_pad), F32)], axis=0).astype(jnp.int32)

    @pl.when(phase == 1)
    def _():
        ti = lax.broadcasted_iota(jnp.int32, (TP, TP), 0)
        tj = lax.broadcasted_iota(jnp.int32, (TP, TP), 1)
        upper = jnp.where(ti < tj, 1.0, 0.0).astype(BF16)
        before = jnp.dot(oh.astype(BF16), upper, preferred_element_type=F32)
        row = start_scr[...] + run_scr[...] + before
        p1 = jnp.sum(jnp.where(oh1, row, 0.0), axis=0, keepdims=True)
        p2 = jnp.sum(jnp.where(oh2, row, 0.0), axis=0, keepdims=True)
        pos_ref[...] = jnp.concatenate([p1, p2, jnp.zeros((6, TP), F32)], axis=0).astype(jnp.int32)
        run_scr[...] += jnp.sum(oh, axis=1, keepdims=True)

        tid = lax.broadcasted_iota(jnp.int32, (1, TP), 1) + (b * TP + first_token + 1)
        t_hi = (tid // 256).astype(F32)
        t_lo = (tid % 256).astype(F32)
        tile_id = lax.broadcasted_iota(jnp.int32, (nt_rows, TP), 0).astype(F32)
        row_id = lax.broadcasted_iota(jnp.int32, (TME, TP), 0).astype(F32)
        for p in (p1, p2):
            hi = jnp.floor(p * (1.0 / TME))
            lo = p - hi * TME
            in_tile = jnp.where(tile_id == hi, 1.0, 0.0).astype(BF16)
            at_row = row_id == lo
            tokhi_scr[...] += _dot_nt(in_tile, jnp.where(at_row, t_hi, 0.0).astype(BF16))
            toklo_scr[...] += _dot_nt(in_tile, jnp.where(at_row, t_lo, 0.0).astype(BF16))

    @pl.when((phase == 1) & (b == pl.num_programs(1) - 1))
    def _():
        hit = (tokhi_scr[...] * 256.0 + toklo_scr[...]).astype(jnp.int32)
        row = (lax.broadcasted_iota(jnp.int32, hit.shape, 0) * TME + lax.broadcasted_iota(jnp.int32, hit.shape, 1))
        tok_ref[...] = jnp.where(hit > 0, hit - 1, row % n_tokens)


def _plan(route, part):
    t_all = route.shape[1]
    assert t_all < 256 * 256, "token id + 1 is carried as two base-256 digits"
    t = t_all // MOE_PARTS
    steps = t // TP
    nt_pad = -(-_n_expert_tiles(t) // LANES) * LANES
    nt_rows = -(-_n_expert_tiles(t) // 8) * 8
    col = pltpu.VMEM((N_EXPERTS, 1), F32)
    table = pltpu.VMEM((nt_rows, TME), F32)
    return pl.pallas_call(
        functools.partial(_plan_body, nt_pad=nt_pad, nt_rows=nt_rows, n_tokens=t_all, first_token=part * t),
        grid=(2, steps),
        in_specs=[pl.BlockSpec((8, TP), lambda ph, b: (0, b + part * steps))],
        out_specs=[pl.BlockSpec((8, TP), lambda ph, b: (0, b * ph)),
                   pl.BlockSpec((8, nt_pad), lambda ph, b: (0, 0)),
                   pl.BlockSpec((nt_rows, TME), lambda ph, b: (0, 0))],
        out_shape=[jax.ShapeDtypeStruct((8, t), jnp.int32), jax.ShapeDtypeStruct((8, nt_pad), jnp.int32),
                   jax.ShapeDtypeStruct((nt_rows, TME), jnp.int32)],
        scratch_shapes=[col, col, col, table, table],
        compiler_params=_params(("arbitrary", "arbitrary")),
        name="plan",
    )(route)


def _experts_body(texp_ref, tvalid_ref, xs_ref, wg_ref, wu_ref, wd_ref, ys_ref, wgb_scr, wub_scr, wdb_scr):
    n = pl.program_id(0)

    @pl.when((n == 0) | (texp_ref[n] != texp_ref[jnp.maximum(n - 1, 0)]))
    def _():
        wgb_scr[...] = wg_ref[...].astype(BF16)
        wub_scr[...] = wu_ref[...].astype(BF16)
        wdb_scr[...] = wd_ref[...].astype(BF16)

    @pl.when(tvalid_ref[n] > 0)
    def _():
        xb = _unpack_halves(xs_ref[...]).astype(BF16)
        gate = jnp.dot(xb, wgb_scr[...], preferred_element_type=F32)
        up = jnp.dot(xb, wub_scr[...], preferred_element_type=F32)
        he = (gate * jax.nn.sigmoid(gate) * up).astype(BF16)
        ys_ref[...] = _pack_halves(jnp.dot(he, wdb_scr[...], preferred_element_type=F32))

    @pl.when(tvalid_ref[n] == 0)
    def _():
        ys_ref[...] = jnp.zeros_like(ys_ref)


def _experts(texp, tvalid, xs, wg, wu, wd):
    nt = texp.shape[0]
    weight = lambda shape: pl.BlockSpec((None,) + shape, lambda n, te, tv: (te[n], 0, 0))
    return pl.pallas_call(
        _experts_body,
        grid_spec=pltpu.PrefetchScalarGridSpec(
            num_scalar_prefetch=2,
            grid=(nt,),
            in_specs=[pl.BlockSpec((TME, PACKED_W), lambda n, te, tv: (jnp.where(tv[n] > 0, n, 0), 0)),
                      weight((D_MODEL, D_EXPERT)), weight((D_MODEL, D_EXPERT)), weight((D_EXPERT, D_MODEL))],
            out_specs=pl.BlockSpec((TME, PACKED_W), lambda n, te, tv: (n, 0)),
            scratch_shapes=[pltpu.VMEM((D_MODEL, D_EXPERT), BF16), pltpu.VMEM((D_MODEL, D_EXPERT), BF16),
                            pltpu.VMEM((D_EXPERT, D_MODEL), BF16)],
        ),
        out_shape=jax.ShapeDtypeStruct((nt * TME, PACKED_W), F32),
        compiler_params=_params(("arbitrary",)),
        name="experts",
    )(texp, tvalid, xs, wg, wu, wd)


def _rowgather(table, idx):
    n_rows, width = idx.shape[0], table.shape[1]
    n_workers = SC_CORES * SC_SUBCORES
    per_worker = n_rows // n_workers
    assert per_worker * n_workers == n_rows and per_worker % SC_ROWS == 0
    mesh = plsc.VectorSubcoreMesh(core_axis_name="c", subcore_axis_name="s", num_cores=SC_CORES,
                                  num_subcores=SC_SUBCORES)

    chunks = per_worker // SC_ROWS
    assert chunks % 2 == 0
    buf = lambda: [pltpu.VMEM((SC_ROWS,), jnp.int32), pltpu.VMEM((SC_ROWS, width), table.dtype),
                   pltpu.SemaphoreType.DMA]

    @functools.partial(pl.kernel, mesh=mesh, out_type=jax.ShapeDtypeStruct((n_rows, width), table.dtype),
                       scratch_types=buf() + buf(), name="rowgather")
    def gather(table_hbm, idx_hbm, out_hbm, idx_a, rows_a, sem_a, idx_b, rows_b, sem_b):
        worker = lax.axis_index("s") * SC_CORES + lax.axis_index("c")
        base = worker * per_worker

        def fetch(c, idx_v, rows_v, sem):
            pltpu.sync_copy(idx_hbm.at[pl.ds(base + c * SC_ROWS, SC_ROWS)], idx_v)
            pltpu.async_copy(table_hbm.at[idx_v], rows_v, sem)

        def flush(c, idx_v, rows_v, sem):
            pltpu.make_async_copy(table_hbm.at[idx_v], rows_v, sem).wait()
            pltpu.sync_copy(rows_v, out_hbm.at[pl.ds(base + c * SC_ROWS, SC_ROWS)])

        fetch(0, idx_a, rows_a, sem_a)

        @pl.loop(0, chunks, step=2)
        def _(c):
            fetch(c + 1, idx_b, rows_b, sem_b)
            flush(c, idx_a, rows_a, sem_a)

            @pl.when(c + 2 < chunks)
            def _():
                fetch(c + 2, idx_a, rows_a, sem_a)

            flush(c + 1, idx_b, rows_b, sem_b)

    return gather(table, idx)


def _combine_body(g1_ref, g2_ref, x2_ref, wcol_ref, gf_ref, *rest):
    out_ref = rest[-1]
    w = wcol_ref[...]
    y = w[:, 2:3] * _unpack_halves(g1_ref[...]) + w[:, 3:4] * _unpack_halves(g2_ref[...])
    out_ref[...] = _rms(x2_ref[...] + y, gf_ref[...])


def _combine(g, x2, wcol, gf, part, out_prev):
    t = x2.shape[0]
    steps = t // MOE_PARTS // TF
    mine = lambda w: pl.BlockSpec((TF, w), lambda i: (i + part * steps, 0))
    in_specs = [pl.BlockSpec((TF, PACKED_W), lambda i: (i, 0)),
                pl.BlockSpec((TF, PACKED_W), lambda i: (i + steps, 0)),
                mine(D_MODEL), mine(LANES), _resident((1, D_MODEL))]
    operands = [g, g, x2, wcol, gf]
    aliases = {}
    if out_prev is not None:
        in_specs.append(pl.BlockSpec(memory_space=pl.ANY))
        operands.append(out_prev)
        aliases = {len(operands) - 1: 0}
    return pl.pallas_call(
        _combine_body,
        grid=(steps,),
        in_specs=in_specs,
        out_specs=mine(D_MODEL),
        out_shape=jax.ShapeDtypeStruct((t, D_MODEL), F32),
        input_output_aliases=aliases,
        compiler_params=_params(("parallel",)),
        name="combine",
    )(*operands)


def _layer(x, mem, g_mix, w_in, b_ml_i, b_ml_f, b_fx_f, b_gate_ml, b_gate_fx, g_ml_head, w_proj_ml, w_proj_fx,
           w_out, g_xq, g_xmem, w_xq, w_xkv, w_xo, g_moe, w_rg, b_rg, w_re, b_re, w_gate, w_up, w_down):
    nb, seq, d = x.shape
    t = nb * seq
    row = lambda v: v.reshape(1, -1).astype(F32)

    o = 0
    parts = {}
    for name, width in (("ml_q", 512), ("ml_k", 512), ("ml_v", 1024), ("ml_o", 1024), ("ml_i", 4), ("ml_f", 4),
                        ("fx_q", 1024), ("fx_k", 1024), ("fx_v", 1024), ("fx_f", 8), ("gt_ml", 1024),
                        ("gt_fx", 1024)):
        parts[name] = w_in[:, o:o + width]
        o += width
    w_main = jnp.concatenate(
        [parts["ml_q"], parts["ml_k"] * (ML_QK_DIM ** -0.5), parts["ml_o"],
         parts["fx_q"] * (FX_HEAD_DIM ** -0.5 * LOG2E), parts["fx_k"], parts["gt_ml"], parts["gt_fx"]],
        axis=1).astype(BF16)
    w_vt = jnp.concatenate([parts["ml_v"], parts["fx_v"]], axis=1).T.astype(BF16)
    w_gates_t = jnp.concatenate([parts["ml_i"], parts["ml_f"], parts["fx_f"]], axis=1).T
    gate_bias = jnp.concatenate([b_ml_i, b_ml_f, b_fx_f]).reshape(N_GATES, 1).astype(F32)

    x2d = x.reshape(t, d)
    z, vt, gates_t = _inproj(x2d, row(g_mix), w_main, w_vt, w_gates_t)
    gt3 = gates_t.reshape(N_GATES, nb, seq).transpose(1, 0, 2)
    rows, cols, caug = _gateprep(gt3, gate_bias)
    z3 = z.reshape(nb, seq, Z_W)
    vt4 = vt.reshape(nb, seq // TK, VT_W, TK)
    y_ml = _mlstm(z3, vt4, cols, rows, row(g_ml_head))
    y_fx = _fox(z3, vt4, caug, rows)
    x1 = _merge(x2d, y_ml.reshape(t, d), y_fx.reshape(t, d), z, row(b_gate_ml), row(b_gate_fx),
                w_proj_ml.astype(BF16), w_proj_fx.astype(BF16), w_out.astype(BF16))

    n_mem = mem.shape[1]
    kv = _memkv(mem.reshape(nb * n_mem, d), row(g_xmem), w_xkv.astype(BF16))
    w_router_t = jnp.concatenate([w_re, w_rg, jnp.zeros((d, ROUTER_ROWS - N_EXPERTS - N_GROUPS), F32)], axis=1).T
    b_router_t = jnp.concatenate([b_re, b_rg, jnp.zeros((ROUTER_ROWS - N_EXPERTS - N_GROUPS,), F32)]).reshape(ROUTER_ROWS, 1)
    x2, h3, route, wcol = _xattn(x1, kv.reshape(nb, n_mem, 2 * d), row(g_xq), w_xq.astype(BF16),
                                 w_xo.astype(BF16), row(g_moe), w_router_t, b_router_t, seq)
    return x2, h3, route, wcol


def _moe(x2, h3, route, wcol, w_gate, w_up, w_down, g_final):
    t = x2.shape[0]
    nt = _n_expert_tiles(t // MOE_PARTS)
    out = None
    for part in range(MOE_PARTS):
        pos, tinfo, tok = _plan(route, part)
        texp, tvalid = tinfo[0, :nt], tinfo[1, :nt]
        xs = _rowgather(h3, tok[:nt].reshape(-1))
        ys = _experts(texp, tvalid, xs, w_gate, w_up, w_down)
        g = _rowgather(ys, pos[0:2].reshape(-1))
        out = _combine(g, x2, wcol, g_final, part, out)
    return out


def kernel(x, mem, g_mix, w_in, b_ml_i, b_ml_f, b_fx_f, b_gate_ml, b_gate_fx, g_ml_head, w_proj_ml, w_proj_fx, w_out, g_xq, g_xmem, w_xq, w_xkv, w_xo, g_moe, w_rg, b_rg, w_re, b_re, w_gate, w_up, w_down, g_final):
    nb, seq, d = x.shape
    depth = g_mix.shape[0]
    assert depth == 1, "the final rmsnorm is fused into the (single) layer's combine kernel"
    x2, h3, route, wcol = _layer(
        x, mem, g_mix[0], w_in[0], b_ml_i[0], b_ml_f[0], b_fx_f[0], b_gate_ml[0], b_gate_fx[0], g_ml_head[0],
        w_proj_ml[0], w_proj_fx[0], w_out[0], g_xq[0], g_xmem[0], w_xq[0], w_xkv[0], w_xo[0], g_moe[0],
        w_rg[0], b_rg[0], w_re[0], b_re[0], w_gate[0], w_up[0], w_down[0])
    out = _moe(x2, h3, route, wcol, w_gate[0], w_up[0], w_down[0], g_final.reshape(1, d).astype(F32))
    return out.reshape(nb, seq, d)
```

```python
import functools

import jax
import jax.numpy as jnp
from jax import lax
from jax.experimental import pallas as pl
from jax.experimental.pallas import tpu as pltpu
from jax.experimental.pallas import tpu_sc as plsc

F32 = jnp.float32
BF16 = jnp.bfloat16

D_MODEL = 1024
EPS = 1e-6
ML_HEADS = 4
ML_QK_DIM = 128
ML_V_DIM = 256
FX_HEADS = 8
FX_HEAD_DIM = 128
XA_HEADS = 4
XA_HEAD_DIM = 256
N_GROUPS = 4
EXPERTS_PER_GROUP = 8
N_EXPERTS = 32
D_EXPERT = 512

LANES = 128
N_GATES = 16
ROUTER_ROWS = 40
LOG2E = 1.4426950408889634
Z_W = 6144
Z_ML_O, Z_FX_Q, Z_FX_K, Z_GT_ML, Z_GT_FX = 1, 2, 3, 4, 5
VT_W = 2048

VMEM_LIMIT = 56 * 1024 * 1024

IN_CHUNK = 1024
TK = 256
TQ = 512
TIN = 512
KT_PER_Q = TQ // TK
MXU_LOOKAHEAD = 4
TM = 512
TME = 512
TP = 512
TF = 512
MOE_PARTS = 1
SC_CORES, SC_SUBCORES = 2, 16
PACKED_W = D_MODEL // 2
SC_ROWS = 64


def _params(sem, flags=None):
    return pltpu.CompilerParams(dimension_semantics=sem, vmem_limit_bytes=VMEM_LIMIT, flags=flags)


def _rms(x, g):
    return x * lax.rsqrt(jnp.mean(x * x, axis=-1, keepdims=True) + EPS) * g


def _dot_nt(a, b, **kw):
    return lax.dot_general(a, b, (((1,), (1,)), ((), ())), preferred_element_type=F32, **kw)


def _split_bf16(x):
    hi = x.astype(BF16)
    return hi, (x - hi.astype(F32)).astype(BF16)


def _dot_nt_split(a, b):
    a_hi, a_lo = _split_bf16(a)
    b_hi, b_lo = _split_bf16(b)
    return _dot_nt(a_hi, b_hi) + (_dot_nt(a_hi, b_lo) + _dot_nt(a_lo, b_hi))


def _pack_halves(x):
    n = x.shape[1] // 2
    bits = lambda v: lax.bitcast_convert_type(v.astype(BF16).astype(F32), jnp.uint32)
    w = (bits(x[:, :n]) >> 16) | (bits(x[:, n:]) & jnp.uint32(0xFFFF0000))
    return lax.bitcast_convert_type(w, F32)


def _unpack_halves(p):
    w = lax.bitcast_convert_type(p, jnp.uint32)
    lo = lax.bitcast_convert_type(w << 16, F32)
    hi = lax.bitcast_convert_type(w & jnp.uint32(0xFFFF0000), F32)
    return jnp.concatenate([lo, hi], axis=1)


def _resident(shape):
    zeros = (0,) * len(shape)
    return pl.BlockSpec(shape, lambda *_: zeros, pipeline_mode=pl.Buffered(1))


def _inproj_body(x_ref, g_ref, w_ref, wvt_ref, wgt_ref, z_ref, vt_ref, gt_ref):
    h = _rms(x_ref[...], g_ref[...])
    hb = h.astype(BF16)
    for c in range(Z_W // IN_CHUNK):
        sl = slice(c * IN_CHUNK, (c + 1) * IN_CHUNK)
        z_ref[:, sl] = jnp.dot(hb, w_ref[:, sl], preferred_element_type=F32).astype(BF16)
    for kt in range(TIN // TK):
        hk = hb[kt * TK:(kt + 1) * TK]
        for c in range(VT_W // IN_CHUNK):
            sl = slice(c * IN_CHUNK, (c + 1) * IN_CHUNK)
            vt_ref[kt, sl, :] = _dot_nt(wvt_ref[sl, :], hk).astype(BF16)
    gt_ref[...] = _dot_nt_split(wgt_ref[...], h)


def _inproj(x2d, g, w, wvt, wgt, seq):
    t = x2d.shape[0]
    per_b = seq // TIN
    return pl.pallas_call(
        _inproj_body,
        grid=(t // TIN,),
        in_specs=[
            pl.BlockSpec((TIN, D_MODEL), lambda i: (i, 0)),
            _resident((1, D_MODEL)),
            _resident((D_MODEL, Z_W)),
            _resident((VT_W, D_MODEL)),
            _resident((N_GATES, D_MODEL)),
        ],
        out_specs=[
            pl.BlockSpec((TIN, Z_W), lambda i: (i, 0)),
            pl.BlockSpec((TIN // TK, VT_W, TK), lambda i: (i, 0, 0)),
            pl.BlockSpec((None, N_GATES, TIN), lambda i: (i // per_b, 0, i % per_b)),
        ],
        out_shape=[jax.ShapeDtypeStruct((t, Z_W), BF16), jax.ShapeDtypeStruct((t // TK, VT_W, TK), BF16),
                   jax.ShapeDtypeStruct((t // seq, N_GATES, seq), F32)],
        compiler_params=_params(("parallel",)),
        name="inproj",
    )(x2d, g, w, wvt, wgt)


def _scan_lanes(x, op, identity):
    n = x.shape[-1]
    idx = lax.broadcasted_iota(jnp.int32, x.shape, 1)
    s = 1
    while s < n:
        shifted = pltpu.roll(x, s, axis=1)
        x = op(x, jnp.where(idx >= s, shifted, identity))
        s *= 2
    return x


def _log_sigmoid(x):
    return jnp.minimum(x, 0.0) - jnp.log1p(jnp.exp(-jnp.abs(x)))


def _gateprep_body(gt_ref, bias_ref, rows_ref, cols_ref, caug_ref):
    g = gt_ref[...] + bias_ref[...]
    s = g.shape[1]
    cs = _scan_lanes(_log_sigmoid(g), jnp.add, 0.0)
    b = cs[4:8]
    c2 = cs[8:16] * LOG2E
    a = g[0:4] - b
    m = _scan_lanes(jnp.concatenate([a, a], axis=0), jnp.maximum, -jnp.inf)[0:4]
    rows_ref[...] = jnp.concatenate([m * LOG2E, b + m, c2], axis=0)
    cols_ref[...] = jnp.concatenate([a * LOG2E, jnp.zeros((LANES - ML_HEADS, s), F32)], axis=0).T
    hi = c2.astype(BF16).astype(F32)
    r1 = c2 - hi
    mid = r1.astype(BF16).astype(F32)
    lo = r1 - mid
    aug = jnp.concatenate([-hi, -mid, -lo, jnp.zeros((LANES - 3 * FX_HEADS, s), F32)], axis=0)
    caug_ref[...] = aug.T.astype(BF16)


def _gateprep(gt3, bias):
    nb, _, s = gt3.shape
    return pl.pallas_call(
        _gateprep_body,
        grid=(nb,),
        in_specs=[
            pl.BlockSpec((None, N_GATES, s), lambda b: (b, 0, 0)),
            _resident((N_GATES, 1)),
        ],
        out_specs=[
            pl.BlockSpec((None, N_GATES, s), lambda b: (b, 0, 0)),
            pl.BlockSpec((None, s, LANES), lambda b: (b, 0, 0)),
            pl.BlockSpec((None, s, LANES), lambda b: (b, 0, 0)),
        ],
        out_shape=[jax.ShapeDtypeStruct((nb, N_GATES, s), F32), jax.ShapeDtypeStruct((nb, s, LANES), F32),
                   jax.ShapeDtypeStruct((nb, s, LANES), BF16)],
        compiler_params=_params(("parallel",)),
        name="gateprep",
    )(gt3, bias)


def _causal_mask_t(d):
    s = lax.broadcasted_iota(jnp.int32, (TK, TQ - d * TK), 0)
    t = lax.broadcasted_iota(jnp.int32, (TK, TQ - d * TK), 1)
    return s <= t


def _mlstm_body(q_ref, k_ref, vt_ref, o_ref, cols_ref, rows_ref, gh_ref, y_ref, num_scr, den_scr):
    i = pl.program_id(1)
    rows = rows_ref[...]
    num_scr[...] = jnp.zeros_like(num_scr)
    den_scr[...] = jnp.zeros_like(den_scr)

    def scores(j, h, q0):
        ks = pl.ds(pl.multiple_of(j * TK, TK), TK)
        qk = slice(h * ML_QK_DIM, (h + 1) * ML_QK_DIM)
        return _dot_nt(k_ref[ks, qk], q_ref[q0:, qk])

    def update(j, h, s, mask, q0):
        ks = pl.ds(pl.multiple_of(j * TK, TK), TK)
        vv = slice(h * ML_V_DIM, (h + 1) * ML_V_DIM)
        w = jnp.exp2(cols_ref[ks, h:h + 1] - rows[h:h + 1, q0:])
        if mask is not None:
            w = jnp.where(mask, w, 0.0)
        s = s * w
        den_scr[h, :, q0:] += jnp.sum(s, axis=0, keepdims=True)
        num_scr[h, :, q0:] += jnp.dot(vt_ref[j, vv, :], s.astype(BF16), preferred_element_type=F32)

    def key_tile(j, mask=None, q0=0):
        s = {h: scores(j, h, q0) for h in range(min(MXU_LOOKAHEAD, ML_HEADS))}
        for h in range(ML_HEADS):
            if h + MXU_LOOKAHEAD < ML_HEADS:
                s[h + MXU_LOOKAHEAD] = scores(j, h + MXU_LOOKAHEAD, q0)
            update(j, h, s.pop(h), mask, q0)

    def body(jj, carry):
        for d in range(KT_PER_Q):
            key_tile(jj * KT_PER_Q + d)
        return carry

    lax.fori_loop(0, i, body, 0)
    for d in range(KT_PER_Q):
        key_tile(i * KT_PER_Q + d, _causal_mask_t(d), d * TK)
    for h in range(ML_HEADS):
        vv = slice(h * ML_V_DIM, (h + 1) * ML_V_DIM)
        floor = jnp.exp(-rows[ML_HEADS + h:ML_HEADS + h + 1])
        hh = num_scr[h] * (1.0 / jnp.maximum(jnp.abs(den_scr[h]), floor))
        yt = hh * lax.rsqrt(jnp.mean(hh * hh, axis=0, keepdims=True) + EPS)
        y = yt.T * gh_ref[:, vv]
        y_ref[:, vv] = (y * jax.nn.sigmoid(o_ref[:, vv].astype(F32))).astype(BF16)


def _mlstm(z3, vt4, cols3, rows3, g_head):
    nb, s, _ = z3.shape
    nq = s // TQ
    return pl.pallas_call(
        _mlstm_body,
        grid=(nb, nq),
        in_specs=[
            pl.BlockSpec((None, TQ, ML_HEADS * ML_QK_DIM), lambda b, i: (b, i, 0)),
            pl.BlockSpec((None, s, ML_HEADS * ML_QK_DIM), lambda b, i: (b, 0, 1)),
            pl.BlockSpec((None, s // TK, D_MODEL, TK), lambda b, i: (b, 0, 0, 0)),
            pl.BlockSpec((None, TQ, D_MODEL), lambda b, i: (b, i, Z_ML_O)),
            pl.BlockSpec((None, s, LANES), lambda b, i: (b, 0, 0)),
            pl.BlockSpec((None, N_GATES, TQ), lambda b, i: (b, 0, i)),
            _resident((1, D_MODEL)),
        ],
        out_specs=pl.BlockSpec((None, TQ, D_MODEL), lambda b, i: (b, i, 0)),
        out_shape=jax.ShapeDtypeStruct((nb, s, D_MODEL), BF16),
        scratch_shapes=[pltpu.VMEM((ML_HEADS, ML_V_DIM, TQ), F32), pltpu.VMEM((ML_HEADS, 1, TQ), F32)],
        compiler_params=_params(("parallel", "arbitrary")),
        name="mlstm",
    )(z3, z3, vt4, z3, cols3, rows3, g_head)


def _fox_body(q_ref, k_ref, vt_ref, caug_ref, rows_ref, y_ref, qa_scr, acc_scr, m_scr, l_scr):
    i = pl.program_id(1)
    rows = rows_ref[...]
    lane = lax.broadcasted_iota(jnp.int32, (TQ, LANES), 1)
    for h in range(FX_HEADS):
        hd = slice(h * FX_HEAD_DIM, (h + 1) * FX_HEAD_DIM)
        ones = jnp.where((lane < 3 * FX_HEADS) & (lane % FX_HEADS == h), 1.0, 0.0).astype(BF16)
        qa_scr[h] = jnp.concatenate([q_ref[:, hd], ones], axis=1)
    m_scr[...] = jnp.full_like(m_scr, -jnp.inf)
    l_scr[...] = jnp.zeros_like(l_scr)
    acc_scr[...] = jnp.zeros_like(acc_scr)

    def scores(j, h, q0):
        ks = pl.ds(pl.multiple_of(j * TK, TK), TK)
        hd = slice(h * FX_HEAD_DIM, (h + 1) * FX_HEAD_DIM)
        k_aug = jnp.concatenate([k_ref[ks, hd], caug_ref[ks, :]], axis=1)
        return _dot_nt(k_aug, qa_scr[h, q0:, :])

    def update(j, h, u, mask, q0):
        hd = slice(h * FX_HEAD_DIM, (h + 1) * FX_HEAD_DIM)
        if mask is not None:
            u = jnp.where(mask, u, -jnp.inf)
        c_row = rows[2 * ML_HEADS + h:2 * ML_HEADS + h + 1, q0:]
        m_prev = m_scr[h, :, q0:]
        m_new = jnp.maximum(m_prev, jnp.max(u, axis=0, keepdims=True) + c_row)
        p = jnp.exp2(u - (m_new - c_row))
        alpha = jnp.exp2(m_prev - m_new)
        l_scr[h, :, q0:] = alpha * l_scr[h, :, q0:] + jnp.sum(p, axis=0, keepdims=True)
        acc_scr[h, :, q0:] = alpha * acc_scr[h, :, q0:] + jnp.dot(vt_ref[j, hd, :], p.astype(BF16),
                                                                  preferred_element_type=F32)
        m_scr[h, :, q0:] = m_new

    def key_tile(j, mask=None, q0=0):
        u = {h: scores(j, h, q0) for h in range(MXU_LOOKAHEAD)}
        for h in range(FX_HEADS):
            if h + MXU_LOOKAHEAD < FX_HEADS:
                u[h + MXU_LOOKAHEAD] = scores(j, h + MXU_LOOKAHEAD, q0)
            update(j, h, u.pop(h), mask, q0)

    def body(jj, carry):
        for d in range(KT_PER_Q):
            key_tile(jj * KT_PER_Q + d)
        return carry

    lax.fori_loop(0, i, body, 0)
    for d in range(KT_PER_Q):
        key_tile(i * KT_PER_Q + d, _causal_mask_t(d), d * TK)
    for h in range(FX_HEADS):
        hd = slice(h * FX_HEAD_DIM, (h + 1) * FX_HEAD_DIM)
        y_ref[:, hd] = (acc_scr[h] * (1.0 / l_scr[h])).T.astype(BF16)


def _fox(z3, vt4, caug3, rows3):
    nb, s, _ = z3.shape
    nq = s // TQ
    return pl.pallas_call(
        _fox_body,
        grid=(nb, nq),
        in_specs=[
            pl.BlockSpec((None, TQ, D_MODEL), lambda b, i: (b, i, Z_FX_Q)),
            pl.BlockSpec((None, s, D_MODEL), lambda b, i: (b, 0, Z_FX_K)),
            pl.BlockSpec((None, s // TK, D_MODEL, TK), lambda b, i: (b, 0, 1, 0)),
            pl.BlockSpec((None, s, LANES), lambda b, i: (b, 0, 0)),
            pl.BlockSpec((None, N_GATES, TQ), lambda b, i: (b, 0, i)),
        ],
        out_specs=pl.BlockSpec((None, TQ, D_MODEL), lambda b, i: (b, i, 0)),
        out_shape=jax.ShapeDtypeStruct((nb, s, D_MODEL), BF16),
        scratch_shapes=[pltpu.VMEM((FX_HEADS, TQ, 2 * FX_HEAD_DIM), BF16),
                        pltpu.VMEM((FX_HEADS, FX_HEAD_DIM, TQ), F32),
                        pltpu.VMEM((FX_HEADS, 1, TQ), F32), pltpu.VMEM((FX_HEADS, 1, TQ), F32)],
        compiler_params=_params(("parallel", "arbitrary")),
        name="fox",
    )(z3, z3, vt4, caug3, rows3)


def _merge_body(x_ref, yml_ref, yfx_ref, gml_ref, gfx_ref, bml_ref, bfx_ref, wml_ref, wfx_ref, wout_ref, x1_ref):
    p_ml = jnp.dot(yml_ref[...], wml_ref[...], preferred_element_type=F32)
    p_fx = jnp.dot(yfx_ref[...], wfx_ref[...], preferred_element_type=F32)
    merged = (jax.nn.sigmoid(gml_ref[...].astype(F32) + bml_ref[...]) * p_ml
              + jax.nn.sigmoid(gfx_ref[...].astype(F32) + bfx_ref[...]) * p_fx)
    x1_ref[...] = x_ref[...] + jnp.dot(merged.astype(BF16), wout_ref[...], preferred_element_type=F32)


def _merge(x2d, yml, yfx, z2d, bml, bfx, wml, wfx, wout):
    t = x2d.shape[0]
    tile = lambda col: pl.BlockSpec((TM, D_MODEL), lambda i, col=col: (i, col))
    return pl.pallas_call(
        _merge_body,
        grid=(t // TM,),
        in_specs=[tile(0), tile(0), tile(0), tile(Z_GT_ML), tile(Z_GT_FX),
                  _resident((1, D_MODEL)), _resident((1, D_MODEL)),
                  _resident((D_MODEL, D_MODEL)), _resident((D_MODEL, D_MODEL)), _resident((D_MODEL, D_MODEL))],
        out_specs=tile(0),
        out_shape=jax.ShapeDtypeStruct((t, D_MODEL), F32),
        compiler_params=_params(("parallel",)),
        name="merge",
    )(x2d, yml, yfx, z2d, z2d, bml, bfx, wml, wfx, wout)


def _memkv_body(m_ref, g_ref, w_ref, kv_ref):
    hb = _rms(m_ref[...], g_ref[...]).astype(BF16)
    kv_ref[...] = jnp.dot(hb, w_ref[...], preferred_element_type=F32).astype(BF16)


def _memkv(mem2d, g, w):
    t = mem2d.shape[0]
    return pl.pallas_call(
        _memkv_body,
        grid=(t // TM,),
        in_specs=[pl.BlockSpec((TM, D_MODEL), lambda i: (i, 0)), _resident((1, D_MODEL)),
                  _resident((D_MODEL, 2 * D_MODEL))],
        out_specs=pl.BlockSpec((TM, 2 * D_MODEL), lambda i: (i, 0)),
        out_shape=jax.ShapeDtypeStruct((t, 2 * D_MODEL), BF16),
        compiler_params=_params(("parallel",)),
        name="memkv",
    )(mem2d, g, w)


def _route_t(lg_t):
    tm = lg_t.shape[1]
    ninf = -jnp.inf
    big = jnp.float32(LANES)
    gid = lax.broadcasted_iota(jnp.int32, (8, tm), 0).astype(F32)
    eid = lax.broadcasted_iota(jnp.int32, (N_EXPERTS, tm), 0).astype(F32)
    gl = jnp.where(gid < N_GROUPS, lg_t[N_EXPERTS:N_EXPERTS + 8], ninf)
    gmax = jnp.max(gl, axis=0, keepdims=True)
    gidx = jnp.min(jnp.where(gl == gmax, gid, big), axis=0, keepdims=True)
    g_p = 1.0 / jnp.sum(jnp.exp(gl - gmax), axis=0, keepdims=True)
    lo = gidx * EXPERTS_PER_GROUP
    el = jnp.where(eid >= lo, jnp.where(eid < lo + EXPERTS_PER_GROUP, lg_t[0:N_EXPERTS], ninf), ninf)
    v1 = jnp.max(el, axis=0, keepdims=True)
    i1 = jnp.min(jnp.where(el == v1, eid, big), axis=0, keepdims=True)
    el2 = jnp.where(eid == i1, ninf, el)
    v2 = jnp.max(el2, axis=0, keepdims=True)
    i2 = jnp.min(jnp.where(el2 == v2, eid, big), axis=0, keepdims=True)
    t = jnp.exp(v2 - v1)
    w1 = g_p / (1.0 + t)
    w2 = w1 * t
    return jnp.concatenate([i1, i2, w1, w2, jnp.zeros((4, tm), F32)], axis=0)


def _xattn_body(x1_ref, kv_ref, gq_ref, wq_ref, wo_ref, gm_ref, wrt_ref, brt_ref, x2_ref, h3_ref, route_ref,
                wcol_ref):
    x1 = x1_ref[...]
    hb = _rms(x1, gq_ref[...]).astype(BF16)
    q = (jnp.dot(hb, wq_ref[...], preferred_element_type=F32) * (XA_HEAD_DIM ** -0.5)).astype(BF16)
    outs = []
    for h in range(XA_HEADS):
        hd = slice(h * XA_HEAD_DIM, (h + 1) * XA_HEAD_DIM)
        vd = slice(D_MODEL + h * XA_HEAD_DIM, D_MODEL + (h + 1) * XA_HEAD_DIM)
        s = _dot_nt(q[:, hd], kv_ref[:, hd])
        p = jnp.exp(s - jnp.max(s, axis=1, keepdims=True))
        p = p * (1.0 / jnp.sum(p, axis=1, keepdims=True))
        outs.append(jnp.dot(p.astype(BF16), kv_ref[:, vd], preferred_element_type=F32).astype(BF16))
    o = jnp.concatenate(outs, axis=1)
    x2 = x1 + jnp.dot(o, wo_ref[...], preferred_element_type=F32)
    x2_ref[...] = x2
    h3 = _rms(x2, gm_ref[...])
    h3_ref[...] = _pack_halves(h3)
    lg_t = _dot_nt_split(wrt_ref[...], h3) + brt_ref[...]
    route = _route_t(lg_t)
    route_ref[...] = route
    wcol_ref[...] = jnp.concatenate([route, jnp.zeros((LANES - 8, route.shape[1]), F32)], axis=0).T


def _xattn(x1, kv3, gq, wq, wo, gm, wrt, brt, seq):
    t = x1.shape[0]
    per_b = seq // TM
    n_mem = kv3.shape[1]
    tile = pl.BlockSpec((TM, D_MODEL), lambda i: (i, 0))
    return pl.pallas_call(
        _xattn_body,
        grid=(t // TM,),
        in_specs=[tile,
                  pl.BlockSpec((None, n_mem, 2 * D_MODEL), lambda i: (i // per_b, 0, 0)),
                  _resident((1, D_MODEL)), _resident((D_MODEL, D_MODEL)), _resident((D_MODEL, D_MODEL)),
                  _resident((1, D_MODEL)), _resident((ROUTER_ROWS, D_MODEL)), _resident((ROUTER_ROWS, 1))],
        out_specs=[tile, pl.BlockSpec((TM, PACKED_W), lambda i: (i, 0)), pl.BlockSpec((8, TM), lambda i: (0, i)),
                   pl.BlockSpec((TM, LANES), lambda i: (i, 0))],
        out_shape=[jax.ShapeDtypeStruct((t, D_MODEL), F32), jax.ShapeDtypeStruct((t, PACKED_W), F32),
                   jax.ShapeDtypeStruct((8, t), F32), jax.ShapeDtypeStruct((t, LANES), F32)],
        compiler_params=_params(("parallel",)),
        name="xattn",
    )(x1, kv3, gq, wq, wo, gm, wrt, brt)


def _n_expert_tiles(n_tokens):
    return 2 * n_tokens // TME + N_EXPERTS


def _plan_body(route_ref, pos_ref, tinfo_ref, tok_ref, cnt_scr, run_scr, start_scr, tokhi_scr, toklo_scr, *, nt_pad,
               nt_rows, n_tokens, first_token):
    phase = pl.program_id(0)
    b = pl.program_id(1)
    r = route_ref[...]
    eid = lax.broadcasted_iota(jnp.int32, (N_EXPERTS, TP), 0).astype(F32)
    oh1 = eid == r[0:1]
    oh2 = eid == r[1:2]
    oh = jnp.where(oh1 | oh2, 1.0, 0.0)

    @pl.when((phase == 0) & (b == 0))
    def _():
        cnt_scr[...] = jnp.zeros_like(cnt_scr)

    @pl.when(phase == 0)
    def _():
        cnt_scr[...] += jnp.sum(oh, axis=1, keepdims=True)

    @pl.when((phase == 1) & (b == 0))
    def _():
        cnt = cnt_scr[...]
        n_tiles = jnp.floor((cnt + (TME - 1)) * (1.0 / TME))
        ri = lax.broadcasted_iota(jnp.int32, (N_EXPERTS, N_EXPERTS), 0)
        ci = lax.broadcasted_iota(jnp.int32, (N_EXPERTS, N_EXPERTS), 1)
        lower = jnp.where(ci < ri, 1.0, 0.0).astype(BF16)
        nt_hi, nt_lo = _split_bf16(jnp.broadcast_to(n_tiles, (N_EXPERTS, LANES)))
        start = (jnp.dot(lower, nt_hi, preferred_element_type=F32)
                 + jnp.dot(lower, nt_lo, preferred_element_type=F32))[:, 0:1]
        start_scr[...] = start * TME
        run_scr[...] = jnp.zeros_like(run_scr)
        tokhi_scr[...] = jnp.zeros_like(tokhi_scr)
        toklo_scr[...] = jnp.zeros_like(toklo_scr)
        n = lax.broadcasted_iota(jnp.int32, (N_EXPERTS, nt_pad), 1).astype(F32)
        e_n = lax.broadcasted_iota(jnp.int32, (N_EXPERTS, nt_pad), 0).astype(F32)
        owner = jnp.sum(jnp.where(start <= n, 1.0, 0.0), axis=0, keepdims=True) - 1.0
        own = e_n == owner
        cnt_o = jnp.sum(jnp.where(own, cnt, 0.0), axis=0, keepdims=True)
        start_o = jnp.sum(jnp.where(own, start, 0.0), axis=0, keepdims=True)
        valid = jnp.clip(cnt_o - (n[0:1] - start_o) * TME, 0.0, float(TME))
        tinfo_ref[...] = jnp.concatenate([owner, valid, jnp.zeros((6, nt_pad), F32)], axis=0).astype(jnp.int32)

    @pl.when(phase == 1)
    def _():
        ti = lax.broadcasted_iota(jnp.int32, (TP, TP), 0)
        tj = lax.broadcasted_iota(jnp.int32, (TP, TP), 1)
        upper = jnp.where(ti < tj, 1.0, 0.0).astype(BF16)
        before = jnp.dot(oh.astype(BF16), upper, preferred_element_type=F32)
        row = start_scr[...] + run_scr[...] + before
        p1 = jnp.sum(jnp.where(oh1, row, 0.0), axis=0, keepdims=True)
        p2 = jnp.sum(jnp.where(oh2, row, 0.0), axis=0, keepdims=True)
        pos_ref[...] = jnp.concatenate([p1, p2, jnp.zeros((6, TP), F32)], axis=0).astype(jnp.int32)
        run_scr[...] += jnp.sum(oh, axis=1, keepdims=True)

        tid = lax.broadcasted_iota(jnp.int32, (1, TP), 1) + (b * TP + first_token + 1)
        t_hi = (tid // 256).astype(F32)
        t_lo = (tid % 256).astype(F32)
        tile_id = lax.broadcasted_iota(jnp.int32, (nt_rows, TP), 0).astype(F32)
        row_id = lax.broadcasted_iota(jnp.int32, (TME, TP), 0).astype(F32)
        for p in (p1, p2):
            hi = jnp.floor(p * (1.0 / TME))
            lo = p - hi * TME
            in_tile = jnp.where(tile_id == hi, 1.0, 0.0).astype(BF16)
            at_row = row_id == lo
            tokhi_scr[...] += _dot_nt(in_tile, jnp.where(at_row, t_hi, 0.0).astype(BF16))
            toklo_scr[...] += _dot_nt(in_tile, jnp.where(at_row, t_lo, 0.0).astype(BF16))

    @pl.when((phase == 1) & (b == pl.num_programs(1) - 1))
    def _():
        hit = (tokhi_scr[...] * 256.0 + toklo_scr[...]).astype(jnp.int32)
        row = (lax.broadcasted_iota(jnp.int32, hit.shape, 0) * TME + lax.broadcasted_iota(jnp.int32, hit.shape, 1))
        tok_ref[...] = jnp.where(hit > 0, hit - 1, row % n_tokens)


def _plan(route, part):
    t_all = route.shape[1]
    assert t_all < 256 * 256, "token id + 1 is carried as two base-256 digits"
    t = t_all // MOE_PARTS
    steps = t // TP
    nt_pad = -(-_n_expert_tiles(t) // LANES) * LANES
    nt_rows = -(-_n_expert_tiles(t) // 8) * 8
    col = pltpu.VMEM((N_EXPERTS, 1), F32)
    table = pltpu.VMEM((nt_rows, TME), F32)
    return pl.pallas_call(
        functools.partial(_plan_body, nt_pad=nt_pad, nt_rows=nt_rows, n_tokens=t_all, first_token=part * t),
        grid=(2, steps),
        in_specs=[pl.BlockSpec((8, TP), lambda ph, b: (0, b + part * steps))],
        out_specs=[pl.BlockSpec((8, TP), lambda ph, b: (0, b * ph)),
                   pl.BlockSpec((8, nt_pad), lambda ph, b: (0, 0)),
                   pl.BlockSpec((nt_rows, TME), lambda ph, b: (0, 0))],
        out_shape=[jax.ShapeDtypeStruct((8, t), jnp.int32), jax.ShapeDtypeStruct((8, nt_pad), jnp.int32),
                   jax.ShapeDtypeStruct((nt_rows, TME), jnp.int32)],
        scratch_shapes=[col, col, col, table, table],
        compiler_params=_params(("arbitrary", "arbitrary")),
        name="plan",
    )(route)


def _experts_body(texp_ref, tvalid_ref, xs_ref, wg_ref, wu_ref, wd_ref, ys_ref, wgb_scr, wub_scr, wdb_scr):
    n = pl.program_id(0)

    @pl.when((n == 0) | (texp_ref[n] != texp_ref[jnp.maximum(n - 1, 0)]))
    def _():
        wgb_scr[...] = wg_ref[...].astype(BF16)
        wub_scr[...] = wu_ref[...].astype(BF16)
        wdb_scr[...] = wd_ref[...].astype(BF16)

    @pl.when(tvalid_ref[n] > 0)
    def _():
        xb = _unpack_halves(xs_ref[...]).astype(BF16)
        gate = jnp.dot(xb, wgb_scr[...], preferred_element_type=F32)
        up = jnp.dot(xb, wub_scr[...], preferred_element_type=F32)
        he = (gate * jax.nn.sigmoid(gate) * up).astype(BF16)
        ys_ref[...] = _pack_halves(jnp.dot(he, wdb_scr[...], preferred_element_type=F32))

    @pl.when(tvalid_ref[n] == 0)
    def _():
        ys_ref[...] = jnp.zeros_like(ys_ref)


def _experts(texp, tvalid, xs, wg, wu, wd):
    nt = texp.shape[0]
    weight = lambda shape: pl.BlockSpec((None,) + shape, lambda n, te, tv: (te[n], 0, 0))
    return pl.pallas_call(
        _experts_body,
        grid_spec=pltpu.PrefetchScalarGridSpec(
            num_scalar_prefetch=2,
            grid=(nt,),
            in_specs=[pl.BlockSpec((TME, PACKED_W), lambda n, te, tv: (jnp.where(tv[n] > 0, n, 0), 0)),
                      weight((D_MODEL, D_EXPERT)), weight((D_MODEL, D_EXPERT)), weight((D_EXPERT, D_MODEL))],
            out_specs=pl.BlockSpec((TME, PACKED_W), lambda n, te, tv: (n, 0)),
            scratch_shapes=[pltpu.VMEM((D_MODEL, D_EXPERT), BF16), pltpu.VMEM((D_MODEL, D_EXPERT), BF16),
                            pltpu.VMEM((D_EXPERT, D_MODEL), BF16)],
        ),
        out_shape=jax.ShapeDtypeStruct((nt * TME, PACKED_W), F32),
        compiler_params=_params(("arbitrary",)),
        name="experts",
    )(texp, tvalid, xs, wg, wu, wd)


def _rowgather(table, idx):
    n_rows, width = idx.shape[0], table.shape[1]
    n_workers = SC_CORES * SC_SUBCORES
    per_worker = n_rows // n_workers
    assert per_worker * n_workers == n_rows and per_worker % SC_ROWS == 0
    mesh = plsc.VectorSubcoreMesh(core_axis_name="c", subcore_axis_name="s", num_cores=SC_CORES,
                                  num_subcores=SC_SUBCORES)

    chunks = per_worker // SC_ROWS
    assert chunks % 2 == 0
    buf = lambda: [pltpu.VMEM((SC_ROWS,), jnp.int32), pltpu.VMEM((SC_ROWS, width), table.dtype),
                   pltpu.SemaphoreType.DMA]

    @functools.partial(pl.kernel, mesh=mesh, out_type=jax.ShapeDtypeStruct((n_rows, width), table.dtype),
                       scratch_types=buf() + buf(), name="rowgather")
    def gather(table_hbm, idx_hbm, out_hbm, idx_a, rows_a, sem_a, idx_b, rows_b, sem_b):
        worker = lax.axis_index("s") * SC_CORES + lax.axis_index("c")
        base = worker * per_worker

        def fetch(c, idx_v, rows_v, sem):
            pltpu.sync_copy(idx_hbm.at[pl.ds(base + c * SC_ROWS, SC_ROWS)], idx_v)
            pltpu.async_copy(table_hbm.at[idx_v], rows_v, sem)

        def flush(c, idx_v, rows_v, sem):
            pltpu.make_async_copy(table_hbm.at[idx_v], rows_v, sem).wait()
            pltpu.sync_copy(rows_v, out_hbm.at[pl.ds(base + c * SC_ROWS, SC_ROWS)])

        fetch(0, idx_a, rows_a, sem_a)

        @pl.loop(0, chunks, step=2)
        def _(c):
            fetch(c + 1, idx_b, rows_b, sem_b)
            flush(c, idx_a, rows_a, sem_a)

            @pl.when(c + 2 < chunks)
            def _():
                fetch(c + 2, idx_a, rows_a, sem_a)

            flush(c + 1, idx_b, rows_b, sem_b)

    return gather(table, idx)


def _combine_body(g1_ref, g2_ref, x2_ref, wcol_ref, gf_ref, *rest):
    out_ref = rest[-1]
    w = wcol_ref[...]
    y = w[:, 2:3] * _unpack_halves(g1_ref[...]) + w[:, 3:4] * _unpack_halves(g2_ref[...])
    out_ref[...] = _rms(x2_ref[...] + y, gf_ref[...])


def _combine(g, x2, wcol, gf, part, out_prev):
    t = x2.shape[0]
    steps = t // MOE_PARTS // TF
    mine = lambda w: pl.BlockSpec((TF, w), lambda i: (i + part * steps, 0))
    in_specs = [pl.BlockSpec((TF, PACKED_W), lambda i: (i, 0)),
                pl.BlockSpec((TF, PACKED_W), lambda i: (i + steps, 0)),
                mine(D_MODEL), mine(LANES), _resident((1, D_MODEL))]
    operands = [g, g, x2, wcol, gf]
    aliases = {}
    if out_prev is not None:
        in_specs.append(pl.BlockSpec(memory_space=pl.ANY))
        operands.append(out_prev)
        aliases = {len(operands) - 1: 0}
    return pl.pallas_call(
        _combine_body,
        grid=(steps,),
        in_specs=in_specs,
        out_specs=mine(D_MODEL),
        out_shape=jax.ShapeDtypeStruct((t, D_MODEL), F32),
        input_output_aliases=aliases,
        compiler_params=_params(("parallel",)),
        name="combine",
    )(*operands)


def _layer(x, mem, g_mix, w_in, b_ml_i, b_ml_f, b_fx_f, b_gate_ml, b_gate_fx, g_ml_head, w_proj_ml, w_proj_fx,
           w_out, g_xq, g_xmem, w_xq, w_xkv, w_xo, g_moe, w_rg, b_rg, w_re, b_re, w_gate, w_up, w_down):
    nb, seq, d = x.shape
    t = nb * seq
    row = lambda v: v.reshape(1, -1).astype(F32)

    o = 0
    parts = {}
    for name, width in (("ml_q", 512), ("ml_k", 512), ("ml_v", 1024), ("ml_o", 1024), ("ml_i", 4), ("ml_f", 4),
                        ("fx_q", 1024), ("fx_k", 1024), ("fx_v", 1024), ("fx_f", 8), ("gt_ml", 1024),
                        ("gt_fx", 1024)):
        parts[name] = w_in[:, o:o + width]
        o += width
    w_main = jnp.concatenate(
        [parts["ml_q"], parts["ml_k"] * (ML_QK_DIM ** -0.5), parts["ml_o"],
         parts["fx_q"] * (FX_HEAD_DIM ** -0.5 * LOG2E), parts["fx_k"], parts["gt_ml"], parts["gt_fx"]],
        axis=1).astype(BF16)
    w_vt = jnp.concatenate([parts["ml_v"], parts["fx_v"]], axis=1).T.astype(BF16)
    w_gates_t = jnp.concatenate([parts["ml_i"], parts["ml_f"], parts["fx_f"]], axis=1).T
    gate_bias = jnp.concatenate([b_ml_i, b_ml_f, b_fx_f]).reshape(N_GATES, 1).astype(F32)

    x2d = x.reshape(t, d)
    z, vt, gt3 = _inproj(x2d, row(g_mix), w_main, w_vt, w_gates_t, seq)
    rows, cols, caug = _gateprep(gt3, gate_bias)
    z3 = z.reshape(nb, seq, Z_W)
    vt4 = vt.reshape(nb, seq // TK, VT_W, TK)
    y_ml = _mlstm(z3, vt4, cols, rows, row(g_ml_head))
    y_fx = _fox(z3, vt4, caug, rows)
    x1 = _merge(x2d, y_ml.reshape(t, d), y_fx.reshape(t, d), z, row(b_gate_ml), row(b_gate_fx),
                w_proj_ml.astype(BF16), w_proj_fx.astype(BF16), w_out.astype(BF16))

    n_mem = mem.shape[1]
    kv = _memkv(mem.reshape(nb * n_mem, d), row(g_xmem), w_xkv.astype(BF16))
    w_router_t = jnp.concatenate([w_re, w_rg, jnp.zeros((d, ROUTER_ROWS - N_EXPERTS - N_GROUPS), F32)], axis=1).T
    b_router_t = jnp.concatenate([b_re, b_rg, jnp.zeros((ROUTER_ROWS - N_EXPERTS - N_GROUPS,), F32)]).reshape(ROUTER_ROWS, 1)
    x2, h3, route, wcol = _xattn(x1, kv.reshape(nb, n_mem, 2 * d), row(g_xq), w_xq.astype(BF16),
                                 w_xo.astype(BF16), row(g_moe), w_router_t, b_router_t, seq)
    return x2, h3, route, wcol


def _moe(x2, h3, route, wcol, w_gate, w_up, w_down, g_final):
    t = x2.shape[0]
    nt = _n_expert_tiles(t // MOE_PARTS)
    out = None
    for part in range(MOE_PARTS):
        pos, tinfo, tok = _plan(route, part)
        texp, tvalid = tinfo[0, :nt], tinfo[1, :nt]
        xs = _rowgather(h3, tok[:nt].reshape(-1))
        ys = _experts(texp, tvalid, xs, w_gate, w_up, w_down)
        g = _rowgather(ys, pos[0:2].reshape(-1))
        out = _combine(g, x2, wcol, g_final, part, out)
    return out


def kernel(x, mem, g_mix, w_in, b_ml_i, b_ml_f, b_fx_f, b_gate_ml, b_gate_fx, g_ml_head, w_proj_ml, w_proj_fx, w_out, g_xq, g_xmem, w_xq, w_xkv, w_xo, g_moe, w_rg, b_rg, w_re, b_re, w_gate, w_up, w_down, g_final):
    nb, seq, d = x.shape
    depth = g_mix.shape[0]
    assert depth == 1, "the final rmsnorm is fused into the (single) layer's combine kernel"
    x2, h3, route, wcol = _layer(
        x, mem, g_mix[0], w_in[0], b_ml_i[0], b_ml_f[0], b_fx_f[0], b_gate_ml[0], b_gate_fx[0], g_ml_head[0],
        w_proj_ml[0], w_proj_fx[0], w_out[0], g_xq[0], g_xmem[0], w_xq[0], w_xkv[0], w_xo[0], g_moe[0],
        w_rg[0], b_rg[0], w_re[0], b_re[0], w_gate[0], w_up[0], w_down[0])
    out = _moe(x2, h3, route, wcol, w_gate[0], w_up[0], w_down[0], g_final.reshape(1, d).astype(F32))
    return out.reshape(nb, seq, d)
```

```python
import functools

import jax
import jax.numpy as jnp
from jax import lax
from jax.experimental import pallas as pl
from jax.experimental.pallas import tpu as pltpu
from jax.experimental.pallas import tpu_sc as plsc

F32 = jnp.float32
BF16 = jnp.bfloat16

D_MODEL = 1024
EPS = 1e-6
ML_HEADS = 4
ML_QK_DIM = 128
ML_V_DIM = 256
FX_HEADS = 8
FX_HEAD_DIM = 128
XA_HEADS = 4
XA_HEAD_DIM = 256
N_GROUPS = 4
EXPERTS_PER_GROUP = 8
N_EXPERTS = 32
D_EXPERT = 512

LANES = 128
N_GATES = 16
ROUTER_ROWS = 40
LOG2E = 1.4426950408889634
Z_W = 6144
Z_ML_O, Z_FX_Q, Z_FX_K, Z_GT_ML, Z_GT_FX = 1, 2, 3, 4, 5
VT_W = 2048

VMEM_LIMIT = 56 * 1024 * 1024

IN_CHUNK = 1024
TK = 256
TQ = 512
TIN = 512
KT_PER_Q = TQ // TK
MXU_LOOKAHEAD = 16
TM = 1024
TME = 512
TP = 512
TF = 512
MOE_PARTS = 1
SC_CORES, SC_SUBCORES = 2, 16
PACKED_W = D_MODEL // 2
SC_ROWS = 64


def _params(sem, flags=None):
    return pltpu.CompilerParams(dimension_semantics=sem, vmem_limit_bytes=VMEM_LIMIT, flags=flags)


def _rms(x, g):
    return x * lax.rsqrt(jnp.mean(x * x, axis=-1, keepdims=True) + EPS) * g


def _dot_nt(a, b, **kw):
    return lax.dot_general(a, b, (((1,), (1,)), ((), ())), preferred_element_type=F32, **kw)


def _split_bf16(x):
    hi = x.astype(BF16)
    return hi, (x - hi.astype(F32)).astype(BF16)


def _dot_nt_split(a, b):
    a_hi, a_lo = _split_bf16(a)
    b_hi, b_lo = _split_bf16(b)
    return _dot_nt(a_hi, b_hi) + (_dot_nt(a_hi, b_lo) + _dot_nt(a_lo, b_hi))


def _pack_halves(x):
    n = x.shape[1] // 2
    bits = lambda v: lax.bitcast_convert_type(v.astype(BF16).astype(F32), jnp.uint32)
    w = (bits(x[:, :n]) >> 16) | (bits(x[:, n:]) & jnp.uint32(0xFFFF0000))
    return lax.bitcast_convert_type(w, F32)


def _unpack_halves(p):
    w = lax.bitcast_convert_type(p, jnp.uint32)
    lo = lax.bitcast_convert_type(w << 16, F32)
    hi = lax.bitcast_convert_type(w & jnp.uint32(0xFFFF0000), F32)
    return jnp.concatenate([lo, hi], axis=1)


def _resident(shape):
    zeros = (0,) * len(shape)
    return pl.BlockSpec(shape, lambda *_: zeros, pipeline_mode=pl.Buffered(1))


def _inproj_body(x_ref, g_ref, w_ref, wvt_ref, wgt_ref, z_ref, vt_ref, gt_ref):
    h = _rms(x_ref[...], g_ref[...])
    hb = h.astype(BF16)
    for c in range(Z_W // IN_CHUNK):
        sl = slice(c * IN_CHUNK, (c + 1) * IN_CHUNK)
        z_ref[:, sl] = jnp.dot(hb, w_ref[:, sl], preferred_element_type=F32).astype(BF16)
    for kt in range(TIN // TK):
        hk = hb[kt * TK:(kt + 1) * TK]
        for c in range(VT_W // IN_CHUNK):
            sl = slice(c * IN_CHUNK, (c + 1) * IN_CHUNK)
            vt_ref[kt, sl, :] = _dot_nt(wvt_ref[sl, :], hk).astype(BF16)
    gt_ref[...] = _dot_nt_split(wgt_ref[...], h)


def _inproj(x2d, g, w, wvt, wgt, seq):
    t = x2d.shape[0]
    per_b = seq // TIN
    return pl.pallas_call(
        _inproj_body,
        grid=(t // TIN,),
        in_specs=[
            pl.BlockSpec((TIN, D_MODEL), lambda i: (i, 0)),
            _resident((1, D_MODEL)),
            _resident((D_MODEL, Z_W)),
            _resident((VT_W, D_MODEL)),
            _resident((N_GATES, D_MODEL)),
        ],
        out_specs=[
            pl.BlockSpec((TIN, Z_W), lambda i: (i, 0)),
            pl.BlockSpec((TIN // TK, VT_W, TK), lambda i: (i, 0, 0)),
            pl.BlockSpec((None, N_GATES, TIN), lambda i: (i // per_b, 0, i % per_b)),
        ],
        out_shape=[jax.ShapeDtypeStruct((t, Z_W), BF16), jax.ShapeDtypeStruct((t // TK, VT_W, TK), BF16),
                   jax.ShapeDtypeStruct((t // seq, N_GATES, seq), F32)],
        compiler_params=_params(("parallel",)),
        name="inproj",
    )(x2d, g, w, wvt, wgt)


def _scan_lanes(x, op, identity):
    n = x.shape[-1]
    idx = lax.broadcasted_iota(jnp.int32, x.shape, 1)
    s = 1
    while s < n:
        shifted = pltpu.roll(x, s, axis=1)
        x = op(x, jnp.where(idx >= s, shifted, identity))
        s *= 2
    return x


def _log_sigmoid(x):
    return jnp.minimum(x, 0.0) - jnp.log1p(jnp.exp(-jnp.abs(x)))


def _gateprep_body(gt_ref, bias_ref, rows_ref, cols_ref, caug_ref):
    g = gt_ref[...] + bias_ref[...]
    s = g.shape[1]
    cs = _scan_lanes(_log_sigmoid(g), jnp.add, 0.0)
    b = cs[4:8]
    c2 = cs[8:16] * LOG2E
    a = g[0:4] - b
    m = _scan_lanes(jnp.concatenate([a, a], axis=0), jnp.maximum, -jnp.inf)[0:4]
    rows_ref[...] = jnp.concatenate([m * LOG2E, b + m, c2], axis=0)
    cols_ref[...] = jnp.concatenate([a * LOG2E, jnp.zeros((LANES - ML_HEADS, s), F32)], axis=0).T
    hi = c2.astype(BF16).astype(F32)
    r1 = c2 - hi
    mid = r1.astype(BF16).astype(F32)
    lo = r1 - mid
    aug = jnp.concatenate([-hi, -mid, -lo, jnp.zeros((LANES - 3 * FX_HEADS, s), F32)], axis=0)
    caug_ref[...] = aug.T.astype(BF16)


def _gateprep(gt3, bias):
    nb, _, s = gt3.shape
    return pl.pallas_call(
        _gateprep_body,
        grid=(nb,),
        in_specs=[
            pl.BlockSpec((None, N_GATES, s), lambda b: (b, 0, 0)),
            _resident((N_GATES, 1)),
        ],
        out_specs=[
            pl.BlockSpec((None, N_GATES, s), lambda b: (b, 0, 0)),
            pl.BlockSpec((None, s, LANES), lambda b: (b, 0, 0)),
            pl.BlockSpec((None, s, LANES), lambda b: (b, 0, 0)),
        ],
        out_shape=[jax.ShapeDtypeStruct((nb, N_GATES, s), F32), jax.ShapeDtypeStruct((nb, s, LANES), F32),
                   jax.ShapeDtypeStruct((nb, s, LANES), BF16)],
        compiler_params=_params(("parallel",)),
        name="gateprep",
    )(gt3, bias)


def _pipelined(stages, scores, update):
    pending = {k: scores(*stages[k]) for k in range(min(MXU_LOOKAHEAD, len(stages)))}
    for k, stage in enumerate(stages):
        if k + MXU_LOOKAHEAD < len(stages):
            pending[k + MXU_LOOKAHEAD] = scores(*stages[k + MXU_LOOKAHEAD])
        update(*stage, pending.pop(k))


def _causal_mask_t(d):
    s = lax.broadcasted_iota(jnp.int32, (TK, TQ - d * TK), 0)
    t = lax.broadcasted_iota(jnp.int32, (TK, TQ - d * TK), 1)
    return s <= t


def _mlstm_body(q_ref, k_ref, vt_ref, o_ref, cols_ref, rows_ref, gh_ref, y_ref, num_scr, den_scr):
    i = pl.program_id(1)
    rows = rows_ref[...]
    num_scr[...] = jnp.zeros_like(num_scr)
    den_scr[...] = jnp.zeros_like(den_scr)

    def scores(j, h, mask, q0):
        ks = pl.ds(pl.multiple_of(j * TK, TK), TK)
        qk = slice(h * ML_QK_DIM, (h + 1) * ML_QK_DIM)
        return _dot_nt(k_ref[ks, qk], q_ref[q0:, qk])

    def update(j, h, mask, q0, s):
        ks = pl.ds(pl.multiple_of(j * TK, TK), TK)
        vv = slice(h * ML_V_DIM, (h + 1) * ML_V_DIM)
        w = jnp.exp2(cols_ref[ks, h:h + 1] - rows[h:h + 1, q0:])
        if mask is not None:
            w = jnp.where(mask, w, 0.0)
        s = s * w
        den_scr[h, :, q0:] += jnp.sum(s, axis=0, keepdims=True)
        num_scr[h, :, q0:] += jnp.dot(vt_ref[j, vv, :], s.astype(BF16), preferred_element_type=F32)

    def body(jj, carry):
        _pipelined([(jj * KT_PER_Q + d, h, None, 0) for d in range(KT_PER_Q) for h in range(ML_HEADS)], scores, update)
        return carry

    lax.fori_loop(0, i, body, 0)
    masks = [_causal_mask_t(d) for d in range(KT_PER_Q)]
    _pipelined([(i * KT_PER_Q + d, h, masks[d], d * TK) for d in range(KT_PER_Q) for h in range(ML_HEADS)],
               scores, update)
    for h in range(ML_HEADS):
        vv = slice(h * ML_V_DIM, (h + 1) * ML_V_DIM)
        floor = jnp.exp(-rows[ML_HEADS + h:ML_HEADS + h + 1])
        hh = num_scr[h] * (1.0 / jnp.maximum(jnp.abs(den_scr[h]), floor))
        yt = hh * lax.rsqrt(jnp.mean(hh * hh, axis=0, keepdims=True) + EPS)
        y = yt.T * gh_ref[:, vv]
        y_ref[:, vv] = (y * jax.nn.sigmoid(o_ref[:, vv].astype(F32))).astype(BF16)


def _mlstm(z3, vt4, cols3, rows3, g_head):
    nb, s, _ = z3.shape
    nq = s // TQ
    return pl.pallas_call(
        _mlstm_body,
        grid=(nb, nq),
        in_specs=[
            pl.BlockSpec((None, TQ, ML_HEADS * ML_QK_DIM), lambda b, i: (b, i, 0)),
            pl.BlockSpec((None, s, ML_HEADS * ML_QK_DIM), lambda b, i: (b, 0, 1)),
            pl.BlockSpec((None, s // TK, D_MODEL, TK), lambda b, i: (b, 0, 0, 0)),
            pl.BlockSpec((None, TQ, D_MODEL), lambda b, i: (b, i, Z_ML_O)),
            pl.BlockSpec((None, s, LANES), lambda b, i: (b, 0, 0)),
            pl.BlockSpec((None, N_GATES, TQ), lambda b, i: (b, 0, i)),
            _resident((1, D_MODEL)),
        ],
        out_specs=pl.BlockSpec((None, TQ, D_MODEL), lambda b, i: (b, i, 0)),
        out_shape=jax.ShapeDtypeStruct((nb, s, D_MODEL), BF16),
        scratch_shapes=[pltpu.VMEM((ML_HEADS, ML_V_DIM, TQ), F32), pltpu.VMEM((ML_HEADS, 1, TQ), F32)],
        compiler_params=_params(("parallel", "arbitrary")),
        name="mlstm",
    )(z3, z3, vt4, z3, cols3, rows3, g_head)


def _fox_body(q_ref, k_ref, vt_ref, caug_ref, rows_ref, y_ref, qa_scr, acc_scr, m_scr, l_scr):
    i = pl.program_id(1)
    rows = rows_ref[...]
    lane = lax.broadcasted_iota(jnp.int32, (TQ, LANES), 1)
    for h in range(FX_HEADS):
        hd = slice(h * FX_HEAD_DIM, (h + 1) * FX_HEAD_DIM)
        ones = jnp.where((lane < 3 * FX_HEADS) & (lane % FX_HEADS == h), 1.0, 0.0).astype(BF16)
        qa_scr[h] = jnp.concatenate([q_ref[:, hd], ones], axis=1)
    m_scr[...] = jnp.full_like(m_scr, -jnp.inf)
    l_scr[...] = jnp.zeros_like(l_scr)
    acc_scr[...] = jnp.zeros_like(acc_scr)

    def scores(j, h, mask, q0):
        ks = pl.ds(pl.multiple_of(j * TK, TK), TK)
        hd = slice(h * FX_HEAD_DIM, (h + 1) * FX_HEAD_DIM)
        k_aug = jnp.concatenate([k_ref[ks, hd], caug_ref[ks, :]], axis=1)
        return _dot_nt(k_aug, qa_scr[h, q0:, :])

    def update(j, h, mask, q0, u):
        hd = slice(h * FX_HEAD_DIM, (h + 1) * FX_HEAD_DIM)
        if mask is not None:
            u = jnp.where(mask, u, -jnp.inf)
        c_row = rows[2 * ML_HEADS + h:2 * ML_HEADS + h + 1, q0:]
        m_prev = m_scr[h, :, q0:]
        m_new = jnp.maximum(m_prev, jnp.max(u, axis=0, keepdims=True) + c_row)
        p = jnp.exp2(u - (m_new - c_row))
        alpha = jnp.exp2(m_prev - m_new)
        l_scr[h, :, q0:] = alpha * l_scr[h, :, q0:] + jnp.sum(p, axis=0, keepdims=True)
        acc_scr[h, :, q0:] = alpha * acc_scr[h, :, q0:] + jnp.dot(vt_ref[j, hd, :], p.astype(BF16),
                                                                  preferred_element_type=F32)
        m_scr[h, :, q0:] = m_new

    def body(jj, carry):
        _pipelined([(jj * KT_PER_Q + d, h, None, 0) for d in range(KT_PER_Q) for h in range(FX_HEADS)], scores, update)
        return carry

    lax.fori_loop(0, i, body, 0)
    masks = [_causal_mask_t(d) for d in range(KT_PER_Q)]
    _pipelined([(i * KT_PER_Q + d, h, masks[d], d * TK) for d in range(KT_PER_Q) for h in range(FX_HEADS)],
               scores, update)
    for h in range(FX_HEADS):
        hd = slice(h * FX_HEAD_DIM, (h + 1) * FX_HEAD_DIM)
        y_ref[:, hd] = (acc_scr[h] * (1.0 / l_scr[h])).T.astype(BF16)


def _fox(z3, vt4, caug3, rows3):
    nb, s, _ = z3.shape
    nq = s // TQ
    return pl.pallas_call(
        _fox_body,
        grid=(nb, nq),
        in_specs=[
            pl.BlockSpec((None, TQ, D_MODEL), lambda b, i: (b, i, Z_FX_Q)),
            pl.BlockSpec((None, s, D_MODEL), lambda b, i: (b, 0, Z_FX_K)),
            pl.BlockSpec((None, s // TK, D_MODEL, TK), lambda b, i: (b, 0, 1, 0)),
            pl.BlockSpec((None, s, LANES), lambda b, i: (b, 0, 0)),
            pl.BlockSpec((None, N_GATES, TQ), lambda b, i: (b, 0, i)),
        ],
        out_specs=pl.BlockSpec((None, TQ, D_MODEL), lambda b, i: (b, i, 0)),
        out_shape=jax.ShapeDtypeStruct((nb, s, D_MODEL), BF16),
        scratch_shapes=[pltpu.VMEM((FX_HEADS, TQ, 2 * FX_HEAD_DIM), BF16),
                        pltpu.VMEM((FX_HEADS, FX_HEAD_DIM, TQ), F32),
                        pltpu.VMEM((FX_HEADS, 1, TQ), F32), pltpu.VMEM((FX_HEADS, 1, TQ), F32)],
        compiler_params=_params(("parallel", "arbitrary")),
        name="fox",
    )(z3, z3, vt4, caug3, rows3)


def _merge_body(x_ref, yml_ref, yfx_ref, gml_ref, gfx_ref, bml_ref, bfx_ref, wml_ref, wfx_ref, wout_ref, x1_ref):
    p_ml = jnp.dot(yml_ref[...], wml_ref[...], preferred_element_type=F32)
    p_fx = jnp.dot(yfx_ref[...], wfx_ref[...], preferred_element_type=F32)
    merged = (jax.nn.sigmoid(gml_ref[...].astype(F32) + bml_ref[...]) * p_ml
              + jax.nn.sigmoid(gfx_ref[...].astype(F32) + bfx_ref[...]) * p_fx)
    x1_ref[...] = x_ref[...] + jnp.dot(merged.astype(BF16), wout_ref[...], preferred_element_type=F32)


def _merge(x2d, yml, yfx, z2d, bml, bfx, wml, wfx, wout):
    t = x2d.shape[0]
    tile = lambda col: pl.BlockSpec((TM, D_MODEL), lambda i, col=col: (i, col))
    return pl.pallas_call(
        _merge_body,
        grid=(t // TM,),
        in_specs=[tile(0), tile(0), tile(0), tile(Z_GT_ML), tile(Z_GT_FX),
                  _resident((1, D_MODEL)), _resident((1, D_MODEL)),
                  _resident((D_MODEL, D_MODEL)), _resident((D_MODEL, D_MODEL)), _resident((D_MODEL, D_MODEL))],
        out_specs=tile(0),
        out_shape=jax.ShapeDtypeStruct((t, D_MODEL), F32),
        compiler_params=_params(("parallel",)),
        name="merge",
    )(x2d, yml, yfx, z2d, z2d, bml, bfx, wml, wfx, wout)


def _memkv_body(m_ref, g_ref, w_ref, kv_ref):
    hb = _rms(m_ref[...], g_ref[...]).astype(BF16)
    kv_ref[...] = jnp.dot(hb, w_ref[...], preferred_element_type=F32).astype(BF16)


def _memkv(mem2d, g, w):
    t = mem2d.shape[0]
    return pl.pallas_call(
        _memkv_body,
        grid=(t // TM,),
        in_specs=[pl.BlockSpec((TM, D_MODEL), lambda i: (i, 0)), _resident((1, D_MODEL)),
                  _resident((D_MODEL, 2 * D_MODEL))],
        out_specs=pl.BlockSpec((TM, 2 * D_MODEL), lambda i: (i, 0)),
        out_shape=jax.ShapeDtypeStruct((t, 2 * D_MODEL), BF16),
        compiler_params=_params(("parallel",)),
        name="memkv",
    )(mem2d, g, w)


def _route_t(lg_t):
    tm = lg_t.shape[1]
    ninf = -jnp.inf
    big = jnp.float32(LANES)
    gid = lax.broadcasted_iota(jnp.int32, (8, tm), 0).astype(F32)
    eid = lax.broadcasted_iota(jnp.int32, (N_EXPERTS, tm), 0).astype(F32)
    gl = jnp.where(gid < N_GROUPS, lg_t[N_EXPERTS:N_EXPERTS + 8], ninf)
    gmax = jnp.max(gl, axis=0, keepdims=True)
    gidx = jnp.min(jnp.where(gl == gmax, gid, big), axis=0, keepdims=True)
    g_p = 1.0 / jnp.sum(jnp.exp(gl - gmax), axis=0, keepdims=True)
    lo = gidx * EXPERTS_PER_GROUP
    el = jnp.where(eid >= lo, jnp.where(eid < lo + EXPERTS_PER_GROUP, lg_t[0:N_EXPERTS], ninf), ninf)
    v1 = jnp.max(el, axis=0, keepdims=True)
    i1 = jnp.min(jnp.where(el == v1, eid, big), axis=0, keepdims=True)
    el2 = jnp.where(eid == i1, ninf, el)
    v2 = jnp.max(el2, axis=0, keepdims=True)
    i2 = jnp.min(jnp.where(el2 == v2, eid, big), axis=0, keepdims=True)
    t = jnp.exp(v2 - v1)
    w1 = g_p / (1.0 + t)
    w2 = w1 * t
    return jnp.concatenate([i1, i2, w1, w2, jnp.zeros((4, tm), F32)], axis=0)


def _xattn_body(x1_ref, kv_ref, gq_ref, wq_ref, wo_ref, gm_ref, wrt_ref, brt_ref, x2_ref, h3_ref, route_ref,
                wcol_ref):
    x1 = x1_ref[...]
    hb = _rms(x1, gq_ref[...]).astype(BF16)
    q = (jnp.dot(hb, wq_ref[...], preferred_element_type=F32) * (XA_HEAD_DIM ** -0.5)).astype(BF16)
    outs = []
    for h in range(XA_HEADS):
        hd = slice(h * XA_HEAD_DIM, (h + 1) * XA_HEAD_DIM)
        vd = slice(D_MODEL + h * XA_HEAD_DIM, D_MODEL + (h + 1) * XA_HEAD_DIM)
        s = _dot_nt(q[:, hd], kv_ref[:, hd])
        p = jnp.exp(s - jnp.max(s, axis=1, keepdims=True))
        p = p * (1.0 / jnp.sum(p, axis=1, keepdims=True))
        outs.append(jnp.dot(p.astype(BF16), kv_ref[:, vd], preferred_element_type=F32).astype(BF16))
    o = jnp.concatenate(outs, axis=1)
    x2 = x1 + jnp.dot(o, wo_ref[...], preferred_element_type=F32)
    x2_ref[...] = x2
    h3 = _rms(x2, gm_ref[...])
    h3_ref[...] = _pack_halves(h3)
    lg_t = _dot_nt_split(wrt_ref[...], h3) + brt_ref[...]
    route = _route_t(lg_t)
    route_ref[...] = route
    wcol_ref[...] = jnp.concatenate([route, jnp.zeros((LANES - 8, route.shape[1]), F32)], axis=0).T


def _xattn(x1, kv3, gq, wq, wo, gm, wrt, brt, seq):
    t = x1.shape[0]
    per_b = seq // TM
    n_mem = kv3.shape[1]
    tile = pl.BlockSpec((TM, D_MODEL), lambda i: (i, 0))
    return pl.pallas_call(
        _xattn_body,
        grid=(t // TM,),
        in_specs=[tile,
                  pl.BlockSpec((None, n_mem, 2 * D_MODEL), lambda i: (i // per_b, 0, 0)),
                  _resident((1, D_MODEL)), _resident((D_MODEL, D_MODEL)), _resident((D_MODEL, D_MODEL)),
                  _resident((1, D_MODEL)), _resident((ROUTER_ROWS, D_MODEL)), _resident((ROUTER_ROWS, 1))],
        out_specs=[tile, pl.BlockSpec((TM, PACKED_W), lambda i: (i, 0)), pl.BlockSpec((8, TM), lambda i: (0, i)),
                   pl.BlockSpec((TM, LANES), lambda i: (i, 0))],
        out_shape=[jax.ShapeDtypeStruct((t, D_MODEL), F32), jax.ShapeDtypeStruct((t, PACKED_W), F32),
                   jax.ShapeDtypeStruct((8, t), F32), jax.ShapeDtypeStruct((t, LANES), F32)],
        compiler_params=_params(("parallel",)),
        name="xattn",
    )(x1, kv3, gq, wq, wo, gm, wrt, brt)


def _n_expert_tiles(n_tokens):
    return 2 * n_tokens // TME + N_EXPERTS


def _plan_body(route_ref, pos_ref, tinfo_ref, tok_ref, cnt_scr, run_scr, start_scr, tokhi_scr, toklo_scr, *, nt_pad,
               nt_rows, n_tokens, first_token):
    phase = pl.program_id(0)
    b = pl.program_id(1)
    r = route_ref[...]
    eid = lax.broadcasted_iota(jnp.int32, (N_EXPERTS, TP), 0).astype(F32)
    oh1 = eid == r[0:1]
    oh2 = eid == r[1:2]
    oh = jnp.where(oh1 | oh2, 1.0, 0.0)

    @pl.when((phase == 0) & (b == 0))
    def _():
        cnt_scr[...] = jnp.zeros_like(cnt_scr)

    @pl.when(phase == 0)
    def _():
        cnt_scr[...] += jnp.sum(oh, axis=1, keepdims=True)

    @pl.when((phase == 1) & (b == 0))
    def _():
        cnt = cnt_scr[...]
        n_tiles = jnp.floor((cnt + (TME - 1)) * (1.0 / TME))
        ri = lax.broadcasted_iota(jnp.int32, (N_EXPERTS, N_EXPERTS), 0)
        ci = lax.broadcasted_iota(jnp.int32, (N_EXPERTS, N_EXPERTS), 1)
        lower = jnp.where(ci < ri, 1.0, 0.0).astype(BF16)
        nt_hi, nt_lo = _split_bf16(jnp.broadcast_to(n_tiles, (N_EXPERTS, LANES)))
        start = (jnp.dot(lower, nt_hi, preferred_element_type=F32)
                 + jnp.dot(lower, nt_lo, preferred_element_type=F32))[:, 0:1]
        start_scr[...] = start * TME
        run_scr[...] = jnp.zeros_like(run_scr)
        tokhi_scr[...] = jnp.zeros_like(tokhi_scr)
        toklo_scr[...] = jnp.zeros_like(toklo_scr)
        n = lax.broadcasted_iota(jnp.int32, (N_EXPERTS, nt_pad), 1).astype(F32)
        e_n = lax.broadcasted_iota(jnp.int32, (N_EXPERTS, nt_pad), 0).astype(F32)
        owner = jnp.sum(jnp.where(start <= n, 1.0, 0.0), axis=0, keepdims=True) - 1.0
        own = e_n == owner
        cnt_o = jnp.sum(jnp.where(own, cnt, 0.0), axis=0, keepdims=True)
        start_o = jnp.sum(jnp.where(own, start, 0.0), axis=0, keepdims=True)
        valid = jnp.clip(cnt_o - (n[0:1] - start_o) * TME, 0.0, float(TME))
        tinfo_ref[...] = jnp.concatenate([owner, valid, jnp.zeros((6, nt_pad), F32)], axis=0).astype(jnp.int32)

    @pl.when(phase == 1)
    def _():
        ti = lax.broadcasted_iota(jnp.int32, (TP, TP), 0)
        tj = lax.broadcasted_iota(jnp.int32, (TP, TP), 1)
        upper = jnp.where(ti < tj, 1.0, 0.0).astype(BF16)
        before = jnp.dot(oh.astype(BF16), upper, preferred_element_type=F32)
        row = start_scr[...] + run_scr[...] + before
        p1 = jnp.sum(jnp.where(oh1, row, 0.0), axis=0, keepdims=True)
        p2 = jnp.sum(jnp.where(oh2, row, 0.0), axis=0, keepdims=True)
        pos_ref[...] = jnp.concatenate([p1, p2, jnp.zeros((6, TP), F32)], axis=0).astype(jnp.int32)
        run_scr[...] += jnp.sum(oh, axis=1, keepdims=True)

        tid = lax.broadcasted_iota(jnp.int32, (1, TP), 1) + (b * TP + first_token + 1)
        t_hi = (tid // 256).astype(F32)
        t_lo = (tid % 256).astype(F32)
        tile_id = lax.broadcasted_iota(jnp.int32, (nt_rows, TP), 0).astype(F32)
        row_id = lax.broadcasted_iota(jnp.int32, (TME, TP), 0).astype(F32)
        for p in (p1, p2):
            hi = jnp.floor(p * (1.0 / TME))
            lo = p - hi * TME
            in_tile = jnp.where(tile_id == hi, 1.0, 0.0).astype(BF16)
            at_row = row_id == lo
            tokhi_scr[...] += _dot_nt(in_tile, jnp.where(at_row, t_hi, 0.0).astype(BF16))
            toklo_scr[...] += _dot_nt(in_tile, jnp.where(at_row, t_lo, 0.0).astype(BF16))

    @pl.when((phase == 1) & (b == pl.num_programs(1) - 1))
    def _():
        hit = (tokhi_scr[...] * 256.0 + toklo_scr[...]).astype(jnp.int32)
        row = (lax.broadcasted_iota(jnp.int32, hit.shape, 0) * TME + lax.broadcasted_iota(jnp.int32, hit.shape, 1))
        tok_ref[...] = jnp.where(hit > 0, hit - 1, row % n_tokens)


def _plan(route, part):
    t_all = route.shape[1]
    assert t_all < 256 * 256, "token id + 1 is carried as two base-256 digits"
    t = t_all // MOE_PARTS
    steps = t // TP
    nt_pad = -(-_n_expert_tiles(t) // LANES) * LANES
    nt_rows = -(-_n_expert_tiles(t) // 8) * 8
    col = pltpu.VMEM((N_EXPERTS, 1), F32)
    table = pltpu.VMEM((nt_rows, TME), F32)
    return pl.pallas_call(
        functools.partial(_plan_body, nt_pad=nt_pad, nt_rows=nt_rows, n_tokens=t_all, first_token=part * t),
        grid=(2, steps),
        in_specs=[pl.BlockSpec((8, TP), lambda ph, b: (0, b + part * steps))],
        out_specs=[pl.BlockSpec((8, TP), lambda ph, b: (0, b * ph)),
                   pl.BlockSpec((8, nt_pad), lambda ph, b: (0, 0)),
                   pl.BlockSpec((nt_rows, TME), lambda ph, b: (0, 0))],
        out_shape=[jax.ShapeDtypeStruct((8, t), jnp.int32), jax.ShapeDtypeStruct((8, nt_pad), jnp.int32),
                   jax.ShapeDtypeStruct((nt_rows, TME), jnp.int32)],
        scratch_shapes=[col, col, col, table, table],
        compiler_params=_params(("arbitrary", "arbitrary")),
        name="plan",
    )(route)


def _experts_body(texp_ref, tvalid_ref, xs_ref, wg_ref, wu_ref, wd_ref, ys_ref, wgb_scr, wub_scr, wdb_scr):
    n = pl.program_id(0)

    @pl.when((n == 0) | (texp_ref[n] != texp_ref[jnp.maximum(n - 1, 0)]))
    def _():
        wgb_scr[...] = wg_ref[...].astype(BF16)
        wub_scr[...] = wu_ref[...].astype(BF16)
        wdb_scr[...] = wd_ref[...].astype(BF16)

    @pl.when(tvalid_ref[n] > 0)
    def _():
        xb = _unpack_halves(xs_ref[...]).astype(BF16)
        gate = jnp.dot(xb, wgb_scr[...], preferred_element_type=F32)
        up = jnp.dot(xb, wub_scr[...], preferred_element_type=F32)
        he = (gate * jax.nn.sigmoid(gate) * up).astype(BF16)
        ys_ref[...] = _pack_halves(jnp.dot(he, wdb_scr[...], preferred_element_type=F32))

    @pl.when(tvalid_ref[n] == 0)
    def _():
        ys_ref[...] = jnp.zeros_like(ys_ref)


def _experts(texp, tvalid, xs, wg, wu, wd):
    nt = texp.shape[0]
    weight = lambda shape: pl.BlockSpec((None,) + shape, lambda n, te, tv: (te[n], 0, 0))
    return pl.pallas_call(
        _experts_body,
        grid_spec=pltpu.PrefetchScalarGridSpec(
            num_scalar_prefetch=2,
            grid=(nt,),
            in_specs=[pl.BlockSpec((TME, PACKED_W), lambda n, te, tv: (jnp.where(tv[n] > 0, n, 0), 0)),
                      weight((D_MODEL, D_EXPERT)), weight((D_MODEL, D_EXPERT)), weight((D_EXPERT, D_MODEL))],
            out_specs=pl.BlockSpec((TME, PACKED_W), lambda n, te, tv: (n, 0)),
            scratch_shapes=[pltpu.VMEM((D_MODEL, D_EXPERT), BF16), pltpu.VMEM((D_MODEL, D_EXPERT), BF16),
                            pltpu.VMEM((D_EXPERT, D_MODEL), BF16)],
        ),
        out_shape=jax.ShapeDtypeStruct((nt * TME, PACKED_W), F32),
        compiler_params=_params(("arbitrary",)),
        name="experts",
    )(texp, tvalid, xs, wg, wu, wd)


def _rowgather(table, idx):
    n_rows, width = idx.shape[0], table.shape[1]
    n_workers = SC_CORES * SC_SUBCORES
    per_worker = n_rows // n_workers
    assert per_worker * n_workers == n_rows and per_worker % SC_ROWS == 0
    mesh = plsc.VectorSubcoreMesh(core_axis_name="c", subcore_axis_name="s", num_cores=SC_CORES,
                                  num_subcores=SC_SUBCORES)

    chunks = per_worker // SC_ROWS
    assert chunks % 2 == 0
    buf = lambda: [pltpu.VMEM((SC_ROWS,), jnp.int32), pltpu.VMEM((SC_ROWS, width), table.dtype),
                   pltpu.SemaphoreType.DMA]

    @functools.partial(pl.kernel, mesh=mesh, out_type=jax.ShapeDtypeStruct((n_rows, width), table.dtype),
                       scratch_types=buf() + buf(), name="rowgather")
    def gather(table_hbm, idx_hbm, out_hbm, idx_a, rows_a, sem_a, idx_b, rows_b, sem_b):
        worker = lax.axis_index("s") * SC_CORES + lax.axis_index("c")
        base = worker * per_worker

        def fetch(c, idx_v, rows_v, sem):
            pltpu.sync_copy(idx_hbm.at[pl.ds(base + c * SC_ROWS, SC_ROWS)], idx_v)
            pltpu.async_copy(table_hbm.at[idx_v], rows_v, sem)

        def flush(c, idx_v, rows_v, sem):
            pltpu.make_async_copy(table_hbm.at[idx_v], rows_v, sem).wait()
            pltpu.sync_copy(rows_v, out_hbm.at[pl.ds(base + c * SC_ROWS, SC_ROWS)])

        fetch(0, idx_a, rows_a, sem_a)

        @pl.loop(0, chunks, step=2)
        def _(c):
            fetch(c + 1, idx_b, rows_b, sem_b)
            flush(c, idx_a, rows_a, sem_a)

            @pl.when(c + 2 < chunks)
            def _():
                fetch(c + 2, idx_a, rows_a, sem_a)

            flush(c + 1, idx_b, rows_b, sem_b)

    return gather(table, idx)


def _combine_body(g1_ref, g2_ref, x2_ref, wcol_ref, gf_ref, *rest):
    out_ref = rest[-1]
    w = wcol_ref[...]
    y = w[:, 2:3] * _unpack_halves(g1_ref[...]) + w[:, 3:4] * _unpack_halves(g2_ref[...])
    out_ref[...] = _rms(x2_ref[...] + y, gf_ref[...])


def _combine(g, x2, wcol, gf, part, out_prev):
    t = x2.shape[0]
    steps = t // MOE_PARTS // TF
    mine = lambda w: pl.BlockSpec((TF, w), lambda i: (i + part * steps, 0))
    in_specs = [pl.BlockSpec((TF, PACKED_W), lambda i: (i, 0)),
                pl.BlockSpec((TF, PACKED_W), lambda i: (i + steps, 0)),
                mine(D_MODEL), mine(LANES), _resident((1, D_MODEL))]
    operands = [g, g, x2, wcol, gf]
    aliases = {}
    if out_prev is not None:
        in_specs.append(pl.BlockSpec(memory_space=pl.ANY))
        operands.append(out_prev)
        aliases = {len(operands) - 1: 0}
    return pl.pallas_call(
        _combine_body,
        grid=(steps,),
        in_specs=in_specs,
        out_specs=mine(D_MODEL),
        out_shape=jax.ShapeDtypeStruct((t, D_MODEL), F32),
        input_output_aliases=aliases,
        compiler_params=_params(("parallel",)),
        name="combine",
    )(*operands)


def _layer(x, mem, g_mix, w_in, b_ml_i, b_ml_f, b_fx_f, b_gate_ml, b_gate_fx, g_ml_head, w_proj_ml, w_proj_fx,
           w_out, g_xq, g_xmem, w_xq, w_xkv, w_xo, g_moe, w_rg, b_rg, w_re, b_re, w_gate, w_up, w_down):
    nb, seq, d = x.shape
    t = nb * seq
    row = lambda v: v.reshape(1, -1).astype(F32)

    o = 0
    parts = {}
    for name, width in (("ml_q", 512), ("ml_k", 512), ("ml_v", 1024), ("ml_o", 1024), ("ml_i", 4), ("ml_f", 4),
                        ("fx_q", 1024), ("fx_k", 1024), ("fx_v", 1024), ("fx_f", 8), ("gt_ml", 1024),
                        ("gt_fx", 1024)):
        parts[name] = w_in[:, o:o + width]
        o += width
    w_main = jnp.concatenate(
        [parts["ml_q"], parts["ml_k"] * (ML_QK_DIM ** -0.5), parts["ml_o"],
         parts["fx_q"] * (FX_HEAD_DIM ** -0.5 * LOG2E), parts["fx_k"], parts["gt_ml"], parts["gt_fx"]],
        axis=1).astype(BF16)
    w_vt = jnp.concatenate([parts["ml_v"], parts["fx_v"]], axis=1).T.astype(BF16)
    w_gates_t = jnp.concatenate([parts["ml_i"], parts["ml_f"], parts["fx_f"]], axis=1).T
    gate_bias = jnp.concatenate([b_ml_i, b_ml_f, b_fx_f]).reshape(N_GATES, 1).astype(F32)

    x2d = x.reshape(t, d)
    z, vt, gt3 = _inproj(x2d, row(g_mix), w_main, w_vt, w_gates_t, seq)
    rows, cols, caug = _gateprep(gt3, gate_bias)
    z3 = z.reshape(nb, seq, Z_W)
    vt4 = vt.reshape(nb, seq // TK, VT_W, TK)
    y_ml = _mlstm(z3, vt4, cols, rows, row(g_ml_head))
    y_fx = _fox(z3, vt4, caug, rows)
    x1 = _merge(x2d, y_ml.reshape(t, d), y_fx.reshape(t, d), z, row(b_gate_ml), row(b_gate_fx),
                w_proj_ml.astype(BF16), w_proj_fx.astype(BF16), w_out.astype(BF16))

    n_mem = mem.shape[1]
    kv = _memkv(mem.reshape(nb * n_mem, d), row(g_xmem), w_xkv.astype(BF16))
    w_router_t = jnp.concatenate([w_re, w_rg, jnp.zeros((d, ROUTER_ROWS - N_EXPERTS - N_GROUPS), F32)], axis=1).T
    b_router_t = jnp.concatenate([b_re, b_rg, jnp.zeros((ROUTER_ROWS - N_EXPERTS - N_GROUPS,), F32)]).reshape(ROUTER_ROWS, 1)
    x2, h3, route, wcol = _xattn(x1, kv.reshape(nb, n_mem, 2 * d), row(g_xq), w_xq.astype(BF16),
                                 w_xo.astype(BF16), row(g_moe), w_router_t, b_router_t, seq)
    return x2, h3, route, wcol


def _moe(x2, h3, route, wcol, w_gate, w_up, w_down, g_final):
    t = x2.shape[0]
    nt = _n_expert_tiles(t // MOE_PARTS)
    out = None
    for part in range(MOE_PARTS):
        pos, tinfo, tok = _plan(route, part)
        texp, tvalid = tinfo[0, :nt], tinfo[1, :nt]
        xs = _rowgather(h3, tok[:nt].reshape(-1))
        ys = _experts(texp, tvalid, xs, w_gate, w_up, w_down)
        g = _rowgather(ys, pos[0:2].reshape(-1))
        out = _combine(g, x2, wcol, g_final, part, out)
    return out


def kernel(x, mem, g_mix, w_in, b_ml_i, b_ml_f, b_fx_f, b_gate_ml, b_gate_fx, g_ml_head, w_proj_ml, w_proj_fx, w_out, g_xq, g_xmem, w_xq, w_xkv, w_xo, g_moe, w_rg, b_rg, w_re, b_re, w_gate, w_up, w_down, g_final):
    nb, seq, d = x.shape
    depth = g_mix.shape[0]
    assert depth == 1, "the final rmsnorm is fused into the (single) layer's combine kernel"
    x2, h3, route, wcol = _layer(
        x, mem, g_mix[0], w_in[0], b_ml_i[0], b_ml_f[0], b_fx_f[0], b_gate_ml[0], b_gate_fx[0], g_ml_head[0],
        w_proj_ml[0], w_proj_fx[0], w_out[0], g_xq[0], g_xmem[0], w_xq[0], w_xkv[0], w_xo[0], g_moe[0],
        w_rg[0], b_rg[0], w_re[0], b_re[0], w_gate[0], w_up[0], w_down[0])
    out = _moe(x2, h3, route, wcol, w_gate[0], w_up[0], w_down[0], g_final.reshape(1, d).astype(F32))
    return out.reshape(nb, seq, d)
```

```python
import functools

import jax
import jax.numpy as jnp
from jax import lax
from jax.experimental import pallas as pl
from jax.experimental.pallas import tpu as pltpu
from jax.experimental.pallas import tpu_sc as plsc

F32 = jnp.float32
BF16 = jnp.bfloat16

D_MODEL = 1024
EPS = 1e-6
ML_HEADS = 4
ML_QK_DIM = 128
ML_V_DIM = 256
FX_HEADS = 8
FX_HEAD_DIM = 128
XA_HEADS = 4
XA_HEAD_DIM = 256
N_GROUPS = 4
EXPERTS_PER_GROUP = 8
N_EXPERTS = 32
D_EXPERT = 512

LANES = 128
N_GATES = 16
ROUTER_ROWS = 40
LOG2E = 1.4426950408889634
Z_W = 6144
Z_ML_O, Z_FX_Q, Z_FX_K, Z_GT_ML, Z_GT_FX = 1, 2, 3, 4, 5
VT_W = 2048

VMEM_LIMIT = 56 * 1024 * 1024

IN_CHUNK = 1024
TK = 256
TQ = 512
TIN = 512
KT_PER_Q = TQ // TK
MXU_LOOKAHEAD = 16
TM = 1024
TME = 512
TP = 512
TF = 512
MOE_PARTS = 1
SC_CORES, SC_SUBCORES = 2, 16
PACKED_W = D_MODEL // 2
SC_ROWS = 64


def _params(sem, flags=None):
    return pltpu.CompilerParams(dimension_semantics=sem, vmem_limit_bytes=VMEM_LIMIT, flags=flags)


def _rms(x, g):
    return x * lax.rsqrt(jnp.mean(x * x, axis=-1, keepdims=True) + EPS) * g


def _dot_nt(a, b, **kw):
    return lax.dot_general(a, b, (((1,), (1,)), ((), ())), preferred_element_type=F32, **kw)


def _split_bf16(x):
    hi = x.astype(BF16)
    return hi, (x - hi.astype(F32)).astype(BF16)


def _dot_nt_split(a, b):
    a_hi, a_lo = _split_bf16(a)
    b_hi, b_lo = _split_bf16(b)
    return _dot_nt(a_hi, b_hi) + (_dot_nt(a_hi, b_lo) + _dot_nt(a_lo, b_hi))


def _pack_halves(x):
    n = x.shape[1] // 2
    bits = lambda v: lax.bitcast_convert_type(v.astype(BF16).astype(F32), jnp.uint32)
    w = (bits(x[:, :n]) >> 16) | (bits(x[:, n:]) & jnp.uint32(0xFFFF0000))
    return lax.bitcast_convert_type(w, F32)


def _unpack_halves(p):
    w = lax.bitcast_convert_type(p, jnp.uint32)
    lo = lax.bitcast_convert_type(w << 16, F32)
    hi = lax.bitcast_convert_type(w & jnp.uint32(0xFFFF0000), F32)
    return jnp.concatenate([lo, hi], axis=1)


def _resident(shape):
    zeros = (0,) * len(shape)
    return pl.BlockSpec(shape, lambda *_: zeros, pipeline_mode=pl.Buffered(1))


def _inproj_body(x_ref, g_ref, w_ref, wvt_ref, wgt_ref, z_ref, vt_ref, gt_ref):
    h = _rms(x_ref[...], g_ref[...])
    hb = h.astype(BF16)
    for c in range(Z_W // IN_CHUNK):
        sl = slice(c * IN_CHUNK, (c + 1) * IN_CHUNK)
        z_ref[:, sl] = jnp.dot(hb, w_ref[:, sl], preferred_element_type=F32).astype(BF16)
    for kt in range(TIN // TK):
        hk = hb[kt * TK:(kt + 1) * TK]
        for c in range(VT_W // IN_CHUNK):
            sl = slice(c * IN_CHUNK, (c + 1) * IN_CHUNK)
            vt_ref[kt, sl, :] = _dot_nt(wvt_ref[sl, :], hk).astype(BF16)
    gt_ref[...] = _dot_nt_split(wgt_ref[...], h)


def _inproj(x2d, g, w, wvt, wgt, seq):
    t = x2d.shape[0]
    per_b = seq // TIN
    return pl.pallas_call(
        _inproj_body,
        grid=(t // TIN,),
        in_specs=[
            pl.BlockSpec((TIN, D_MODEL), lambda i: (i, 0)),
            _resident((1, D_MODEL)),
            _resident((D_MODEL, Z_W)),
            _resident((VT_W, D_MODEL)),
            _resident((N_GATES, D_MODEL)),
        ],
        out_specs=[
            pl.BlockSpec((TIN, Z_W), lambda i: (i, 0)),
            pl.BlockSpec((TIN // TK, VT_W, TK), lambda i: (i, 0, 0)),
            pl.BlockSpec((None, N_GATES, TIN), lambda i: (i // per_b, 0, i % per_b)),
        ],
        out_shape=[jax.ShapeDtypeStruct((t, Z_W), BF16), jax.ShapeDtypeStruct((t // TK, VT_W, TK), BF16),
                   jax.ShapeDtypeStruct((t // seq, N_GATES, seq), F32)],
        compiler_params=_params(("parallel",)),
        name="inproj",
    )(x2d, g, w, wvt, wgt)


def _scan_lanes(x, op, identity):
    n = x.shape[-1]
    idx = lax.broadcasted_iota(jnp.int32, x.shape, 1)
    s = 1
    while s < n:
        shifted = pltpu.roll(x, s, axis=1)
        x = op(x, jnp.where(idx >= s, shifted, identity))
        s *= 2
    return x


def _log_sigmoid(x):
    return jnp.minimum(x, 0.0) - jnp.log1p(jnp.exp(-jnp.abs(x)))


def _gateprep_body(gt_ref, bias_ref, rows_ref, cols_ref, caug_ref):
    g = gt_ref[...] + bias_ref[...]
    s = g.shape[1]
    cs = _scan_lanes(_log_sigmoid(g), jnp.add, 0.0)
    b = cs[4:8]
    c2 = cs[8:16] * LOG2E
    a = g[0:4] - b
    m = _scan_lanes(jnp.concatenate([a, a], axis=0), jnp.maximum, -jnp.inf)[0:4]
    rows_ref[...] = jnp.concatenate([m * LOG2E, b + m, c2], axis=0)
    cols_ref[...] = jnp.concatenate([a * LOG2E, jnp.zeros((LANES - ML_HEADS, s), F32)], axis=0).T
    hi = c2.astype(BF16).astype(F32)
    r1 = c2 - hi
    mid = r1.astype(BF16).astype(F32)
    lo = r1 - mid
    aug = jnp.concatenate([-hi, -mid, -lo, jnp.zeros((LANES - 3 * FX_HEADS, s), F32)], axis=0)
    caug_ref[...] = aug.T.astype(BF16)


def _gateprep(gt3, bias):
    nb, _, s = gt3.shape
    return pl.pallas_call(
        _gateprep_body,
        grid=(nb,),
        in_specs=[
            pl.BlockSpec((None, N_GATES, s), lambda b: (b, 0, 0)),
            _resident((N_GATES, 1)),
        ],
        out_specs=[
            pl.BlockSpec((None, N_GATES, s), lambda b: (b, 0, 0)),
            pl.BlockSpec((None, s, LANES), lambda b: (b, 0, 0)),
            pl.BlockSpec((None, s, LANES), lambda b: (b, 0, 0)),
        ],
        out_shape=[jax.ShapeDtypeStruct((nb, N_GATES, s), F32), jax.ShapeDtypeStruct((nb, s, LANES), F32),
                   jax.ShapeDtypeStruct((nb, s, LANES), BF16)],
        compiler_params=_params(("parallel",)),
        name="gateprep",
    )(gt3, bias)


def _pipelined(stages, scores, update):
    pending = {k: scores(*stages[k]) for k in range(min(MXU_LOOKAHEAD, len(stages)))}
    for k, stage in enumerate(stages):
        if k + MXU_LOOKAHEAD < len(stages):
            pending[k + MXU_LOOKAHEAD] = scores(*stages[k + MXU_LOOKAHEAD])
        update(*stage, pending.pop(k))


def _causal_mask_t(d):
    s = lax.broadcasted_iota(jnp.int32, (TK, TQ - d * TK), 0)
    t = lax.broadcasted_iota(jnp.int32, (TK, TQ - d * TK), 1)
    return s <= t


def _mlstm_body(q_ref, k_ref, vt_ref, o_ref, cols_ref, rows_ref, gh_ref, y_ref, num_scr, den_scr):
    i = pl.program_id(1)
    rows = rows_ref[...]
    num_scr[...] = jnp.zeros_like(num_scr)
    den_scr[...] = jnp.zeros_like(den_scr)

    def scores(j, h, mask, q0):
        ks = pl.ds(pl.multiple_of(j * TK, TK), TK)
        qk = slice(h * ML_QK_DIM, (h + 1) * ML_QK_DIM)
        return _dot_nt(k_ref[ks, qk], q_ref[q0:, qk])

    def update(j, h, mask, q0, s):
        ks = pl.ds(pl.multiple_of(j * TK, TK), TK)
        vv = slice(h * ML_V_DIM, (h + 1) * ML_V_DIM)
        w = jnp.exp2(cols_ref[ks, h:h + 1] - rows[h:h + 1, q0:])
        if mask is not None:
            w = jnp.where(mask, w, 0.0)
        s = s * w
        den_scr[h, :, q0:] += jnp.sum(s, axis=0, keepdims=True)
        num_scr[h, :, q0:] += jnp.dot(vt_ref[j, vv, :], s.astype(BF16), preferred_element_type=F32)

    def body(jj, carry):
        _pipelined([(jj * KT_PER_Q + d, h, None, 0) for d in range(KT_PER_Q) for h in range(ML_HEADS)], scores, update)
        return carry

    lax.fori_loop(0, i, body, 0)
    masks = [_causal_mask_t(d) for d in range(KT_PER_Q)]
    _pipelined([(i * KT_PER_Q + d, h, masks[d], d * TK) for d in range(KT_PER_Q) for h in range(ML_HEADS)],
               scores, update)
    for h in range(ML_HEADS):
        vv = slice(h * ML_V_DIM, (h + 1) * ML_V_DIM)
        floor = jnp.exp(-rows[ML_HEADS + h:ML_HEADS + h + 1])
        hh = num_scr[h] * (1.0 / jnp.maximum(jnp.abs(den_scr[h]), floor))
        yt = hh * lax.rsqrt(jnp.mean(hh * hh, axis=0, keepdims=True) + EPS)
        y = yt.T * gh_ref[:, vv]
        y_ref[:, vv] = (y * jax.nn.sigmoid(o_ref[:, vv].astype(F32))).astype(BF16)


def _mlstm(z3, vt4, cols3, rows3, g_head):
    nb, s, _ = z3.shape
    nq = s // TQ
    return pl.pallas_call(
        _mlstm_body,
        grid=(nb, nq),
        in_specs=[
            pl.BlockSpec((None, TQ, ML_HEADS * ML_QK_DIM), lambda b, i: (b, i, 0)),
            pl.BlockSpec((None, s, ML_HEADS * ML_QK_DIM), lambda b, i: (b, 0, 1)),
            pl.BlockSpec((None, s // TK, D_MODEL, TK), lambda b, i: (b, 0, 0, 0)),
            pl.BlockSpec((None, TQ, D_MODEL), lambda b, i: (b, i, Z_ML_O)),
            pl.BlockSpec((None, s, LANES), lambda b, i: (b, 0, 0)),
            pl.BlockSpec((None, N_GATES, TQ), lambda b, i: (b, 0, i)),
            _resident((1, D_MODEL)),
        ],
        out_specs=pl.BlockSpec((None, TQ, D_MODEL), lambda b, i: (b, i, 0)),
        out_shape=jax.ShapeDtypeStruct((nb, s, D_MODEL), BF16),
        scratch_shapes=[pltpu.VMEM((ML_HEADS, ML_V_DIM, TQ), F32), pltpu.VMEM((ML_HEADS, 1, TQ), F32)],
        compiler_params=_params(("parallel", "arbitrary")),
        name="mlstm",
    )(z3, z3, vt4, z3, cols3, rows3, g_head)


def _fox_body(q_ref, k_ref, vt_ref, caug_ref, rows_ref, y_ref, qa_scr, acc_scr, m_scr, l_scr):
    i = pl.program_id(1)
    rows = rows_ref[...]
    lane = lax.broadcasted_iota(jnp.int32, (TQ, LANES), 1)
    for h in range(FX_HEADS):
        hd = slice(h * FX_HEAD_DIM, (h + 1) * FX_HEAD_DIM)
        ones = jnp.where((lane < 3 * FX_HEADS) & (lane % FX_HEADS == h), 1.0, 0.0).astype(BF16)
        qa_scr[h] = jnp.concatenate([q_ref[:, hd], ones], axis=1)
    m_scr[...] = jnp.full_like(m_scr, -jnp.inf)
    l_scr[...] = jnp.zeros_like(l_scr)
    acc_scr[...] = jnp.zeros_like(acc_scr)

    def scores(j, h, mask, q0):
        ks = pl.ds(pl.multiple_of(j * TK, TK), TK)
        hd = slice(h * FX_HEAD_DIM, (h + 1) * FX_HEAD_DIM)
        k_aug = jnp.concatenate([k_ref[ks, hd], caug_ref[ks, :]], axis=1)
        return _dot_nt(k_aug, qa_scr[h, q0:, :])

    def update(j, h, mask, q0, u):
        hd = slice(h * FX_HEAD_DIM, (h + 1) * FX_HEAD_DIM)
        if mask is not None:
            u = jnp.where(mask, u, -jnp.inf)
        c_row = rows[2 * ML_HEADS + h:2 * ML_HEADS + h + 1, q0:]
        m_prev = m_scr[h, :, q0:]
        m_new = jnp.maximum(m_prev, jnp.max(u, axis=0, keepdims=True) + c_row)
        p = jnp.exp2(u - (m_new - c_row))
        alpha = jnp.exp2(m_prev - m_new)
        l_scr[h, :, q0:] = alpha * l_scr[h, :, q0:] + jnp.sum(p, axis=0, keepdims=True)
        acc_scr[h, :, q0:] = alpha * acc_scr[h, :, q0:] + jnp.dot(vt_ref[j, hd, :], p.astype(BF16),
                                                                  preferred_element_type=F32)
        m_scr[h, :, q0:] = m_new

    def body(jj, carry):
        _pipelined([(jj * KT_PER_Q + d, h, None, 0) for d in range(KT_PER_Q) for h in range(FX_HEADS)], scores, update)
        return carry

    lax.fori_loop(0, i, body, 0)
    masks = [_causal_mask_t(d) for d in range(KT_PER_Q)]
    _pipelined([(i * KT_PER_Q + d, h, masks[d], d * TK) for d in range(KT_PER_Q) for h in range(FX_HEADS)],
               scores, update)
    for h in range(FX_HEADS):
        hd = slice(h * FX_HEAD_DIM, (h + 1) * FX_HEAD_DIM)
        y_ref[:, hd] = (acc_scr[h] * (1.0 / l_scr[h])).T.astype(BF16)


def _fox(z3, vt4, caug3, rows3):
    nb, s, _ = z3.shape
    nq = s // TQ
    return pl.pallas_call(
        _fox_body,
        grid=(nb, nq),
        in_specs=[
            pl.BlockSpec((None, TQ, D_MODEL), lambda b, i: (b, i, Z_FX_Q)),
            pl.BlockSpec((None, s, D_MODEL), lambda b, i: (b, 0, Z_FX_K)),
            pl.BlockSpec((None, s // TK, D_MODEL, TK), lambda b, i: (b, 0, 1, 0)),
            pl.BlockSpec((None, s, LANES), lambda b, i: (b, 0, 0)),
            pl.BlockSpec((None, N_GATES, TQ), lambda b, i: (b, 0, i)),
        ],
        out_specs=pl.BlockSpec((None, TQ, D_MODEL), lambda b, i: (b, i, 0)),
        out_shape=jax.ShapeDtypeStruct((nb, s, D_MODEL), BF16),
        scratch_shapes=[pltpu.VMEM((FX_HEADS, TQ, 2 * FX_HEAD_DIM), BF16),
                        pltpu.VMEM((FX_HEADS, FX_HEAD_DIM, TQ), F32),
                        pltpu.VMEM((FX_HEADS, 1, TQ), F32), pltpu.VMEM((FX_HEADS, 1, TQ), F32)],
        compiler_params=_params(("parallel", "arbitrary")),
        name="fox",
    )(z3, z3, vt4, caug3, rows3)


def _merge_body(x_ref, yml_ref, yfx_ref, gml_ref, gfx_ref, bml_ref, bfx_ref, wml_ref, wfx_ref, wout_ref, x1_ref):
    p_ml = jnp.dot(yml_ref[...], wml_ref[...], preferred_element_type=F32)
    p_fx = jnp.dot(yfx_ref[...], wfx_ref[...], preferred_element_type=F32)
    merged = (jax.nn.sigmoid(gml_ref[...].astype(F32) + bml_ref[...]) * p_ml
              + jax.nn.sigmoid(gfx_ref[...].astype(F32) + bfx_ref[...]) * p_fx)
    x1_ref[...] = x_ref[...] + jnp.dot(merged.astype(BF16), wout_ref[...], preferred_element_type=F32)


def _merge(x2d, yml, yfx, z2d, bml, bfx, wml, wfx, wout):
    t = x2d.shape[0]
    tile = lambda col: pl.BlockSpec((TM, D_MODEL), lambda i, col=col: (i, col))
    return pl.pallas_call(
        _merge_body,
        grid=(t // TM,),
        in_specs=[tile(0), tile(0), tile(0), tile(Z_GT_ML), tile(Z_GT_FX),
                  _resident((1, D_MODEL)), _resident((1, D_MODEL)),
                  _resident((D_MODEL, D_MODEL)), _resident((D_MODEL, D_MODEL)), _resident((D_MODEL, D_MODEL))],
        out_specs=tile(0),
        out_shape=jax.ShapeDtypeStruct((t, D_MODEL), F32),
        compiler_params=_params(("parallel",)),
        name="merge",
    )(x2d, yml, yfx, z2d, z2d, bml, bfx, wml, wfx, wout)


def _memkv_body(m_ref, g_ref, w_ref, kv_ref):
    hb = _rms(m_ref[...], g_ref[...]).astype(BF16)
    kv_ref[...] = jnp.dot(hb, w_ref[...], preferred_element_type=F32).astype(BF16)


def _memkv(mem2d, g, w):
    t = mem2d.shape[0]
    return pl.pallas_call(
        _memkv_body,
        grid=(t // TM,),
        in_specs=[pl.BlockSpec((TM, D_MODEL), lambda i: (i, 0)), _resident((1, D_MODEL)),
                  _resident((D_MODEL, 2 * D_MODEL))],
        out_specs=pl.BlockSpec((TM, 2 * D_MODEL), lambda i: (i, 0)),
        out_shape=jax.ShapeDtypeStruct((t, 2 * D_MODEL), BF16),
        compiler_params=_params(("parallel",)),
        name="memkv",
    )(mem2d, g, w)


def _route_t(lg_t):
    tm = lg_t.shape[1]
    ninf = -jnp.inf
    big = jnp.float32(LANES)
    gid = lax.broadcasted_iota(jnp.int32, (8, tm), 0).astype(F32)
    eid = lax.broadcasted_iota(jnp.int32, (N_EXPERTS, tm), 0).astype(F32)
    gl = jnp.where(gid < N_GROUPS, lg_t[N_EXPERTS:N_EXPERTS + 8], ninf)
    gmax = jnp.max(gl, axis=0, keepdims=True)
    gidx = jnp.min(jnp.where(gl == gmax, gid, big), axis=0, keepdims=True)
    g_p = 1.0 / jnp.sum(jnp.exp(gl - gmax), axis=0, keepdims=True)
    lo = gidx * EXPERTS_PER_GROUP
    el = jnp.where(eid >= lo, jnp.where(eid < lo + EXPERTS_PER_GROUP, lg_t[0:N_EXPERTS], ninf), ninf)
    v1 = jnp.max(el, axis=0, keepdims=True)
    i1 = jnp.min(jnp.where(el == v1, eid, big), axis=0, keepdims=True)
    el2 = jnp.where(eid == i1, ninf, el)
    v2 = jnp.max(el2, axis=0, keepdims=True)
    i2 = jnp.min(jnp.where(el2 == v2, eid, big), axis=0, keepdims=True)
    t = jnp.exp(v2 - v1)
    w1 = g_p / (1.0 + t)
    w2 = w1 * t
    return jnp.concatenate([i1, i2, w1, w2, jnp.zeros((4, tm), F32)], axis=0)


def _xattn_body(x1_ref, kv_ref, gq_ref, wq_ref, wo_ref, gm_ref, wrt_ref, brt_ref, x2_ref, h3_ref, route_ref,
                wcol_ref):
    x1 = x1_ref[...]
    hb = _rms(x1, gq_ref[...]).astype(BF16)
    q = (jnp.dot(hb, wq_ref[...], preferred_element_type=F32) * (XA_HEAD_DIM ** -0.5)).astype(BF16)
    head = lambda h: slice(h * XA_HEAD_DIM, (h + 1) * XA_HEAD_DIM)
    scores = [_dot_nt(q[:, head(h)], kv_ref[:, head(h)]) for h in range(XA_HEADS)]
    outs = []
    for h in range(XA_HEADS):
        vd = slice(D_MODEL + h * XA_HEAD_DIM, D_MODEL + (h + 1) * XA_HEAD_DIM)
        s = scores[h]
        p = jnp.exp(s - jnp.max(s, axis=1, keepdims=True))
        p = p * (1.0 / jnp.sum(p, axis=1, keepdims=True))
        outs.append(jnp.dot(p.astype(BF16), kv_ref[:, vd], preferred_element_type=F32).astype(BF16))
    o = jnp.concatenate(outs, axis=1)
    x2 = x1 + jnp.dot(o, wo_ref[...], preferred_element_type=F32)
    x2_ref[...] = x2
    h3 = _rms(x2, gm_ref[...])
    h3_ref[...] = _pack_halves(h3)
    lg_t = _dot_nt_split(wrt_ref[...], h3) + brt_ref[...]
    route = _route_t(lg_t)
    route_ref[...] = route
    wcol_ref[...] = jnp.concatenate([route, jnp.zeros((LANES - 8, route.shape[1]), F32)], axis=0).T


def _xattn(x1, kv3, gq, wq, wo, gm, wrt, brt, seq):
    t = x1.shape[0]
    per_b = seq // TM
    n_mem = kv3.shape[1]
    tile = pl.BlockSpec((TM, D_MODEL), lambda i: (i, 0))
    return pl.pallas_call(
        _xattn_body,
        grid=(t // TM,),
        in_specs=[tile,
                  pl.BlockSpec((None, n_mem, 2 * D_MODEL), lambda i: (i // per_b, 0, 0)),
                  _resident((1, D_MODEL)), _resident((D_MODEL, D_MODEL)), _resident((D_MODEL, D_MODEL)),
                  _resident((1, D_MODEL)), _resident((ROUTER_ROWS, D_MODEL)), _resident((ROUTER_ROWS, 1))],
        out_specs=[tile, pl.BlockSpec((TM, PACKED_W), lambda i: (i, 0)), pl.BlockSpec((8, TM), lambda i: (0, i)),
                   pl.BlockSpec((TM, LANES), lambda i: (i, 0))],
        out_shape=[jax.ShapeDtypeStruct((t, D_MODEL), F32), jax.ShapeDtypeStruct((t, PACKED_W), F32),
                   jax.ShapeDtypeStruct((8, t), F32), jax.ShapeDtypeStruct((t, LANES), F32)],
        compiler_params=_params(("parallel",)),
        name="xattn",
    )(x1, kv3, gq, wq, wo, gm, wrt, brt)


def _n_expert_tiles(n_tokens):
    return 2 * n_tokens // TME + N_EXPERTS


def _plan_body(route_ref, pos_ref, tinfo_ref, tok_ref, cnt_scr, run_scr, start_scr, tokhi_scr, toklo_scr, *, nt_pad,
               nt_rows, n_tokens, first_token):
    phase = pl.program_id(0)
    b = pl.program_id(1)
    r = route_ref[...]
    eid = lax.broadcasted_iota(jnp.int32, (N_EXPERTS, TP), 0).astype(F32)
    oh1 = eid == r[0:1]
    oh2 = eid == r[1:2]
    oh = jnp.where(oh1 | oh2, 1.0, 0.0)

    @pl.when((phase == 0) & (b == 0))
    def _():
        cnt_scr[...] = jnp.zeros_like(cnt_scr)

    @pl.when(phase == 0)
    def _():
        cnt_scr[...] += jnp.sum(oh, axis=1, keepdims=True)

    @pl.when((phase == 1) & (b == 0))
    def _():
        cnt = cnt_scr[...]
        n_tiles = jnp.floor((cnt + (TME - 1)) * (1.0 / TME))
        ri = lax.broadcasted_iota(jnp.int32, (N_EXPERTS, N_EXPERTS), 0)
        ci = lax.broadcasted_iota(jnp.int32, (N_EXPERTS, N_EXPERTS), 1)
        lower = jnp.where(ci < ri, 1.0, 0.0).astype(BF16)
        nt_hi, nt_lo = _split_bf16(jnp.broadcast_to(n_tiles, (N_EXPERTS, LANES)))
        start = (jnp.dot(lower, nt_hi, preferred_element_type=F32)
                 + jnp.dot(lower, nt_lo, preferred_element_type=F32))[:, 0:1]
        start_scr[...] = start * TME
        run_scr[...] = jnp.zeros_like(run_scr)
        tokhi_scr[...] = jnp.zeros_like(tokhi_scr)
        toklo_scr[...] = jnp.zeros_like(toklo_scr)
        n = lax.broadcasted_iota(jnp.int32, (N_EXPERTS, nt_pad), 1).astype(F32)
        e_n = lax.broadcasted_iota(jnp.int32, (N_EXPERTS, nt_pad), 0).astype(F32)
        owner = jnp.sum(jnp.where(start <= n, 1.0, 0.0), axis=0, keepdims=True) - 1.0
        own = e_n == owner
        cnt_o = jnp.sum(jnp.where(own, cnt, 0.0), axis=0, keepdims=True)
        start_o = jnp.sum(jnp.where(own, start, 0.0), axis=0, keepdims=True)
        valid = jnp.clip(cnt_o - (n[0:1] - start_o) * TME, 0.0, float(TME))
        tinfo_ref[...] = jnp.concatenate([owner, valid, jnp.zeros((6, nt_pad), F32)], axis=0).astype(jnp.int32)

    @pl.when(phase == 1)
    def _():
        ti = lax.broadcasted_iota(jnp.int32, (TP, TP), 0)
        tj = lax.broadcasted_iota(jnp.int32, (TP, TP), 1)
        upper = jnp.where(ti < tj, 1.0, 0.0).astype(BF16)
        before = jnp.dot(oh.astype(BF16), upper, preferred_element_type=F32)
        row = start_scr[...] + run_scr[...] + before
        p1 = jnp.sum(jnp.where(oh1, row, 0.0), axis=0, keepdims=True)
        p2 = jnp.sum(jnp.where(oh2, row, 0.0), axis=0, keepdims=True)
        pos_ref[...] = jnp.concatenate([p1, p2, jnp.zeros((6, TP), F32)], axis=0).astype(jnp.int32)
        run_scr[...] += jnp.sum(oh, axis=1, keepdims=True)

        tid = lax.broadcasted_iota(jnp.int32, (1, TP), 1) + (b * TP + first_token + 1)
        t_hi = (tid // 256).astype(F32)
        t_lo = (tid % 256).astype(F32)
        tile_id = lax.broadcasted_iota(jnp.int32, (nt_rows, TP), 0).astype(F32)
        row_id = lax.broadcasted_iota(jnp.int32, (TME, TP), 0).astype(F32)
        for p in (p1, p2):
            hi = jnp.floor(p * (1.0 / TME))
            lo = p - hi * TME
            in_tile = jnp.where(tile_id == hi, 1.0, 0.0).astype(BF16)
            at_row = row_id == lo
            tokhi_scr[...] += _dot_nt(in_tile, jnp.where(at_row, t_hi, 0.0).astype(BF16))
            toklo_scr[...] += _dot_nt(in_tile, jnp.where(at_row, t_lo, 0.0).astype(BF16))

    @pl.when((phase == 1) & (b == pl.num_programs(1) - 1))
    def _():
        hit = (tokhi_scr[...] * 256.0 + toklo_scr[...]).astype(jnp.int32)
        row = (lax.broadcasted_iota(jnp.int32, hit.shape, 0) * TME + lax.broadcasted_iota(jnp.int32, hit.shape, 1))
        tok_ref[...] = jnp.where(hit > 0, hit - 1, row % n_tokens)


def _plan(route, part):
    t_all = route.shape[1]
    assert t_all < 256 * 256, "token id + 1 is carried as two base-256 digits"
    t = t_all // MOE_PARTS
    steps = t // TP
    nt_pad = -(-_n_expert_tiles(t) // LANES) * LANES
    nt_rows = -(-_n_expert_tiles(t) // 8) * 8
    col = pltpu.VMEM((N_EXPERTS, 1), F32)
    table = pltpu.VMEM((nt_rows, TME), F32)
    return pl.pallas_call(
        functools.partial(_plan_body, nt_pad=nt_pad, nt_rows=nt_rows, n_tokens=t_all, first_token=part * t),
        grid=(2, steps),
        in_specs=[pl.BlockSpec((8, TP), lambda ph, b: (0, b + part * steps))],
        out_specs=[pl.BlockSpec((8, TP), lambda ph, b: (0, b * ph)),
                   pl.BlockSpec((8, nt_pad), lambda ph, b: (0, 0)),
                   pl.BlockSpec((nt_rows, TME), lambda ph, b: (0, 0))],
        out_shape=[jax.ShapeDtypeStruct((8, t), jnp.int32), jax.ShapeDtypeStruct((8, nt_pad), jnp.int32),
                   jax.ShapeDtypeStruct((nt_rows, TME), jnp.int32)],
        scratch_shapes=[col, col, col, table, table],
        compiler_params=_params(("arbitrary", "arbitrary")),
        name="plan",
    )(route)


def _experts_body(texp_ref, tvalid_ref, xs_ref, wg_ref, wu_ref, wd_ref, ys_ref, wgb_scr, wub_scr, wdb_scr):
    n = pl.program_id(0)

    @pl.when((n == 0) | (texp_ref[n] != texp_ref[jnp.maximum(n - 1, 0)]))
    def _():
        wgb_scr[...] = wg_ref[...].astype(BF16)
        wub_scr[...] = wu_ref[...].astype(BF16)
        wdb_scr[...] = wd_ref[...].astype(BF16)

    @pl.when(tvalid_ref[n] > 0)
    def _():
        xb = _unpack_halves(xs_ref[...]).astype(BF16)
        gate = jnp.dot(xb, wgb_scr[...], preferred_element_type=F32)
        up = jnp.dot(xb, wub_scr[...], preferred_element_type=F32)
        he = (gate * jax.nn.sigmoid(gate) * up).astype(BF16)
        ys_ref[...] = _pack_halves(jnp.dot(he, wdb_scr[...], preferred_element_type=F32))

    @pl.when(tvalid_ref[n] == 0)
    def _():
        ys_ref[...] = jnp.zeros_like(ys_ref)


def _experts(texp, tvalid, xs, wg, wu, wd):
    nt = texp.shape[0]
    weight = lambda shape: pl.BlockSpec((None,) + shape, lambda n, te, tv: (te[n], 0, 0))
    return pl.pallas_call(
        _experts_body,
        grid_spec=pltpu.PrefetchScalarGridSpec(
            num_scalar_prefetch=2,
            grid=(nt,),
            in_specs=[pl.BlockSpec((TME, PACKED_W), lambda n, te, tv: (jnp.where(tv[n] > 0, n, 0), 0)),
                      weight((D_MODEL, D_EXPERT)), weight((D_MODEL, D_EXPERT)), weight((D_EXPERT, D_MODEL))],
            out_specs=pl.BlockSpec((TME, PACKED_W), lambda n, te, tv: (n, 0)),
            scratch_shapes=[pltpu.VMEM((D_MODEL, D_EXPERT), BF16), pltpu.VMEM((D_MODEL, D_EXPERT), BF16),
                            pltpu.VMEM((D_EXPERT, D_MODEL), BF16)],
        ),
        out_shape=jax.ShapeDtypeStruct((nt * TME, PACKED_W), F32),
        compiler_params=_params(("arbitrary",)),
        name="experts",
    )(texp, tvalid, xs, wg, wu, wd)


def _rowgather(table, idx):
    n_rows, width = idx.shape[0], table.shape[1]
    n_workers = SC_CORES * SC_SUBCORES
    per_worker = n_rows // n_workers
    assert per_worker * n_workers == n_rows and per_worker % SC_ROWS == 0
    mesh = plsc.VectorSubcoreMesh(core_axis_name="c", subcore_axis_name="s", num_cores=SC_CORES,
                                  num_subcores=SC_SUBCORES)

    chunks = per_worker // SC_ROWS
    assert chunks % 2 == 0
    buf = lambda: [pltpu.VMEM((SC_ROWS,), jnp.int32), pltpu.VMEM((SC_ROWS, width), table.dtype),
                   pltpu.SemaphoreType.DMA]

    @functools.partial(pl.kernel, mesh=mesh, out_type=jax.ShapeDtypeStruct((n_rows, width), table.dtype),
                       scratch_types=buf() + buf(), name="rowgather")
    def gather(table_hbm, idx_hbm, out_hbm, idx_a, rows_a, sem_a, idx_b, rows_b, sem_b):
        worker = lax.axis_index("s") * SC_CORES + lax.axis_index("c")
        base = worker * per_worker

        def fetch(c, idx_v, rows_v, sem):
            pltpu.sync_copy(idx_hbm.at[pl.ds(base + c * SC_ROWS, SC_ROWS)], idx_v)
            pltpu.async_copy(table_hbm.at[idx_v], rows_v, sem)

        def flush(c, idx_v, rows_v, sem):
            pltpu.make_async_copy(table_hbm.at[idx_v], rows_v, sem).wait()
            pltpu.sync_copy(rows_v, out_hbm.at[pl.ds(base + c * SC_ROWS, SC_ROWS)])

        fetch(0, idx_a, rows_a, sem_a)

        @pl.loop(0, chunks, step=2)
        def _(c):
            fetch(c + 1, idx_b, rows_b, sem_b)
            flush(c, idx_a, rows_a, sem_a)

            @pl.when(c + 2 < chunks)
            def _():
                fetch(c + 2, idx_a, rows_a, sem_a)

            flush(c + 1, idx_b, rows_b, sem_b)

    return gather(table, idx)


def _combine_body(g1_ref, g2_ref, x2_ref, wcol_ref, gf_ref, *rest):
    out_ref = rest[-1]
    w = wcol_ref[...]
    y = w[:, 2:3] * _unpack_halves(g1_ref[...]) + w[:, 3:4] * _unpack_halves(g2_ref[...])
    out_ref[...] = _rms(x2_ref[...] + y, gf_ref[...])


def _combine(g, x2, wcol, gf, part, out_prev):
    t = x2.shape[0]
    steps = t // MOE_PARTS // TF
    mine = lambda w: pl.BlockSpec((TF, w), lambda i: (i + part * steps, 0))
    in_specs = [pl.BlockSpec((TF, PACKED_W), lambda i: (i, 0)),
                pl.BlockSpec((TF, PACKED_W), lambda i: (i + steps, 0)),
                mine(D_MODEL), mine(LANES), _resident((1, D_MODEL))]
    operands = [g, g, x2, wcol, gf]
    aliases = {}
    if out_prev is not None:
        in_specs.append(pl.BlockSpec(memory_space=pl.ANY))
        operands.append(out_prev)
        aliases = {len(operands) - 1: 0}
    return pl.pallas_call(
        _combine_body,
        grid=(steps,),
        in_specs=in_specs,
        out_specs=mine(D_MODEL),
        out_shape=jax.ShapeDtypeStruct((t, D_MODEL), F32),
        input_output_aliases=aliases,
        compiler_params=_params(("parallel",)),
        name="combine",
    )(*operands)


def _layer(x, mem, g_mix, w_in, b_ml_i, b_ml_f, b_fx_f, b_gate_ml, b_gate_fx, g_ml_head, w_proj_ml, w_proj_fx,
           w_out, g_xq, g_xmem, w_xq, w_xkv, w_xo, g_moe, w_rg, b_rg, w_re, b_re, w_gate, w_up, w_down):
    nb, seq, d = x.shape
    t = nb * seq
    row = lambda v: v.reshape(1, -1).astype(F32)

    o = 0
    parts = {}
    for name, width in (("ml_q", 512), ("ml_k", 512), ("ml_v", 1024), ("ml_o", 1024), ("ml_i", 4), ("ml_f", 4),
                        ("fx_q", 1024), ("fx_k", 1024), ("fx_v", 1024), ("fx_f", 8), ("gt_ml", 1024),
                        ("gt_fx", 1024)):
        parts[name] = w_in[:, o:o + width]
        o += width
    w_main = jnp.concatenate(
        [parts["ml_q"], parts["ml_k"] * (ML_QK_DIM ** -0.5), parts["ml_o"],
         parts["fx_q"] * (FX_HEAD_DIM ** -0.5 * LOG2E), parts["fx_k"], parts["gt_ml"], parts["gt_fx"]],
        axis=1).astype(BF16)
    w_vt = jnp.concatenate([parts["ml_v"], parts["fx_v"]], axis=1).T.astype(BF16)
    w_gates_t = jnp.concatenate([parts["ml_i"], parts["ml_f"], parts["fx_f"]], axis=1).T
    gate_bias = jnp.concatenate([b_ml_i, b_ml_f, b_fx_f]).reshape(N_GATES, 1).astype(F32)

    x2d = x.reshape(t, d)
    z, vt, gt3 = _inproj(x2d, row(g_mix), w_main, w_vt, w_gates_t, seq)
    rows, cols, caug = _gateprep(gt3, gate_bias)
    z3 = z.reshape(nb, seq, Z_W)
    vt4 = vt.reshape(nb, seq // TK, VT_W, TK)
    y_ml = _mlstm(z3, vt4, cols, rows, row(g_ml_head))
    y_fx = _fox(z3, vt4, caug, rows)
    x1 = _merge(x2d, y_ml.reshape(t, d), y_fx.reshape(t, d), z, row(b_gate_ml), row(b_gate_fx),
                w_proj_ml.astype(BF16), w_proj_fx.astype(BF16), w_out.astype(BF16))

    n_mem = mem.shape[1]
    kv = _memkv(mem.reshape(nb * n_mem, d), row(g_xmem), w_xkv.astype(BF16))
    w_router_t = jnp.concatenate([w_re, w_rg, jnp.zeros((d, ROUTER_ROWS - N_EXPERTS - N_GROUPS), F32)], axis=1).T
    b_router_t = jnp.concatenate([b_re, b_rg, jnp.zeros((ROUTER_ROWS - N_EXPERTS - N_GROUPS,), F32)]).reshape(ROUTER_ROWS, 1)
    x2, h3, route, wcol = _xattn(x1, kv.reshape(nb, n_mem, 2 * d), row(g_xq), w_xq.astype(BF16),
                                 w_xo.astype(BF16), row(g_moe), w_router_t, b_router_t, seq)
    return x2, h3, route, wcol


def _moe(x2, h3, route, wcol, w_gate, w_up, w_down, g_final):
    t = x2.shape[0]
    nt = _n_expert_tiles(t // MOE_PARTS)
    out = None
    for part in range(MOE_PARTS):
        pos, tinfo, tok = _plan(route, part)
        texp, tvalid = tinfo[0, :nt], tinfo[1, :nt]
        xs = _rowgather(h3, tok[:nt].reshape(-1))
        ys = _experts(texp, tvalid, xs, w_gate, w_up, w_down)
        g = _rowgather(ys, pos[0:2].reshape(-1))
        out = _combine(g, x2, wcol, g_final, part, out)
    return out


def kernel(x, mem, g_mix, w_in, b_ml_i, b_ml_f, b_fx_f, b_gate_ml, b_gate_fx, g_ml_head, w_proj_ml, w_proj_fx, w_out, g_xq, g_xmem, w_xq, w_xkv, w_xo, g_moe, w_rg, b_rg, w_re, b_re, w_gate, w_up, w_down, g_final):
    nb, seq, d = x.shape
    depth = g_mix.shape[0]
    assert depth == 1, "the final rmsnorm is fused into the (single) layer's combine kernel"
    x2, h3, route, wcol = _layer(
        x, mem, g_mix[0], w_in[0], b_ml_i[0], b_ml_f[0], b_fx_f[0], b_gate_ml[0], b_gate_fx[0], g_ml_head[0],
        w_proj_ml[0], w_proj_fx[0], w_out[0], g_xq[0], g_xmem[0], w_xq[0], w_xkv[0], w_xo[0], g_moe[0],
        w_rg[0], b_rg[0], w_re[0], b_re[0], w_gate[0], w_up[0], w_down[0])
    out = _moe(x2, h3, route, wcol, w_gate[0], w_up[0], w_down[0], g_final.reshape(1, d).astype(F32))
    return out.reshape(nb, seq, d)
```

```python
import functools

import jax
import jax.numpy as jnp
from jax import lax
from jax.experimental import pallas as pl
from jax.experimental.pallas import tpu as pltpu
from jax.experimental.pallas import tpu_sc as plsc

F32 = jnp.float32
BF16 = jnp.bfloat16

D_MODEL = 1024
EPS = 1e-6
ML_HEADS = 4
ML_QK_DIM = 128
ML_V_DIM = 256
FX_HEADS = 8
FX_HEAD_DIM = 128
XA_HEADS = 4
XA_HEAD_DIM = 256
N_GROUPS = 4
EXPERTS_PER_GROUP = 8
N_EXPERTS = 32
D_EXPERT = 512

LANES = 128
N_GATES = 16
ROUTER_ROWS = 40
LOG2E = 1.4426950408889634
Z_W = 6144
Z_ML_O, Z_FX_Q, Z_FX_K, Z_GT_ML, Z_GT_FX = 1, 2, 3, 4, 5
VT_W = 2048

VMEM_LIMIT = 56 * 1024 * 1024

IN_CHUNK = 1024
TK = 256
TQ = 512
TIN = 512
KT_PER_Q = TQ // TK
MXU_LOOKAHEAD = 16
TM = 1024
TME = 512
TP = 512
TF = 1024
SC_CORES, SC_SUBCORES = 2, 16
PACKED_W = D_MODEL // 2
SC_ROWS = 64


def _params(sem, flags=None):
    return pltpu.CompilerParams(dimension_semantics=sem, vmem_limit_bytes=VMEM_LIMIT, flags=flags)


def _rms(x, g):
    return x * lax.rsqrt(jnp.mean(x * x, axis=-1, keepdims=True) + EPS) * g


def _dot_nt(a, b, **kw):
    return lax.dot_general(a, b, (((1,), (1,)), ((), ())), preferred_element_type=F32, **kw)


def _split_bf16(x):
    hi = x.astype(BF16)
    return hi, (x - hi.astype(F32)).astype(BF16)


def _dot_nt_split(a, b):
    a_hi, a_lo = _split_bf16(a)
    b_hi, b_lo = _split_bf16(b)
    return _dot_nt(a_hi, b_hi) + (_dot_nt(a_hi, b_lo) + _dot_nt(a_lo, b_hi))


def _pack_halves(x):
    n = x.shape[1] // 2
    bits = lambda v: lax.bitcast_convert_type(v.astype(BF16).astype(F32), jnp.uint32)
    w = (bits(x[:, :n]) >> 16) | (bits(x[:, n:]) & jnp.uint32(0xFFFF0000))
    return lax.bitcast_convert_type(w, F32)


def _unpack_halves(p):
    w = lax.bitcast_convert_type(p, jnp.uint32)
    lo = lax.bitcast_convert_type(w << 16, F32)
    hi = lax.bitcast_convert_type(w & jnp.uint32(0xFFFF0000), F32)
    return jnp.concatenate([lo, hi], axis=1)


def _resident(shape):
    zeros = (0,) * len(shape)
    return pl.BlockSpec(shape, lambda *_: zeros, pipeline_mode=pl.Buffered(1))


def _inproj_body(x_ref, g_ref, w_ref, wvt_ref, wgt_ref, z_ref, vt_ref, gt_ref):
    h = _rms(x_ref[...], g_ref[...])
    hb = h.astype(BF16)
    for c in range(Z_W // IN_CHUNK):
        sl = slice(c * IN_CHUNK, (c + 1) * IN_CHUNK)
        z_ref[:, sl] = jnp.dot(hb, w_ref[:, sl], preferred_element_type=F32).astype(BF16)
    for kt in range(TIN // TK):
        hk = hb[kt * TK:(kt + 1) * TK]
        for c in range(VT_W // IN_CHUNK):
            sl = slice(c * IN_CHUNK, (c + 1) * IN_CHUNK)
            vt_ref[kt, sl, :] = _dot_nt(wvt_ref[sl, :], hk).astype(BF16)
    gt_ref[...] = _dot_nt_split(wgt_ref[...], h)


def _inproj(x2d, g, w, wvt, wgt, seq):
    t = x2d.shape[0]
    per_b = seq // TIN
    return pl.pallas_call(
        _inproj_body,
        grid=(t // TIN,),
        in_specs=[
            pl.BlockSpec((TIN, D_MODEL), lambda i: (i, 0)),
            _resident((1, D_MODEL)),
            _resident((D_MODEL, Z_W)),
            _resident((VT_W, D_MODEL)),
            _resident((N_GATES, D_MODEL)),
        ],
        out_specs=[
            pl.BlockSpec((TIN, Z_W), lambda i: (i, 0)),
            pl.BlockSpec((TIN // TK, VT_W, TK), lambda i: (i, 0, 0)),
            pl.BlockSpec((None, N_GATES, TIN), lambda i: (i // per_b, 0, i % per_b)),
        ],
        out_shape=[jax.ShapeDtypeStruct((t, Z_W), BF16), jax.ShapeDtypeStruct((t // TK, VT_W, TK), BF16),
                   jax.ShapeDtypeStruct((t // seq, N_GATES, seq), F32)],
        compiler_params=_params(("parallel",)),
        name="inproj",
    )(x2d, g, w, wvt, wgt)


def _scan_lanes(x, op, identity):
    n = x.shape[-1]
    idx = lax.broadcasted_iota(jnp.int32, x.shape, 1)
    s = 1
    while s < n:
        shifted = pltpu.roll(x, s, axis=1)
        x = op(x, jnp.where(idx >= s, shifted, identity))
        s *= 2
    return x


def _log_sigmoid(x):
    return jnp.minimum(x, 0.0) - jnp.log1p(jnp.exp(-jnp.abs(x)))


def _gateprep_body(gt_ref, bias_ref, rows_ref, cols_ref, caug_ref):
    g = gt_ref[...] + bias_ref[...]
    s = g.shape[1]
    cs = _scan_lanes(_log_sigmoid(g), jnp.add, 0.0)
    b = cs[4:8]
    c2 = cs[8:16] * LOG2E
    a = g[0:4] - b
    m = _scan_lanes(jnp.concatenate([a, a], axis=0), jnp.maximum, -jnp.inf)[0:4]
    rows_ref[...] = jnp.concatenate([m * LOG2E, b + m, c2], axis=0)
    cols_ref[...] = jnp.concatenate([a * LOG2E, jnp.zeros((LANES - ML_HEADS, s), F32)], axis=0).T
    hi = c2.astype(BF16).astype(F32)
    r1 = c2 - hi
    mid = r1.astype(BF16).astype(F32)
    lo = r1 - mid
    aug = jnp.concatenate([-hi, -mid, -lo, jnp.zeros((LANES - 3 * FX_HEADS, s), F32)], axis=0)
    caug_ref[...] = aug.T.astype(BF16)


def _gateprep(gt3, bias):
    nb, _, s = gt3.shape
    return pl.pallas_call(
        _gateprep_body,
        grid=(nb,),
        in_specs=[
            pl.BlockSpec((None, N_GATES, s), lambda b: (b, 0, 0)),
            _resident((N_GATES, 1)),
        ],
        out_specs=[
            pl.BlockSpec((None, N_GATES, s), lambda b: (b, 0, 0)),
            pl.BlockSpec((None, s, LANES), lambda b: (b, 0, 0)),
            pl.BlockSpec((None, s, LANES), lambda b: (b, 0, 0)),
        ],
        out_shape=[jax.ShapeDtypeStruct((nb, N_GATES, s), F32), jax.ShapeDtypeStruct((nb, s, LANES), F32),
                   jax.ShapeDtypeStruct((nb, s, LANES), BF16)],
        compiler_params=_params(("parallel",)),
        name="gateprep",
    )(gt3, bias)


def _pipelined(stages, scores, update):
    pending = {k: scores(*stages[k]) for k in range(min(MXU_LOOKAHEAD, len(stages)))}
    for k, stage in enumerate(stages):
        if k + MXU_LOOKAHEAD < len(stages):
            pending[k + MXU_LOOKAHEAD] = scores(*stages[k + MXU_LOOKAHEAD])
        update(*stage, pending.pop(k))


def _causal_mask_t(d):
    s = lax.broadcasted_iota(jnp.int32, (TK, TQ - d * TK), 0)
    t = lax.broadcasted_iota(jnp.int32, (TK, TQ - d * TK), 1)
    return s <= t


def _mlstm_body(q_ref, k_ref, vt_ref, o_ref, cols_ref, rows_ref, gh_ref, y_ref, num_scr, den_scr):
    i = pl.program_id(1)
    rows = rows_ref[...]
    num_scr[...] = jnp.zeros_like(num_scr)
    den_scr[...] = jnp.zeros_like(den_scr)

    def scores(j, h, mask, q0):
        ks = pl.ds(pl.multiple_of(j * TK, TK), TK)
        qk = slice(h * ML_QK_DIM, (h + 1) * ML_QK_DIM)
        return _dot_nt(k_ref[ks, qk], q_ref[q0:, qk])

    def update(j, h, mask, q0, s):
        ks = pl.ds(pl.multiple_of(j * TK, TK), TK)
        vv = slice(h * ML_V_DIM, (h + 1) * ML_V_DIM)
        w = jnp.exp2(cols_ref[ks, h:h + 1] - rows[h:h + 1, q0:])
        if mask is not None:
            w = jnp.where(mask, w, 0.0)
        s = s * w
        den_scr[h, :, q0:] += jnp.sum(s, axis=0, keepdims=True)
        num_scr[h, :, q0:] += jnp.dot(vt_ref[j, vv, :], s.astype(BF16), preferred_element_type=F32)

    def body(jj, carry):
        _pipelined([(jj * KT_PER_Q + d, h, None, 0) for d in range(KT_PER_Q) for h in range(ML_HEADS)], scores, update)
        return carry

    lax.fori_loop(0, i, body, 0)
    masks = [_causal_mask_t(d) for d in range(KT_PER_Q)]
    _pipelined([(i * KT_PER_Q + d, h, masks[d], d * TK) for d in range(KT_PER_Q) for h in range(ML_HEADS)],
               scores, update)
    for h in range(ML_HEADS):
        vv = slice(h * ML_V_DIM, (h + 1) * ML_V_DIM)
        floor = jnp.exp(-rows[ML_HEADS + h:ML_HEADS + h + 1])
        hh = num_scr[h] * (1.0 / jnp.maximum(jnp.abs(den_scr[h]), floor))
        yt = hh * lax.rsqrt(jnp.mean(hh * hh, axis=0, keepdims=True) + EPS)
        y = yt.T * gh_ref[:, vv]
        y_ref[:, vv] = (y * jax.nn.sigmoid(o_ref[:, vv].astype(F32))).astype(BF16)


def _mlstm(z3, vt4, cols3, rows3, g_head):
    nb, s, _ = z3.shape
    nq = s // TQ
    return pl.pallas_call(
        _mlstm_body,
        grid=(nb, nq),
        in_specs=[
            pl.BlockSpec((None, TQ, ML_HEADS * ML_QK_DIM), lambda b, i: (b, i, 0)),
            pl.BlockSpec((None, s, ML_HEADS * ML_QK_DIM), lambda b, i: (b, 0, 1)),
            pl.BlockSpec((None, s // TK, D_MODEL, TK), lambda b, i: (b, 0, 0, 0)),
            pl.BlockSpec((None, TQ, D_MODEL), lambda b, i: (b, i, Z_ML_O)),
            pl.BlockSpec((None, s, LANES), lambda b, i: (b, 0, 0)),
            pl.BlockSpec((None, N_GATES, TQ), lambda b, i: (b, 0, i)),
            _resident((1, D_MODEL)),
        ],
        out_specs=pl.BlockSpec((None, TQ, D_MODEL), lambda b, i: (b, i, 0)),
        out_shape=jax.ShapeDtypeStruct((nb, s, D_MODEL), BF16),
        scratch_shapes=[pltpu.VMEM((ML_HEADS, ML_V_DIM, TQ), F32), pltpu.VMEM((ML_HEADS, 1, TQ), F32)],
        compiler_params=_params(("parallel", "arbitrary")),
        name="mlstm",
    )(z3, z3, vt4, z3, cols3, rows3, g_head)


def _fox_body(q_ref, k_ref, vt_ref, caug_ref, rows_ref, y_ref, qa_scr, acc_scr, m_scr, l_scr):
    i = pl.program_id(1)
    rows = rows_ref[...]
    lane = lax.broadcasted_iota(jnp.int32, (TQ, LANES), 1)
    for h in range(FX_HEADS):
        hd = slice(h * FX_HEAD_DIM, (h + 1) * FX_HEAD_DIM)
        ones = jnp.where((lane < 3 * FX_HEADS) & (lane % FX_HEADS == h), 1.0, 0.0).astype(BF16)
        qa_scr[h] = jnp.concatenate([q_ref[:, hd], ones], axis=1)
    m_scr[...] = jnp.full_like(m_scr, -jnp.inf)
    l_scr[...] = jnp.zeros_like(l_scr)
    acc_scr[...] = jnp.zeros_like(acc_scr)

    def scores(j, h, mask, q0):
        ks = pl.ds(pl.multiple_of(j * TK, TK), TK)
        hd = slice(h * FX_HEAD_DIM, (h + 1) * FX_HEAD_DIM)
        k_aug = jnp.concatenate([k_ref[ks, hd], caug_ref[ks, :]], axis=1)
        return _dot_nt(k_aug, qa_scr[h, q0:, :])

    def update(j, h, mask, q0, u):
        hd = slice(h * FX_HEAD_DIM, (h + 1) * FX_HEAD_DIM)
        if mask is not None:
            u = jnp.where(mask, u, -jnp.inf)
        c_row = rows[2 * ML_HEADS + h:2 * ML_HEADS + h + 1, q0:]
        m_prev = m_scr[h, :, q0:]
        m_new = jnp.maximum(m_prev, jnp.max(u, axis=0, keepdims=True) + c_row)
        p = jnp.exp2(u - (m_new - c_row))
        alpha = jnp.exp2(m_prev - m_new)
        l_scr[h, :, q0:] = alpha * l_scr[h, :, q0:] + jnp.sum(p, axis=0, keepdims=True)
        acc_scr[h, :, q0:] = alpha * acc_scr[h, :, q0:] + jnp.dot(vt_ref[j, hd, :], p.astype(BF16),
                                                                  preferred_element_type=F32)
        m_scr[h, :, q0:] = m_new

    def body(jj, carry):
        _pipelined([(jj * KT_PER_Q + d, h, None, 0) for d in range(KT_PER_Q) for h in range(FX_HEADS)], scores, update)
        return carry

    lax.fori_loop(0, i, body, 0)
    masks = [_causal_mask_t(d) for d in range(KT_PER_Q)]
    _pipelined([(i * KT_PER_Q + d, h, masks[d], d * TK) for d in range(KT_PER_Q) for h in range(FX_HEADS)],
               scores, update)
    for h in range(FX_HEADS):
        hd = slice(h * FX_HEAD_DIM, (h + 1) * FX_HEAD_DIM)
        y_ref[:, hd] = (acc_scr[h] * (1.0 / l_scr[h])).T.astype(BF16)


def _fox(z3, vt4, caug3, rows3):
    nb, s, _ = z3.shape
    nq = s // TQ
    return pl.pallas_call(
        _fox_body,
        grid=(nb, nq),
        in_specs=[
            pl.BlockSpec((None, TQ, D_MODEL), lambda b, i: (b, i, Z_FX_Q)),
            pl.BlockSpec((None, s, D_MODEL), lambda b, i: (b, 0, Z_FX_K)),
            pl.BlockSpec((None, s // TK, D_MODEL, TK), lambda b, i: (b, 0, 1, 0)),
            pl.BlockSpec((None, s, LANES), lambda b, i: (b, 0, 0)),
            pl.BlockSpec((None, N_GATES, TQ), lambda b, i: (b, 0, i)),
        ],
        out_specs=pl.BlockSpec((None, TQ, D_MODEL), lambda b, i: (b, i, 0)),
        out_shape=jax.ShapeDtypeStruct((nb, s, D_MODEL), BF16),
        scratch_shapes=[pltpu.VMEM((FX_HEADS, TQ, 2 * FX_HEAD_DIM), BF16),
                        pltpu.VMEM((FX_HEADS, FX_HEAD_DIM, TQ), F32),
                        pltpu.VMEM((FX_HEADS, 1, TQ), F32), pltpu.VMEM((FX_HEADS, 1, TQ), F32)],
        compiler_params=_params(("parallel", "arbitrary")),
        name="fox",
    )(z3, z3, vt4, caug3, rows3)


def _merge_body(x_ref, yml_ref, yfx_ref, gml_ref, gfx_ref, bml_ref, bfx_ref, wml_ref, wfx_ref, wout_ref, x1_ref):
    p_ml = jnp.dot(yml_ref[...], wml_ref[...], preferred_element_type=F32)
    p_fx = jnp.dot(yfx_ref[...], wfx_ref[...], preferred_element_type=F32)
    merged = (jax.nn.sigmoid(gml_ref[...].astype(F32) + bml_ref[...]) * p_ml
              + jax.nn.sigmoid(gfx_ref[...].astype(F32) + bfx_ref[...]) * p_fx)
    x1_ref[...] = x_ref[...] + jnp.dot(merged.astype(BF16), wout_ref[...], preferred_element_type=F32)


def _merge(x2d, yml, yfx, z2d, bml, bfx, wml, wfx, wout):
    t = x2d.shape[0]
    tile = lambda col: pl.BlockSpec((TM, D_MODEL), lambda i, col=col: (i, col))
    return pl.pallas_call(
        _merge_body,
        grid=(t // TM,),
        in_specs=[tile(0), tile(0), tile(0), tile(Z_GT_ML), tile(Z_GT_FX),
                  _resident((1, D_MODEL)), _resident((1, D_MODEL)),
                  _resident((D_MODEL, D_MODEL)), _resident((D_MODEL, D_MODEL)), _resident((D_MODEL, D_MODEL))],
        out_specs=tile(0),
        out_shape=jax.ShapeDtypeStruct((t, D_MODEL), F32),
        compiler_params=_params(("parallel",)),
        name="merge",
    )(x2d, yml, yfx, z2d, z2d, bml, bfx, wml, wfx, wout)


def _memkv_body(m_ref, g_ref, w_ref, kv_ref):
    hb = _rms(m_ref[...], g_ref[...]).astype(BF16)
    kv_ref[...] = jnp.dot(hb, w_ref[...], preferred_element_type=F32).astype(BF16)


def _memkv(mem2d, g, w):
    t = mem2d.shape[0]
    return pl.pallas_call(
        _memkv_body,
        grid=(t // TM,),
        in_specs=[pl.BlockSpec((TM, D_MODEL), lambda i: (i, 0)), _resident((1, D_MODEL)),
                  _resident((D_MODEL, 2 * D_MODEL))],
        out_specs=pl.BlockSpec((TM, 2 * D_MODEL), lambda i: (i, 0)),
        out_shape=jax.ShapeDtypeStruct((t, 2 * D_MODEL), BF16),
        compiler_params=_params(("parallel",)),
        name="memkv",
    )(mem2d, g, w)


def _route_t(lg_t):
    tm = lg_t.shape[1]
    ninf = -jnp.inf
    big = jnp.float32(LANES)
    gid = lax.broadcasted_iota(jnp.int32, (8, tm), 0).astype(F32)
    eid = lax.broadcasted_iota(jnp.int32, (N_EXPERTS, tm), 0).astype(F32)
    gl = jnp.where(gid < N_GROUPS, lg_t[N_EXPERTS:N_EXPERTS + 8], ninf)
    gmax = jnp.max(gl, axis=0, keepdims=True)
    gidx = jnp.min(jnp.where(gl == gmax, gid, big), axis=0, keepdims=True)
    g_p = 1.0 / jnp.sum(jnp.exp(gl - gmax), axis=0, keepdims=True)
    lo = gidx * EXPERTS_PER_GROUP
    el = jnp.where(eid >= lo, jnp.where(eid < lo + EXPERTS_PER_GROUP, lg_t[0:N_EXPERTS], ninf), ninf)
    v1 = jnp.max(el, axis=0, keepdims=True)
    i1 = jnp.min(jnp.where(el == v1, eid, big), axis=0, keepdims=True)
    el2 = jnp.where(eid == i1, ninf, el)
    v2 = jnp.max(el2, axis=0, keepdims=True)
    i2 = jnp.min(jnp.where(el2 == v2, eid, big), axis=0, keepdims=True)
    t = jnp.exp(v2 - v1)
    w1 = g_p / (1.0 + t)
    w2 = w1 * t
    return jnp.concatenate([i1, i2, w1, w2, jnp.zeros((4, tm), F32)], axis=0)


def _xattn_body(x1_ref, kv_ref, gq_ref, wq_ref, wo_ref, gm_ref, wrt_ref, brt_ref, x2_ref, h3_ref, route_ref,
                wcol_ref):
    x1 = x1_ref[...]
    hb = _rms(x1, gq_ref[...]).astype(BF16)
    q = (jnp.dot(hb, wq_ref[...], preferred_element_type=F32) * (XA_HEAD_DIM ** -0.5)).astype(BF16)
    head = lambda h: slice(h * XA_HEAD_DIM, (h + 1) * XA_HEAD_DIM)
    scores = [_dot_nt(q[:, head(h)], kv_ref[:, head(h)]) for h in range(XA_HEADS)]
    outs = []
    for h in range(XA_HEADS):
        vd = slice(D_MODEL + h * XA_HEAD_DIM, D_MODEL + (h + 1) * XA_HEAD_DIM)
        s = scores[h]
        p = jnp.exp(s - jnp.max(s, axis=1, keepdims=True))
        p = p * (1.0 / jnp.sum(p, axis=1, keepdims=True))
        outs.append(jnp.dot(p.astype(BF16), kv_ref[:, vd], preferred_element_type=F32).astype(BF16))
    o = jnp.concatenate(outs, axis=1)
    x2 = x1 + jnp.dot(o, wo_ref[...], preferred_element_type=F32)
    x2_ref[...] = x2
    h3 = _rms(x2, gm_ref[...])
    h3_ref[...] = _pack_halves(h3)
    lg_t = _dot_nt_split(wrt_ref[...], h3) + brt_ref[...]
    route = _route_t(lg_t)
    route_ref[...] = route
    wcol_ref[...] = jnp.concatenate([route, jnp.zeros((LANES - 8, route.shape[1]), F32)], axis=0).T


def _xattn(x1, kv3, gq, wq, wo, gm, wrt, brt, seq):
    t = x1.shape[0]
    per_b = seq // TM
    n_mem = kv3.shape[1]
    tile = pl.BlockSpec((TM, D_MODEL), lambda i: (i, 0))
    return pl.pallas_call(
        _xattn_body,
        grid=(t // TM,),
        in_specs=[tile,
                  pl.BlockSpec((None, n_mem, 2 * D_MODEL), lambda i: (i // per_b, 0, 0)),
                  _resident((1, D_MODEL)), _resident((D_MODEL, D_MODEL)), _resident((D_MODEL, D_MODEL)),
                  _resident((1, D_MODEL)), _resident((ROUTER_ROWS, D_MODEL)), _resident((ROUTER_ROWS, 1))],
        out_specs=[tile, pl.BlockSpec((TM, PACKED_W), lambda i: (i, 0)), pl.BlockSpec((8, TM), lambda i: (0, i)),
                   pl.BlockSpec((TM, LANES), lambda i: (i, 0))],
        out_shape=[jax.ShapeDtypeStruct((t, D_MODEL), F32), jax.ShapeDtypeStruct((t, PACKED_W), F32),
                   jax.ShapeDtypeStruct((8, t), F32), jax.ShapeDtypeStruct((t, LANES), F32)],
        compiler_params=_params(("parallel",)),
        name="xattn",
    )(x1, kv3, gq, wq, wo, gm, wrt, brt)


def _n_expert_tiles(n_tokens):
    return 2 * n_tokens // TME + N_EXPERTS


def _plan_body(route_ref, pos_ref, tinfo_ref, tok_ref, cnt_scr, run_scr, start_scr, tokhi_scr, toklo_scr, *, nt_pad,
               nt_rows, n_tokens):
    phase = pl.program_id(0)
    b = pl.program_id(1)
    r = route_ref[...]
    eid = lax.broadcasted_iota(jnp.int32, (N_EXPERTS, TP), 0).astype(F32)
    oh1 = eid == r[0:1]
    oh2 = eid == r[1:2]
    oh = jnp.where(oh1 | oh2, 1.0, 0.0)

    @pl.when((phase == 0) & (b == 0))
    def _():
        cnt_scr[...] = jnp.zeros_like(cnt_scr)

    @pl.when(phase == 0)
    def _():
        cnt_scr[...] += jnp.sum(oh, axis=1, keepdims=True)

    @pl.when((phase == 1) & (b == 0))
    def _():
        cnt = cnt_scr[...]
        n_tiles = jnp.floor((cnt + (TME - 1)) * (1.0 / TME))
        ri = lax.broadcasted_iota(jnp.int32, (N_EXPERTS, N_EXPERTS), 0)
        ci = lax.broadcasted_iota(jnp.int32, (N_EXPERTS, N_EXPERTS), 1)
        lower = jnp.where(ci < ri, 1.0, 0.0).astype(BF16)
        nt_hi, nt_lo = _split_bf16(jnp.broadcast_to(n_tiles, (N_EXPERTS, LANES)))
        start = (jnp.dot(lower, nt_hi, preferred_element_type=F32)
                 + jnp.dot(lower, nt_lo, preferred_element_type=F32))[:, 0:1]
        start_scr[...] = start * TME
        run_scr[...] = jnp.zeros_like(run_scr)
        tokhi_scr[...] = jnp.zeros_like(tokhi_scr)
        toklo_scr[...] = jnp.zeros_like(toklo_scr)
        n = lax.broadcasted_iota(jnp.int32, (N_EXPERTS, nt_pad), 1).astype(F32)
        e_n = lax.broadcasted_iota(jnp.int32, (N_EXPERTS, nt_pad), 0).astype(F32)
        owner = jnp.sum(jnp.where(start <= n, 1.0, 0.0), axis=0, keepdims=True) - 1.0
        own = e_n == owner
        cnt_o = jnp.sum(jnp.where(own, cnt, 0.0), axis=0, keepdims=True)
        start_o = jnp.sum(jnp.where(own, start, 0.0), axis=0, keepdims=True)
        valid = jnp.clip(cnt_o - (n[0:1] - start_o) * TME, 0.0, float(TME))
        tinfo_ref[...] = jnp.concatenate([owner, valid, jnp.zeros((6, nt_pad), F32)], axis=0).astype(jnp.int32)

    @pl.when(phase == 1)
    def _():
        ti = lax.broadcasted_iota(jnp.int32, (TP, TP), 0)
        tj = lax.broadcasted_iota(jnp.int32, (TP, TP), 1)
        upper = jnp.where(ti < tj, 1.0, 0.0).astype(BF16)
        before = jnp.dot(oh.astype(BF16), upper, preferred_element_type=F32)
        row = start_scr[...] + run_scr[...] + before
        p1 = jnp.sum(jnp.where(oh1, row, 0.0), axis=0, keepdims=True)
        p2 = jnp.sum(jnp.where(oh2, row, 0.0), axis=0, keepdims=True)
        pos_ref[...] = jnp.concatenate([p1, p2, jnp.zeros((6, TP), F32)], axis=0).astype(jnp.int32)
        run_scr[...] += jnp.sum(oh, axis=1, keepdims=True)

        tid = lax.broadcasted_iota(jnp.int32, (1, TP), 1) + (b * TP + 1)
        t_hi = (tid // 256).astype(F32)
        t_lo = (tid % 256).astype(F32)
        tile_id = lax.broadcasted_iota(jnp.int32, (nt_rows, TP), 0).astype(F32)
        row_id = lax.broadcasted_iota(jnp.int32, (TME, TP), 0).astype(F32)
        for p in (p1, p2):
            hi = jnp.floor(p * (1.0 / TME))
            lo = p - hi * TME
            in_tile = jnp.where(tile_id == hi, 1.0, 0.0).astype(BF16)
            at_row = row_id == lo
            tokhi_scr[...] += _dot_nt(in_tile, jnp.where(at_row, t_hi, 0.0).astype(BF16))
            toklo_scr[...] += _dot_nt(in_tile, jnp.where(at_row, t_lo, 0.0).astype(BF16))

    @pl.when((phase == 1) & (b == pl.num_programs(1) - 1))
    def _():
        hit = (tokhi_scr[...] * 256.0 + toklo_scr[...]).astype(jnp.int32)
        row = (lax.broadcasted_iota(jnp.int32, hit.shape, 0) * TME + lax.broadcasted_iota(jnp.int32, hit.shape, 1))
        tok_ref[...] = jnp.where(hit > 0, hit - 1, row % n_tokens)


def _plan(route):
    t = route.shape[1]
    assert t < 256 * 256, "token id + 1 is carried as two base-256 digits"
    nt_pad = -(-_n_expert_tiles(t) // LANES) * LANES
    nt_rows = -(-_n_expert_tiles(t) // 8) * 8
    col = pltpu.VMEM((N_EXPERTS, 1), F32)
    table = pltpu.VMEM((nt_rows, TME), F32)
    return pl.pallas_call(
        functools.partial(_plan_body, nt_pad=nt_pad, nt_rows=nt_rows, n_tokens=t),
        grid=(2, t // TP),
        in_specs=[pl.BlockSpec((8, TP), lambda ph, b: (0, b))],
        out_specs=[pl.BlockSpec((8, TP), lambda ph, b: (0, b * ph)),
                   pl.BlockSpec((8, nt_pad), lambda ph, b: (0, 0)),
                   pl.BlockSpec((nt_rows, TME), lambda ph, b: (0, 0))],
        out_shape=[jax.ShapeDtypeStruct((8, t), jnp.int32), jax.ShapeDtypeStruct((8, nt_pad), jnp.int32),
                   jax.ShapeDtypeStruct((nt_rows, TME), jnp.int32)],
        scratch_shapes=[col, col, col, table, table],
        compiler_params=_params(("arbitrary", "arbitrary")),
        name="plan",
    )(route)


def _experts_body(texp_ref, tvalid_ref, xs_ref, wg_ref, wu_ref, wd_ref, ys_ref, wgb_scr, wub_scr, wdb_scr):
    n = pl.program_id(0)

    @pl.when((n == 0) | (texp_ref[n] != texp_ref[jnp.maximum(n - 1, 0)]))
    def _():
        wgb_scr[...] = wg_ref[...].astype(BF16)
        wub_scr[...] = wu_ref[...].astype(BF16)
        wdb_scr[...] = wd_ref[...].astype(BF16)

    @pl.when(tvalid_ref[n] > 0)
    def _():
        xb = _unpack_halves(xs_ref[...]).astype(BF16)
        gate = jnp.dot(xb, wgb_scr[...], preferred_element_type=F32)
        up = jnp.dot(xb, wub_scr[...], preferred_element_type=F32)
        he = (gate * jax.nn.sigmoid(gate) * up).astype(BF16)
        ys_ref[...] = _pack_halves(jnp.dot(he, wdb_scr[...], preferred_element_type=F32))

    @pl.when(tvalid_ref[n] == 0)
    def _():
        ys_ref[...] = jnp.zeros_like(ys_ref)


def _experts(texp, tvalid, xs, wg, wu, wd):
    nt = texp.shape[0]
    weight = lambda shape: pl.BlockSpec((None,) + shape, lambda n, te, tv: (te[n], 0, 0))
    return pl.pallas_call(
        _experts_body,
        grid_spec=pltpu.PrefetchScalarGridSpec(
            num_scalar_prefetch=2,
            grid=(nt,),
            in_specs=[pl.BlockSpec((TME, PACKED_W), lambda n, te, tv: (jnp.where(tv[n] > 0, n, 0), 0)),
                      weight((D_MODEL, D_EXPERT)), weight((D_MODEL, D_EXPERT)), weight((D_EXPERT, D_MODEL))],
            out_specs=pl.BlockSpec((TME, PACKED_W), lambda n, te, tv: (n, 0)),
            scratch_shapes=[pltpu.VMEM((D_MODEL, D_EXPERT), BF16), pltpu.VMEM((D_MODEL, D_EXPERT), BF16),
                            pltpu.VMEM((D_EXPERT, D_MODEL), BF16)],
        ),
        out_shape=jax.ShapeDtypeStruct((nt * TME, PACKED_W), F32),
        compiler_params=_params(("arbitrary",)),
        name="experts",
    )(texp, tvalid, xs, wg, wu, wd)


def _rowgather(table, idx):
    n_rows, width = idx.shape[0], table.shape[1]
    n_workers = SC_CORES * SC_SUBCORES
    per_worker = n_rows // n_workers
    assert per_worker * n_workers == n_rows and per_worker % SC_ROWS == 0
    mesh = plsc.VectorSubcoreMesh(core_axis_name="c", subcore_axis_name="s", num_cores=SC_CORES,
                                  num_subcores=SC_SUBCORES)

    chunks = per_worker // SC_ROWS
    assert chunks % 2 == 0
    buf = lambda: [pltpu.VMEM((SC_ROWS,), jnp.int32), pltpu.VMEM((SC_ROWS, width), table.dtype),
                   pltpu.SemaphoreType.DMA]

    @functools.partial(pl.kernel, mesh=mesh, out_type=jax.ShapeDtypeStruct((n_rows, width), table.dtype),
                       scratch_types=buf() + buf(), name="rowgather")
    def gather(table_hbm, idx_hbm, out_hbm, idx_a, rows_a, sem_a, idx_b, rows_b, sem_b):
        worker = lax.axis_index("s") * SC_CORES + lax.axis_index("c")
        base = worker * per_worker

        def fetch(c, idx_v, rows_v, sem):
            pltpu.sync_copy(idx_hbm.at[pl.ds(base + c * SC_ROWS, SC_ROWS)], idx_v)
            pltpu.async_copy(table_hbm.at[idx_v], rows_v, sem)

        def flush(c, idx_v, rows_v, sem):
            pltpu.make_async_copy(table_hbm.at[idx_v], rows_v, sem).wait()
            pltpu.sync_copy(rows_v, out_hbm.at[pl.ds(base + c * SC_ROWS, SC_ROWS)])

        fetch(0, idx_a, rows_a, sem_a)

        @pl.loop(0, chunks, step=2)
        def _(c):
            fetch(c + 1, idx_b, rows_b, sem_b)
            flush(c, idx_a, rows_a, sem_a)

            @pl.when(c + 2 < chunks)
            def _():
                fetch(c + 2, idx_a, rows_a, sem_a)

            flush(c + 1, idx_b, rows_b, sem_b)

    return gather(table, idx)


def _combine_body(g1_ref, g2_ref, x2_ref, wcol_ref, gf_ref, out_ref):
    w = wcol_ref[...]
    y = w[:, 2:3] * _unpack_halves(g1_ref[...]) + w[:, 3:4] * _unpack_halves(g2_ref[...])
    out_ref[...] = _rms(x2_ref[...] + y, gf_ref[...])


def _combine(g, x2, wcol, gf):
    t = x2.shape[0]
    steps = t // TF
    tile = lambda w: pl.BlockSpec((TF, w), lambda i: (i, 0))
    return pl.pallas_call(
        _combine_body,
        grid=(steps,),
        in_specs=[tile(PACKED_W), pl.BlockSpec((TF, PACKED_W), lambda i: (i + steps, 0)),
                  tile(D_MODEL), tile(LANES), _resident((1, D_MODEL))],
        out_specs=tile(D_MODEL),
        out_shape=jax.ShapeDtypeStruct((t, D_MODEL), F32),
        compiler_params=_params(("parallel",)),
        name="combine",
    )(g, g, x2, wcol, gf)


def _layer(x, mem, g_mix, w_in, b_ml_i, b_ml_f, b_fx_f, b_gate_ml, b_gate_fx, g_ml_head, w_proj_ml, w_proj_fx,
           w_out, g_xq, g_xmem, w_xq, w_xkv, w_xo, g_moe, w_rg, b_rg, w_re, b_re, w_gate, w_up, w_down):
    nb, seq, d = x.shape
    t = nb * seq
    row = lambda v: v.reshape(1, -1).astype(F32)

    o = 0
    parts = {}
    for name, width in (("ml_q", 512), ("ml_k", 512), ("ml_v", 1024), ("ml_o", 1024), ("ml_i", 4), ("ml_f", 4),
                        ("fx_q", 1024), ("fx_k", 1024), ("fx_v", 1024), ("fx_f", 8), ("gt_ml", 1024),
                        ("gt_fx", 1024)):
        parts[name] = w_in[:, o:o + width]
        o += width
    w_main = jnp.concatenate(
        [parts["ml_q"], parts["ml_k"] * (ML_QK_DIM ** -0.5), parts["ml_o"],
         parts["fx_q"] * (FX_HEAD_DIM ** -0.5 * LOG2E), parts["fx_k"], parts["gt_ml"], parts["gt_fx"]],
        axis=1).astype(BF16)
    w_vt = jnp.concatenate([parts["ml_v"], parts["fx_v"]], axis=1).T.astype(BF16)
    w_gates_t = jnp.concatenate([parts["ml_i"], parts["ml_f"], parts["fx_f"]], axis=1).T
    gate_bias = jnp.concatenate([b_ml_i, b_ml_f, b_fx_f]).reshape(N_GATES, 1).astype(F32)

    x2d = x.reshape(t, d)
    z, vt, gt3 = _inproj(x2d, row(g_mix), w_main, w_vt, w_gates_t, seq)
    rows, cols, caug = _gateprep(gt3, gate_bias)
    z3 = z.reshape(nb, seq, Z_W)
    vt4 = vt.reshape(nb, seq // TK, VT_W, TK)
    y_ml = _mlstm(z3, vt4, cols, rows, row(g_ml_head))
    y_fx = _fox(z3, vt4, caug, rows)
    x1 = _merge(x2d, y_ml.reshape(t, d), y_fx.reshape(t, d), z, row(b_gate_ml), row(b_gate_fx),
                w_proj_ml.astype(BF16), w_proj_fx.astype(BF16), w_out.astype(BF16))

    n_mem = mem.shape[1]
    kv = _memkv(mem.reshape(nb * n_mem, d), row(g_xmem), w_xkv.astype(BF16))
    w_router_t = jnp.concatenate([w_re, w_rg, jnp.zeros((d, ROUTER_ROWS - N_EXPERTS - N_GROUPS), F32)], axis=1).T
    b_router_t = jnp.concatenate([b_re, b_rg, jnp.zeros((ROUTER_ROWS - N_EXPERTS - N_GROUPS,), F32)]).reshape(ROUTER_ROWS, 1)
    x2, h3, route, wcol = _xattn(x1, kv.reshape(nb, n_mem, 2 * d), row(g_xq), w_xq.astype(BF16),
                                 w_xo.astype(BF16), row(g_moe), w_router_t, b_router_t, seq)
    return x2, h3, route, wcol


def _moe(x2, h3, route, wcol, w_gate, w_up, w_down, g_final):
    nt = _n_expert_tiles(x2.shape[0])
    pos, tinfo, tok = _plan(route)
    texp, tvalid = tinfo[0, :nt], tinfo[1, :nt]
    xs = _rowgather(h3, tok[:nt].reshape(-1))
    ys = _experts(texp, tvalid, xs, w_gate, w_up, w_down)
    g = _rowgather(ys, pos[0:2].reshape(-1))
    return _combine(g, x2, wcol, g_final)


def kernel(x, mem, g_mix, w_in, b_ml_i, b_ml_f, b_fx_f, b_gate_ml, b_gate_fx, g_ml_head, w_proj_ml, w_proj_fx, w_out, g_xq, g_xmem, w_xq, w_xkv, w_xo, g_moe, w_rg, b_rg, w_re, b_re, w_gate, w_up, w_down, g_final):
    nb, seq, d = x.shape
    depth = g_mix.shape[0]
    assert depth == 1, "the final rmsnorm is fused into the (single) layer's combine kernel"
    x2, h3, route, wcol = _layer(
        x, mem, g_mix[0], w_in[0], b_ml_i[0], b_ml_f[0], b_fx_f[0], b_gate_ml[0], b_gate_fx[0], g_ml_head[0],
        w_proj_ml[0], w_proj_fx[0], w_out[0], g_xq[0], g_xmem[0], w_xq[0], w_xkv[0], w_xo[0], g_moe[0],
        w_rg[0], b_rg[0], w_re[0], b_re[0], w_gate[0], w_up[0], w_down[0])
    out = _moe(x2, h3, route, wcol, w_gate[0], w_up[0], w_down[0], g_final.reshape(1, d).astype(F32))
    return out.reshape(nb, seq, d)
```

```python
import functools

import jax
import jax.numpy as jnp
from jax import lax
from jax.experimental import pallas as pl
from jax.experimental.pallas import tpu as pltpu
from jax.experimental.pallas import tpu_sc as plsc

F32 = jnp.float32
BF16 = jnp.bfloat16

D_MODEL = 1024
EPS = 1e-6
ML_HEADS = 4
ML_QK_DIM = 128
ML_V_DIM = 256
FX_HEADS = 8
FX_HEAD_DIM = 128
XA_HEADS = 4
XA_HEAD_DIM = 256
N_GROUPS = 4
EXPERTS_PER_GROUP = 8
N_EXPERTS = 32
D_EXPERT = 512

LANES = 128
N_GATES = 16
ROUTER_ROWS = 48
LOG2E = 1.4426950408889634
Z_W = 6144
Z_ML_O, Z_FX_Q, Z_FX_K, Z_GT_ML, Z_GT_FX = 1, 2, 3, 4, 5
VT_W = 2048

VMEM_LIMIT = 56 * 1024 * 1024

IN_CHUNK = 1024
TK = 256
TQ = 512
TIN = 512
KT_PER_Q = TQ // TK
MXU_LOOKAHEAD = 16
TM = 1024
TME = 512
TP = 512
TF = 1024
SC_CORES, SC_SUBCORES = 2, 16
PACKED_W = D_MODEL // 2
SC_ROWS = 64


def _params(sem, flags=None):
    return pltpu.CompilerParams(dimension_semantics=sem, vmem_limit_bytes=VMEM_LIMIT, flags=flags)


def _rms(x, g):
    return x * lax.rsqrt(jnp.mean(x * x, axis=-1, keepdims=True) + EPS) * g


def _dot_nt(a, b, **kw):
    return lax.dot_general(a, b, (((1,), (1,)), ((), ())), preferred_element_type=F32, **kw)


def _split_bf16(x):
    hi = x.astype(BF16)
    return hi, (x - hi.astype(F32)).astype(BF16)


def _dot_nt_split(a, b):
    a_hi, a_lo = _split_bf16(a)
    b_hi, b_lo = _split_bf16(b)
    m = a.shape[0]
    by_hi = _dot_nt(jnp.concatenate([a_hi, a_lo], axis=0), b_hi)
    return by_hi[:m] + (_dot_nt(a_hi, b_lo) + by_hi[m:])


def _pack_halves(x):
    n = x.shape[1] // 2
    bits = lambda v: lax.bitcast_convert_type(v.astype(BF16).astype(F32), jnp.uint32)
    w = (bits(x[:, :n]) >> 16) | (bits(x[:, n:]) & jnp.uint32(0xFFFF0000))
    return lax.bitcast_convert_type(w, F32)


def _unpack_halves(p):
    w = lax.bitcast_convert_type(p, jnp.uint32)
    lo = lax.bitcast_convert_type(w << 16, F32)
    hi = lax.bitcast_convert_type(w & jnp.uint32(0xFFFF0000), F32)
    return jnp.concatenate([lo, hi], axis=1)


def _resident(shape):
    zeros = (0,) * len(shape)
    return pl.BlockSpec(shape, lambda *_: zeros, pipeline_mode=pl.Buffered(1))


def _inproj_body(x_ref, g_ref, w_ref, wvt_ref, wgt_ref, z_ref, vt_ref, gt_ref):
    h = _rms(x_ref[...], g_ref[...])
    hb = h.astype(BF16)
    for c in range(Z_W // IN_CHUNK):
        sl = slice(c * IN_CHUNK, (c + 1) * IN_CHUNK)
        z_ref[:, sl] = jnp.dot(hb, w_ref[:, sl], preferred_element_type=F32).astype(BF16)
    for kt in range(TIN // TK):
        hk = hb[kt * TK:(kt + 1) * TK]
        for c in range(VT_W // IN_CHUNK):
            sl = slice(c * IN_CHUNK, (c + 1) * IN_CHUNK)
            vt_ref[kt, sl, :] = _dot_nt(wvt_ref[sl, :], hk).astype(BF16)
    gt_ref[...] = _dot_nt_split(wgt_ref[...], h)


def _inproj(x2d, g, w, wvt, wgt, seq):
    t = x2d.shape[0]
    per_b = seq // TIN
    return pl.pallas_call(
        _inproj_body,
        grid=(t // TIN,),
        in_specs=[
            pl.BlockSpec((TIN, D_MODEL), lambda i: (i, 0)),
            _resident((1, D_MODEL)),
            _resident((D_MODEL, Z_W)),
            _resident((VT_W, D_MODEL)),
            _resident((N_GATES, D_MODEL)),
        ],
        out_specs=[
            pl.BlockSpec((TIN, Z_W), lambda i: (i, 0)),
            pl.BlockSpec((TIN // TK, VT_W, TK), lambda i: (i, 0, 0)),
            pl.BlockSpec((None, N_GATES, TIN), lambda i: (i // per_b, 0, i % per_b)),
        ],
        out_shape=[jax.ShapeDtypeStruct((t, Z_W), BF16), jax.ShapeDtypeStruct((t // TK, VT_W, TK), BF16),
                   jax.ShapeDtypeStruct((t // seq, N_GATES, seq), F32)],
        compiler_params=_params(("parallel",)),
        name="inproj",
    )(x2d, g, w, wvt, wgt)


def _scan_lanes(x, op, identity):
    n = x.shape[-1]
    idx = lax.broadcasted_iota(jnp.int32, x.shape, 1)
    s = 1
    while s < n:
        shifted = pltpu.roll(x, s, axis=1)
        x = op(x, jnp.where(idx >= s, shifted, identity))
        s *= 2
    return x


def _log_sigmoid(x):
    return jnp.minimum(x, 0.0) - jnp.log1p(jnp.exp(-jnp.abs(x)))


def _gateprep_body(gt_ref, bias_ref, rows_ref, cols_ref, caug_ref):
    g = gt_ref[...] + bias_ref[...]
    s = g.shape[1]
    cs = _scan_lanes(_log_sigmoid(g), jnp.add, 0.0)
    b = cs[4:8]
    c2 = cs[8:16] * LOG2E
    a = g[0:4] - b
    m = _scan_lanes(jnp.concatenate([a, a], axis=0), jnp.maximum, -jnp.inf)[0:4]
    rows_ref[...] = jnp.concatenate([m * LOG2E, b + m, c2], axis=0)
    cols_ref[...] = jnp.concatenate([a * LOG2E, jnp.zeros((LANES - ML_HEADS, s), F32)], axis=0).T
    hi = c2.astype(BF16).astype(F32)
    r1 = c2 - hi
    mid = r1.astype(BF16).astype(F32)
    lo = r1 - mid
    aug = jnp.concatenate([-hi, -mid, -lo, jnp.zeros((LANES - 3 * FX_HEADS, s), F32)], axis=0)
    caug_ref[...] = aug.T.astype(BF16)


def _gateprep(gt3, bias):
    nb, _, s = gt3.shape
    return pl.pallas_call(
        _gateprep_body,
        grid=(nb,),
        in_specs=[
            pl.BlockSpec((None, N_GATES, s), lambda b: (b, 0, 0)),
            _resident((N_GATES, 1)),
        ],
        out_specs=[
            pl.BlockSpec((None, N_GATES, s), lambda b: (b, 0, 0)),
            pl.BlockSpec((None, s, LANES), lambda b: (b, 0, 0)),
            pl.BlockSpec((None, s, LANES), lambda b: (b, 0, 0)),
        ],
        out_shape=[jax.ShapeDtypeStruct((nb, N_GATES, s), F32), jax.ShapeDtypeStruct((nb, s, LANES), F32),
                   jax.ShapeDtypeStruct((nb, s, LANES), BF16)],
        compiler_params=_params(("parallel",)),
        name="gateprep",
    )(gt3, bias)


def _pipelined(stages, scores, update):
    pending = {k: scores(*stages[k]) for k in range(min(MXU_LOOKAHEAD, len(stages)))}
    for k, stage in enumerate(stages):
        if k + MXU_LOOKAHEAD < len(stages):
            pending[k + MXU_LOOKAHEAD] = scores(*stages[k + MXU_LOOKAHEAD])
        update(*stage, pending.pop(k))


def _causal_mask_t(d):
    s = lax.broadcasted_iota(jnp.int32, (TK, TQ - d * TK), 0)
    t = lax.broadcasted_iota(jnp.int32, (TK, TQ - d * TK), 1)
    return s <= t


def _mlstm_body(q_ref, k_ref, vt_ref, o_ref, cols_ref, rows_ref, gh_ref, y_ref, num_scr, den_scr):
    i = pl.program_id(1)
    rows = rows_ref[...]
    num_scr[...] = jnp.zeros_like(num_scr)
    den_scr[...] = jnp.zeros_like(den_scr)

    def scores(j, h, mask, q0):
        ks = pl.ds(pl.multiple_of(j * TK, TK), TK)
        qk = slice(h * ML_QK_DIM, (h + 1) * ML_QK_DIM)
        return _dot_nt(k_ref[ks, qk], q_ref[q0:, qk])

    def update(j, h, mask, q0, s):
        ks = pl.ds(pl.multiple_of(j * TK, TK), TK)
        vv = slice(h * ML_V_DIM, (h + 1) * ML_V_DIM)
        w = jnp.exp2(cols_ref[ks, h:h + 1] - rows[h:h + 1, q0:])
        if mask is not None:
            w = jnp.where(mask, w, 0.0)
        s = s * w
        den_scr[h, :, q0:] += jnp.sum(s, axis=0, keepdims=True)
        num_scr[h, :, q0:] += jnp.dot(vt_ref[j, vv, :], s.astype(BF16), preferred_element_type=F32)

    def body(jj, carry):
        _pipelined([(jj * KT_PER_Q + d, h, None, 0) for d in range(KT_PER_Q) for h in range(ML_HEADS)], scores, update)
        return carry

    lax.fori_loop(0, i, body, 0)
    masks = [_causal_mask_t(d) for d in range(KT_PER_Q)]
    _pipelined([(i * KT_PER_Q + d, h, masks[d], d * TK) for d in range(KT_PER_Q) for h in range(ML_HEADS)],
               scores, update)
    for h in range(ML_HEADS):
        vv = slice(h * ML_V_DIM, (h + 1) * ML_V_DIM)
        floor = jnp.exp(-rows[ML_HEADS + h:ML_HEADS + h + 1])
        hh = num_scr[h] * (1.0 / jnp.maximum(jnp.abs(den_scr[h]), floor))
        yt = hh * lax.rsqrt(jnp.mean(hh * hh, axis=0, keepdims=True) + EPS)
        y = yt.T * gh_ref[:, vv]
        y_ref[:, vv] = (y * jax.nn.sigmoid(o_ref[:, vv].astype(F32))).astype(BF16)


def _mlstm(z3, vt4, cols3, rows3, g_head):
    nb, s, _ = z3.shape
    nq = s // TQ
    return pl.pallas_call(
        _mlstm_body,
        grid=(nb, nq),
        in_specs=[
            pl.BlockSpec((None, TQ, ML_HEADS * ML_QK_DIM), lambda b, i: (b, i, 0)),
            pl.BlockSpec((None, s, ML_HEADS * ML_QK_DIM), lambda b, i: (b, 0, 1)),
            pl.BlockSpec((None, s // TK, D_MODEL, TK), lambda b, i: (b, 0, 0, 0)),
            pl.BlockSpec((None, TQ, D_MODEL), lambda b, i: (b, i, Z_ML_O)),
            pl.BlockSpec((None, s, LANES), lambda b, i: (b, 0, 0)),
            pl.BlockSpec((None, N_GATES, TQ), lambda b, i: (b, 0, i)),
            _resident((1, D_MODEL)),
        ],
        out_specs=pl.BlockSpec((None, TQ, D_MODEL), lambda b, i: (b, i, 0)),
        out_shape=jax.ShapeDtypeStruct((nb, s, D_MODEL), BF16),
        scratch_shapes=[pltpu.VMEM((ML_HEADS, ML_V_DIM, TQ), F32), pltpu.VMEM((ML_HEADS, 1, TQ), F32)],
        compiler_params=_params(("parallel", "arbitrary")),
        name="mlstm",
    )(z3, z3, vt4, z3, cols3, rows3, g_head)


def _fox_body(q_ref, k_ref, vt_ref, caug_ref, rows_ref, y_ref, qa_scr, acc_scr, m_scr, l_scr):
    i = pl.program_id(1)
    rows = rows_ref[...]
    lane = lax.broadcasted_iota(jnp.int32, (TQ, LANES), 1)
    for h in range(FX_HEADS):
        hd = slice(h * FX_HEAD_DIM, (h + 1) * FX_HEAD_DIM)
        ones = jnp.where((lane < 3 * FX_HEADS) & (lane % FX_HEADS == h), 1.0, 0.0).astype(BF16)
        qa_scr[h] = jnp.concatenate([q_ref[:, hd], ones], axis=1)
    m_scr[...] = jnp.full_like(m_scr, -jnp.inf)
    l_scr[...] = jnp.zeros_like(l_scr)
    acc_scr[...] = jnp.zeros_like(acc_scr)

    def scores(j, h, mask, q0):
        ks = pl.ds(pl.multiple_of(j * TK, TK), TK)
        hd = slice(h * FX_HEAD_DIM, (h + 1) * FX_HEAD_DIM)
        k_aug = jnp.concatenate([k_ref[ks, hd], caug_ref[ks, :]], axis=1)
        return _dot_nt(k_aug, qa_scr[h, q0:, :])

    def update(j, h, mask, q0, u):
        hd = slice(h * FX_HEAD_DIM, (h + 1) * FX_HEAD_DIM)
        if mask is not None:
            u = jnp.where(mask, u, -jnp.inf)
        c_row = rows[2 * ML_HEADS + h:2 * ML_HEADS + h + 1, q0:]
        m_prev = m_scr[h, :, q0:]
        m_new = jnp.maximum(m_prev, jnp.max(u, axis=0, keepdims=True) + c_row)
        p = jnp.exp2(u - (m_new - c_row))
        alpha = jnp.exp2(m_prev - m_new)
        l_scr[h, :, q0:] = alpha * l_scr[h, :, q0:] + jnp.sum(p, axis=0, keepdims=True)
        acc_scr[h, :, q0:] = alpha * acc_scr[h, :, q0:] + jnp.dot(vt_ref[j, hd, :], p.astype(BF16),
                                                                  preferred_element_type=F32)
        m_scr[h, :, q0:] = m_new

    def body(jj, carry):
        _pipelined([(jj * KT_PER_Q + d, h, None, 0) for d in range(KT_PER_Q) for h in range(FX_HEADS)], scores, update)
        return carry

    lax.fori_loop(0, i, body, 0)
    masks = [_causal_mask_t(d) for d in range(KT_PER_Q)]
    _pipelined([(i * KT_PER_Q + d, h, masks[d], d * TK) for d in range(KT_PER_Q) for h in range(FX_HEADS)],
               scores, update)
    for h in range(FX_HEADS):
        hd = slice(h * FX_HEAD_DIM, (h + 1) * FX_HEAD_DIM)
        y_ref[:, hd] = (acc_scr[h] * (1.0 / l_scr[h])).T.astype(BF16)


def _fox(z3, vt4, caug3, rows3):
    nb, s, _ = z3.shape
    nq = s // TQ
    return pl.pallas_call(
        _fox_body,
        grid=(nb, nq),
        in_specs=[
            pl.BlockSpec((None, TQ, D_MODEL), lambda b, i: (b, i, Z_FX_Q)),
            pl.BlockSpec((None, s, D_MODEL), lambda b, i: (b, 0, Z_FX_K)),
            pl.BlockSpec((None, s // TK, D_MODEL, TK), lambda b, i: (b, 0, 1, 0)),
            pl.BlockSpec((None, s, LANES), lambda b, i: (b, 0, 0)),
            pl.BlockSpec((None, N_GATES, TQ), lambda b, i: (b, 0, i)),
        ],
        out_specs=pl.BlockSpec((None, TQ, D_MODEL), lambda b, i: (b, i, 0)),
        out_shape=jax.ShapeDtypeStruct((nb, s, D_MODEL), BF16),
        scratch_shapes=[pltpu.VMEM((FX_HEADS, TQ, 2 * FX_HEAD_DIM), BF16),
                        pltpu.VMEM((FX_HEADS, FX_HEAD_DIM, TQ), F32),
                        pltpu.VMEM((FX_HEADS, 1, TQ), F32), pltpu.VMEM((FX_HEADS, 1, TQ), F32)],
        compiler_params=_params(("parallel", "arbitrary")),
        name="fox",
    )(z3, z3, vt4, caug3, rows3)


def _merge_body(x_ref, yml_ref, yfx_ref, gml_ref, gfx_ref, bml_ref, bfx_ref, wml_ref, wfx_ref, wout_ref, x1_ref):
    p_ml = jnp.dot(yml_ref[...], wml_ref[...], preferred_element_type=F32)
    p_fx = jnp.dot(yfx_ref[...], wfx_ref[...], preferred_element_type=F32)
    merged = (jax.nn.sigmoid(gml_ref[...].astype(F32) + bml_ref[...]) * p_ml
              + jax.nn.sigmoid(gfx_ref[...].astype(F32) + bfx_ref[...]) * p_fx)
    x1_ref[...] = x_ref[...] + jnp.dot(merged.astype(BF16), wout_ref[...], preferred_element_type=F32)


def _merge(x2d, yml, yfx, z2d, bml, bfx, wml, wfx, wout):
    t = x2d.shape[0]
    tile = lambda col: pl.BlockSpec((TM, D_MODEL), lambda i, col=col: (i, col))
    return pl.pallas_call(
        _merge_body,
        grid=(t // TM,),
        in_specs=[tile(0), tile(0), tile(0), tile(Z_GT_ML), tile(Z_GT_FX),
                  _resident((1, D_MODEL)), _resident((1, D_MODEL)),
                  _resident((D_MODEL, D_MODEL)), _resident((D_MODEL, D_MODEL)), _resident((D_MODEL, D_MODEL))],
        out_specs=tile(0),
        out_shape=jax.ShapeDtypeStruct((t, D_MODEL), F32),
        compiler_params=_params(("parallel",)),
        name="merge",
    )(x2d, yml, yfx, z2d, z2d, bml, bfx, wml, wfx, wout)


def _memkv_body(m_ref, g_ref, w_ref, kv_ref):
    hb = _rms(m_ref[...], g_ref[...]).astype(BF16)
    kv_ref[...] = jnp.dot(hb, w_ref[...], preferred_element_type=F32).astype(BF16)


def _memkv(mem2d, g, w):
    t = mem2d.shape[0]
    return pl.pallas_call(
        _memkv_body,
        grid=(t // TM,),
        in_specs=[pl.BlockSpec((TM, D_MODEL), lambda i: (i, 0)), _resident((1, D_MODEL)),
                  _resident((D_MODEL, 2 * D_MODEL))],
        out_specs=pl.BlockSpec((TM, 2 * D_MODEL), lambda i: (i, 0)),
        out_shape=jax.ShapeDtypeStruct((t, 2 * D_MODEL), BF16),
        compiler_params=_params(("parallel",)),
        name="memkv",
    )(mem2d, g, w)


def _route_t(lg_t):
    tm = lg_t.shape[1]
    ninf = -jnp.inf
    big = jnp.float32(LANES)
    gid = lax.broadcasted_iota(jnp.int32, (8, tm), 0).astype(F32)
    eid = lax.broadcasted_iota(jnp.int32, (N_EXPERTS, tm), 0).astype(F32)
    gl = jnp.where(gid < N_GROUPS, lg_t[N_EXPERTS:N_EXPERTS + 8], ninf)
    gmax = jnp.max(gl, axis=0, keepdims=True)
    gidx = jnp.min(jnp.where(gl == gmax, gid, big), axis=0, keepdims=True)
    g_p = 1.0 / jnp.sum(jnp.exp(gl - gmax), axis=0, keepdims=True)
    lo = gidx * EXPERTS_PER_GROUP
    el = jnp.where(eid >= lo, jnp.where(eid < lo + EXPERTS_PER_GROUP, lg_t[0:N_EXPERTS], ninf), ninf)
    v1 = jnp.max(el, axis=0, keepdims=True)
    i1 = jnp.min(jnp.where(el == v1, eid, big), axis=0, keepdims=True)
    el2 = jnp.where(eid == i1, ninf, el)
    v2 = jnp.max(el2, axis=0, keepdims=True)
    i2 = jnp.min(jnp.where(el2 == v2, eid, big), axis=0, keepdims=True)
    t = jnp.exp(v2 - v1)
    w1 = g_p / (1.0 + t)
    w2 = w1 * t
    return jnp.concatenate([i1, i2, w1, w2, jnp.zeros((4, tm), F32)], axis=0)


def _xattn_body(x1_ref, kv_ref, gq_ref, wq_ref, wo_ref, gm_ref, wrt_ref, brt_ref, x2_ref, h3_ref, route_ref,
                wcol_ref):
    x1 = x1_ref[...]
    hb = _rms(x1, gq_ref[...]).astype(BF16)
    q = (jnp.dot(hb, wq_ref[...], preferred_element_type=F32) * (XA_HEAD_DIM ** -0.5)).astype(BF16)
    head = lambda h: slice(h * XA_HEAD_DIM, (h + 1) * XA_HEAD_DIM)
    scores = [_dot_nt(q[:, head(h)], kv_ref[:, head(h)]) for h in range(XA_HEADS)]
    outs = []
    for h in range(XA_HEADS):
        vd = slice(D_MODEL + h * XA_HEAD_DIM, D_MODEL + (h + 1) * XA_HEAD_DIM)
        s = scores[h]
        p = jnp.exp(s - jnp.max(s, axis=1, keepdims=True))
        p = p * (1.0 / jnp.sum(p, axis=1, keepdims=True))
        outs.append(jnp.dot(p.astype(BF16), kv_ref[:, vd], preferred_element_type=F32).astype(BF16))
    o = jnp.concatenate(outs, axis=1)
    x2 = x1 + jnp.dot(o, wo_ref[...], preferred_element_type=F32)
    x2_ref[...] = x2
    h3 = _rms(x2, gm_ref[...])
    h3_ref[...] = _pack_halves(h3)
    lg_t = _dot_nt_split(wrt_ref[...], h3) + brt_ref[...]
    route = _route_t(lg_t)
    route_ref[...] = route
    wcol_ref[...] = jnp.concatenate([route, jnp.zeros((LANES - 8, route.shape[1]), F32)], axis=0).T


def _xattn(x1, kv3, gq, wq, wo, gm, wrt, brt, seq):
    t = x1.shape[0]
    per_b = seq // TM
    n_mem = kv3.shape[1]
    tile = pl.BlockSpec((TM, D_MODEL), lambda i: (i, 0))
    return pl.pallas_call(
        _xattn_body,
        grid=(t // TM,),
        in_specs=[tile,
                  pl.BlockSpec((None, n_mem, 2 * D_MODEL), lambda i: (i // per_b, 0, 0)),
                  _resident((1, D_MODEL)), _resident((D_MODEL, D_MODEL)), _resident((D_MODEL, D_MODEL)),
                  _resident((1, D_MODEL)), _resident((ROUTER_ROWS, D_MODEL)), _resident((ROUTER_ROWS, 1))],
        out_specs=[tile, pl.BlockSpec((TM, PACKED_W), lambda i: (i, 0)), pl.BlockSpec((8, TM), lambda i: (0, i)),
                   pl.BlockSpec((TM, LANES), lambda i: (i, 0))],
        out_shape=[jax.ShapeDtypeStruct((t, D_MODEL), F32), jax.ShapeDtypeStruct((t, PACKED_W), F32),
                   jax.ShapeDtypeStruct((8, t), F32), jax.ShapeDtypeStruct((t, LANES), F32)],
        compiler_params=_params(("parallel",)),
        name="xattn",
    )(x1, kv3, gq, wq, wo, gm, wrt, brt)


def _n_expert_tiles(n_tokens):
    return 2 * n_tokens // TME + N_EXPERTS


def _plan_body(route_ref, pos_ref, tinfo_ref, tok_ref, cnt_scr, run_scr, start_scr, tokhi_scr, toklo_scr, *, nt_pad,
               nt_rows, n_tokens):
    phase = pl.program_id(0)
    b = pl.program_id(1)
    r = route_ref[...]
    eid = lax.broadcasted_iota(jnp.int32, (N_EXPERTS, TP), 0).astype(F32)
    oh1 = eid == r[0:1]
    oh2 = eid == r[1:2]
    oh = jnp.where(oh1 | oh2, 1.0, 0.0)

    @pl.when((phase == 0) & (b == 0))
    def _():
        cnt_scr[...] = jnp.zeros_like(cnt_scr)

    @pl.when(phase == 0)
    def _():
        cnt_scr[...] += jnp.sum(oh, axis=1, keepdims=True)

    @pl.when((phase == 1) & (b == 0))
    def _():
        cnt = cnt_scr[...]
        n_tiles = jnp.floor((cnt + (TME - 1)) * (1.0 / TME))
        ri = lax.broadcasted_iota(jnp.int32, (N_EXPERTS, N_EXPERTS), 0)
        ci = lax.broadcasted_iota(jnp.int32, (N_EXPERTS, N_EXPERTS), 1)
        lower = jnp.where(ci < ri, 1.0, 0.0).astype(BF16)
        nt_hi, nt_lo = _split_bf16(jnp.broadcast_to(n_tiles, (N_EXPERTS, LANES)))
        start = (jnp.dot(lower, nt_hi, preferred_element_type=F32)
                 + jnp.dot(lower, nt_lo, preferred_element_type=F32))[:, 0:1]
        start_scr[...] = start * TME
        run_scr[...] = jnp.zeros_like(run_scr)
        tokhi_scr[...] = jnp.zeros_like(tokhi_scr)
        toklo_scr[...] = jnp.zeros_like(toklo_scr)
        n = lax.broadcasted_iota(jnp.int32, (N_EXPERTS, nt_pad), 1).astype(F32)
        e_n = lax.broadcasted_iota(jnp.int32, (N_EXPERTS, nt_pad), 0).astype(F32)
        owner = jnp.sum(jnp.where(start <= n, 1.0, 0.0), axis=0, keepdims=True) - 1.0
        own = e_n == owner
        cnt_o = jnp.sum(jnp.where(own, cnt, 0.0), axis=0, keepdims=True)
        start_o = jnp.sum(jnp.where(own, start, 0.0), axis=0, keepdims=True)
        valid = jnp.clip(cnt_o - (n[0:1] - start_o) * TME, 0.0, float(TME))
        tinfo_ref[...] = jnp.concatenate([owner, valid, jnp.zeros((6, nt_pad), F32)], axis=0).astype(jnp.int32)

    @pl.when(phase == 1)
    def _():
        ti = lax.broadcasted_iota(jnp.int32, (TP, TP), 0)
        tj = lax.broadcasted_iota(jnp.int32, (TP, TP), 1)
        upper = jnp.where(ti < tj, 1.0, 0.0).astype(BF16)
        before = jnp.dot(oh.astype(BF16), upper, preferred_element_type=F32)
        row = start_scr[...] + run_scr[...] + before
        p1 = jnp.sum(jnp.where(oh1, row, 0.0), axis=0, keepdims=True)
        p2 = jnp.sum(jnp.where(oh2, row, 0.0), axis=0, keepdims=True)
        pos_ref[...] = jnp.concatenate([p1, p2, jnp.zeros((6, TP), F32)], axis=0).astype(jnp.int32)
        run_scr[...] += jnp.sum(oh, axis=1, keepdims=True)

        tid = lax.broadcasted_iota(jnp.int32, (1, TP), 1) + (b * TP + 1)
        t_hi = (tid // 256).astype(F32)
        t_lo = (tid % 256).astype(F32)
        tile_id = lax.broadcasted_iota(jnp.int32, (nt_rows, TP), 0).astype(F32)
        row_id = lax.broadcasted_iota(jnp.int32, (TME, TP), 0).astype(F32)
        for p in (p1, p2):
            hi = jnp.floor(p * (1.0 / TME))
            lo = p - hi * TME
            in_tile = jnp.where(tile_id == hi, 1.0, 0.0).astype(BF16)
            at_row = row_id == lo
            tokhi_scr[...] += _dot_nt(in_tile, jnp.where(at_row, t_hi, 0.0).astype(BF16))
            toklo_scr[...] += _dot_nt(in_tile, jnp.where(at_row, t_lo, 0.0).astype(BF16))

    @pl.when((phase == 1) & (b == pl.num_programs(1) - 1))
    def _():
        hit = (tokhi_scr[...] * 256.0 + toklo_scr[...]).astype(jnp.int32)
        row = (lax.broadcasted_iota(jnp.int32, hit.shape, 0) * TME + lax.broadcasted_iota(jnp.int32, hit.shape, 1))
        tok_ref[...] = jnp.where(hit > 0, hit - 1, row % n_tokens)


def _plan(route):
    t = route.shape[1]
    assert t < 256 * 256, "token id + 1 is carried as two base-256 digits"
    nt_pad = -(-_n_expert_tiles(t) // LANES) * LANES
    nt_rows = -(-_n_expert_tiles(t) // 8) * 8
    col = pltpu.VMEM((N_EXPERTS, 1), F32)
    table = pltpu.VMEM((nt_rows, TME), F32)
    return pl.pallas_call(
        functools.partial(_plan_body, nt_pad=nt_pad, nt_rows=nt_rows, n_tokens=t),
        grid=(2, t // TP),
        in_specs=[pl.BlockSpec((8, TP), lambda ph, b: (0, b))],
        out_specs=[pl.BlockSpec((8, TP), lambda ph, b: (0, b * ph)),
                   pl.BlockSpec((8, nt_pad), lambda ph, b: (0, 0)),
                   pl.BlockSpec((nt_rows, TME), lambda ph, b: (0, 0))],
        out_shape=[jax.ShapeDtypeStruct((8, t), jnp.int32), jax.ShapeDtypeStruct((8, nt_pad), jnp.int32),
                   jax.ShapeDtypeStruct((nt_rows, TME), jnp.int32)],
        scratch_shapes=[col, col, col, table, table],
        compiler_params=_params(("arbitrary", "arbitrary")),
        name="plan",
    )(route)


def _experts_body(texp_ref, tvalid_ref, xs_ref, wg_ref, wu_ref, wd_ref, ys_ref, wgb_scr, wub_scr, wdb_scr):
    n = pl.program_id(0)

    @pl.when((n == 0) | (texp_ref[n] != texp_ref[jnp.maximum(n - 1, 0)]))
    def _():
        wgb_scr[...] = wg_ref[...].astype(BF16)
        wub_scr[...] = wu_ref[...].astype(BF16)
        wdb_scr[...] = wd_ref[...].astype(BF16)

    @pl.when(tvalid_ref[n] > 0)
    def _():
        xb = _unpack_halves(xs_ref[...]).astype(BF16)
        gate = jnp.dot(xb, wgb_scr[...], preferred_element_type=F32)
        up = jnp.dot(xb, wub_scr[...], preferred_element_type=F32)
        he = (gate * jax.nn.sigmoid(gate) * up).astype(BF16)
        ys_ref[...] = _pack_halves(jnp.dot(he, wdb_scr[...], preferred_element_type=F32))

    @pl.when(tvalid_ref[n] == 0)
    def _():
        ys_ref[...] = jnp.zeros_like(ys_ref)


def _experts(texp, tvalid, xs, wg, wu, wd):
    nt = texp.shape[0]
    weight = lambda shape: pl.BlockSpec((None,) + shape, lambda n, te, tv: (te[n], 0, 0))
    return pl.pallas_call(
        _experts_body,
        grid_spec=pltpu.PrefetchScalarGridSpec(
            num_scalar_prefetch=2,
            grid=(nt,),
            in_specs=[pl.BlockSpec((TME, PACKED_W), lambda n, te, tv: (jnp.where(tv[n] > 0, n, 0), 0)),
                      weight((D_MODEL, D_EXPERT)), weight((D_MODEL, D_EXPERT)), weight((D_EXPERT, D_MODEL))],
            out_specs=pl.BlockSpec((TME, PACKED_W), lambda n, te, tv: (n, 0)),
            scratch_shapes=[pltpu.VMEM((D_MODEL, D_EXPERT), BF16), pltpu.VMEM((D_MODEL, D_EXPERT), BF16),
                            pltpu.VMEM((D_EXPERT, D_MODEL), BF16)],
        ),
        out_shape=jax.ShapeDtypeStruct((nt * TME, PACKED_W), F32),
        compiler_params=_params(("arbitrary",)),
        name="experts",
    )(texp, tvalid, xs, wg, wu, wd)


def _rowgather(table, idx):
    n_rows, width = idx.shape[0], table.shape[1]
    n_workers = SC_CORES * SC_SUBCORES
    per_worker = n_rows // n_workers
    assert per_worker * n_workers == n_rows and per_worker % SC_ROWS == 0
    mesh = plsc.VectorSubcoreMesh(core_axis_name="c", subcore_axis_name="s", num_cores=SC_CORES,
                                  num_subcores=SC_SUBCORES)

    chunks = per_worker // SC_ROWS
    assert chunks % 2 == 0
    buf = lambda: [pltpu.VMEM((SC_ROWS,), jnp.int32), pltpu.VMEM((SC_ROWS, width), table.dtype),
                   pltpu.SemaphoreType.DMA]

    @functools.partial(pl.kernel, mesh=mesh, out_type=jax.ShapeDtypeStruct((n_rows, width), table.dtype),
                       scratch_types=buf() + buf(), name="rowgather")
    def gather(table_hbm, idx_hbm, out_hbm, idx_a, rows_a, sem_a, idx_b, rows_b, sem_b):
        worker = lax.axis_index("s") * SC_CORES + lax.axis_index("c")
        base = worker * per_worker

        def fetch(c, idx_v, rows_v, sem):
            pltpu.sync_copy(idx_hbm.at[pl.ds(base + c * SC_ROWS, SC_ROWS)], idx_v)
            pltpu.async_copy(table_hbm.at[idx_v], rows_v, sem)

        def flush(c, idx_v, rows_v, sem):
            pltpu.make_async_copy(table_hbm.at[idx_v], rows_v, sem).wait()
            pltpu.sync_copy(rows_v, out_hbm.at[pl.ds(base + c * SC_ROWS, SC_ROWS)])

        fetch(0, idx_a, rows_a, sem_a)

        @pl.loop(0, chunks, step=2)
        def _(c):
            fetch(c + 1, idx_b, rows_b, sem_b)
            flush(c, idx_a, rows_a, sem_a)

            @pl.when(c + 2 < chunks)
            def _():
                fetch(c + 2, idx_a, rows_a, sem_a)

            flush(c + 1, idx_b, rows_b, sem_b)

    return gather(table, idx)


def _combine_body(g1_ref, g2_ref, x2_ref, wcol_ref, gf_ref, out_ref):
    w = wcol_ref[...]
    y = w[:, 2:3] * _unpack_halves(g1_ref[...]) + w[:, 3:4] * _unpack_halves(g2_ref[...])
    out_ref[...] = _rms(x2_ref[...] + y, gf_ref[...])


def _combine(g, x2, wcol, gf):
    t = x2.shape[0]
    steps = t // TF
    tile = lambda w: pl.BlockSpec((TF, w), lambda i: (i, 0))
    return pl.pallas_call(
        _combine_body,
        grid=(steps,),
        in_specs=[tile(PACKED_W), pl.BlockSpec((TF, PACKED_W), lambda i: (i + steps, 0)),
                  tile(D_MODEL), tile(LANES), _resident((1, D_MODEL))],
        out_specs=tile(D_MODEL),
        out_shape=jax.ShapeDtypeStruct((t, D_MODEL), F32),
        compiler_params=_params(("parallel",)),
        name="combine",
    )(g, g, x2, wcol, gf)


def _layer(x, mem, g_mix, w_in, b_ml_i, b_ml_f, b_fx_f, b_gate_ml, b_gate_fx, g_ml_head, w_proj_ml, w_proj_fx,
           w_out, g_xq, g_xmem, w_xq, w_xkv, w_xo, g_moe, w_rg, b_rg, w_re, b_re, w_gate, w_up, w_down):
    nb, seq, d = x.shape
    t = nb * seq
    row = lambda v: v.reshape(1, -1).astype(F32)

    o = 0
    parts = {}
    for name, width in (("ml_q", 512), ("ml_k", 512), ("ml_v", 1024), ("ml_o", 1024), ("ml_i", 4), ("ml_f", 4),
                        ("fx_q", 1024), ("fx_k", 1024), ("fx_v", 1024), ("fx_f", 8), ("gt_ml", 1024),
                        ("gt_fx", 1024)):
        parts[name] = w_in[:, o:o + width]
        o += width
    w_main = jnp.concatenate(
        [parts["ml_q"], parts["ml_k"] * (ML_QK_DIM ** -0.5), parts["ml_o"],
         parts["fx_q"] * (FX_HEAD_DIM ** -0.5 * LOG2E), parts["fx_k"], parts["gt_ml"], parts["gt_fx"]],
        axis=1).astype(BF16)
    w_vt = jnp.concatenate([parts["ml_v"], parts["fx_v"]], axis=1).T.astype(BF16)
    w_gates_t = jnp.concatenate([parts["ml_i"], parts["ml_f"], parts["fx_f"]], axis=1).T
    gate_bias = jnp.concatenate([b_ml_i, b_ml_f, b_fx_f]).reshape(N_GATES, 1).astype(F32)

    x2d = x.reshape(t, d)
    z, vt, gt3 = _inproj(x2d, row(g_mix), w_main, w_vt, w_gates_t, seq)
    rows, cols, caug = _gateprep(gt3, gate_bias)
    z3 = z.reshape(nb, seq, Z_W)
    vt4 = vt.reshape(nb, seq // TK, VT_W, TK)
    y_ml = _mlstm(z3, vt4, cols, rows, row(g_ml_head))
    y_fx = _fox(z3, vt4, caug, rows)
    x1 = _merge(x2d, y_ml.reshape(t, d), y_fx.reshape(t, d), z, row(b_gate_ml), row(b_gate_fx),
                w_proj_ml.astype(BF16), w_proj_fx.astype(BF16), w_out.astype(BF16))

    n_mem = mem.shape[1]
    kv = _memkv(mem.reshape(nb * n_mem, d), row(g_xmem), w_xkv.astype(BF16))
    w_router_t = jnp.concatenate([w_re, w_rg, jnp.zeros((d, ROUTER_ROWS - N_EXPERTS - N_GROUPS), F32)], axis=1).T
    b_router_t = jnp.concatenate([b_re, b_rg, jnp.zeros((ROUTER_ROWS - N_EXPERTS - N_GROUPS,), F32)]).reshape(ROUTER_ROWS, 1)
    x2, h3, route, wcol = _xattn(x1, kv.reshape(nb, n_mem, 2 * d), row(g_xq), w_xq.astype(BF16),
                                 w_xo.astype(BF16), row(g_moe), w_router_t, b_router_t, seq)
    return x2, h3, route, wcol


def _moe(x2, h3, route, wcol, w_gate, w_up, w_down, g_final):
    nt = _n_expert_tiles(x2.shape[0])
    pos, tinfo, tok = _plan(route)
    texp, tvalid = tinfo[0, :nt], tinfo[1, :nt]
    xs = _rowgather(h3, tok[:nt].reshape(-1))
    ys = _experts(texp, tvalid, xs, w_gate, w_up, w_down)
    g = _rowgather(ys, pos[0:2].reshape(-1))
    return _combine(g, x2, wcol, g_final)


def kernel(x, mem, g_mix, w_in, b_ml_i, b_ml_f, b_fx_f, b_gate_ml, b_gate_fx, g_ml_head, w_proj_ml, w_proj_fx, w_out, g_xq, g_xmem, w_xq, w_xkv, w_xo, g_moe, w_rg, b_rg, w_re, b_re, w_gate, w_up, w_down, g_final):
    nb, seq, d = x.shape
    depth = g_mix.shape[0]
    assert depth == 1, "the final rmsnorm is fused into the (single) layer's combine kernel"
    x2, h3, route, wcol = _layer(
        x, mem, g_mix[0], w_in[0], b_ml_i[0], b_ml_f[0], b_fx_f[0], b_gate_ml[0], b_gate_fx[0], g_ml_head[0],
        w_proj_ml[0], w_proj_fx[0], w_out[0], g_xq[0], g_xmem[0], w_xq[0], w_xkv[0], w_xo[0], g_moe[0],
        w_rg[0], b_rg[0], w_re[0], b_re[0], w_gate[0], w_up[0], w_down[0])
    out = _moe(x2, h3, route, wcol, w_gate[0], w_up[0], w_down[0], g_final.reshape(1, d).astype(F32))
    return out.reshape(nb, seq, d)
```

```python
import functools

import jax
import jax.numpy as jnp
from jax import lax
from jax.experimental import pallas as pl
from jax.experimental.pallas import tpu as pltpu
from jax.experimental.pallas import tpu_sc as plsc

F32 = jnp.float32
BF16 = jnp.bfloat16

D_MODEL = 1024
EPS = 1e-6
ML_HEADS = 4
ML_QK_DIM = 128
ML_V_DIM = 256
FX_HEADS = 8
FX_HEAD_DIM = 128
XA_HEADS = 4
XA_HEAD_DIM = 256
N_GROUPS = 4
EXPERTS_PER_GROUP = 8
N_EXPERTS = 32
D_EXPERT = 512

LANES = 128
N_GATES = 16
ROUTER_ROWS = 48
LOG2E = 1.4426950408889634
Z_W = 6144
Z_ML_O, Z_FX_Q, Z_FX_K, Z_GT_ML, Z_GT_FX = 1, 2, 3, 4, 5
VT_W = 2048

VMEM_LIMIT = 56 * 1024 * 1024

IN_CHUNK = 1024
TK = 256
TQ = 512
TIN = 512
KT_PER_Q = TQ // TK
MXU_LOOKAHEAD = 16
TM = 1024
TME = 512
TP = 512
TF = 1024
SC_CORES, SC_SUBCORES = 2, 16
PACKED_W = D_MODEL // 2
SC_ROWS = 64


def _params(sem, flags=None):
    return pltpu.CompilerParams(dimension_semantics=sem, vmem_limit_bytes=VMEM_LIMIT, flags=flags)


def _rms(x, g):
    return x * lax.rsqrt(jnp.mean(x * x, axis=-1, keepdims=True) + EPS) * g


def _dot_nt(a, b, **kw):
    return lax.dot_general(a, b, (((1,), (1,)), ((), ())), preferred_element_type=F32, **kw)


def _split_bf16(x):
    hi = x.astype(BF16)
    return hi, (x - hi.astype(F32)).astype(BF16)


def _dot_nt_split(a, b):
    a_hi, a_lo = _split_bf16(a)
    b_hi, b_lo = _split_bf16(b)
    m = a.shape[0]
    by_hi = _dot_nt(jnp.concatenate([a_hi, a_lo], axis=0), b_hi)
    return by_hi[:m] + (_dot_nt(a_hi, b_lo) + by_hi[m:])


def _pack_halves(x):
    n = x.shape[1] // 2
    bits = lambda v: lax.bitcast_convert_type(v.astype(BF16).astype(F32), jnp.uint32)
    w = (bits(x[:, :n]) >> 16) | (bits(x[:, n:]) & jnp.uint32(0xFFFF0000))
    return lax.bitcast_convert_type(w, F32)


def _unpack_halves(p):
    w = lax.bitcast_convert_type(p, jnp.uint32)
    lo = lax.bitcast_convert_type(w << 16, F32)
    hi = lax.bitcast_convert_type(w & jnp.uint32(0xFFFF0000), F32)
    return jnp.concatenate([lo, hi], axis=1)


def _resident(shape):
    zeros = (0,) * len(shape)
    return pl.BlockSpec(shape, lambda *_: zeros, pipeline_mode=pl.Buffered(1))


def _inproj_body(x_ref, g_ref, w_ref, wvt_ref, wgt_ref, z_ref, vt_ref, gt_ref):
    h = _rms(x_ref[...], g_ref[...])
    hb = h.astype(BF16)
    for c in range(Z_W // IN_CHUNK):
        sl = slice(c * IN_CHUNK, (c + 1) * IN_CHUNK)
        z_ref[:, sl] = jnp.dot(hb, w_ref[:, sl], preferred_element_type=F32).astype(BF16)
    for kt in range(TIN // TK):
        hk = hb[kt * TK:(kt + 1) * TK]
        for c in range(VT_W // IN_CHUNK):
            sl = slice(c * IN_CHUNK, (c + 1) * IN_CHUNK)
            vt_ref[kt, sl, :] = _dot_nt(wvt_ref[sl, :], hk).astype(BF16)
    gt_ref[...] = _dot_nt_split(wgt_ref[...], h)


def _inproj(x2d, g, w, wvt, wgt, seq):
    t = x2d.shape[0]
    per_b = seq // TIN
    return pl.pallas_call(
        _inproj_body,
        grid=(t // TIN,),
        in_specs=[
            pl.BlockSpec((TIN, D_MODEL), lambda i: (i, 0)),
            _resident((1, D_MODEL)),
            _resident((D_MODEL, Z_W)),
            _resident((VT_W, D_MODEL)),
            _resident((N_GATES, D_MODEL)),
        ],
        out_specs=[
            pl.BlockSpec((TIN, Z_W), lambda i: (i, 0)),
            pl.BlockSpec((TIN // TK, VT_W, TK), lambda i: (i, 0, 0)),
            pl.BlockSpec((None, N_GATES, TIN), lambda i: (i // per_b, 0, i % per_b)),
        ],
        out_shape=[jax.ShapeDtypeStruct((t, Z_W), BF16), jax.ShapeDtypeStruct((t // TK, VT_W, TK), BF16),
                   jax.ShapeDtypeStruct((t // seq, N_GATES, seq), F32)],
        compiler_params=_params(("parallel",)),
        name="inproj",
    )(x2d, g, w, wvt, wgt)


def _scan_lanes(x, op, identity):
    n = x.shape[-1]
    idx = lax.broadcasted_iota(jnp.int32, x.shape, 1)
    s = 1
    while s < n:
        shifted = pltpu.roll(x, s, axis=1)
        x = op(x, jnp.where(idx >= s, shifted, identity))
        s *= 2
    return x


def _log_sigmoid(x):
    return jnp.minimum(x, 0.0) - jnp.log1p(jnp.exp(-jnp.abs(x)))


def _gateprep_body(gt_ref, bias_ref, rows_ref, cols_ref, caug_ref):
    g = gt_ref[...] + bias_ref[...]
    s = g.shape[1]
    cs = _scan_lanes(_log_sigmoid(g), jnp.add, 0.0)
    b = cs[4:8]
    c2 = cs[8:16] * LOG2E
    a = g[0:4] - b
    m = _scan_lanes(jnp.concatenate([a, a], axis=0), jnp.maximum, -jnp.inf)[0:4]
    rows_ref[...] = jnp.concatenate([m * LOG2E, b + m, c2], axis=0)
    cols_ref[...] = jnp.concatenate([a * LOG2E, jnp.zeros((LANES - ML_HEADS, s), F32)], axis=0).T
    hi = c2.astype(BF16).astype(F32)
    r1 = c2 - hi
    mid = r1.astype(BF16).astype(F32)
    lo = r1 - mid
    aug = jnp.concatenate([-hi, -mid, -lo, jnp.zeros((LANES - 3 * FX_HEADS, s), F32)], axis=0)
    caug_ref[...] = aug.T.astype(BF16)


def _gateprep(gt3, bias):
    nb, _, s = gt3.shape
    return pl.pallas_call(
        _gateprep_body,
        grid=(nb,),
        in_specs=[
            pl.BlockSpec((None, N_GATES, s), lambda b: (b, 0, 0)),
            _resident((N_GATES, 1)),
        ],
        out_specs=[
            pl.BlockSpec((None, N_GATES, s), lambda b: (b, 0, 0)),
            pl.BlockSpec((None, s, LANES), lambda b: (b, 0, 0)),
            pl.BlockSpec((None, s, LANES), lambda b: (b, 0, 0)),
        ],
        out_shape=[jax.ShapeDtypeStruct((nb, N_GATES, s), F32), jax.ShapeDtypeStruct((nb, s, LANES), F32),
                   jax.ShapeDtypeStruct((nb, s, LANES), BF16)],
        compiler_params=_params(("parallel",)),
        name="gateprep",
    )(gt3, bias)


def _pipelined(stages, scores, update):
    pending = {k: scores(*stages[k]) for k in range(min(MXU_LOOKAHEAD, len(stages)))}
    for k, stage in enumerate(stages):
        if k + MXU_LOOKAHEAD < len(stages):
            pending[k + MXU_LOOKAHEAD] = scores(*stages[k + MXU_LOOKAHEAD])
        update(*stage, pending.pop(k))


def _causal_mask_t(d):
    s = lax.broadcasted_iota(jnp.int32, (TK, TQ - d * TK), 0)
    t = lax.broadcasted_iota(jnp.int32, (TK, TQ - d * TK), 1)
    return s <= t


def _mlstm_body(q_ref, k_ref, vt_ref, o_ref, cols_ref, rows_ref, gh_ref, y_ref, num_scr, den_scr):
    i = pl.program_id(1)
    rows = rows_ref[...]
    num_scr[...] = jnp.zeros_like(num_scr)
    den_scr[...] = jnp.zeros_like(den_scr)

    def scores(j, h, mask, q0):
        ks = pl.ds(pl.multiple_of(j * TK, TK), TK)
        qk = slice(h * ML_QK_DIM, (h + 1) * ML_QK_DIM)
        return _dot_nt(k_ref[ks, qk], q_ref[q0:, qk])

    def update(j, h, mask, q0, s):
        ks = pl.ds(pl.multiple_of(j * TK, TK), TK)
        vv = slice(h * ML_V_DIM, (h + 1) * ML_V_DIM)
        w = jnp.exp2(cols_ref[ks, h:h + 1] - rows[h:h + 1, q0:])
        if mask is not None:
            w = jnp.where(mask, w, 0.0)
        s = s * w
        den_scr[h, :, q0:] += jnp.sum(s, axis=0, keepdims=True)
        num_scr[h, :, q0:] += jnp.dot(vt_ref[j, vv, :], s.astype(BF16), preferred_element_type=F32)

    def body(jj, carry):
        js = [jj * KT_PER_Q + d for d in range(KT_PER_Q)]
        sc = {(d, h): scores(js[d], h, None, 0) for h in range(ML_HEADS) for d in range(KT_PER_Q)}
        for h in range(ML_HEADS):
            vv = slice(h * ML_V_DIM, (h + 1) * ML_V_DIM)
            weighted = []
            for d in range(KT_PER_Q):
                ks = pl.ds(pl.multiple_of(js[d] * TK, TK), TK)
                s = sc[(d, h)] * jnp.exp2(cols_ref[ks, h:h + 1] - rows[h:h + 1])
                den_scr[h] += jnp.sum(s, axis=0, keepdims=True)
                weighted.append(s.astype(BF16))
            vt = jnp.concatenate([vt_ref[js[d], vv, :] for d in range(KT_PER_Q)], axis=1)
            num_scr[h] += jnp.dot(vt, jnp.concatenate(weighted, axis=0), preferred_element_type=F32)
        return carry

    lax.fori_loop(0, i, body, 0)
    masks = [_causal_mask_t(d) for d in range(KT_PER_Q)]
    _pipelined([(i * KT_PER_Q + d, h, masks[d], d * TK) for d in range(KT_PER_Q) for h in range(ML_HEADS)],
               scores, update)
    for h in range(ML_HEADS):
        vv = slice(h * ML_V_DIM, (h + 1) * ML_V_DIM)
        floor = jnp.exp(-rows[ML_HEADS + h:ML_HEADS + h + 1])
        hh = num_scr[h] * (1.0 / jnp.maximum(jnp.abs(den_scr[h]), floor))
        yt = hh * lax.rsqrt(jnp.mean(hh * hh, axis=0, keepdims=True) + EPS)
        y = yt.T * gh_ref[:, vv]
        y_ref[:, vv] = (y * jax.nn.sigmoid(o_ref[:, vv].astype(F32))).astype(BF16)


def _mlstm(z3, vt4, cols3, rows3, g_head):
    nb, s, _ = z3.shape
    nq = s // TQ
    return pl.pallas_call(
        _mlstm_body,
        grid=(nb, nq),
        in_specs=[
            pl.BlockSpec((None, TQ, ML_HEADS * ML_QK_DIM), lambda b, i: (b, i, 0)),
            pl.BlockSpec((None, s, ML_HEADS * ML_QK_DIM), lambda b, i: (b, 0, 1)),
            pl.BlockSpec((None, s // TK, D_MODEL, TK), lambda b, i: (b, 0, 0, 0)),
            pl.BlockSpec((None, TQ, D_MODEL), lambda b, i: (b, i, Z_ML_O)),
            pl.BlockSpec((None, s, LANES), lambda b, i: (b, 0, 0)),
            pl.BlockSpec((None, N_GATES, TQ), lambda b, i: (b, 0, i)),
            _resident((1, D_MODEL)),
        ],
        out_specs=pl.BlockSpec((None, TQ, D_MODEL), lambda b, i: (b, i, 0)),
        out_shape=jax.ShapeDtypeStruct((nb, s, D_MODEL), BF16),
        scratch_shapes=[pltpu.VMEM((ML_HEADS, ML_V_DIM, TQ), F32), pltpu.VMEM((ML_HEADS, 1, TQ), F32)],
        compiler_params=_params(("parallel", "arbitrary")),
        name="mlstm",
    )(z3, z3, vt4, z3, cols3, rows3, g_head)


def _fox_body(q_ref, k_ref, vt_ref, caug_ref, rows_ref, y_ref, qa_scr, acc_scr, m_scr, l_scr):
    i = pl.program_id(1)
    rows = rows_ref[...]
    lane = lax.broadcasted_iota(jnp.int32, (TQ, LANES), 1)
    for h in range(FX_HEADS):
        hd = slice(h * FX_HEAD_DIM, (h + 1) * FX_HEAD_DIM)
        ones = jnp.where((lane < 3 * FX_HEADS) & (lane % FX_HEADS == h), 1.0, 0.0).astype(BF16)
        qa_scr[h] = jnp.concatenate([q_ref[:, hd], ones], axis=1)
    m_scr[...] = jnp.full_like(m_scr, -jnp.inf)
    l_scr[...] = jnp.zeros_like(l_scr)
    acc_scr[...] = jnp.zeros_like(acc_scr)

    def scores(j, h, mask, q0):
        ks = pl.ds(pl.multiple_of(j * TK, TK), TK)
        hd = slice(h * FX_HEAD_DIM, (h + 1) * FX_HEAD_DIM)
        k_aug = jnp.concatenate([k_ref[ks, hd], caug_ref[ks, :]], axis=1)
        return _dot_nt(k_aug, qa_scr[h, q0:, :])

    def update(j, h, mask, q0, u):
        hd = slice(h * FX_HEAD_DIM, (h + 1) * FX_HEAD_DIM)
        if mask is not None:
            u = jnp.where(mask, u, -jnp.inf)
        c_row = rows[2 * ML_HEADS + h:2 * ML_HEADS + h + 1, q0:]
        m_prev = m_scr[h, :, q0:]
        m_new = jnp.maximum(m_prev, jnp.max(u, axis=0, keepdims=True) + c_row)
        p = jnp.exp2(u - (m_new - c_row))
        alpha = jnp.exp2(m_prev - m_new)
        l_scr[h, :, q0:] = alpha * l_scr[h, :, q0:] + jnp.sum(p, axis=0, keepdims=True)
        acc_scr[h, :, q0:] = alpha * acc_scr[h, :, q0:] + jnp.dot(vt_ref[j, hd, :], p.astype(BF16),
                                                                  preferred_element_type=F32)
        m_scr[h, :, q0:] = m_new

    def body(jj, carry):
        _pipelined([(jj * KT_PER_Q + d, h, None, 0) for d in range(KT_PER_Q) for h in range(FX_HEADS)], scores, update)
        return carry

    lax.fori_loop(0, i, body, 0)
    masks = [_causal_mask_t(d) for d in range(KT_PER_Q)]
    _pipelined([(i * KT_PER_Q + d, h, masks[d], d * TK) for d in range(KT_PER_Q) for h in range(FX_HEADS)],
               scores, update)
    for h in range(FX_HEADS):
        hd = slice(h * FX_HEAD_DIM, (h + 1) * FX_HEAD_DIM)
        y_ref[:, hd] = (acc_scr[h] * (1.0 / l_scr[h])).T.astype(BF16)


def _fox(z3, vt4, caug3, rows3):
    nb, s, _ = z3.shape
    nq = s // TQ
    return pl.pallas_call(
        _fox_body,
        grid=(nb, nq),
        in_specs=[
            pl.BlockSpec((None, TQ, D_MODEL), lambda b, i: (b, i, Z_FX_Q)),
            pl.BlockSpec((None, s, D_MODEL), lambda b, i: (b, 0, Z_FX_K)),
            pl.BlockSpec((None, s // TK, D_MODEL, TK), lambda b, i: (b, 0, 1, 0)),
            pl.BlockSpec((None, s, LANES), lambda b, i: (b, 0, 0)),
            pl.BlockSpec((None, N_GATES, TQ), lambda b, i: (b, 0, i)),
        ],
        out_specs=pl.BlockSpec((None, TQ, D_MODEL), lambda b, i: (b, i, 0)),
        out_shape=jax.ShapeDtypeStruct((nb, s, D_MODEL), BF16),
        scratch_shapes=[pltpu.VMEM((FX_HEADS, TQ, 2 * FX_HEAD_DIM), BF16),
                        pltpu.VMEM((FX_HEADS, FX_HEAD_DIM, TQ), F32),
                        pltpu.VMEM((FX_HEADS, 1, TQ), F32), pltpu.VMEM((FX_HEADS, 1, TQ), F32)],
        compiler_params=_params(("parallel", "arbitrary")),
        name="fox",
    )(z3, z3, vt4, caug3, rows3)


def _merge_body(x_ref, yml_ref, yfx_ref, gml_ref, gfx_ref, bml_ref, bfx_ref, wml_ref, wfx_ref, wout_ref, x1_ref):
    p_ml = jnp.dot(yml_ref[...], wml_ref[...], preferred_element_type=F32)
    p_fx = jnp.dot(yfx_ref[...], wfx_ref[...], preferred_element_type=F32)
    merged = (jax.nn.sigmoid(gml_ref[...].astype(F32) + bml_ref[...]) * p_ml
              + jax.nn.sigmoid(gfx_ref[...].astype(F32) + bfx_ref[...]) * p_fx)
    x1_ref[...] = x_ref[...] + jnp.dot(merged.astype(BF16), wout_ref[...], preferred_element_type=F32)


def _merge(x2d, yml, yfx, z2d, bml, bfx, wml, wfx, wout):
    t = x2d.shape[0]
    tile = lambda col: pl.BlockSpec((TM, D_MODEL), lambda i, col=col: (i, col))
    return pl.pallas_call(
        _merge_body,
        grid=(t // TM,),
        in_specs=[tile(0), tile(0), tile(0), tile(Z_GT_ML), tile(Z_GT_FX),
                  _resident((1, D_MODEL)), _resident((1, D_MODEL)),
                  _resident((D_MODEL, D_MODEL)), _resident((D_MODEL, D_MODEL)), _resident((D_MODEL, D_MODEL))],
        out_specs=tile(0),
        out_shape=jax.ShapeDtypeStruct((t, D_MODEL), F32),
        compiler_params=_params(("parallel",)),
        name="merge",
    )(x2d, yml, yfx, z2d, z2d, bml, bfx, wml, wfx, wout)


def _memkv_body(m_ref, g_ref, w_ref, kv_ref):
    hb = _rms(m_ref[...], g_ref[...]).astype(BF16)
    kv_ref[...] = jnp.dot(hb, w_ref[...], preferred_element_type=F32).astype(BF16)


def _memkv(mem2d, g, w):
    t = mem2d.shape[0]
    return pl.pallas_call(
        _memkv_body,
        grid=(t // TM,),
        in_specs=[pl.BlockSpec((TM, D_MODEL), lambda i: (i, 0)), _resident((1, D_MODEL)),
                  _resident((D_MODEL, 2 * D_MODEL))],
        out_specs=pl.BlockSpec((TM, 2 * D_MODEL), lambda i: (i, 0)),
        out_shape=jax.ShapeDtypeStruct((t, 2 * D_MODEL), BF16),
        compiler_params=_params(("parallel",)),
        name="memkv",
    )(mem2d, g, w)


def _route_t(lg_t):
    tm = lg_t.shape[1]
    ninf = -jnp.inf
    big = jnp.float32(LANES)
    gid = lax.broadcasted_iota(jnp.int32, (8, tm), 0).astype(F32)
    eid = lax.broadcasted_iota(jnp.int32, (N_EXPERTS, tm), 0).astype(F32)
    gl = jnp.where(gid < N_GROUPS, lg_t[N_EXPERTS:N_EXPERTS + 8], ninf)
    gmax = jnp.max(gl, axis=0, keepdims=True)
    gidx = jnp.min(jnp.where(gl == gmax, gid, big), axis=0, keepdims=True)
    g_p = 1.0 / jnp.sum(jnp.exp(gl - gmax), axis=0, keepdims=True)
    lo = gidx * EXPERTS_PER_GROUP
    el = jnp.where(eid >= lo, jnp.where(eid < lo + EXPERTS_PER_GROUP, lg_t[0:N_EXPERTS], ninf), ninf)
    v1 = jnp.max(el, axis=0, keepdims=True)
    i1 = jnp.min(jnp.where(el == v1, eid, big), axis=0, keepdims=True)
    el2 = jnp.where(eid == i1, ninf, el)
    v2 = jnp.max(el2, axis=0, keepdims=True)
    i2 = jnp.min(jnp.where(el2 == v2, eid, big), axis=0, keepdims=True)
    t = jnp.exp(v2 - v1)
    w1 = g_p / (1.0 + t)
    w2 = w1 * t
    return jnp.concatenate([i1, i2, w1, w2, jnp.zeros((4, tm), F32)], axis=0)


def _xattn_body(x1_ref, kv_ref, gq_ref, wq_ref, wo_ref, gm_ref, wrt_ref, brt_ref, x2_ref, h3_ref, route_ref,
                wcol_ref):
    x1 = x1_ref[...]
    hb = _rms(x1, gq_ref[...]).astype(BF16)
    q = (jnp.dot(hb, wq_ref[...], preferred_element_type=F32) * (XA_HEAD_DIM ** -0.5)).astype(BF16)
    head = lambda h: slice(h * XA_HEAD_DIM, (h + 1) * XA_HEAD_DIM)
    scores = [_dot_nt(q[:, head(h)], kv_ref[:, head(h)]) for h in range(XA_HEADS)]
    outs = []
    for h in range(XA_HEADS):
        vd = slice(D_MODEL + h * XA_HEAD_DIM, D_MODEL + (h + 1) * XA_HEAD_DIM)
        s = scores[h]
        p = jnp.exp(s - jnp.max(s, axis=1, keepdims=True))
        p = p * (1.0 / jnp.sum(p, axis=1, keepdims=True))
        outs.append(jnp.dot(p.astype(BF16), kv_ref[:, vd], preferred_element_type=F32).astype(BF16))
    o = jnp.concatenate(outs, axis=1)
    x2 = x1 + jnp.dot(o, wo_ref[...], preferred_element_type=F32)
    x2_ref[...] = x2
    h3 = _rms(x2, gm_ref[...])
    h3_ref[...] = _pack_halves(h3)
    lg_t = _dot_nt_split(wrt_ref[...], h3) + brt_ref[...]
    route = _route_t(lg_t)
    route_ref[...] = route
    wcol_ref[...] = jnp.concatenate([route, jnp.zeros((LANES - 8, route.shape[1]), F32)], axis=0).T


def _xattn(x1, kv3, gq, wq, wo, gm, wrt, brt, seq):
    t = x1.shape[0]
    per_b = seq // TM
    n_mem = kv3.shape[1]
    tile = pl.BlockSpec((TM, D_MODEL), lambda i: (i, 0))
    return pl.pallas_call(
        _xattn_body,
        grid=(t // TM,),
        in_specs=[tile,
                  pl.BlockSpec((None, n_mem, 2 * D_MODEL), lambda i: (i // per_b, 0, 0)),
                  _resident((1, D_MODEL)), _resident((D_MODEL, D_MODEL)), _resident((D_MODEL, D_MODEL)),
                  _resident((1, D_MODEL)), _resident((ROUTER_ROWS, D_MODEL)), _resident((ROUTER_ROWS, 1))],
        out_specs=[tile, pl.BlockSpec((TM, PACKED_W), lambda i: (i, 0)), pl.BlockSpec((8, TM), lambda i: (0, i)),
                   pl.BlockSpec((TM, LANES), lambda i: (i, 0))],
        out_shape=[jax.ShapeDtypeStruct((t, D_MODEL), F32), jax.ShapeDtypeStruct((t, PACKED_W), F32),
                   jax.ShapeDtypeStruct((8, t), F32), jax.ShapeDtypeStruct((t, LANES), F32)],
        compiler_params=_params(("parallel",)),
        name="xattn",
    )(x1, kv3, gq, wq, wo, gm, wrt, brt)


def _n_expert_tiles(n_tokens):
    return 2 * n_tokens // TME + N_EXPERTS


def _plan_body(route_ref, pos_ref, tinfo_ref, tok_ref, cnt_scr, run_scr, start_scr, tokhi_scr, toklo_scr, *, nt_pad,
               nt_rows, n_tokens):
    phase = pl.program_id(0)
    b = pl.program_id(1)
    r = route_ref[...]
    eid = lax.broadcasted_iota(jnp.int32, (N_EXPERTS, TP), 0).astype(F32)
    oh1 = eid == r[0:1]
    oh2 = eid == r[1:2]
    oh = jnp.where(oh1 | oh2, 1.0, 0.0)

    @pl.when((phase == 0) & (b == 0))
    def _():
        cnt_scr[...] = jnp.zeros_like(cnt_scr)

    @pl.when(phase == 0)
    def _():
        cnt_scr[...] += jnp.sum(oh, axis=1, keepdims=True)

    @pl.when((phase == 1) & (b == 0))
    def _():
        cnt = cnt_scr[...]
        n_tiles = jnp.floor((cnt + (TME - 1)) * (1.0 / TME))
        ri = lax.broadcasted_iota(jnp.int32, (N_EXPERTS, N_EXPERTS), 0)
        ci = lax.broadcasted_iota(jnp.int32, (N_EXPERTS, N_EXPERTS), 1)
        lower = jnp.where(ci < ri, 1.0, 0.0).astype(BF16)
        nt_hi, nt_lo = _split_bf16(jnp.broadcast_to(n_tiles, (N_EXPERTS, LANES)))
        start = (jnp.dot(lower, nt_hi, preferred_element_type=F32)
                 + jnp.dot(lower, nt_lo, preferred_element_type=F32))[:, 0:1]
        start_scr[...] = start * TME
        run_scr[...] = jnp.zeros_like(run_scr)
        tokhi_scr[...] = jnp.zeros_like(tokhi_scr)
        toklo_scr[...] = jnp.zeros_like(toklo_scr)
        n = lax.broadcasted_iota(jnp.int32, (N_EXPERTS, nt_pad), 1).astype(F32)
        e_n = lax.broadcasted_iota(jnp.int32, (N_EXPERTS, nt_pad), 0).astype(F32)
        owner = jnp.sum(jnp.where(start <= n, 1.0, 0.0), axis=0, keepdims=True) - 1.0
        own = e_n == owner
        cnt_o = jnp.sum(jnp.where(own, cnt, 0.0), axis=0, keepdims=True)
        start_o = jnp.sum(jnp.where(own, start, 0.0), axis=0, keepdims=True)
        valid = jnp.clip(cnt_o - (n[0:1] - start_o) * TME, 0.0, float(TME))
        tinfo_ref[...] = jnp.concatenate([owner, valid, jnp.zeros((6, nt_pad), F32)], axis=0).astype(jnp.int32)

    @pl.when(phase == 1)
    def _():
        ti = lax.broadcasted_iota(jnp.int32, (TP, TP), 0)
        tj = lax.broadcasted_iota(jnp.int32, (TP, TP), 1)
        upper = jnp.where(ti < tj, 1.0, 0.0).astype(BF16)
        before = jnp.dot(oh.astype(BF16), upper, preferred_element_type=F32)
        row = start_scr[...] + run_scr[...] + before
        p1 = jnp.sum(jnp.where(oh1, row, 0.0), axis=0, keepdims=True)
        p2 = jnp.sum(jnp.where(oh2, row, 0.0), axis=0, keepdims=True)
        pos_ref[...] = jnp.concatenate([p1, p2, jnp.zeros((6, TP), F32)], axis=0).astype(jnp.int32)
        run_scr[...] += jnp.sum(oh, axis=1, keepdims=True)

        tid = lax.broadcasted_iota(jnp.int32, (1, TP), 1) + (b * TP + 1)
        t_hi = (tid // 256).astype(F32)
        t_lo = (tid % 256).astype(F32)
        tile_id = lax.broadcasted_iota(jnp.int32, (nt_rows, TP), 0).astype(F32)
        row_id = lax.broadcasted_iota(jnp.int32, (TME, TP), 0).astype(F32)
        for p in (p1, p2):
            hi = jnp.floor(p * (1.0 / TME))
            lo = p - hi * TME
            in_tile = jnp.where(tile_id == hi, 1.0, 0.0).astype(BF16)
            at_row = row_id == lo
            tokhi_scr[...] += _dot_nt(in_tile, jnp.where(at_row, t_hi, 0.0).astype(BF16))
            toklo_scr[...] += _dot_nt(in_tile, jnp.where(at_row, t_lo, 0.0).astype(BF16))

    @pl.when((phase == 1) & (b == pl.num_programs(1) - 1))
    def _():
        hit = (tokhi_scr[...] * 256.0 + toklo_scr[...]).astype(jnp.int32)
        row = (lax.broadcasted_iota(jnp.int32, hit.shape, 0) * TME + lax.broadcasted_iota(jnp.int32, hit.shape, 1))
        tok_ref[...] = jnp.where(hit > 0, hit - 1, row % n_tokens)


def _plan(route):
    t = route.shape[1]
    assert t < 256 * 256, "token id + 1 is carried as two base-256 digits"
    nt_pad = -(-_n_expert_tiles(t) // LANES) * LANES
    nt_rows = -(-_n_expert_tiles(t) // 8) * 8
    col = pltpu.VMEM((N_EXPERTS, 1), F32)
    table = pltpu.VMEM((nt_rows, TME), F32)
    return pl.pallas_call(
        functools.partial(_plan_body, nt_pad=nt_pad, nt_rows=nt_rows, n_tokens=t),
        grid=(2, t // TP),
        in_specs=[pl.BlockSpec((8, TP), lambda ph, b: (0, b))],
        out_specs=[pl.BlockSpec((8, TP), lambda ph, b: (0, b * ph)),
                   pl.BlockSpec((8, nt_pad), lambda ph, b: (0, 0)),
                   pl.BlockSpec((nt_rows, TME), lambda ph, b: (0, 0))],
        out_shape=[jax.ShapeDtypeStruct((8, t), jnp.int32), jax.ShapeDtypeStruct((8, nt_pad), jnp.int32),
                   jax.ShapeDtypeStruct((nt_rows, TME), jnp.int32)],
        scratch_shapes=[col, col, col, table, table],
        compiler_params=_params(("arbitrary", "arbitrary")),
        name="plan",
    )(route)


def _experts_body(texp_ref, tvalid_ref, xs_ref, wg_ref, wu_ref, wd_ref, ys_ref, wgb_scr, wub_scr, wdb_scr):
    n = pl.program_id(0)

    @pl.when((n == 0) | (texp_ref[n] != texp_ref[jnp.maximum(n - 1, 0)]))
    def _():
        wgb_scr[...] = wg_ref[...].astype(BF16)
        wub_scr[...] = wu_ref[...].astype(BF16)
        wdb_scr[...] = wd_ref[...].astype(BF16)

    @pl.when(tvalid_ref[n] > 0)
    def _():
        xb = _unpack_halves(xs_ref[...]).astype(BF16)
        gate = jnp.dot(xb, wgb_scr[...], preferred_element_type=F32)
        up = jnp.dot(xb, wub_scr[...], preferred_element_type=F32)
        he = (gate * jax.nn.sigmoid(gate) * up).astype(BF16)
        ys_ref[...] = _pack_halves(jnp.dot(he, wdb_scr[...], preferred_element_type=F32))

    @pl.when(tvalid_ref[n] == 0)
    def _():
        ys_ref[...] = jnp.zeros_like(ys_ref)


def _experts(texp, tvalid, xs, wg, wu, wd):
    nt = texp.shape[0]
    weight = lambda shape: pl.BlockSpec((None,) + shape, lambda n, te, tv: (te[n], 0, 0))
    return pl.pallas_call(
        _experts_body,
        grid_spec=pltpu.PrefetchScalarGridSpec(
            num_scalar_prefetch=2,
            grid=(nt,),
            in_specs=[pl.BlockSpec((TME, PACKED_W), lambda n, te, tv: (jnp.where(tv[n] > 0, n, 0), 0)),
                      weight((D_MODEL, D_EXPERT)), weight((D_MODEL, D_EXPERT)), weight((D_EXPERT, D_MODEL))],
            out_specs=pl.BlockSpec((TME, PACKED_W), lambda n, te, tv: (n, 0)),
            scratch_shapes=[pltpu.VMEM((D_MODEL, D_EXPERT), BF16), pltpu.VMEM((D_MODEL, D_EXPERT), BF16),
                            pltpu.VMEM((D_EXPERT, D_MODEL), BF16)],
        ),
        out_shape=jax.ShapeDtypeStruct((nt * TME, PACKED_W), F32),
        compiler_params=_params(("arbitrary",)),
        name="experts",
    )(texp, tvalid, xs, wg, wu, wd)


def _rowgather(table, idx):
    n_rows, width = idx.shape[0], table.shape[1]
    n_workers = SC_CORES * SC_SUBCORES
    per_worker = n_rows // n_workers
    assert per_worker * n_workers == n_rows and per_worker % SC_ROWS == 0
    mesh = plsc.VectorSubcoreMesh(core_axis_name="c", subcore_axis_name="s", num_cores=SC_CORES,
                                  num_subcores=SC_SUBCORES)

    chunks = per_worker // SC_ROWS
    assert chunks % 2 == 0
    buf = lambda: [pltpu.VMEM((SC_ROWS,), jnp.int32), pltpu.VMEM((SC_ROWS, width), table.dtype),
                   pltpu.SemaphoreType.DMA]

    @functools.partial(pl.kernel, mesh=mesh, out_type=jax.ShapeDtypeStruct((n_rows, width), table.dtype),
                       scratch_types=buf() + buf(), name="rowgather")
    def gather(table_hbm, idx_hbm, out_hbm, idx_a, rows_a, sem_a, idx_b, rows_b, sem_b):
        worker = lax.axis_index("s") * SC_CORES + lax.axis_index("c")
        base = worker * per_worker

        def fetch(c, idx_v, rows_v, sem):
            pltpu.sync_copy(idx_hbm.at[pl.ds(base + c * SC_ROWS, SC_ROWS)], idx_v)
            pltpu.async_copy(table_hbm.at[idx_v], rows_v, sem)

        def flush(c, idx_v, rows_v, sem):
            pltpu.make_async_copy(table_hbm.at[idx_v], rows_v, sem).wait()
            pltpu.sync_copy(rows_v, out_hbm.at[pl.ds(base + c * SC_ROWS, SC_ROWS)])

        fetch(0, idx_a, rows_a, sem_a)

        @pl.loop(0, chunks, step=2)
        def _(c):
            fetch(c + 1, idx_b, rows_b, sem_b)
            flush(c, idx_a, rows_a, sem_a)

            @pl.when(c + 2 < chunks)
            def _():
                fetch(c + 2, idx_a, rows_a, sem_a)

            flush(c + 1, idx_b, rows_b, sem_b)

    return gather(table, idx)


def _combine_body(g1_ref, g2_ref, x2_ref, wcol_ref, gf_ref, out_ref):
    w = wcol_ref[...]
    y = w[:, 2:3] * _unpack_halves(g1_ref[...]) + w[:, 3:4] * _unpack_halves(g2_ref[...])
    out_ref[...] = _rms(x2_ref[...] + y, gf_ref[...])


def _combine(g, x2, wcol, gf):
    t = x2.shape[0]
    steps = t // TF
    tile = lambda w: pl.BlockSpec((TF, w), lambda i: (i, 0))
    return pl.pallas_call(
        _combine_body,
        grid=(steps,),
        in_specs=[tile(PACKED_W), pl.BlockSpec((TF, PACKED_W), lambda i: (i + steps, 0)),
                  tile(D_MODEL), tile(LANES), _resident((1, D_MODEL))],
        out_specs=tile(D_MODEL),
        out_shape=jax.ShapeDtypeStruct((t, D_MODEL), F32),
        compiler_params=_params(("parallel",)),
        name="combine",
    )(g, g, x2, wcol, gf)


def _layer(x, mem, g_mix, w_in, b_ml_i, b_ml_f, b_fx_f, b_gate_ml, b_gate_fx, g_ml_head, w_proj_ml, w_proj_fx,
           w_out, g_xq, g_xmem, w_xq, w_xkv, w_xo, g_moe, w_rg, b_rg, w_re, b_re, w_gate, w_up, w_down):
    nb, seq, d = x.shape
    t = nb * seq
    row = lambda v: v.reshape(1, -1).astype(F32)

    o = 0
    parts = {}
    for name, width in (("ml_q", 512), ("ml_k", 512), ("ml_v", 1024), ("ml_o", 1024), ("ml_i", 4), ("ml_f", 4),
                        ("fx_q", 1024), ("fx_k", 1024), ("fx_v", 1024), ("fx_f", 8), ("gt_ml", 1024),
                        ("gt_fx", 1024)):
        parts[name] = w_in[:, o:o + width]
        o += width
    w_main = jnp.concatenate(
        [parts["ml_q"], parts["ml_k"] * (ML_QK_DIM ** -0.5), parts["ml_o"],
         parts["fx_q"] * (FX_HEAD_DIM ** -0.5 * LOG2E), parts["fx_k"], parts["gt_ml"], parts["gt_fx"]],
        axis=1).astype(BF16)
    w_vt = jnp.concatenate([parts["ml_v"], parts["fx_v"]], axis=1).T.astype(BF16)
    w_gates_t = jnp.concatenate([parts["ml_i"], parts["ml_f"], parts["fx_f"]], axis=1).T
    gate_bias = jnp.concatenate([b_ml_i, b_ml_f, b_fx_f]).reshape(N_GATES, 1).astype(F32)

    x2d = x.reshape(t, d)
    z, vt, gt3 = _inproj(x2d, row(g_mix), w_main, w_vt, w_gates_t, seq)
    rows, cols, caug = _gateprep(gt3, gate_bias)
    z3 = z.reshape(nb, seq, Z_W)
    vt4 = vt.reshape(nb, seq // TK, VT_W, TK)
    y_ml = _mlstm(z3, vt4, cols, rows, row(g_ml_head))
    y_fx = _fox(z3, vt4, caug, rows)
    x1 = _merge(x2d, y_ml.reshape(t, d), y_fx.reshape(t, d), z, row(b_gate_ml), row(b_gate_fx),
                w_proj_ml.astype(BF16), w_proj_fx.astype(BF16), w_out.astype(BF16))

    n_mem = mem.shape[1]
    kv = _memkv(mem.reshape(nb * n_mem, d), row(g_xmem), w_xkv.astype(BF16))
    w_router_t = jnp.concatenate([w_re, w_rg, jnp.zeros((d, ROUTER_ROWS - N_EXPERTS - N_GROUPS), F32)], axis=1).T
    b_router_t = jnp.concatenate([b_re, b_rg, jnp.zeros((ROUTER_ROWS - N_EXPERTS - N_GROUPS,), F32)]).reshape(ROUTER_ROWS, 1)
    x2, h3, route, wcol = _xattn(x1, kv.reshape(nb, n_mem, 2 * d), row(g_xq), w_xq.astype(BF16),
                                 w_xo.astype(BF16), row(g_moe), w_router_t, b_router_t, seq)
    return x2, h3, route, wcol


def _moe(x2, h3, route, wcol, w_gate, w_up, w_down, g_final):
    nt = _n_expert_tiles(x2.shape[0])
    pos, tinfo, tok = _plan(route)
    texp, tvalid = tinfo[0, :nt], tinfo[1, :nt]
    xs = _rowgather(h3, tok[:nt].reshape(-1))
    ys = _experts(texp, tvalid, xs, w_gate, w_up, w_down)
    g = _rowgather(ys, pos[0:2].reshape(-1))
    return _combine(g, x2, wcol, g_final)


def kernel(x, mem, g_mix, w_in, b_ml_i, b_ml_f, b_fx_f, b_gate_ml, b_gate_fx, g_ml_head, w_proj_ml, w_proj_fx, w_out, g_xq, g_xmem, w_xq, w_xkv, w_xo, g_moe, w_rg, b_rg, w_re, b_re, w_gate, w_up, w_down, g_final):
    nb, seq, d = x.shape
    depth = g_mix.shape[0]
    assert depth == 1, "the final rmsnorm is fused into the (single) layer's combine kernel"
    x2, h3, route, wcol = _layer(
        x, mem, g_mix[0], w_in[0], b_ml_i[0], b_ml_f[0], b_fx_f[0], b_gate_ml[0], b_gate_fx[0], g_ml_head[0],
        w_proj_ml[0], w_proj_fx[0], w_out[0], g_xq[0], g_xmem[0], w_xq[0], w_xkv[0], w_xo[0], g_moe[0],
        w_rg[0], b_rg[0], w_re[0], b_re[0], w_gate[0], w_up[0], w_down[0])
    out = _moe(x2, h3, route, wcol, w_gate[0], w_up[0], w_down[0], g_final.reshape(1, d).astype(F32))
    return out.reshape(nb, seq, d)
```
